```python
import jax, jax.numpy as jnp
from jax import lax
import numpy as np

D_MODEL = 4096
BATCH = 8
SEQ = 4096
DEPTH = 1

H_A = 16
DH_A = 128
H_B = 32
HKV_B = 4
G_B = H_B // HKV_B
DH_B = 64
WINDOW = 128
NUM_BUCKETS = 32
MAX_DISTANCE = 128
BLOCK = 128
D_FF = ((8 * D_MODEL // 3 + 255) // 256) * 256
EPS = 1e-6

W_QA = H_A * DH_A
W_KA = H_A * DH_A
W_VA = H_A * DH_A
W_FA = H_A
W_QB = H_B * DH_B
W_KB = HKV_B * DH_B
W_VB = HKV_B * DH_B
W_GA = D_MODEL
W_GB = D_MODEL
W_IN = W_QA + W_KA + W_VA + W_FA + W_QB + W_KB + W_VB + W_GA + W_GB

kernel_name = "fox_swa_sink_gated_hybrid_block"


def rms_norm(x, g):
    xf = x.astype(jnp.float32)
    y = xf * lax.rsqrt(jnp.mean(xf * xf, axis=-1, keepdims=True) + EPS)
    return (y * g.astype(jnp.float32)).astype(x.dtype)


def t5_bucket(dist):
    max_exact = NUM_BUCKETS // 2
    small = dist < max_exact
    large = max_exact + (np.log(np.maximum(dist, 1) / max_exact) / np.log(MAX_DISTANCE / max_exact)
                         * (NUM_BUCKETS - max_exact)).astype(np.int64)
    large = np.minimum(large, NUM_BUCKETS - 1)
    return np.where(small, dist, large)


def band_geometry(n_blocks):
    ql = np.arange(BLOCK)[:, None]
    kl = np.arange(2 * BLOCK)[None, :]
    dist = ql + BLOCK - kl
    in_window = (dist >= 0) & (dist < WINDOW)
    key_global = np.arange(n_blocks)[:, None, None] * BLOCK - BLOCK + kl[None]
    mask = in_window[None] & (key_global >= 0)
    bucket = t5_bucket(np.clip(dist, 0, None))
    return jnp.asarray(mask), jnp.asarray(bucket.astype(np.int32))


def forgetting_attention(q, k, v, f_logit):
    B, S, H, D = q.shape
    n_blocks = S // BLOCK
    log_f = jax.nn.log_sigmoid(f_logit.astype(jnp.float32))
    c = lax.cumsum(log_f, axis=1).transpose(0, 2, 1)
    key_pos = jnp.arange(S)
    scale = D ** -0.5

    def one_block(i):
        start = i * BLOCK
        q_blk = lax.dynamic_slice_in_dim(q, start, BLOCK, axis=1)
        c_q = lax.dynamic_slice_in_dim(c, start, BLOCK, axis=2)
        s = jnp.einsum('bqhd,bkhd->bhqk', q_blk, k, preferred_element_type=jnp.float32) * scale
        s = s + c_q[..., None] - c[:, :, None, :]
        q_pos = start + jnp.arange(BLOCK)
        causal = key_pos[None, :] <= q_pos[:, None]
        p = jax.nn.softmax(jnp.where(causal, s, -jnp.inf), axis=-1)
        return jnp.einsum('bhqk,bkhd->bqhd', p.astype(v.dtype), v)

    out = lax.map(one_block, jnp.arange(n_blocks))
    return out.transpose(1, 0, 2, 3, 4).reshape(B, S, H * D)


def sliding_window_sink_attention(q, k, v, sinks, rel_bias):
    B, S = q.shape[:2]
    n_blocks = S // BLOCK
    mask, bucket = band_geometry(n_blocks)
    bias = rel_bias.astype(jnp.float32)[bucket].transpose(2, 0, 1).reshape(HKV_B, G_B, BLOCK, 2 * BLOCK)
    qb = q.reshape(B, n_blocks, BLOCK, HKV_B, G_B, DH_B)

    def band(t):
        tp = jnp.pad(t, ((0, 0), (BLOCK, 0), (0, 0), (0, 0))).reshape(B, n_blocks + 1, BLOCK, HKV_B, DH_B)
        return jnp.concatenate([tp[:, :-1], tp[:, 1:]], axis=2)

    kb, vb = band(k), band(v)
    s = jnp.einsum('bnqhgd,bnkhd->bnhgqk', qb, kb, preferred_element_type=jnp.float32) * (DH_B ** -0.5)
    s = jnp.where(mask[None, :, None, None], s + bias, -jnp.inf)
    sink = sinks.astype(jnp.float32).reshape(1, 1, HKV_B, G_B, 1, 1)
    m = jnp.maximum(jnp.max(s, axis=-1, keepdims=True), sink)
    p = jnp.exp(s - m)
    p = p / (jnp.sum(p, axis=-1, keepdims=True) + jnp.exp(sink - m))
    o = jnp.einsum('bnhgqk,bnkhd->bnqhgd', p.astype(v.dtype), vb)
    return o.reshape(B, S, H_B * DH_B)


def _fwd_setup_inputs(seed: int = 0) -> dict:
    key = jax.random.key(seed)
    ks = jax.random.split(key, 16)
    f32 = jnp.float32

    def w(k, shape, fan_in):
        return jax.random.normal(k, shape, f32) * (fan_in ** -0.5)

    return {
        "x": jax.random.normal(ks[0], (BATCH, SEQ, D_MODEL), f32),
        "norm1_g": 1.0 + 0.02 * jax.random.normal(ks[1], (DEPTH, D_MODEL), f32),
        "w_in": w(ks[2], (DEPTH, D_MODEL, W_IN), D_MODEL),
        "b_forget": 0.1 * jax.random.normal(ks[3], (DEPTH, H_A), f32),
        "attn_sinks": 0.5 * jax.random.normal(ks[4], (DEPTH, H_B), f32),
        "rel_bias": 0.1 * jax.random.normal(ks[5], (NUM_BUCKETS, H_B), f32),
        "w_branch_a": w(ks[6], (DEPTH, H_A * DH_A, D_MODEL), H_A * DH_A),
        "w_branch_b": w(ks[7], (DEPTH, H_B * DH_B, D_MODEL), H_B * DH_B),
        "w_out": w(ks[8], (DEPTH, D_MODEL, D_MODEL), D_MODEL),
        "norm2_g": 1.0 + 0.02 * jax.random.normal(ks[9], (DEPTH, D_MODEL), f32),
        "w_ffn_gate": w(ks[10], (DEPTH, D_MODEL, D_FF), D_MODEL),
        "w_ffn_up": w(ks[11], (DEPTH, D_MODEL, D_FF), D_MODEL),
        "w_ffn_down": w(ks[12], (DEPTH, D_FF, D_MODEL), D_FF),
        "final_g": 1.0 + 0.02 * jax.random.normal(ks[13], (D_MODEL,), f32),
    }


def _fwd_reference(x, norm1_g, w_in, b_forget, attn_sinks, rel_bias, w_branch_a, w_branch_b,
              w_out, norm2_g, w_ffn_gate, w_ffn_up, w_ffn_down, final_g):
    B, S, _ = x.shape
    split_at = list(np.cumsum([W_QA, W_KA, W_VA, W_FA, W_QB, W_KB, W_VB, W_GA])[:])
    for l in range(DEPTH):
        h = rms_norm(x, norm1_g[l])
        proj = jnp.einsum('bsd,dn->bsn', h, w_in[l])
        qa, ka, va, fa, qb, kb, vb, ga, gb = jnp.split(proj, split_at, axis=-1)
        qa = qa.reshape(B, S, H_A, DH_A)
        ka = ka.reshape(B, S, H_A, DH_A)
        va = va.reshape(B, S, H_A, DH_A)
        fa = fa + b_forget[l]
        qb = qb.reshape(B, S, HKV_B, G_B, DH_B)
        kb = kb.reshape(B, S, HKV_B, DH_B)
        vb = vb.reshape(B, S, HKV_B, DH_B)

        ya = jnp.einsum('bsc,cd->bsd', forgetting_attention(qa, ka, va, fa), w_branch_a[l])
        yb = jnp.einsum('bsc,cd->bsd', sliding_window_sink_attention(qb, kb, vb, attn_sinks[l], rel_bias),
                        w_branch_b[l])
        mixed = jax.nn.sigmoid(ga) * ya + jax.nn.sigmoid(gb) * yb
        x = x + jnp.einsum('bsd,de->bse', mixed, w_out[l])

        h = rms_norm(x, norm2_g[l])
        hidden = jax.nn.silu(jnp.einsum('bsd,df->bsf', h, w_ffn_gate[l])) * jnp.einsum('bsd,df->bsf', h, w_ffn_up[l])
        x = x + jnp.einsum('bsf,fd->bsd', hidden, w_ffn_down[l])
    return rms_norm(x, final_g)


import jax as _jax
import jax.numpy as _jnp

TWIN_FORMAT = 'train_step'
FWD_PARAMS = ['x', 'norm1_g', 'w_in', 'b_forget', 'attn_sinks', 'rel_bias', 'w_branch_a', 'w_branch_b', 'w_out', 'norm2_g', 'w_ffn_gate', 'w_ffn_up', 'w_ffn_down', 'final_g']
TWIN_WEIGHTS = ['norm1_g', 'w_in', 'b_forget', 'attn_sinks', 'rel_bias', 'w_branch_a', 'w_branch_b', 'w_out', 'norm2_g', 'w_ffn_gate', 'w_ffn_up', 'w_ffn_down', 'final_g']
TWIN_DIFF_INPUT = 'x'
TWIN_INPUTS = ['x', 'norm1_g', 'w_in', 'b_forget', 'attn_sinks', 'rel_bias', 'w_branch_a', 'w_branch_b', 'w_out', 'norm2_g', 'w_ffn_gate', 'w_ffn_up', 'w_ffn_down', 'final_g', 'loss_target', 'm_norm1_g', 'm_w_in', 'm_b_forget', 'm_attn_sinks', 'm_rel_bias', 'm_w_branch_a', 'm_w_branch_b', 'm_w_out', 'm_norm2_g', 'm_w_ffn_gate', 'm_w_ffn_up', 'm_w_ffn_down', 'm_final_g', 'v_norm1_g', 'v_w_in', 'v_b_forget', 'v_attn_sinks', 'v_rel_bias', 'v_w_branch_a', 'v_w_branch_b', 'v_w_out', 'v_norm2_g', 'v_w_ffn_gate', 'v_w_ffn_up', 'v_w_ffn_down', 'v_final_g']
TWIN_OUTPUTS = ['loss', 'grad_x', 'grad_norm1_g', 'grad_w_in', 'grad_b_forget', 'grad_attn_sinks', 'grad_rel_bias', 'grad_w_branch_a', 'grad_w_branch_b', 'grad_w_out', 'grad_norm2_g', 'grad_w_ffn_gate', 'grad_w_ffn_up', 'grad_w_ffn_down', 'grad_final_g', 'delta_norm1_g', 'delta_w_in', 'delta_b_forget', 'delta_attn_sinks', 'delta_rel_bias', 'delta_w_branch_a', 'delta_w_branch_b', 'delta_w_out', 'delta_norm2_g', 'delta_w_ffn_gate', 'delta_w_ffn_up', 'delta_w_ffn_down', 'delta_final_g', 'new_m_norm1_g', 'new_m_w_in', 'new_m_b_forget', 'new_m_attn_sinks', 'new_m_rel_bias', 'new_m_w_branch_a', 'new_m_w_branch_b', 'new_m_w_out', 'new_m_norm2_g', 'new_m_w_ffn_gate', 'new_m_w_ffn_up', 'new_m_w_ffn_down', 'new_m_final_g', 'new_v_norm1_g', 'new_v_w_in', 'new_v_b_forget', 'new_v_attn_sinks', 'new_v_rel_bias', 'new_v_w_branch_a', 'new_v_w_branch_b', 'new_v_w_out', 'new_v_norm2_g', 'new_v_w_ffn_gate', 'new_v_w_ffn_up', 'new_v_w_ffn_down', 'new_v_final_g']
TWIN_LEAF_KINDS = {'loss': 'loss', 'grad_x': 'grad_x', 'grad_norm1_g': 'grad_w', 'grad_w_in': 'grad_w', 'grad_b_forget': 'grad_w', 'grad_attn_sinks': 'grad_w', 'grad_rel_bias': 'grad_w', 'grad_w_branch_a': 'grad_w', 'grad_w_branch_b': 'grad_w', 'grad_w_out': 'grad_w', 'grad_norm2_g': 'grad_w', 'grad_w_ffn_gate': 'grad_w', 'grad_w_ffn_up': 'grad_w', 'grad_w_ffn_down': 'grad_w', 'grad_final_g': 'grad_w', 'delta_norm1_g': 'delta_w', 'delta_w_in': 'delta_w', 'delta_b_forget': 'delta_w', 'delta_attn_sinks': 'delta_w', 'delta_rel_bias': 'delta_w', 'delta_w_branch_a': 'delta_w', 'delta_w_branch_b': 'delta_w', 'delta_w_out': 'delta_w', 'delta_norm2_g': 'delta_w', 'delta_w_ffn_gate': 'delta_w', 'delta_w_ffn_up': 'delta_w', 'delta_w_ffn_down': 'delta_w', 'delta_final_g': 'delta_w', 'new_m_norm1_g': 'new_m', 'new_m_w_in': 'new_m', 'new_m_b_forget': 'new_m', 'new_m_attn_sinks': 'new_m', 'new_m_rel_bias': 'new_m', 'new_m_w_branch_a': 'new_m', 'new_m_w_branch_b': 'new_m', 'new_m_w_out': 'new_m', 'new_m_norm2_g': 'new_m', 'new_m_w_ffn_gate': 'new_m', 'new_m_w_ffn_up': 'new_m', 'new_m_w_ffn_down': 'new_m', 'new_m_final_g': 'new_m', 'new_v_norm1_g': 'new_v', 'new_v_w_in': 'new_v', 'new_v_b_forget': 'new_v', 'new_v_attn_sinks': 'new_v', 'new_v_rel_bias': 'new_v', 'new_v_w_branch_a': 'new_v', 'new_v_w_branch_b': 'new_v', 'new_v_w_out': 'new_v', 'new_v_norm2_g': 'new_v', 'new_v_w_ffn_gate': 'new_v', 'new_v_w_ffn_up': 'new_v', 'new_v_w_ffn_down': 'new_v', 'new_v_final_g': 'new_v'}


def _forward(args):
    return _fwd_reference(*[args[k] for k in FWD_PARAMS])


def _output_shape():
    out = _jax.eval_shape(lambda: _forward(_fwd_setup_inputs(0)))
    return out.shape, out.dtype

N_MICROBATCH = 1
ADAM_LR = 0.001
ADAM_B1 = 0.9
ADAM_B2 = 0.999
ADAM_EPS = 1e-08
ADAM_WD = 0.01
ADAM_STEP = 10
PER_EXAMPLE_BATCH_AXIS = {'x': 0, 'loss_target': 0}
SHARED_INPUTS = []
_WEIGHT_DTYPES = {'norm1_g': _jnp.float32, 'w_in': _jnp.float32, 'b_forget': _jnp.float32, 'attn_sinks': _jnp.float32, 'rel_bias': _jnp.float32, 'w_branch_a': _jnp.float32, 'w_branch_b': _jnp.float32, 'w_out': _jnp.float32, 'norm2_g': _jnp.float32, 'w_ffn_gate': _jnp.float32, 'w_ffn_up': _jnp.float32, 'w_ffn_down': _jnp.float32, 'final_g': _jnp.float32}
MOMENT_SCALE = {'norm1_g': 2.384929e-02, 'w_in': 1.182981e-02, 'b_forget': 1.274657e-01, 'attn_sinks': 4.855037e-03, 'rel_bias': 8.495917e-03, 'w_branch_a': 1.677375e-02, 'w_branch_b': 5.094402e-03, 'w_out': 1.751371e-02, 'norm2_g': 3.103920e-02, 'w_ffn_gate': 1.366147e-02, 'w_ffn_up': 1.321435e-02, 'w_ffn_down': 2.170083e-02, 'final_g': 7.988618e+00}


def _to_microbatches(a, axis):
    t = _jnp.moveaxis(a, axis, 0)
    t = t.reshape((N_MICROBATCH, t.shape[0] // N_MICROBATCH) + t.shape[1:])
    return _jnp.moveaxis(t, 1, axis + 1)


def setup_inputs(seed: int = 0) -> dict:
    inp = _fwd_setup_inputs(seed)
    key = _jax.random.fold_in(_jax.random.key(seed), 7919)
    shape, _ = _output_shape()
    out = dict(inp)
    out["loss_target"] = _jax.random.normal(_jax.random.fold_in(key, 0), shape, _jnp.float32)
    for i, name in enumerate(TWIN_WEIGHTS):
        w = inp[name].astype(_jnp.float32)
        if MOMENT_SCALE is None:
            s = _jnp.sqrt(_jnp.mean(_jnp.square(w)) + 1e-30)
        else:
            s = MOMENT_SCALE[name]
        km, kv = _jax.random.split(_jax.random.fold_in(key, i + 1))
        out[name] = w
        out["m_" + name] = s * _jax.random.normal(km, w.shape, _jnp.float32)
        out["v_" + name] = (s * s) * _jax.random.uniform(kv, w.shape, _jnp.float32, 0.5, 1.5)
    if N_MICROBATCH > 1:
        for name, axis in PER_EXAMPLE_BATCH_AXIS.items():
            out[name] = _to_microbatches(out[name], axis)
    return {'x': out['x'], 'norm1_g': out['norm1_g'], 'w_in': out['w_in'], 'b_forget': out['b_forget'], 'attn_sinks': out['attn_sinks'], 'rel_bias': out['rel_bias'], 'w_branch_a': out['w_branch_a'], 'w_branch_b': out['w_branch_b'], 'w_out': out['w_out'], 'norm2_g': out['norm2_g'], 'w_ffn_gate': out['w_ffn_gate'], 'w_ffn_up': out['w_ffn_up'], 'w_ffn_down': out['w_ffn_down'], 'final_g': out['final_g'], 'loss_target': out['loss_target'], 'm_norm1_g': out['m_norm1_g'], 'm_w_in': out['m_w_in'], 'm_b_forget': out['m_b_forget'], 'm_attn_sinks': out['m_attn_sinks'], 'm_rel_bias': out['m_rel_bias'], 'm_w_branch_a': out['m_w_branch_a'], 'm_w_branch_b': out['m_w_branch_b'], 'm_w_out': out['m_w_out'], 'm_norm2_g': out['m_norm2_g'], 'm_w_ffn_gate': out['m_w_ffn_gate'], 'm_w_ffn_up': out['m_w_ffn_up'], 'm_w_ffn_down': out['m_w_ffn_down'], 'm_final_g': out['m_final_g'], 'v_norm1_g': out['v_norm1_g'], 'v_w_in': out['v_w_in'], 'v_b_forget': out['v_b_forget'], 'v_attn_sinks': out['v_attn_sinks'], 'v_rel_bias': out['v_rel_bias'], 'v_w_branch_a': out['v_w_branch_a'], 'v_w_branch_b': out['v_w_branch_b'], 'v_w_out': out['v_w_out'], 'v_norm2_g': out['v_norm2_g'], 'v_w_ffn_gate': out['v_w_ffn_gate'], 'v_w_ffn_up': out['v_w_ffn_up'], 'v_w_ffn_down': out['v_w_ffn_down'], 'v_final_g': out['v_final_g']}


def _loss(weights, diff, rest, loss_target):
    with _jax.named_scope("forward"):
        args = {**rest, TWIN_DIFF_INPUT: diff, **{k: w.astype(_WEIGHT_DTYPES[k]) for k, w in weights.items()}}
        y = _forward(args)
    with _jax.named_scope("loss_head"):
        err = _jnp.square(y.astype(_jnp.float32) - loss_target)
        return 0.5 * _jnp.sum(_jnp.mean(err, axis=-1)) if err.ndim else 0.5 * err


def _adamw(w, g, m, v):
    m = ADAM_B1 * m + (1.0 - ADAM_B1) * g
    v = ADAM_B2 * v + (1.0 - ADAM_B2) * _jnp.square(g)
    m_hat = m / (1.0 - ADAM_B1 ** ADAM_STEP)
    v_hat = v / (1.0 - ADAM_B2 ** ADAM_STEP)
    delta = -ADAM_LR * (m_hat / (_jnp.sqrt(v_hat) + ADAM_EPS) + ADAM_WD * w)
    return delta, m, v


def reference(x, norm1_g, w_in, b_forget, attn_sinks, rel_bias, w_branch_a, w_branch_b, w_out, norm2_g, w_ffn_gate, w_ffn_up, w_ffn_down, final_g, loss_target, m_norm1_g, m_w_in, m_b_forget, m_attn_sinks, m_rel_bias, m_w_branch_a, m_w_branch_b, m_w_out, m_norm2_g, m_w_ffn_gate, m_w_ffn_up, m_w_ffn_down, m_final_g, v_norm1_g, v_w_in, v_b_forget, v_attn_sinks, v_rel_bias, v_w_branch_a, v_w_branch_b, v_w_out, v_norm2_g, v_w_ffn_gate, v_w_ffn_up, v_w_ffn_down, v_final_g):
    given = dict(x=x, norm1_g=norm1_g, w_in=w_in, b_forget=b_forget, attn_sinks=attn_sinks, rel_bias=rel_bias, w_branch_a=w_branch_a, w_branch_b=w_branch_b, w_out=w_out, norm2_g=norm2_g, w_ffn_gate=w_ffn_gate, w_ffn_up=w_ffn_up, w_ffn_down=w_ffn_down, final_g=final_g, loss_target=loss_target, m_norm1_g=m_norm1_g, m_w_in=m_w_in, m_b_forget=m_b_forget, m_attn_sinks=m_attn_sinks, m_rel_bias=m_rel_bias, m_w_branch_a=m_w_branch_a, m_w_branch_b=m_w_branch_b, m_w_out=m_w_out, m_norm2_g=m_norm2_g, m_w_ffn_gate=m_w_ffn_gate, m_w_ffn_up=m_w_ffn_up, m_w_ffn_down=m_w_ffn_down, m_final_g=m_final_g, v_norm1_g=v_norm1_g, v_w_in=v_w_in, v_b_forget=v_b_forget, v_attn_sinks=v_attn_sinks, v_rel_bias=v_rel_bias, v_w_branch_a=v_w_branch_a, v_w_branch_b=v_w_branch_b, v_w_out=v_w_out, v_norm2_g=v_norm2_g, v_w_ffn_gate=v_w_ffn_gate, v_w_ffn_up=v_w_ffn_up, v_w_ffn_down=v_w_ffn_down, v_final_g=v_final_g)
    weights = {n: given[n] for n in TWIN_WEIGHTS}
    shared = {n: given[n] for n in SHARED_INPUTS}
    per_example = {n: given[n] for n in ['x']}
    grad_fn = _jax.value_and_grad(_loss, argnums=(0, 1))

    def one_microbatch(ex, loss_target):
        ex = dict(ex)
        diff = ex.pop(TWIN_DIFF_INPUT)
        return grad_fn(weights, diff, {**shared, **ex}, loss_target)

    if N_MICROBATCH == 1:
        loss, (grad_w, grad_x) = one_microbatch(per_example, given["loss_target"])
    else:
        def body(carry, xs):
            loss_sum, grad_sum = carry
            l_k, (gw_k, gx_k) = one_microbatch(xs[0], xs[1])
            with _jax.named_scope("update"):
                return (loss_sum + l_k, _jax.tree.map(_jnp.add, grad_sum, gw_k)), gx_k

        init = (_jnp.zeros((), _jnp.float32), _jax.tree.map(_jnp.zeros_like, weights))
        (loss, grad_w), grad_x = _jax.lax.scan(body, init, (per_example, given["loss_target"]))
    with _jax.named_scope("update"):
        delta_w, new_m, new_v = {}, {}, {}
        for n in TWIN_WEIGHTS:
            delta_w[n], new_m[n], new_v[n] = _adamw(weights[n], grad_w[n], given["m_" + n], given["v_" + n])
    return (loss, grad_x, *[grad_w[n] for n in TWIN_WEIGHTS], *[delta_w[n] for n in TWIN_WEIGHTS],
            *[new_m[n] for n in TWIN_WEIGHTS], *[new_v[n] for n in TWIN_WEIGHTS])
```

```python
import functools
import math

import numpy as np
import jax
import jax.numpy as jnp
from jax import lax
from jax.experimental import pallas as pl
from jax.experimental.pallas import tpu as pltpu

F32 = jnp.float32
BF16 = jnp.bfloat16
MESH = pl.DeviceIdType.MESH

D_MODEL = 4096
SEQ = 4096
H_A = 16
DH_A = 128
H_B = 32
HKV_B = 4
G_B = H_B // HKV_B
DH_B = 64
WINDOW = 128
NUM_BUCKETS = 32
MAX_DISTANCE = 128
BLOCK = 128
D_FF = ((8 * D_MODEL // 3 + 255) // 256) * 256
EPS = 1e-6
ADAM_LR = 0.001
ADAM_B1 = 0.9
ADAM_B2 = 0.999
ADAM_EPS = 1e-08
ADAM_WD = 0.01
ADAM_STEP = 10

N_CHIP = 4
LANE = 128
VMEM_LIMIT = 56 * 1024 * 1024

TM = 512
TN = 512
TK = 4096
TQ_A = 512
TR_EW = 256

W_A = H_A * DH_A
W_QB = H_B * DH_B
W_KB = HKV_B * DH_B
OFF_QA = 0
OFF_KA = W_A
OFF_VA = 2 * W_A
OFF_QB = 3 * W_A
OFF_KB = OFF_QB + W_QB
OFF_VB = OFF_KB + W_KB
OFF_GA = OFF_VB + W_KB
OFF_GB = OFF_GA + D_MODEL
OFF_FA = OFF_GB + D_MODEL
PW = OFF_FA + LANE
W_QKV = OFF_GA
W_IN = 3 * W_A + H_A + W_QB + 2 * W_KB + 2 * D_MODEL
W_IN_SH = W_IN // N_CHIP
A_SH = D_MODEL // N_CHIP
FF_SH = D_FF // N_CHIP
FF_P = -(-FF_SH // LANE) * LANE
FP = N_CHIP * FF_P
PACK_ROWS = 16
RB_ROWS = -(-(NUM_BUCKETS * H_B) // D_MODEL)


def _tile(n, target, mult=LANE):
    t = min(target, n) // mult * mult
    while t > mult and n % t:
        t -= mult
    assert t > 0 and n % t == 0, (n, target, mult)
    return t


def _cparams(sem):
    return pltpu.CompilerParams(dimension_semantics=sem, vmem_limit_bytes=VMEM_LIMIT)


def _sigmoid(x):
    return 1.0 / (1.0 + jnp.exp(-x))


def _dot(a, b):
    return jnp.dot(a, b, preferred_element_type=F32)


def _dot_nt(a, b):
    return lax.dot_general(a, b, (((1,), (1,)), ((), ())), preferred_element_type=F32)


def _slot_spec(shape, kind, br, bc, rc):
    if kind == "2d":
        return pl.BlockSpec((br, bc), lambda *g: rc(*g))
    if kind == "col":
        assert shape[2] % bc == 0, (shape, bc)
        per = shape[2] // bc

        def im_col(*g):
            rb, cb = rc(*g)
            return (cb // per, rb, cb % per)
        return pl.BlockSpec((None, br, bc), im_col)
    assert kind == "row" and shape[1] % br == 0, (shape, kind, br)
    per = shape[1] // br

    def im_row(*g):
        rb, cb = rc(*g)
        return (rb // per, rb % per, cb)
    return pl.BlockSpec((None, br, bc), im_row)


def _matmul(name, a, b, *, m, n, k, nt=False, b_kind="2d", o_kind="2d", out_shape, out_dtype,
            tm=None, tn=None, tk=None, b_noff=0, res=None, b_outer=False):
    tm = tm or _tile(m, TM, 16)
    tn = tn or _tile(n, TN)
    tk = tk or _tile(k, TK)
    assert m % tm == 0 and n % tn == 0 and k % tk == 0 and b_noff % tn == 0
    nk = k // tk
    noff = b_noff // tn
    if b_outer:
        grid = (n // tn, m // tm, nk)
        ij = lambda g0, g1: (g1, g0)
    else:
        grid = (m // tm, n // tn, nk)
        ij = lambda g0, g1: (g0, g1)

    a_spec = pl.BlockSpec((tm, tk), lambda g0, g1, kk: (ij(g0, g1)[0], kk))
    if nt:
        b_spec = _slot_spec(b.shape, b_kind, tn, tk, lambda g0, g1, kk: (ij(g0, g1)[1] + noff, kk))
    else:
        b_spec = _slot_spec(b.shape, b_kind, tk, tn, lambda g0, g1, kk: (kk, ij(g0, g1)[1] + noff))
    o_spec = _slot_spec(out_shape, o_kind, tm, tn, lambda g0, g1, kk: ij(g0, g1))
    in_specs = [a_spec, b_spec]
    args = [a, b]
    if res is not None:
        in_specs.append(pl.BlockSpec((tm, tn), lambda g0, g1, kk: ij(g0, g1)))
        args.append(res)

    def body(*refs):
        a_ref, b_ref = refs[0], refs[1]
        r_ref = refs[2] if res is not None else None
        o_ref = refs[3] if res is not None else refs[2]

        def prod():
            return _dot_nt(a_ref[...], b_ref[...]) if nt else _dot(a_ref[...], b_ref[...])

        def finish(acc):
            if r_ref is not None:
                acc = acc + r_ref[...].astype(F32)
            o_ref[...] = acc.astype(o_ref.dtype)

        if nk == 1:
            finish(prod())
            return
        acc_ref = refs[-1]
        kk = pl.program_id(2)

        @pl.when(kk == 0)
        def _():
            acc_ref[...] = prod()

        @pl.when(jnp.logical_and(kk > 0, kk < nk - 1))
        def _():
            acc_ref[...] += prod()

        @pl.when(kk == nk - 1)
        def _():
            finish(acc_ref[...] + prod())

    return pl.pallas_call(
        body, name=name, grid=grid, in_specs=in_specs, out_specs=o_spec,
        out_shape=jax.ShapeDtypeStruct(out_shape, out_dtype),
        scratch_shapes=[pltpu.VMEM((tm, tn), F32)] if nk > 1 else [],
        compiler_params=_cparams(("parallel", "parallel", "arbitrary")),
    )(*args)


def _rms_fwd(name, x, g):
    s, d = x.shape
    tr = _tile(s, TR_EW, 16)

    def body(x_ref, g_ref, h_ref):
        xf = x_ref[...]
        rstd = lax.rsqrt(jnp.mean(xf * xf, axis=-1, keepdims=True) + EPS)
        h_ref[...] = (xf * rstd * g_ref[...]).astype(h_ref.dtype)

    return pl.pallas_call(
        body, name=name, grid=(s // tr,),
        in_specs=[pl.BlockSpec((tr, d), lambda i: (i, 0)), pl.BlockSpec((1, d), lambda i: (0, 0))],
        out_specs=pl.BlockSpec((tr, d), lambda i: (i, 0)),
        out_shape=jax.ShapeDtypeStruct((s, d), BF16),
        compiler_params=_cparams(("parallel",)),
    )(x, g)


def _rms_bwd_rows(dh, xf, g):
    rstd = lax.rsqrt(jnp.mean(xf * xf, axis=-1, keepdims=True) + EPS)
    xhat = xf * rstd
    dxhat = dh * g
    dx = rstd * (dxhat - xhat * jnp.mean(dxhat * xhat, axis=-1, keepdims=True))
    return dx, dh * xhat


def _fold8(v):
    return jnp.sum(v.reshape(v.shape[0] // 8, 8, v.shape[1]), axis=0)


def _rms_bwd(name, dh, x, g, dres):
    s, d = x.shape
    tr = _tile(s, TR_EW, 16)
    n = s // tr

    def body(dh_ref, x_ref, g_ref, r_ref, dx_ref, dxb_ref, dg_ref, acc_ref):
        i = pl.program_id(0)
        dx, dgrows = _rms_bwd_rows(dh_ref[...].astype(F32), x_ref[...], g_ref[...])
        dx = dx + r_ref[...]
        dx_ref[...] = dx
        dxb_ref[...] = dx.astype(BF16)

        @pl.when(i == 0)
        def _():
            acc_ref[...] = jnp.zeros_like(acc_ref)
        acc_ref[...] += _fold8(dgrows)

        @pl.when(i == n - 1)
        def _():
            dg_ref[...] = jnp.sum(acc_ref[...], axis=0, keepdims=True)

    row = pl.BlockSpec((tr, d), lambda i: (i, 0))
    vec = pl.BlockSpec((1, d), lambda i: (0, 0))
    return pl.pallas_call(
        body, name=name, grid=(n,), in_specs=[row, row, vec, row], out_specs=[row, row, vec],
        out_shape=[jax.ShapeDtypeStruct((s, d), F32), jax.ShapeDtypeStruct((s, d), BF16),
                   jax.ShapeDtypeStruct((1, d), F32)],
        scratch_shapes=[pltpu.VMEM((8, d), F32)],
        compiler_params=_cparams(("arbitrary",)),
    )(dh, x, g, dres)


def _loss_head(x2, target, g):
    s, d = x2.shape
    tr = _tile(s, TR_EW, 16)
    n = s // tr

    def body(x_ref, t_ref, g_ref, dx_ref, dxb_ref, dg_ref, loss_ref, acc_ref):
        i = pl.program_id(0)
        xf = x_ref[...]
        gv = g_ref[...]
        rstd = lax.rsqrt(jnp.mean(xf * xf, axis=-1, keepdims=True) + EPS)
        err = xf * rstd * gv - t_ref[...]
        row_loss = jnp.mean(err * err, axis=-1, keepdims=True)
        dx, dgrows = _rms_bwd_rows(err / d, xf, gv)
        dx_ref[...] = dx
        dxb_ref[...] = dx.astype(BF16)

        @pl.when(i == 0)
        def _():
            acc_ref[...] = jnp.zeros_like(acc_ref)
            loss_ref[...] = jnp.zeros_like(loss_ref)
        acc_ref[...] += _fold8(dgrows)
        loss_ref[...] += jnp.broadcast_to(0.5 * jnp.sum(row_loss, axis=0, keepdims=True), (8, LANE))

        @pl.when(i == n - 1)
        def _():
            dg_ref[...] = jnp.sum(acc_ref[...], axis=0, keepdims=True)

    row = pl.BlockSpec((tr, d), lambda i: (i, 0))
    vec = pl.BlockSpec((1, d), lambda i: (0, 0))
    return pl.pallas_call(
        body, name="loss_head", grid=(n,), in_specs=[row, row, vec],
        out_specs=[row, row, vec, pl.BlockSpec((8, LANE), lambda i: (0, 0))],
        out_shape=[jax.ShapeDtypeStruct((s, d), F32), jax.ShapeDtypeStruct((s, d), BF16),
                   jax.ShapeDtypeStruct((1, d), F32), jax.ShapeDtypeStruct((8, LANE), F32)],
        scratch_shapes=[pltpu.VMEM((8, d), F32)],
        compiler_params=_cparams(("arbitrary",)),
    )(x2, target, g)


def _gate_fwd(ya, yb, proj_g):
    s, d = ya.shape
    tr = _tile(s, TR_EW, 16)
    tc = _tile(d, 1024)
    nc = d // tc

    def body(ya_ref, yb_ref, ga_ref, gb_ref, o_ref):
        o_ref[...] = (_sigmoid(ga_ref[...]) * ya_ref[...] + _sigmoid(gb_ref[...]) * yb_ref[...]).astype(o_ref.dtype)

    blk = pl.BlockSpec((tr, tc), lambda i, j: (i, j))
    return pl.pallas_call(
        body, name="gate_fwd", grid=(s // tr, nc),
        in_specs=[blk, blk, blk, pl.BlockSpec((tr, tc), lambda i, j: (i, j + nc))],
        out_specs=blk, out_shape=jax.ShapeDtypeStruct((s, d), BF16),
        compiler_params=_cparams(("parallel", "parallel")),
    )(ya, yb, proj_g, proj_g)


def _gate_bwd(dmixed, ya, yb, proj_g):
    s, d = ya.shape
    tr = _tile(s, TR_EW, 16)
    tc = _tile(d, 1024)
    nc = d // tc

    def body(dm_ref, ya_ref, yb_ref, ga_ref, gb_ref, dya_ref, dyb_ref, dga_ref, dgb_ref):
        dm = dm_ref[...].astype(F32)
        sa = _sigmoid(ga_ref[...])
        sb = _sigmoid(gb_ref[...])
        dya_ref[...] = (dm * sa).astype(BF16)
        dyb_ref[...] = (dm * sb).astype(BF16)
        dga_ref[...] = (dm * ya_ref[...] * sa * (1.0 - sa)).astype(BF16)
        dgb_ref[...] = (dm * yb_ref[...] * sb * (1.0 - sb)).astype(BF16)

    blk = pl.BlockSpec((tr, tc), lambda i, j: (i, j))
    out = jax.ShapeDtypeStruct((s, d), BF16)
    return pl.pallas_call(
        body, name="gate_bwd", grid=(s // tr, nc),
        in_specs=[blk, blk, blk, blk, pl.BlockSpec((tr, tc), lambda i, j: (i, j + nc))],
        out_specs=[blk, blk, blk, blk], out_shape=[out, out, out, out],
        compiler_params=_cparams(("parallel", "parallel")),
    )(dmixed, ya, yb, proj_g, proj_g)


def _swiglu_fwd(gu):
    s = gu.shape[0]
    tr = _tile(s, 128, 16)

    def body(gu_ref, h_ref):
        gate = gu_ref[:, :FF_P].astype(F32)
        up = gu_ref[:, FF_P:].astype(F32)
        h_ref[...] = (gate * _sigmoid(gate) * up).astype(h_ref.dtype)

    return pl.pallas_call(
        body, name="swiglu_fwd", grid=(s // tr, N_CHIP),
        in_specs=[pl.BlockSpec((tr, 2 * FF_P), lambda i, j: (i, j))],
        out_specs=pl.BlockSpec((tr, FF_P), lambda i, j: (i, j)),
        out_shape=jax.ShapeDtypeStruct((s, FP), BF16),
        compiler_params=_cparams(("parallel", "parallel")),
    )(gu)


def _swiglu_bwd(dhidden, gu):
    s = gu.shape[0]
    tr = _tile(s, 128, 16)

    def body(dh_ref, gu_ref, dgu_ref):
        gate = gu_ref[:, :FF_P].astype(F32)
        up = gu_ref[:, FF_P:].astype(F32)
        dh = dh_ref[...].astype(F32)
        sg = _sigmoid(gate)
        dgu_ref[:, :FF_P] = (dh * up * sg * (1.0 + gate * (1.0 - sg))).astype(BF16)
        dgu_ref[:, FF_P:] = (dh * gate * sg).astype(BF16)

    return pl.pallas_call(
        body, name="swiglu_bwd", grid=(s // tr, N_CHIP),
        in_specs=[pl.BlockSpec((tr, FF_P), lambda i, j: (i, j)), pl.BlockSpec((tr, 2 * FF_P), lambda i, j: (i, j))],
        out_specs=pl.BlockSpec((tr, 2 * FF_P), lambda i, j: (i, j)),
        out_shape=jax.ShapeDtypeStruct((s, 2 * FP), BF16),
        compiler_params=_cparams(("parallel", "parallel")),
    )(dhidden, gu)


def _tri(n, upper):
    r = lax.broadcasted_iota(jnp.int32, (n, n), 0)
    c = lax.broadcasted_iota(jnp.int32, (n, n), 1)
    return (c >= r if upper else c <= r).astype(F32)


def _forget_fwd(fa, bias):
    s = fa.shape[0]
    tb = _tile(s, 512, 8)

    def body(f_ref, b_ref, c_ref, carry_ref):
        i = pl.program_id(0)

        @pl.when(i == 0)
        def _():
            carry_ref[...] = jnp.zeros_like(carry_ref)
        z = f_ref[...] + b_ref[...]
        logf = jnp.minimum(z, 0.0) - jnp.log(1.0 + jnp.exp(-jnp.abs(z)))
        c = jnp.dot(_tri(tb, False), logf, precision=lax.Precision.HIGHEST, preferred_element_type=F32)
        c_ref[...] = c + carry_ref[0:1, :]
        carry_ref[...] = jnp.broadcast_to(c_ref[tb - 1:tb, :], carry_ref.shape)

    return pl.pallas_call(
        body, name="forget_fwd", grid=(s // tb,),
        in_specs=[pl.BlockSpec((tb, LANE), lambda i: (i, 0)), pl.BlockSpec((1, LANE), lambda i: (0, 0))],
        out_specs=pl.BlockSpec((tb, LANE), lambda i: (i, 0)),
        out_shape=jax.ShapeDtypeStruct((s, LANE), F32),
        scratch_shapes=[pltpu.VMEM((8, LANE), F32)],
        compiler_params=_cparams(("arbitrary",)),
    )(fa, bias)


def _forget_bwd(dc, fa, bias):
    s = fa.shape[0]
    tb = _tile(s, 512, 16)
    n = s // tb

    def body(dc_ref, f_ref, b_ref, df_ref, db_ref, carry_ref, acc_ref, tmp_ref):
        i = pl.program_id(0)

        @pl.when(i == 0)
        def _():
            carry_ref[...] = jnp.zeros_like(carry_ref)
            acc_ref[...] = jnp.zeros_like(acc_ref)
        dlogf = jnp.dot(_tri(tb, True), dc_ref[...], precision=lax.Precision.HIGHEST, preferred_element_type=F32)
        tmp_ref[...] = dlogf + carry_ref[0:1, :]
        carry_ref[...] = jnp.broadcast_to(tmp_ref[0:1, :], carry_ref.shape)
        df = tmp_ref[...] * _sigmoid(-(f_ref[...] + b_ref[...]))
        df_ref[...] = df.astype(BF16)
        acc_ref[...] += _fold8(df)

        @pl.when(i == n - 1)
        def _():
            db_ref[...] = jnp.sum(acc_ref[...], axis=0, keepdims=True)

    rev = pl.BlockSpec((tb, LANE), lambda i: (n - 1 - i, 0))
    vec = pl.BlockSpec((1, LANE), lambda i: (0, 0))
    return pl.pallas_call(
        body, name="forget_bwd", grid=(n,), in_specs=[rev, rev, vec], out_specs=[rev, vec],
        out_shape=[jax.ShapeDtypeStruct((s, LANE), BF16), jax.ShapeDtypeStruct((1, LANE), F32)],
        scratch_shapes=[pltpu.VMEM((8, LANE), F32), pltpu.VMEM((8, LANE), F32), pltpu.VMEM((tb, LANE), F32)],
        compiler_params=_cparams(("arbitrary",)),
    )(dc, fa, bias)


_A_SCALE = DH_A ** -0.5


def _causal(tq, tk, transposed):
    r = lax.broadcasted_iota(jnp.int32, (tq, tk), 0)
    c = lax.broadcasted_iota(jnp.int32, (tq, tk), 1)
    return c >= r if transposed else r >= c


def _fox_fwd(qkv, c_col, c_row):
    s = qkv.shape[0]
    t = _tile(s, TQ_A)
    n = s // t
    kb, vb = OFF_KA // DH_A, OFF_VA // DH_A

    def body(q_ref, k_ref, v_ref, cq_ref, ck_ref, o_ref, lse_ref, m_s, l_s, acc_s):
        i, j = pl.program_id(1), pl.program_id(2)

        @pl.when(j == 0)
        def _():
            m_s[...] = jnp.full_like(m_s, -jnp.inf)
            l_s[...] = jnp.zeros_like(l_s)
            acc_s[...] = jnp.zeros_like(acc_s)

        def step(diag):
            sc = _dot_nt(q_ref[...], k_ref[...]) * _A_SCALE + (cq_ref[:, 0:1] - ck_ref[...])
            if diag:
                sc = jnp.where(_causal(t, t, False), sc, -jnp.inf)
            m_prev = m_s[...]
            m_new = jnp.maximum(m_prev, jnp.max(sc, axis=1, keepdims=True))
            alpha = jnp.exp(m_prev - m_new)
            p = jnp.exp(sc - m_new[:, 0:1])
            l_s[...] = alpha * l_s[...] + jnp.sum(p, axis=1, keepdims=True)
            acc_s[...] = alpha * acc_s[...] + _dot(p.astype(BF16), v_ref[...])
            m_s[...] = m_new

        @pl.when(j < i)
        def _():
            step(False)

        @pl.when(j == i)
        def _():
            step(True)
            o_ref[...] = (acc_s[...] / l_s[...]).astype(o_ref.dtype)
            lse_ref[...] = m_s[...] + jnp.log(l_s[...])

    jc = lambda i, j: jnp.minimum(j, i)
    return pl.pallas_call(
        body, name="fox_fwd", grid=(H_A, n, n),
        in_specs=[pl.BlockSpec((t, DH_A), lambda h, i, j: (i, h)),
                  pl.BlockSpec((t, DH_A), lambda h, i, j: (jc(i, j), kb + h)),
                  pl.BlockSpec((t, DH_A), lambda h, i, j: (jc(i, j), vb + h)),
                  pl.BlockSpec((None, t, LANE), lambda h, i, j: (h, i, 0)),
                  pl.BlockSpec((None, 1, t), lambda h, i, j: (h, 0, jc(i, j)))],
        out_specs=[pl.BlockSpec((t, DH_A), lambda h, i, j: (i, h)),
                   pl.BlockSpec((None, t, LANE), lambda h, i, j: (h, i, 0))],
        out_shape=[jax.ShapeDtypeStruct((s, W_A), BF16), jax.ShapeDtypeStruct((H_A, s, LANE), F32)],
        scratch_shapes=[pltpu.VMEM((t, LANE), F32)] * 3,
        compiler_params=_cparams(("parallel", "parallel", "arbitrary")),
    )(qkv, qkv, qkv, c_col, c_row)


def _fox_dq(qkv, do, c_col, c_row, lse_col):
    s = qkv.shape[0]
    t = _tile(s, TQ_A)
    n = s // t
    kb, vb = OFF_KA // DH_A, OFF_VA // DH_A

    def body(q_ref, k_ref, v_ref, do_ref, cq_ref, ck_ref, lse_ref, dq_ref, dl_ref, pdk_s, pk_s, dl_s):
        i, j = pl.program_id(1), pl.program_id(2)

        @pl.when(j == 0)
        def _():
            pdk_s[...] = jnp.zeros_like(pdk_s)
            pk_s[...] = jnp.zeros_like(pk_s)
            dl_s[...] = jnp.zeros_like(dl_s)

        def step(diag):
            sc = _dot_nt(q_ref[...], k_ref[...]) * _A_SCALE + (cq_ref[:, 0:1] - ck_ref[...])
            if diag:
                sc = jnp.where(_causal(t, t, False), sc, -jnp.inf)
            p = jnp.exp(sc - lse_ref[:, 0:1])
            pdp = p * _dot_nt(do_ref[...], v_ref[...])
            dl_s[...] += jnp.sum(pdp, axis=1, keepdims=True)
            pdk_s[...] += _dot(pdp.astype(BF16), k_ref[...])
            pk_s[...] += _dot(p.astype(BF16), k_ref[...])

        @pl.when(j < i)
        def _():
            step(False)

        @pl.when(j == i)
        def _():
            step(True)
            dq_ref[...] = ((pdk_s[...] - dl_s[...] * pk_s[...]) * _A_SCALE).astype(dq_ref.dtype)
            dl_ref[...] = dl_s[...]

    jc = lambda i, j: jnp.minimum(j, i)
    col = pl.BlockSpec((None, t, LANE), lambda h, i, j: (h, i, 0))
    return pl.pallas_call(
        body, name="fox_dq", grid=(H_A, n, n),
        in_specs=[pl.BlockSpec((t, DH_A), lambda h, i, j: (i, h)),
                  pl.BlockSpec((t, DH_A), lambda h, i, j: (jc(i, j), kb + h)),
                  pl.BlockSpec((t, DH_A), lambda h, i, j: (jc(i, j), vb + h)),
                  pl.BlockSpec((t, DH_A), lambda h, i, j: (i, h)),
                  col,
                  pl.BlockSpec((None, 1, t), lambda h, i, j: (h, 0, jc(i, j))),
                  col],
        out_specs=[pl.BlockSpec((t, DH_A), lambda h, i, j: (i, h)), col],
        out_shape=[jax.ShapeDtypeStruct((s, W_A), BF16), jax.ShapeDtypeStruct((H_A, s, LANE), F32)],
        scratch_shapes=[pltpu.VMEM((t, DH_A), F32), pltpu.VMEM((t, DH_A), F32), pltpu.VMEM((t, LANE), F32)],
        compiler_params=_cparams(("parallel", "parallel", "arbitrary")),
    )(qkv, qkv, qkv, do, c_col, c_row, lse_col)


def _fox_dkv(qkv, do, c_col, c_row, lse_row, delta_row):
    s = qkv.shape[0]
    t = _tile(s, TQ_A)
    n = s // t
    kb, vb = OFF_KA // DH_A, OFF_VA // DH_A

    def body(q_ref, k_ref, v_ref, do_ref, cq_ref, ck_ref, lse_ref, dl_ref, dk_ref, dv_ref, dc_ref,
             dk_s, dv_s, dc_s):
        j, i = pl.program_id(1), pl.program_id(2)

        @pl.when(i == 0)
        def _():
            dk_s[...] = jnp.zeros_like(dk_s)
            dv_s[...] = jnp.zeros_like(dv_s)
            dc_s[...] = jnp.zeros_like(dc_s)

        def step(diag):
            st = _dot_nt(k_ref[...], q_ref[...]) * _A_SCALE + (cq_ref[...] - ck_ref[:, 0:1])
            if diag:
                st = jnp.where(_causal(t, t, True), st, -jnp.inf)
            pt = jnp.exp(st - lse_ref[...])
            dv_s[...] += _dot(pt.astype(BF16), do_ref[...])
            dpt = _dot_nt(v_ref[...], do_ref[...])
            dst = pt * (dpt - dl_ref[...])
            dk_s[...] += _dot(dst.astype(BF16), q_ref[...])
            dc_s[...] -= jnp.sum(dst, axis=1, keepdims=True)

        @pl.when(i == j)
        def _():
            step(True)

        @pl.when(i > j)
        def _():
            step(False)

        @pl.when(i == n - 1)
        def _():
            dk_ref[...] = (dk_s[...] * _A_SCALE).astype(dk_ref.dtype)
            dv_ref[...] = dv_s[...].astype(dv_ref.dtype)
            dc_ref[...] = dc_s[...]

    ic = lambda j, i: jnp.maximum(i, j)
    rowq = pl.BlockSpec((None, 1, t), lambda h, j, i: (h, 0, ic(j, i)))
    kv_out = pl.BlockSpec((t, DH_A), lambda h, j, i: (j, h))
    return pl.pallas_call(
        body, name="fox_dkv", grid=(H_A, n, n),
        in_specs=[pl.BlockSpec((t, DH_A), lambda h, j, i: (ic(j, i), h)),
                  pl.BlockSpec((t, DH_A), lambda h, j, i: (j, kb + h)),
                  pl.BlockSpec((t, DH_A), lambda h, j, i: (j, vb + h)),
                  pl.BlockSpec((t, DH_A), lambda h, j, i: (ic(j, i), h)),
                  rowq,
                  pl.BlockSpec((None, t, LANE), lambda h, j, i: (h, j, 0)),
                  rowq, rowq],
        out_specs=[kv_out, kv_out, pl.BlockSpec((None, t, LANE), lambda h, j, i: (h, j, 0))],
        out_shape=[jax.ShapeDtypeStruct((s, W_A), BF16), jax.ShapeDtypeStruct((s, W_A), BF16),
                   jax.ShapeDtypeStruct((H_A, s, LANE), F32)],
        scratch_shapes=[pltpu.VMEM((t, DH_A), F32), pltpu.VMEM((t, DH_A), F32), pltpu.VMEM((t, LANE), F32)],
        compiler_params=_cparams(("parallel", "parallel", "arbitrary")),
    )(qkv, qkv, qkv, do, c_row, c_col, lse_row, delta_row)


_B_SCALE = DH_B ** -0.5
_HALF = LANE // 2


def _t5_bucket_table():
    ql = np.arange(BLOCK)[:, None]
    kl = np.arange(2 * BLOCK)[None, :]
    dist = np.clip(ql + BLOCK - kl, 0, None)
    max_exact = NUM_BUCKETS // 2
    large = max_exact + (np.log(np.maximum(dist, 1) / max_exact) / np.log(MAX_DISTANCE / max_exact)
                         * (NUM_BUCKETS - max_exact)).astype(np.int64)
    large = np.minimum(large, NUM_BUCKETS - 1)
    return np.where(dist < max_exact, dist, large).astype(np.int32).reshape(1, BLOCK * 2 * BLOCK)


def _one_hot_buckets(bucket_ref, n):
    return (lax.broadcasted_iota(jnp.int32, (NUM_BUCKETS, n), 0) == bucket_ref[...]).astype(F32)


def _bias_table(rel_bias_t, bucket):
    nqk = bucket.shape[1]
    tc = _tile(nqk, 4096)

    def body(rb_ref, bk_ref, o_ref):
        o_ref[...] = jnp.dot(rb_ref[...], _one_hot_buckets(bk_ref, tc), precision=lax.Precision.HIGHEST,
                             preferred_element_type=F32)

    return pl.pallas_call(
        body, name="bias_table", grid=(nqk // tc,),
        in_specs=[pl.BlockSpec((H_B, NUM_BUCKETS), lambda i: (0, 0)), pl.BlockSpec((1, tc), lambda i: (0, i))],
        out_specs=pl.BlockSpec((H_B, tc), lambda i: (0, i)),
        out_shape=jax.ShapeDtypeStruct((H_B, nqk), F32),
        compiler_params=_cparams(("parallel",)),
    )(rel_bias_t, bucket)


def _bias_table_bwd(dbias, bucket):
    nqk = bucket.shape[1]
    tc = _tile(nqk, 4096)
    n = nqk // tc

    def body(db_ref, bk_ref, o_ref):
        i = pl.program_id(0)

        @pl.when(i == 0)
        def _():
            o_ref[...] = jnp.zeros_like(o_ref)
        o_ref[...] += lax.dot_general(db_ref[...], _one_hot_buckets(bk_ref, tc), (((1,), (1,)), ((), ())),
                                      precision=lax.Precision.HIGHEST, preferred_element_type=F32)

    return pl.pallas_call(
        body, name="bias_table_bwd", grid=(n,),
        in_specs=[pl.BlockSpec((H_B, tc), lambda i: (0, i)), pl.BlockSpec((1, tc), lambda i: (0, i))],
        out_specs=pl.BlockSpec((H_B, NUM_BUCKETS), lambda i: (0, 0)),
        out_shape=jax.ShapeDtypeStruct((H_B, NUM_BUCKETS), F32),
        compiler_params=_cparams(("arbitrary",)),
    )(dbias, bucket)


def _lane_lo():
    return lax.broadcasted_iota(jnp.int32, (1, LANE), 1) < _HALF


def _dup_kv_head(cat, hk):
    xcol = cat[:, (hk // 2) * LANE:(hk // 2 + 1) * LANE].astype(F32)
    swapped = pltpu.roll(xcol, _HALF, 1)
    lo = _lane_lo()
    return (jnp.where(lo, xcol, swapped) if hk % 2 == 0 else jnp.where(lo, swapped, xcol)).astype(BF16)


def _band_mask(first_block):
    ql = lax.broadcasted_iota(jnp.int32, (BLOCK, 2 * BLOCK), 0)
    kl = lax.broadcasted_iota(jnp.int32, (BLOCK, 2 * BLOCK), 1)
    dist = ql + BLOCK - kl
    ok = jnp.logical_and(dist >= 0, dist < WINDOW)
    return jnp.logical_and(ok, jnp.logical_or(jnp.logical_not(first_block), kl >= BLOCK))


def _swa_probs(qh, kdup, bias_h, sink, mask):
    sc = _dot_nt(qh, kdup) * _B_SCALE + bias_h
    sc = jnp.where(mask, sc, -jnp.inf)
    m = jnp.maximum(jnp.max(sc, axis=1, keepdims=True), sink)
    p = jnp.exp(sc - m)
    e_sink = jnp.exp(sink - m)
    inv = 1.0 / (jnp.sum(p, axis=1, keepdims=True) + e_sink)
    return p * inv, e_sink * inv


def _split_pair(ref, col):
    x = ref[:, col * LANE:(col + 1) * LANE].astype(F32)
    lo = _lane_lo()
    return jnp.where(lo, x, 0.0).astype(BF16), jnp.where(lo, 0.0, x).astype(BF16)


def _swa_fwd(qkv, bias, sinks):
    s = qkv.shape[0]
    nb = s // BLOCK
    qb, kb, vb = OFF_QB // W_QB, OFF_KB // W_KB, OFF_VB // W_KB
    assert OFF_QB % W_QB == 0 and OFF_KB % W_KB == 0 and OFF_VB % W_KB == 0 and W_KB % LANE == 0 and G_B % 2 == 0

    def body(q_ref, kp_ref, kc_ref, vp_ref, vc_ref, bias_ref, sink_ref, o_ref):
        i = pl.program_id(0)
        mask = _band_mask(i == 0)
        kcat = jnp.concatenate([kp_ref[...], kc_ref[...]], axis=0)
        vcat = jnp.concatenate([vp_ref[...], vc_ref[...]], axis=0)
        lo = _lane_lo()
        for hk in range(HKV_B):
            kdup, vdup = _dup_kv_head(kcat, hk), _dup_kv_head(vcat, hk)
            for pr in range(G_B // 2):
                h0 = hk * G_B + 2 * pr
                q0, q1 = _split_pair(q_ref, h0 // 2)
                p0, _ = _swa_probs(q0, kdup, bias_ref[h0], sink_ref[0, h0], mask)
                p1, _ = _swa_probs(q1, kdup, bias_ref[h0 + 1], sink_ref[0, h0 + 1], mask)
                o0 = _dot(p0.astype(BF16), vdup)
                o1 = _dot(p1.astype(BF16), vdup)
                o_ref[:, (h0 // 2) * LANE:(h0 // 2 + 1) * LANE] = jnp.where(lo, o0, o1).astype(o_ref.dtype)

    prev = lambda i: jnp.maximum(i - 1, 0)
    return pl.pallas_call(
        body, name="swa_fwd", grid=(nb,),
        in_specs=[pl.BlockSpec((BLOCK, W_QB), lambda i: (i, qb)),
                  pl.BlockSpec((BLOCK, W_KB), lambda i: (prev(i), kb)),
                  pl.BlockSpec((BLOCK, W_KB), lambda i: (i, kb)),
                  pl.BlockSpec((BLOCK, W_KB), lambda i: (prev(i), vb)),
                  pl.BlockSpec((BLOCK, W_KB), lambda i: (i, vb)),
                  pl.BlockSpec((H_B, BLOCK, 2 * BLOCK), lambda i: (0, 0, 0)),
                  pl.BlockSpec(memory_space=pltpu.SMEM)],
        out_specs=pl.BlockSpec((BLOCK, W_QB), lambda i: (i, 0)),
        out_shape=jax.ShapeDtypeStruct((s, W_QB), BF16),
        compiler_params=_cparams(("parallel",)),
    )(qkv, qkv, qkv, qkv, qkv, bias, sinks)


def _swa_bwd(qkv, do, bias, sinks):
    s = qkv.shape[0]
    nb = s // BLOCK
    qb, kb, vb = OFF_QB // W_QB, OFF_KB // W_KB, OFF_VB // W_KB

    def body(q_ref, kp_ref, kc_ref, vp_ref, vc_ref, do_ref, bias_ref, sink_ref,
             dq_ref, dk_ref, dv_ref, dbias_ref, dsink_ref, carry_k, carry_v):
        i = pl.program_id(0)
        lo = _lane_lo()

        @pl.when(i == 0)
        def _():
            dbias_ref[...] = jnp.zeros_like(dbias_ref)
            dsink_ref[...] = jnp.zeros_like(dsink_ref)
            carry_k[...] = jnp.zeros_like(carry_k)
            carry_v[...] = jnp.zeros_like(carry_v)

        @pl.when(i < nb)
        def _():
            mask = _band_mask(i == 0)
            kcat = jnp.concatenate([kp_ref[...], kc_ref[...]], axis=0)
            vcat = jnp.concatenate([vp_ref[...], vc_ref[...]], axis=0)
            lane = lax.broadcasted_iota(jnp.int32, (1, LANE), 1)
            dsink = jnp.zeros((1, LANE), F32)
            dk_cols = [jnp.zeros((2 * BLOCK, LANE), F32) for _ in range(W_KB // LANE)]
            dv_cols = [jnp.zeros((2 * BLOCK, LANE), F32) for _ in range(W_KB // LANE)]
            for hk in range(HKV_B):
                kdup, vdup = _dup_kv_head(kcat, hk), _dup_kv_head(vcat, hk)
                dk_acc = jnp.zeros((2 * BLOCK, LANE), F32)
                dv_acc = jnp.zeros((2 * BLOCK, LANE), F32)
                for pr in range(G_B // 2):
                    h0 = hk * G_B + 2 * pr
                    col = h0 // 2
                    qs = _split_pair(q_ref, col)
                    dos = _split_pair(do_ref, col)
                    dqs = []
                    for e in range(2):
                        h = h0 + e
                        p, p_sink = _swa_probs(qs[e], kdup, bias_ref[h], sink_ref[0, h], mask)
                        dp = _dot_nt(dos[e], vdup)
                        delta = jnp.sum(p * dp, axis=1, keepdims=True)
                        ds = p * (dp - delta)
                        dbias_ref[h] += ds
                        dsink = dsink - jnp.where(lane == h, jnp.sum(p_sink * delta, axis=0, keepdims=True), 0.0)
                        dqs.append(_dot(ds.astype(BF16), kdup))
                        dk_acc = dk_acc + _dot(ds.T.astype(BF16), qs[e])
                        dv_acc = dv_acc + _dot(p.T.astype(BF16), dos[e])
                    dq_ref[:, col * LANE:(col + 1) * LANE] = (jnp.where(lo, dqs[0], dqs[1]) * _B_SCALE).astype(dq_ref.dtype)
                dk_tot = (dk_acc + pltpu.roll(dk_acc, _HALF, 1)) * _B_SCALE
                dv_tot = dv_acc + pltpu.roll(dv_acc, _HALF, 1)
                mine = lo if hk % 2 == 0 else jnp.logical_not(lo)
                dk_cols[hk // 2] = jnp.where(mine, dk_tot, dk_cols[hk // 2])
                dv_cols[hk // 2] = jnp.where(mine, dv_tot, dv_cols[hk // 2])
            dsink_ref[...] += dsink
            dk_cat = jnp.concatenate(dk_cols, axis=1)
            dv_cat = jnp.concatenate(dv_cols, axis=1)
            dk_ref[...] = (carry_k[...] + dk_cat[:BLOCK]).astype(dk_ref.dtype)
            dv_ref[...] = (carry_v[...] + dv_cat[:BLOCK]).astype(dv_ref.dtype)
            carry_k[...] = dk_cat[BLOCK:]
            carry_v[...] = dv_cat[BLOCK:]

        @pl.when(i == nb)
        def _():
            dk_ref[...] = carry_k[...].astype(dk_ref.dtype)
            dv_ref[...] = carry_v[...].astype(dv_ref.dtype)

    cur = lambda i: jnp.minimum(i, nb - 1)
    prev = lambda i: jnp.clip(i - 1, 0, nb - 1)
    kv_out = pl.BlockSpec((BLOCK, W_KB), lambda i: (prev(i), 0))
    return pl.pallas_call(
        body, name="swa_bwd", grid=(nb + 1,),
        in_specs=[pl.BlockSpec((BLOCK, W_QB), lambda i: (cur(i), qb)),
                  pl.BlockSpec((BLOCK, W_KB), lambda i: (prev(i), kb)),
                  pl.BlockSpec((BLOCK, W_KB), lambda i: (cur(i), kb)),
                  pl.BlockSpec((BLOCK, W_KB), lambda i: (prev(i), vb)),
                  pl.BlockSpec((BLOCK, W_KB), lambda i: (cur(i), vb)),
                  pl.BlockSpec((BLOCK, W_QB), lambda i: (cur(i), 0)),
                  pl.BlockSpec((H_B, BLOCK, 2 * BLOCK), lambda i: (0, 0, 0)),
                  pl.BlockSpec(memory_space=pltpu.SMEM)],
        out_specs=[pl.BlockSpec((BLOCK, W_QB), lambda i: (cur(i), 0)), kv_out, kv_out,
                   pl.BlockSpec((H_B, BLOCK, 2 * BLOCK), lambda i: (0, 0, 0)),
                   pl.BlockSpec((1, LANE), lambda i: (0, 0))],
        out_shape=[jax.ShapeDtypeStruct((s, W_QB), BF16), jax.ShapeDtypeStruct((s, W_KB), BF16),
                   jax.ShapeDtypeStruct((s, W_KB), BF16),
                   jax.ShapeDtypeStruct((H_B, BLOCK, 2 * BLOCK), F32), jax.ShapeDtypeStruct((1, LANE), F32)],
        scratch_shapes=[pltpu.VMEM((BLOCK, W_KB), F32), pltpu.VMEM((BLOCK, W_KB), F32)],
        compiler_params=_cparams(("arbitrary",)),
    )(qkv, qkv, qkv, qkv, qkv, do, bias, sinks)


_RELS = ((1, 0), (0, 1), (1, 1))
ANY = pl.BlockSpec(memory_space=pl.ANY)


def _place():
    x, y, c = lax.axis_index("x"), lax.axis_index("y"), lax.axis_index("c")
    return x, y, c


def _rel_chip(x, y, rel):
    px = 1 - x if rel[0] else x
    py = 1 - y if rel[1] else y
    return px, py, 2 * px + py


def _half_rows(ref_shape, c):
    half = ref_shape[-2] // 2
    return pl.ds(pl.multiple_of(c * half, 16), half)


def _all_gather(srcs):
    nt = len(srcs)

    def body(*refs):
        src, dst = refs[:nt], refs[nt:2 * nt]
        send, recv, lsem = refs[2 * nt:]
        x, y, c = _place()
        me_chip = 2 * x + y
        sib = (x, y, 1 - c)

        def slots(t, chip):
            ns = src[t].shape[0]
            return pl.ds(chip * ns, ns)

        def rcopy(t, k, chip, rows, to, from_src):
            d = dst[t].at[slots(t, chip), rows]
            return pltpu.make_async_remote_copy(
                src_ref=src[t].at[:, rows] if from_src else d, dst_ref=d,
                send_sem=send.at[t, k], recv_sem=recv.at[t, k], device_id=to, device_id_type=MESH)

        local = [pltpu.make_async_copy(src[t], dst[t].at[slots(t, me_chip)], lsem.at[t]) for t in range(nt)]
        for cp in local:
            cp.start()
        first = []
        for t in range(nt):
            mine = _half_rows(src[t].shape, c)
            for k, rel in enumerate(_RELS):
                px, py, _ = _rel_chip(x, y, rel)
                first.append(rcopy(t, k, me_chip, mine, (px, py, c), True))
        for cp in first:
            cp.start()
        passed = []
        for t in range(nt):
            mine = _half_rows(src[t].shape, c)
            for k, rel in enumerate(_RELS):
                _, _, chip = _rel_chip(x, y, rel)
                rcopy(t, k, chip, mine, sib, False).wait_recv()
                fwd = rcopy(t, 3 + k, chip, mine, sib, False)
                fwd.start()
                passed.append(fwd)
        for t in range(nt):
            theirs = _half_rows(src[t].shape, 1 - c)
            for k, rel in enumerate(_RELS):
                _, _, chip = _rel_chip(x, y, rel)
                rcopy(t, 3 + k, chip, theirs, sib, False).wait_recv()
        for cp in first + passed:
            cp.wait_send()
        for cp in local:
            cp.wait()

    return pl.pallas_call(
        body, name="all_gather_weights",
        in_specs=[ANY] * nt, out_specs=[ANY] * nt,
        out_shape=[jax.ShapeDtypeStruct((N_CHIP * a.shape[0],) + a.shape[1:], a.dtype) for a in srcs],
        scratch_shapes=[pltpu.SemaphoreType.DMA((nt, 6)), pltpu.SemaphoreType.DMA((nt, 6)),
                        pltpu.SemaphoreType.DMA((nt,))],
    )(*srcs)


def _pair_exchange(grads):
    nt = len(grads)

    def body(*refs):
        g, got = refs[:nt], refs[nt:2 * nt]
        send, recv = refs[2 * nt:]
        x, y, c = _place()
        cps = []
        for t in range(nt):
            theirs = _half_rows(g[t].shape, 1 - c)
            cps.append(pltpu.make_async_remote_copy(
                src_ref=g[t].at[:, theirs], dst_ref=got[t], send_sem=send.at[t], recv_sem=recv.at[t],
                device_id=(x, y, 1 - c), device_id_type=MESH))
        for cp in cps:
            cp.start()
        for cp in cps:
            cp.wait()

    return pl.pallas_call(
        body, name="grad_pair_exchange", in_specs=[ANY] * nt, out_specs=[ANY] * nt,
        out_shape=[jax.ShapeDtypeStruct((a.shape[0], a.shape[1] // 2, a.shape[2]), a.dtype) for a in grads],
        scratch_shapes=[pltpu.SemaphoreType.DMA((nt,)), pltpu.SemaphoreType.DMA((nt,))],
    )(*grads)


def _ew_rows(rows, cols, itemsize=4):
    tr = 16
    while tr * 2 <= 128 and rows % (tr * 2) == 0 and tr * 2 * cols * itemsize <= (1 << 20):
        tr *= 2
    assert rows % tr == 0, (rows, tr)
    return tr


def _pair_sum(name, grad, got, place):
    ns, rows, cols = grad.shape
    half = rows // 2
    tr = _ew_rows(half, cols)
    nh = half // tr

    def body(p_ref, g_ref, r_ref, o_ref):
        o_ref[...] = (g_ref[...] + r_ref[...]).astype(o_ref.dtype)

    return pl.pallas_call(
        body, name=name,
        grid_spec=pltpu.PrefetchScalarGridSpec(
            num_scalar_prefetch=1, grid=(ns, nh),
            in_specs=[pl.BlockSpec((None, tr, cols), lambda s, i, p: (s, p[1] * nh + i, 0)),
                      pl.BlockSpec((None, tr, cols), lambda s, i, p: (s, i, 0))],
            out_specs=pl.BlockSpec((None, tr, cols), lambda s, i, p: (s, i, 0))),
        out_shape=jax.ShapeDtypeStruct((ns, half, cols), BF16),
        compiler_params=_cparams(("parallel", "parallel")),
    )(place, grad, got)


def _chip_exchange(psums):
    nt = len(psums)

    def body(*refs):
        p, got = refs[:nt], refs[nt:2 * nt]
        send, recv = refs[2 * nt:]
        x, y, c = _place()
        cps = []
        for t in range(nt):
            ns = p[t].shape[0] // N_CHIP
            for k, rel in enumerate(_RELS):
                px, py, chip = _rel_chip(x, y, rel)
                cps.append(pltpu.make_async_remote_copy(
                    src_ref=p[t].at[pl.ds(chip * ns, ns)], dst_ref=got[t].at[k],
                    send_sem=send.at[t, k], recv_sem=recv.at[t, k], device_id=(px, py, c), device_id_type=MESH))
        for cp in cps:
            cp.start()
        for cp in cps:
            cp.wait()

    return pl.pallas_call(
        body, name="grad_chip_exchange", in_specs=[ANY] * nt, out_specs=[ANY] * nt,
        out_shape=[jax.ShapeDtypeStruct((3, a.shape[0] // N_CHIP) + a.shape[1:], a.dtype) for a in psums],
        scratch_shapes=[pltpu.SemaphoreType.DMA((nt, 3)), pltpu.SemaphoreType.DMA((nt, 3))],
    )(*psums)


def _chip_sum(name, psum, got, place):
    ns4, half, cols = psum.shape
    ns = ns4 // N_CHIP
    tr = _ew_rows(half, cols)
    nh = half // tr

    def body(p_ref, mine_ref, got_ref, o_ref):
        acc = mine_ref[...].astype(F32)
        for k in range(3):
            acc = acc + got_ref[k].astype(F32)
        o_ref[...] = acc

    return pl.pallas_call(
        body, name=name,
        grid_spec=pltpu.PrefetchScalarGridSpec(
            num_scalar_prefetch=1, grid=(ns, nh),
            in_specs=[pl.BlockSpec((None, tr, cols), lambda s, i, p: (p[0] * ns + s, i, 0)),
                      pl.BlockSpec((3, None, tr, cols), lambda s, i, p: (0, s, i, 0))],
            out_specs=pl.BlockSpec((None, tr, cols), lambda s, i, p: (s, p[1] * nh + i, 0))),
        out_shape=jax.ShapeDtypeStruct((ns, 2 * half, cols), F32),
        compiler_params=_cparams(("parallel", "parallel")),
    )(place, psum, got)


def _pair_share(halves):
    nt = len(halves)

    def body(*refs):
        full = refs[nt:2 * nt]
        send, recv = refs[2 * nt:]
        x, y, c = _place()
        cps = []
        for t in range(nt):
            mine = _half_rows(full[t].shape, c)
            cps.append(pltpu.make_async_remote_copy(
                src_ref=full[t].at[:, mine], dst_ref=full[t].at[:, mine], send_sem=send.at[t], recv_sem=recv.at[t],
                device_id=(x, y, 1 - c), device_id_type=MESH))
        for cp in cps:
            cp.start()
        for t, cp in enumerate(cps):
            cp.wait_send()
            theirs = _half_rows(full[t].shape, 1 - c)
            pltpu.make_async_remote_copy(
                src_ref=full[t].at[:, theirs], dst_ref=full[t].at[:, theirs], send_sem=send.at[t], recv_sem=recv.at[t],
                device_id=(x, y, 1 - c), device_id_type=MESH).wait_recv()

    return pl.pallas_call(
        body, name="grad_pair_share", in_specs=[ANY] * nt, out_specs=[ANY] * nt,
        out_shape=[jax.ShapeDtypeStruct(a.shape, a.dtype) for a in halves],
        input_output_aliases={t: t for t in range(nt)},
        scratch_shapes=[pltpu.SemaphoreType.DMA((nt,)), pltpu.SemaphoreType.DMA((nt,))],
    )(*halves)


def _reduce_scatter(names, grads, place):
    got = _pair_exchange(grads)
    psums = [_pair_sum("pair_sum_" + n, g, r, place) for n, g, r in zip(names, grads, got)]
    got2 = _chip_exchange(psums)
    halves = [_chip_sum("chip_sum_" + n, p, r, place) for n, p, r in zip(names, psums, got2)]
    return _pair_share(halves)


def _small_all_reduce(pack):
    rows, d = pack.shape

    def body(x_ref, o_ref, land, send, recv):
        x, y, c = _place()
        me = 4 * x + 2 * y + c
        land[me] = x_ref[...]
        cps = []
        for k in range(1, 8):
            to = (1 - x if k & 4 else x, 1 - y if k & 2 else y, 1 - c if k & 1 else c)
            cps.append(pltpu.make_async_remote_copy(
                src_ref=x_ref, dst_ref=land.at[me], send_sem=send.at[k - 1], recv_sem=recv.at[k - 1],
                device_id=to, device_id_type=MESH))
        for cp in cps:
            cp.start()
        for cp in cps:
            cp.wait()
        acc = land[0]
        for dev in range(1, 8):
            acc = acc + land[dev]
        o_ref[...] = acc

    vm = pl.BlockSpec(memory_space=pltpu.VMEM)
    return pl.pallas_call(
        body, name="small_all_reduce", in_specs=[vm], out_specs=vm,
        out_shape=jax.ShapeDtypeStruct((rows, d), F32),
        scratch_shapes=[pltpu.VMEM((8, rows, d), F32), pltpu.SemaphoreType.DMA((7,)), pltpu.SemaphoreType.DMA((7,))],
    )(pack)


def _adamw(name, w, g, m, v):
    lead, rows, cols = w.shape
    tr = 8
    while tr * 2 <= 256 and rows % (tr * 2) == 0 and tr * 2 * cols * 4 <= (1 << 20):
        tr *= 2
    if rows % tr:
        tr = rows

    def body(w_ref, g_ref, m_ref, v_ref, d_ref, nm_ref, nv_ref):
        gv = g_ref[...]
        nm = ADAM_B1 * m_ref[...] + (1.0 - ADAM_B1) * gv
        nv = ADAM_B2 * v_ref[...] + (1.0 - ADAM_B2) * (gv * gv)
        m_hat = nm / (1.0 - ADAM_B1 ** ADAM_STEP)
        v_hat = nv / (1.0 - ADAM_B2 ** ADAM_STEP)
        d_ref[...] = -ADAM_LR * (m_hat / (jnp.sqrt(v_hat) + ADAM_EPS) + ADAM_WD * w_ref[...])
        nm_ref[...] = nm
        nv_ref[...] = nv

    blk = pl.BlockSpec((None, tr, cols), lambda l, i: (l, i, 0))
    out = jax.ShapeDtypeStruct(w.shape, F32)
    return pl.pallas_call(
        body, name=name, grid=(lead, rows // tr), in_specs=[blk] * 4, out_specs=[blk] * 3,
        out_shape=[out, out, out], compiler_params=_cparams(("parallel", "parallel")),
    )(w, g, m, v)


def _pad_to(a, shape):
    return jnp.pad(a, [(0, t - s) for s, t in zip(a.shape, shape)])


def _pack_small(n1, n2, fg, bf, sk, rb, extra=None):
    rows = [n1.reshape(1, D_MODEL), n2.reshape(1, D_MODEL), fg.reshape(1, D_MODEL),
            _pad_to(bf.reshape(1, H_A), (1, D_MODEL)), _pad_to(sk.reshape(1, H_B), (1, D_MODEL)),
            jnp.zeros((1, D_MODEL), F32) if extra is None else _pad_to(extra.reshape(1, 1), (1, D_MODEL)),
            _pad_to(rb.reshape(1, NUM_BUCKETS * H_B), (1, RB_ROWS * D_MODEL)).reshape(RB_ROWS, D_MODEL)]
    return _pad_to(jnp.concatenate(rows, axis=0), (PACK_ROWS, D_MODEL))


def _unpack_small(p):
    return (p[0:1], p[1:2], p[2], p[3:4, :H_A], p[4:5, :H_B],
            p[6:6 + RB_ROWS].reshape(-1)[:NUM_BUCKETS * H_B].reshape(NUM_BUCKETS, H_B))


def kernel(x, norm1_g, w_in, b_forget, attn_sinks, rel_bias, w_branch_a, w_branch_b, w_out, norm2_g, w_ffn_gate, w_ffn_up, w_ffn_down, final_g, loss_target, m_norm1_g, m_w_in, m_b_forget, m_attn_sinks, m_rel_bias, m_w_branch_a, m_w_branch_b, m_w_out, m_norm2_g, m_w_ffn_gate, m_w_ffn_up, m_w_ffn_down, m_final_g, v_norm1_g, v_w_in, v_b_forget, v_attn_sinks, v_rel_bias, v_w_branch_a, v_w_branch_b, v_w_out, v_norm2_g, v_w_ffn_gate, v_w_ffn_up, v_w_ffn_down, v_final_g):
    s, d = SEQ, D_MODEL
    assert x.shape == (1, s, d) and w_in.shape == (1, d, W_IN_SH)
    xs = x[0]
    place = jnp.stack([2 * lax.axis_index("x") + lax.axis_index("y"), lax.axis_index("c")]).astype(jnp.int32)

    w_gu_l = jnp.stack([_pad_to(w_ffn_gate[0], (d, FF_P)), _pad_to(w_ffn_up[0], (d, FF_P))]).astype(BF16)
    w_dn_l = _pad_to(w_ffn_down, (1, FF_P, d)).astype(BF16)
    w_in4, w_a, w_b, w_o, w_gu, w_dn = _all_gather(
        [w_in.astype(BF16), w_branch_a.astype(BF16), w_branch_b.astype(BF16), w_out.astype(BF16), w_gu_l, w_dn_l])
    w_full = jnp.concatenate([w_in4[j] for j in range(N_CHIP)], axis=1)
    n_qkv_a = 3 * W_A
    w_p = jnp.concatenate([w_full[:, :n_qkv_a], w_full[:, n_qkv_a + H_A:], w_full[:, n_qkv_a:n_qkv_a + H_A],
                           jnp.zeros((d, LANE - H_A), BF16)], axis=1)

    h1 = _rms_fwd("norm1_fwd", xs, norm1_g)
    qkv = _matmul("proj_qkv", h1, w_p, m=s, n=W_QKV, k=d, out_shape=(s, W_QKV), out_dtype=BF16)
    proj_g = _matmul("proj_gates", h1, w_p, m=s, n=2 * d, k=d, b_noff=OFF_GA, out_shape=(s, 2 * d), out_dtype=F32)
    fa = _matmul("proj_forget", h1, w_p, m=s, n=LANE, k=d, b_noff=OFF_FA, tn=LANE, out_shape=(s, LANE), out_dtype=F32)
    b_f = _pad_to(b_forget, (1, LANE))
    c_cum = _forget_fwd(fa, b_f)
    c_t = c_cum[:, :H_A].T
    c_col = jnp.broadcast_to(c_t[:, :, None], (H_A, s, LANE))
    c_row = c_t[:, None, :]
    attn_a, lse_col = _fox_fwd(qkv, c_col, c_row)

    bucket = jnp.asarray(_t5_bucket_table())
    bias = _bias_table(rel_bias.T, bucket).reshape(H_B, BLOCK, 2 * BLOCK)
    attn_b = _swa_fwd(qkv, bias, attn_sinks)

    ya = _matmul("branch_a", attn_a, w_a, m=s, n=d, k=W_A, b_kind="col", tn=_tile(A_SH, TN),
                 out_shape=(s, d), out_dtype=F32)
    yb = _matmul("branch_b", attn_b, w_b, m=s, n=d, k=W_QB, b_kind="col", tn=_tile(A_SH, TN),
                 out_shape=(s, d), out_dtype=F32)
    mixed = _gate_fwd(ya, yb, proj_g)
    x1 = _matmul("out_proj", mixed, w_o, m=s, n=d, k=d, b_kind="row", tk=_tile(A_SH, TK),
                 res=xs, out_shape=(s, d), out_dtype=F32)

    h2 = _rms_fwd("norm2_fwd", x1, norm2_g)
    gu = _matmul("ffn_gate_up", h2, w_gu, m=s, n=2 * FP, k=d, b_kind="col", tn=_tile(FF_P, 1408),
                 out_shape=(s, 2 * FP), out_dtype=BF16)
    hidden = _swiglu_fwd(gu)
    x2 = _matmul("ffn_down", hidden, w_dn, m=s, n=d, k=FP, b_kind="row", tk=_tile(FF_P, TK),
                 res=x1, out_shape=(s, d), out_dtype=F32)

    dx2, dx2_b, d_fg, loss_tile = _loss_head(x2, loss_target[0], final_g.reshape(1, d))
    dhidden = _matmul("ffn_down_dx", dx2_b, w_dn, m=s, n=FP, k=d, nt=True, b_kind="row", tn=_tile(FF_P, 1408),
                      out_shape=(s, FP), out_dtype=BF16)
    dgu = _swiglu_bwd(dhidden, gu)
    g_dn = _matmul("ffn_down_dw", hidden.T, dx2_b, m=FP, n=d, k=s, o_kind="row", tm=_tile(FF_P, 1408, 16),
                   out_shape=(N_CHIP, FF_P, d), out_dtype=F32)
    g_gu = _matmul("ffn_gate_up_dw", h2.T, dgu, m=d, n=2 * FP, k=s, o_kind="col", tn=_tile(FF_P, 1408),
                   out_shape=(2 * N_CHIP, d, FF_P), out_dtype=F32)
    dh2 = _matmul("ffn_gate_up_dx", dgu, w_gu, m=s, n=d, k=2 * FP, nt=True, b_kind="col", tk=_tile(FF_P, TK),
                  out_shape=(s, d), out_dtype=F32)
    dx1, dx1_b, d_n2 = _rms_bwd("norm2_bwd", dh2, x1, norm2_g, dx2)

    dmixed = _matmul("out_proj_dx", dx1_b, w_o, m=s, n=d, k=d, nt=True, b_kind="row", tn=_tile(A_SH, TN),
                     out_shape=(s, d), out_dtype=BF16)
    g_o = _matmul("out_proj_dw", mixed.T, dx1_b, m=d, n=d, k=s, o_kind="row", tm=_tile(A_SH, TM, 16),
                  out_shape=(N_CHIP, A_SH, d), out_dtype=F32)
    dya, dyb, dga, dgb = _gate_bwd(dmixed, ya, yb, proj_g)
    g_a = _matmul("branch_a_dw", attn_a.T, dya, m=W_A, n=d, k=s, o_kind="col", tn=_tile(A_SH, TN),
                  out_shape=(N_CHIP, W_A, A_SH), out_dtype=F32)
    g_b = _matmul("branch_b_dw", attn_b.T, dyb, m=W_QB, n=d, k=s, o_kind="col", tn=_tile(A_SH, TN),
                  out_shape=(N_CHIP, W_QB, A_SH), out_dtype=F32)
    dattn_a = _matmul("branch_a_dx", dya, w_a, m=s, n=W_A, k=d, nt=True, b_kind="col", tk=_tile(A_SH, TK),
                      out_shape=(s, W_A), out_dtype=BF16)
    dattn_b = _matmul("branch_b_dx", dyb, w_b, m=s, n=W_QB, k=d, nt=True, b_kind="col", tk=_tile(A_SH, TK),
                      out_shape=(s, W_QB), out_dtype=BF16)

    dq_a, delta_col = _fox_dq(qkv, dattn_a, c_col, c_row, lse_col)
    lse_row = lse_col[:, :, 0][:, None, :]
    delta_row = delta_col[:, :, 0][:, None, :]
    dk_a, dv_a, dc_col = _fox_dkv(qkv, dattn_a, c_col, c_row, lse_row, delta_row)
    dc = _pad_to(dc_col[:, :, 0].T, (s, LANE))
    df, d_bf = _forget_bwd(dc, fa, b_f)

    dq_b, dk_b, dv_b, dbias, d_sk = _swa_bwd(qkv, dattn_b, bias, attn_sinks)
    d_rb = _bias_table_bwd(dbias.reshape(H_B, BLOCK * 2 * BLOCK), bucket).T

    dproj = jnp.concatenate([dq_a, dk_a, dv_a, dq_b, dk_b, dv_b, dga, dgb, df], axis=1)
    g_in_p = _matmul("in_proj_dw", h1.T, dproj, m=d, n=PW, k=s, tn=_tile(PW, 1024), out_shape=(d, PW), out_dtype=F32)
    dh1 = _matmul("in_proj_dx", dproj, w_p, m=s, n=d, k=PW, nt=True, tk=_tile(PW, 2560),
                  out_shape=(s, d), out_dtype=F32)
    grad_x, _, d_n1 = _rms_bwd("norm1_bwd", dh1, xs, norm1_g, dx1)

    g_in_full = jnp.concatenate([g_in_p[:, :n_qkv_a], g_in_p[:, OFF_FA:OFF_FA + H_A], g_in_p[:, n_qkv_a:OFF_FA]], axis=1)
    g_in = jnp.stack([g_in_full[:, j * W_IN_SH:(j + 1) * W_IN_SH] for j in range(N_CHIP)])
    r_in, r_a, r_b, r_o, r_gu, r_dn = _reduce_scatter(
        ["w_in", "w_branch_a", "w_branch_b", "w_out", "w_ffn_gate_up", "w_ffn_down"],
        [g_in, g_a, g_b, g_o, g_gu, g_dn], place)
    small = _small_all_reduce(_pack_small(d_n1, d_n2, d_fg, d_bf[:, :H_A], d_sk[:, :H_B], d_rb, loss_tile[0:1, 0:1]))
    loss = small[5, 0]

    grads = {
        "w_in": r_in, "w_branch_a": r_a, "w_branch_b": r_b, "w_out": r_o,
        "w_ffn_gate": r_gu[0:1, :, :FF_SH], "w_ffn_up": r_gu[1:2, :, :FF_SH], "w_ffn_down": r_dn[:, :FF_SH, :],
    }
    given = dict(w_in=(w_in, m_w_in, v_w_in), w_branch_a=(w_branch_a, m_w_branch_a, v_w_branch_a),
                 w_branch_b=(w_branch_b, m_w_branch_b, v_w_branch_b), w_out=(w_out, m_w_out, v_w_out),
                 w_ffn_gate=(w_ffn_gate, m_w_ffn_gate, v_w_ffn_gate), w_ffn_up=(w_ffn_up, m_w_ffn_up, v_w_ffn_up),
                 w_ffn_down=(w_ffn_down, m_w_ffn_down, v_w_ffn_down))
    upd = {n: _adamw("adamw_" + n, given[n][0], grads[n], given[n][1], given[n][2]) for n in grads}
    sm = _adamw("adamw_small",
                _pack_small(norm1_g, norm2_g, final_g, b_forget, attn_sinks, rel_bias)[None],
                small.at[5].set(0.0)[None],
                _pack_small(m_norm1_g, m_norm2_g, m_final_g, m_b_forget, m_attn_sinks, m_rel_bias)[None],
                _pack_small(v_norm1_g, v_norm2_g, v_final_g, v_b_forget, v_attn_sinks, v_rel_bias)[None])
    g_small = _unpack_small(small)
    d_small, m_small, v_small = (_unpack_small(t[0]) for t in sm)

    order = ["norm1_g", "w_in", "b_forget", "attn_sinks", "rel_bias", "w_branch_a", "w_branch_b", "w_out",
             "norm2_g", "w_ffn_gate", "w_ffn_up", "w_ffn_down", "final_g"]
    small_at = {"norm1_g": 0, "norm2_g": 1, "final_g": 2, "b_forget": 3, "attn_sinks": 4, "rel_bias": 5}

    def pick(big_idx, small_src, n):
        return small_src[small_at[n]] if n in small_at else (grads[n] if big_idx is None else upd[n][big_idx])

    return (loss, grad_x[None],
            *[pick(None, g_small, n) for n in order], *[pick(0, d_small, n) for n in order],
            *[pick(1, m_small, n) for n in order], *[pick(2, v_small, n) for n in order])
```

```python
import functools
import math

import numpy as np
import jax
import jax.numpy as jnp
from jax import lax
from jax.experimental import pallas as pl
from jax.experimental.pallas import tpu as pltpu

F32 = jnp.float32
BF16 = jnp.bfloat16
MESH = pl.DeviceIdType.MESH

D_MODEL = 4096
SEQ = 4096
H_A = 16
DH_A = 128
H_B = 32
HKV_B = 4
G_B = H_B // HKV_B
DH_B = 64
WINDOW = 128
NUM_BUCKETS = 32
MAX_DISTANCE = 128
BLOCK = 128
D_FF = ((8 * D_MODEL // 3 + 255) // 256) * 256
EPS = 1e-6
ADAM_LR = 0.001
ADAM_B1 = 0.9
ADAM_B2 = 0.999
ADAM_EPS = 1e-08
ADAM_WD = 0.01
ADAM_STEP = 10

N_CHIP = 4
LANE = 128
VMEM_LIMIT = 56 * 1024 * 1024

TM = 1024
TN = 512
TK = 4096
TQ_A = 512
TR_EW = 256

W_A = H_A * DH_A
W_QB = H_B * DH_B
W_KB = HKV_B * DH_B
OFF_QA = 0
OFF_KA = W_A
OFF_VA = 2 * W_A
OFF_QB = 3 * W_A
OFF_KB = OFF_QB + W_QB
OFF_VB = OFF_KB + W_KB
OFF_GA = OFF_VB + W_KB
OFF_GB = OFF_GA + D_MODEL
OFF_FA = OFF_GB + D_MODEL
PW = OFF_FA + LANE
W_QKV = OFF_GA
W_IN = 3 * W_A + H_A + W_QB + 2 * W_KB + 2 * D_MODEL
W_IN_SH = W_IN // N_CHIP
A_SH = D_MODEL // N_CHIP
FF_SH = D_FF // N_CHIP
FF_P = -(-FF_SH // LANE) * LANE
FP = N_CHIP * FF_P
PACK_ROWS = 16
RB_ROWS = -(-(NUM_BUCKETS * H_B) // D_MODEL)


def _tile(n, target, mult=LANE):
    t = min(target, n) // mult * mult
    while t > mult and n % t:
        t -= mult
    assert t > 0 and n % t == 0, (n, target, mult)
    return t


def _cparams(sem):
    return pltpu.CompilerParams(dimension_semantics=sem, vmem_limit_bytes=VMEM_LIMIT)


def _sigmoid(x):
    return 1.0 / (1.0 + jnp.exp(-x))


def _dot(a, b):
    return jnp.dot(a, b, preferred_element_type=F32)


def _dot_nt(a, b):
    return lax.dot_general(a, b, (((1,), (1,)), ((), ())), preferred_element_type=F32)


ANY = pl.BlockSpec(memory_space=pl.ANY)


class _Job:
    def __init__(self, ins, outs, sems, start, wait, alias=None):
        self.ins, self.outs, self.sems = list(ins), list(outs), list(sems)
        self.start, self.wait, self.alias = start, wait, dict(alias or {})
        self.out = None


def _split(seq, sizes):
    parts, p = [], 0
    for n in sizes:
        parts.append(seq[p:p + n])
        p += n
    return parts


def _pcall(body, args, jobs, *, name, grid, in_specs, out_specs, out_shape, scratch_shapes=(), sem):
    if not jobs:
        return pl.pallas_call(body, name=name, grid=grid, in_specs=in_specs, out_specs=out_specs, out_shape=out_shape,
                              scratch_shapes=list(scratch_shapes), compiler_params=_cparams(sem))(*args)
    single = not isinstance(out_shape, (list, tuple))
    out_specs_l = [out_specs] if single else list(out_specs)
    out_shape_l = [out_shape] if single else list(out_shape)
    cin = [a for j in jobs for a in j.ins]
    cout = [o for j in jobs for o in j.outs]
    csem = [s for j in jobs for s in j.sems]
    sizes = [len(args), len(cin), len(out_shape_l), len(cout), len(scratch_shapes), len(csem)]
    aliases, io, oo = {}, len(args), len(out_shape_l)
    for j in jobs:
        for a, b in j.alias.items():
            aliases[io + a] = oo + b
        io, oo = io + len(j.ins), oo + len(j.outs)

    def wrapped(*refs):
        ins, cins, outs, couts, scr, sems = _split(refs, sizes)
        ids = [pl.program_id(a) for a in range(len(grid))]
        first = functools.reduce(jnp.logical_and, [i == 0 for i in ids])
        last = functools.reduce(jnp.logical_and, [i == g - 1 for i, g in zip(ids, grid)])
        per_job = list(zip(jobs, _split(cins, [len(j.ins) for j in jobs]), _split(couts, [len(j.outs) for j in jobs]),
                           _split(sems, [len(j.sems) for j in jobs])))

        @pl.when(first)
        def _():
            for j, ji, jo, js in per_job:
                j.start(ji, jo, js)

        body(*ins, *outs, *scr)

        @pl.when(last)
        def _():
            for j, ji, jo, js in per_job:
                j.wait(ji, jo, js)

    res = pl.pallas_call(
        wrapped, name=name, grid=grid, in_specs=list(in_specs) + [ANY] * len(cin),
        out_specs=out_specs_l + [ANY] * len(cout), out_shape=out_shape_l + cout,
        scratch_shapes=list(scratch_shapes) + csem, input_output_aliases=aliases,
        compiler_params=_cparams(("arbitrary",) * len(grid)))(*args, *cin)
    main, rest = res[:len(out_shape_l)], res[len(out_shape_l):]
    for j, o in zip(jobs, _split(rest, [len(j.outs) for j in jobs])):
        j.out = list(o)
    return main[0] if single else list(main)


def _comm_now(name, jobs):
    cin = [a for j in jobs for a in j.ins]
    cout = [o for j in jobs for o in j.outs]
    csem = [s for j in jobs for s in j.sems]
    sizes = [len(cin), len(cout), len(csem)]
    aliases, io, oo = {}, 0, 0
    for j in jobs:
        for a, b in j.alias.items():
            aliases[io + a] = oo + b
        io, oo = io + len(j.ins), oo + len(j.outs)

    def body(*refs):
        cins, couts, sems = _split(refs, sizes)
        per_job = list(zip(jobs, _split(cins, [len(j.ins) for j in jobs]), _split(couts, [len(j.outs) for j in jobs]),
                           _split(sems, [len(j.sems) for j in jobs])))
        for j, ji, jo, js in per_job:
            j.start(ji, jo, js)
        for j, ji, jo, js in per_job:
            j.wait(ji, jo, js)

    res = pl.pallas_call(body, name=name, in_specs=[ANY] * len(cin), out_specs=[ANY] * len(cout), out_shape=cout,
                         scratch_shapes=csem, input_output_aliases=aliases)(*cin)
    for j, o in zip(jobs, _split(res, [len(j.outs) for j in jobs])):
        j.out = list(o)


def _slot_spec(shape, kind, br, bc, rc):
    if kind == "2d":
        return pl.BlockSpec((br, bc), lambda *g: rc(*g))
    if kind == "col":
        assert shape[2] % bc == 0, (shape, bc)
        per = shape[2] // bc

        def im_col(*g):
            rb, cb = rc(*g)
            return (cb // per, rb, cb % per)
        return pl.BlockSpec((None, br, bc), im_col)
    assert kind == "row" and shape[1] % br == 0, (shape, kind, br)
    per = shape[1] // br

    def im_row(*g):
        rb, cb = rc(*g)
        return (rb // per, rb % per, cb)
    return pl.BlockSpec((None, br, bc), im_row)


def _matmul(name, a, b, *, m, n, k, nt=False, b_kind="2d", o_kind="2d", out_shape, out_dtype,
            tm=None, tn=None, tk=None, b_noff=0, res=None, b_outer=False, jobs=()):
    tm = tm or _tile(m, TM, 16)
    tn = tn or _tile(n, TN)
    tk = tk or _tile(k, TK)
    assert m % tm == 0 and n % tn == 0 and k % tk == 0 and b_noff % tn == 0
    nk = k // tk
    noff = b_noff // tn
    if b_outer:
        grid = (n // tn, m // tm, nk)
        ij = lambda g0, g1: (g1, g0)
    else:
        grid = (m // tm, n // tn, nk)
        ij = lambda g0, g1: (g0, g1)

    a_spec = pl.BlockSpec((tm, tk), lambda g0, g1, kk: (ij(g0, g1)[0], kk))
    if nt:
        b_spec = _slot_spec(b.shape, b_kind, tn, tk, lambda g0, g1, kk: (ij(g0, g1)[1] + noff, kk))
    else:
        b_spec = _slot_spec(b.shape, b_kind, tk, tn, lambda g0, g1, kk: (kk, ij(g0, g1)[1] + noff))
    o_spec = _slot_spec(out_shape, o_kind, tm, tn, lambda g0, g1, kk: ij(g0, g1))
    in_specs = [a_spec, b_spec]
    args = [a, b]
    if res is not None:
        in_specs.append(pl.BlockSpec((tm, tn), lambda g0, g1, kk: ij(g0, g1)))
        args.append(res)

    def body(*refs):
        a_ref, b_ref = refs[0], refs[1]
        r_ref = refs[2] if res is not None else None
        o_ref = refs[3] if res is not None else refs[2]

        def prod():
            return _dot_nt(a_ref[...], b_ref[...]) if nt else _dot(a_ref[...], b_ref[...])

        def finish(acc):
            if r_ref is not None:
                acc = acc + r_ref[...].astype(F32)
            o_ref[...] = acc.astype(o_ref.dtype)

        if nk == 1:
            finish(prod())
            return
        acc_ref = refs[-1]
        kk = pl.program_id(2)

        @pl.when(kk == 0)
        def _():
            acc_ref[...] = prod()

        @pl.when(jnp.logical_and(kk > 0, kk < nk - 1))
        def _():
            acc_ref[...] += prod()

        @pl.when(kk == nk - 1)
        def _():
            finish(acc_ref[...] + prod())

    return _pcall(
        body, args, jobs, name=name, grid=grid, in_specs=in_specs, out_specs=o_spec,
        out_shape=jax.ShapeDtypeStruct(out_shape, out_dtype),
        scratch_shapes=[pltpu.VMEM((tm, tn), F32)] if nk > 1 else [],
        sem=("parallel", "parallel", "arbitrary"))


def _rms_fwd(name, x, g):
    s, d = x.shape
    tr = _tile(s, TR_EW, 16)

    def body(x_ref, g_ref, h_ref):
        xf = x_ref[...]
        rstd = lax.rsqrt(jnp.mean(xf * xf, axis=-1, keepdims=True) + EPS)
        h_ref[...] = (xf * rstd * g_ref[...]).astype(h_ref.dtype)

    return pl.pallas_call(
        body, name=name, grid=(s // tr,),
        in_specs=[pl.BlockSpec((tr, d), lambda i: (i, 0)), pl.BlockSpec((1, d), lambda i: (0, 0))],
        out_specs=pl.BlockSpec((tr, d), lambda i: (i, 0)),
        out_shape=jax.ShapeDtypeStruct((s, d), BF16),
        compiler_params=_cparams(("parallel",)),
    )(x, g)


def _rms_bwd_rows(dh, xf, g):
    rstd = lax.rsqrt(jnp.mean(xf * xf, axis=-1, keepdims=True) + EPS)
    xhat = xf * rstd
    dxhat = dh * g
    dx = rstd * (dxhat - xhat * jnp.mean(dxhat * xhat, axis=-1, keepdims=True))
    return dx, dh * xhat


def _fold8(v):
    return jnp.sum(v.reshape(v.shape[0] // 8, 8, v.shape[1]), axis=0)


def _rms_bwd(name, dh, x, g, dres):
    s, d = x.shape
    tr = _tile(s, TR_EW, 16)
    n = s // tr

    def body(dh_ref, x_ref, g_ref, r_ref, dx_ref, dxb_ref, dg_ref, acc_ref):
        i = pl.program_id(0)
        dx, dgrows = _rms_bwd_rows(dh_ref[...].astype(F32), x_ref[...], g_ref[...])
        dx = dx + r_ref[...]
        dx_ref[...] = dx
        dxb_ref[...] = dx.astype(BF16)

        @pl.when(i == 0)
        def _():
            acc_ref[...] = jnp.zeros_like(acc_ref)
        acc_ref[...] += _fold8(dgrows)

        @pl.when(i == n - 1)
        def _():
            dg_ref[...] = jnp.sum(acc_ref[...], axis=0, keepdims=True)

    row = pl.BlockSpec((tr, d), lambda i: (i, 0))
    vec = pl.BlockSpec((1, d), lambda i: (0, 0))
    return pl.pallas_call(
        body, name=name, grid=(n,), in_specs=[row, row, vec, row], out_specs=[row, row, vec],
        out_shape=[jax.ShapeDtypeStruct((s, d), F32), jax.ShapeDtypeStruct((s, d), BF16),
                   jax.ShapeDtypeStruct((1, d), F32)],
        scratch_shapes=[pltpu.VMEM((8, d), F32)],
        compiler_params=_cparams(("arbitrary",)),
    )(dh, x, g, dres)


def _loss_head(x2, target, g):
    s, d = x2.shape
    tr = _tile(s, TR_EW, 16)
    n = s // tr

    def body(x_ref, t_ref, g_ref, dx_ref, dxb_ref, dg_ref, loss_ref, acc_ref):
        i = pl.program_id(0)
        xf = x_ref[...]
        gv = g_ref[...]
        rstd = lax.rsqrt(jnp.mean(xf * xf, axis=-1, keepdims=True) + EPS)
        err = xf * rstd * gv - t_ref[...]
        row_loss = jnp.mean(err * err, axis=-1, keepdims=True)
        dx, dgrows = _rms_bwd_rows(err / d, xf, gv)
        dx_ref[...] = dx
        dxb_ref[...] = dx.astype(BF16)

        @pl.when(i == 0)
        def _():
            acc_ref[...] = jnp.zeros_like(acc_ref)
            loss_ref[...] = jnp.zeros_like(loss_ref)
        acc_ref[...] += _fold8(dgrows)
        loss_ref[...] += jnp.broadcast_to(0.5 * jnp.sum(row_loss, axis=0, keepdims=True), (8, LANE))

        @pl.when(i == n - 1)
        def _():
            dg_ref[...] = jnp.sum(acc_ref[...], axis=0, keepdims=True)

    row = pl.BlockSpec((tr, d), lambda i: (i, 0))
    vec = pl.BlockSpec((1, d), lambda i: (0, 0))
    return pl.pallas_call(
        body, name="loss_head", grid=(n,), in_specs=[row, row, vec],
        out_specs=[row, row, vec, pl.BlockSpec((8, LANE), lambda i: (0, 0))],
        out_shape=[jax.ShapeDtypeStruct((s, d), F32), jax.ShapeDtypeStruct((s, d), BF16),
                   jax.ShapeDtypeStruct((1, d), F32), jax.ShapeDtypeStruct((8, LANE), F32)],
        scratch_shapes=[pltpu.VMEM((8, d), F32)],
        compiler_params=_cparams(("arbitrary",)),
    )(x2, target, g)


def _gate_fwd(ya, yb, proj_g):
    s, d = ya.shape
    tr = _tile(s, TR_EW, 16)
    tc = _tile(d, 1024)
    nc = d // tc

    def body(ya_ref, yb_ref, ga_ref, gb_ref, o_ref):
        o_ref[...] = (_sigmoid(ga_ref[...]) * ya_ref[...] + _sigmoid(gb_ref[...]) * yb_ref[...]).astype(o_ref.dtype)

    blk = pl.BlockSpec((tr, tc), lambda i, j: (i, j))
    return pl.pallas_call(
        body, name="gate_fwd", grid=(s // tr, nc),
        in_specs=[blk, blk, blk, pl.BlockSpec((tr, tc), lambda i, j: (i, j + nc))],
        out_specs=blk, out_shape=jax.ShapeDtypeStruct((s, d), BF16),
        compiler_params=_cparams(("parallel", "parallel")),
    )(ya, yb, proj_g, proj_g)


def _gate_bwd(dmixed, ya, yb, proj_g):
    s, d = ya.shape
    tr = _tile(s, TR_EW, 16)
    tc = _tile(d, 1024)
    nc = d // tc

    def body(dm_ref, ya_ref, yb_ref, ga_ref, gb_ref, dya_ref, dyb_ref, dga_ref, dgb_ref):
        dm = dm_ref[...].astype(F32)
        sa = _sigmoid(ga_ref[...])
        sb = _sigmoid(gb_ref[...])
        dya_ref[...] = (dm * sa).astype(BF16)
        dyb_ref[...] = (dm * sb).astype(BF16)
        dga_ref[...] = (dm * ya_ref[...] * sa * (1.0 - sa)).astype(BF16)
        dgb_ref[...] = (dm * yb_ref[...] * sb * (1.0 - sb)).astype(BF16)

    blk = pl.BlockSpec((tr, tc), lambda i, j: (i, j))
    out = jax.ShapeDtypeStruct((s, d), BF16)
    return pl.pallas_call(
        body, name="gate_bwd", grid=(s // tr, nc),
        in_specs=[blk, blk, blk, blk, pl.BlockSpec((tr, tc), lambda i, j: (i, j + nc))],
        out_specs=[blk, blk, blk, blk], out_shape=[out, out, out, out],
        compiler_params=_cparams(("parallel", "parallel")),
    )(dmixed, ya, yb, proj_g, proj_g)


def _swiglu_fwd(gu):
    s = gu.shape[0]
    tr = _tile(s, 128, 16)

    def body(gu_ref, h_ref):
        gate = gu_ref[:, :FF_P].astype(F32)
        up = gu_ref[:, FF_P:].astype(F32)
        h_ref[...] = (gate * _sigmoid(gate) * up).astype(h_ref.dtype)

    return pl.pallas_call(
        body, name="swiglu_fwd", grid=(s // tr, N_CHIP),
        in_specs=[pl.BlockSpec((tr, 2 * FF_P), lambda i, j: (i, j))],
        out_specs=pl.BlockSpec((tr, FF_P), lambda i, j: (i, j)),
        out_shape=jax.ShapeDtypeStruct((s, FP), BF16),
        compiler_params=_cparams(("parallel", "parallel")),
    )(gu)


def _swiglu_bwd(dhidden, gu):
    s = gu.shape[0]
    tr = _tile(s, 128, 16)

    def body(dh_ref, gu_ref, dgu_ref):
        gate = gu_ref[:, :FF_P].astype(F32)
        up = gu_ref[:, FF_P:].astype(F32)
        dh = dh_ref[...].astype(F32)
        sg = _sigmoid(gate)
        dgu_ref[:, :FF_P] = (dh * up * sg * (1.0 + gate * (1.0 - sg))).astype(BF16)
        dgu_ref[:, FF_P:] = (dh * gate * sg).astype(BF16)

    return pl.pallas_call(
        body, name="swiglu_bwd", grid=(s // tr, N_CHIP),
        in_specs=[pl.BlockSpec((tr, FF_P), lambda i, j: (i, j)), pl.BlockSpec((tr, 2 * FF_P), lambda i, j: (i, j))],
        out_specs=pl.BlockSpec((tr, 2 * FF_P), lambda i, j: (i, j)),
        out_shape=jax.ShapeDtypeStruct((s, 2 * FP), BF16),
        compiler_params=_cparams(("parallel", "parallel")),
    )(dhidden, gu)


def _tri(n, upper):
    r = lax.broadcasted_iota(jnp.int32, (n, n), 0)
    c = lax.broadcasted_iota(jnp.int32, (n, n), 1)
    return (c >= r if upper else c <= r).astype(F32)


def _forget_fwd(fa, bias):
    s = fa.shape[0]
    tb = _tile(s, 512, 8)

    def body(f_ref, b_ref, c_ref, carry_ref):
        i = pl.program_id(0)

        @pl.when(i == 0)
        def _():
            carry_ref[...] = jnp.zeros_like(carry_ref)
        z = f_ref[...] + b_ref[...]
        logf = jnp.minimum(z, 0.0) - jnp.log(1.0 + jnp.exp(-jnp.abs(z)))
        c = jnp.dot(_tri(tb, False), logf, precision=lax.Precision.HIGHEST, preferred_element_type=F32)
        c_ref[...] = c + carry_ref[0:1, :]
        carry_ref[...] = jnp.broadcast_to(c_ref[tb - 1:tb, :], carry_ref.shape)

    return pl.pallas_call(
        body, name="forget_fwd", grid=(s // tb,),
        in_specs=[pl.BlockSpec((tb, LANE), lambda i: (i, 0)), pl.BlockSpec((1, LANE), lambda i: (0, 0))],
        out_specs=pl.BlockSpec((tb, LANE), lambda i: (i, 0)),
        out_shape=jax.ShapeDtypeStruct((s, LANE), F32),
        scratch_shapes=[pltpu.VMEM((8, LANE), F32)],
        compiler_params=_cparams(("arbitrary",)),
    )(fa, bias)


def _forget_bwd(dc, fa, bias):
    s = fa.shape[0]
    tb = _tile(s, 512, 16)
    n = s // tb

    def body(dc_ref, f_ref, b_ref, df_ref, db_ref, carry_ref, acc_ref, tmp_ref):
        i = pl.program_id(0)

        @pl.when(i == 0)
        def _():
            carry_ref[...] = jnp.zeros_like(carry_ref)
            acc_ref[...] = jnp.zeros_like(acc_ref)
        dlogf = jnp.dot(_tri(tb, True), dc_ref[...], precision=lax.Precision.HIGHEST, preferred_element_type=F32)
        tmp_ref[...] = dlogf + carry_ref[0:1, :]
        carry_ref[...] = jnp.broadcast_to(tmp_ref[0:1, :], carry_ref.shape)
        df = tmp_ref[...] * _sigmoid(-(f_ref[...] + b_ref[...]))
        df_ref[...] = df.astype(BF16)
        acc_ref[...] += _fold8(df)

        @pl.when(i == n - 1)
        def _():
            db_ref[...] = jnp.sum(acc_ref[...], axis=0, keepdims=True)

    rev = pl.BlockSpec((tb, LANE), lambda i: (n - 1 - i, 0))
    vec = pl.BlockSpec((1, LANE), lambda i: (0, 0))
    return pl.pallas_call(
        body, name="forget_bwd", grid=(n,), in_specs=[rev, rev, vec], out_specs=[rev, vec],
        out_shape=[jax.ShapeDtypeStruct((s, LANE), BF16), jax.ShapeDtypeStruct((1, LANE), F32)],
        scratch_shapes=[pltpu.VMEM((8, LANE), F32), pltpu.VMEM((8, LANE), F32), pltpu.VMEM((tb, LANE), F32)],
        compiler_params=_cparams(("arbitrary",)),
    )(dc, fa, bias)


_A_SCALE = DH_A ** -0.5


def _causal(tq, tk, transposed):
    r = lax.broadcasted_iota(jnp.int32, (tq, tk), 0)
    c = lax.broadcasted_iota(jnp.int32, (tq, tk), 1)
    return c >= r if transposed else r >= c


def _fox_fwd(qkv, c_col, c_row, jobs=()):
    s = qkv.shape[0]
    t = _tile(s, TQ_A)
    n = s // t
    kb, vb = OFF_KA // DH_A, OFF_VA // DH_A

    def body(q_ref, k_ref, v_ref, cq_ref, ck_ref, o_ref, lse_ref, m_s, l_s, acc_s):
        i, j = pl.program_id(1), pl.program_id(2)

        @pl.when(j == 0)
        def _():
            m_s[...] = jnp.full_like(m_s, -jnp.inf)
            l_s[...] = jnp.zeros_like(l_s)
            acc_s[...] = jnp.zeros_like(acc_s)

        def step(diag):
            sc = _dot_nt(q_ref[...], k_ref[...]) * _A_SCALE + (cq_ref[:, 0:1] - ck_ref[...])
            if diag:
                sc = jnp.where(_causal(t, t, False), sc, -jnp.inf)
            m_prev = m_s[...]
            m_new = jnp.maximum(m_prev, jnp.max(sc, axis=1, keepdims=True))
            alpha = jnp.exp(m_prev - m_new)
            p = jnp.exp(sc - m_new[:, 0:1])
            l_s[...] = alpha * l_s[...] + jnp.sum(p, axis=1, keepdims=True)
            acc_s[...] = alpha * acc_s[...] + _dot(p.astype(BF16), v_ref[...])
            m_s[...] = m_new

        @pl.when(j < i)
        def _():
            step(False)

        @pl.when(j == i)
        def _():
            step(True)
            o_ref[...] = (acc_s[...] / l_s[...]).astype(o_ref.dtype)
            lse_ref[...] = m_s[...] + jnp.log(l_s[...])

    jc = lambda i, j: jnp.minimum(j, i)
    return _pcall(
        body, (qkv, qkv, qkv, c_col, c_row), jobs, name="fox_fwd", grid=(H_A, n, n),
        in_specs=[pl.BlockSpec((t, DH_A), lambda h, i, j: (i, h)),
                  pl.BlockSpec((t, DH_A), lambda h, i, j: (jc(i, j), kb + h)),
                  pl.BlockSpec((t, DH_A), lambda h, i, j: (jc(i, j), vb + h)),
                  pl.BlockSpec((None, t, LANE), lambda h, i, j: (h, i, 0)),
                  pl.BlockSpec((None, 1, t), lambda h, i, j: (h, 0, jc(i, j)))],
        out_specs=[pl.BlockSpec((t, DH_A), lambda h, i, j: (i, h)),
                   pl.BlockSpec((None, t, LANE), lambda h, i, j: (h, i, 0))],
        out_shape=[jax.ShapeDtypeStruct((s, W_A), BF16), jax.ShapeDtypeStruct((H_A, s, LANE), F32)],
        scratch_shapes=[pltpu.VMEM((t, LANE), F32)] * 3,
        sem=("parallel", "parallel", "arbitrary"))


def _fox_dq(qkv, do, c_col, c_row, lse_col, jobs=()):
    s = qkv.shape[0]
    t = _tile(s, TQ_A)
    n = s // t
    kb, vb = OFF_KA // DH_A, OFF_VA // DH_A

    def body(q_ref, k_ref, v_ref, do_ref, cq_ref, ck_ref, lse_ref, dq_ref, dl_ref, pdk_s, pk_s, dl_s):
        i, j = pl.program_id(1), pl.program_id(2)

        @pl.when(j == 0)
        def _():
            pdk_s[...] = jnp.zeros_like(pdk_s)
            pk_s[...] = jnp.zeros_like(pk_s)
            dl_s[...] = jnp.zeros_like(dl_s)

        def step(diag):
            sc = _dot_nt(q_ref[...], k_ref[...]) * _A_SCALE + (cq_ref[:, 0:1] - ck_ref[...])
            if diag:
                sc = jnp.where(_causal(t, t, False), sc, -jnp.inf)
            p = jnp.exp(sc - lse_ref[:, 0:1])
            pdp = p * _dot_nt(do_ref[...], v_ref[...])
            dl_s[...] += jnp.sum(pdp, axis=1, keepdims=True)
            pdk_s[...] += _dot(pdp.astype(BF16), k_ref[...])
            pk_s[...] += _dot(p.astype(BF16), k_ref[...])

        @pl.when(j < i)
        def _():
            step(False)

        @pl.when(j == i)
        def _():
            step(True)
            dq_ref[...] = ((pdk_s[...] - dl_s[...] * pk_s[...]) * _A_SCALE).astype(dq_ref.dtype)
            dl_ref[...] = dl_s[...]

    jc = lambda i, j: jnp.minimum(j, i)
    col = pl.BlockSpec((None, t, LANE), lambda h, i, j: (h, i, 0))
    return _pcall(
        body, (qkv, qkv, qkv, do, c_col, c_row, lse_col), jobs, name="fox_dq", grid=(H_A, n, n),
        in_specs=[pl.BlockSpec((t, DH_A), lambda h, i, j: (i, h)),
                  pl.BlockSpec((t, DH_A), lambda h, i, j: (jc(i, j), kb + h)),
                  pl.BlockSpec((t, DH_A), lambda h, i, j: (jc(i, j), vb + h)),
                  pl.BlockSpec((t, DH_A), lambda h, i, j: (i, h)),
                  col,
                  pl.BlockSpec((None, 1, t), lambda h, i, j: (h, 0, jc(i, j))),
                  col],
        out_specs=[pl.BlockSpec((t, DH_A), lambda h, i, j: (i, h)), col],
        out_shape=[jax.ShapeDtypeStruct((s, W_A), BF16), jax.ShapeDtypeStruct((H_A, s, LANE), F32)],
        scratch_shapes=[pltpu.VMEM((t, DH_A), F32), pltpu.VMEM((t, DH_A), F32), pltpu.VMEM((t, LANE), F32)],
        sem=("parallel", "parallel", "arbitrary"))


def _fox_dkv(qkv, do, c_col, c_row, lse_row, delta_row, jobs=()):
    s = qkv.shape[0]
    t = _tile(s, TQ_A)
    n = s // t
    kb, vb = OFF_KA // DH_A, OFF_VA // DH_A

    def body(q_ref, k_ref, v_ref, do_ref, cq_ref, ck_ref, lse_ref, dl_ref, dk_ref, dv_ref, dc_ref,
             dk_s, dv_s, dc_s):
        j, i = pl.program_id(1), pl.program_id(2)

        @pl.when(i == 0)
        def _():
            dk_s[...] = jnp.zeros_like(dk_s)
            dv_s[...] = jnp.zeros_like(dv_s)
            dc_s[...] = jnp.zeros_like(dc_s)

        def step(diag):
            st = _dot_nt(k_ref[...], q_ref[...]) * _A_SCALE + (cq_ref[...] - ck_ref[:, 0:1])
            if diag:
                st = jnp.where(_causal(t, t, True), st, -jnp.inf)
            pt = jnp.exp(st - lse_ref[...])
            dv_s[...] += _dot(pt.astype(BF16), do_ref[...])
            dpt = _dot_nt(v_ref[...], do_ref[...])
            dst = pt * (dpt - dl_ref[...])
            dk_s[...] += _dot(dst.astype(BF16), q_ref[...])
            dc_s[...] -= jnp.sum(dst, axis=1, keepdims=True)

        @pl.when(i == j)
        def _():
            step(True)

        @pl.when(i > j)
        def _():
            step(False)

        @pl.when(i == n - 1)
        def _():
            dk_ref[...] = (dk_s[...] * _A_SCALE).astype(dk_ref.dtype)
            dv_ref[...] = dv_s[...].astype(dv_ref.dtype)
            dc_ref[...] = dc_s[...]

    ic = lambda j, i: jnp.maximum(i, j)
    rowq = pl.BlockSpec((None, 1, t), lambda h, j, i: (h, 0, ic(j, i)))
    kv_out = pl.BlockSpec((t, DH_A), lambda h, j, i: (j, h))
    return _pcall(
        body, (qkv, qkv, qkv, do, c_row, c_col, lse_row, delta_row), jobs, name="fox_dkv", grid=(H_A, n, n),
        in_specs=[pl.BlockSpec((t, DH_A), lambda h, j, i: (ic(j, i), h)),
                  pl.BlockSpec((t, DH_A), lambda h, j, i: (j, kb + h)),
                  pl.BlockSpec((t, DH_A), lambda h, j, i: (j, vb + h)),
                  pl.BlockSpec((t, DH_A), lambda h, j, i: (ic(j, i), h)),
                  rowq,
                  pl.BlockSpec((None, t, LANE), lambda h, j, i: (h, j, 0)),
                  rowq, rowq],
        out_specs=[kv_out, kv_out, pl.BlockSpec((None, t, LANE), lambda h, j, i: (h, j, 0))],
        out_shape=[jax.ShapeDtypeStruct((s, W_A), BF16), jax.ShapeDtypeStruct((s, W_A), BF16),
                   jax.ShapeDtypeStruct((H_A, s, LANE), F32)],
        scratch_shapes=[pltpu.VMEM((t, DH_A), F32), pltpu.VMEM((t, DH_A), F32), pltpu.VMEM((t, LANE), F32)],
        sem=("parallel", "parallel", "arbitrary"))


_B_SCALE = DH_B ** -0.5
_HALF = LANE // 2


def _t5_bucket_table():
    ql = np.arange(BLOCK)[:, None]
    kl = np.arange(2 * BLOCK)[None, :]
    dist = np.clip(ql + BLOCK - kl, 0, None)
    max_exact = NUM_BUCKETS // 2
    large = max_exact + (np.log(np.maximum(dist, 1) / max_exact) / np.log(MAX_DISTANCE / max_exact)
                         * (NUM_BUCKETS - max_exact)).astype(np.int64)
    large = np.minimum(large, NUM_BUCKETS - 1)
    return np.where(dist < max_exact, dist, large).astype(np.int32).reshape(1, BLOCK * 2 * BLOCK)


def _one_hot_buckets(bucket_ref, n):
    return (lax.broadcasted_iota(jnp.int32, (NUM_BUCKETS, n), 0) == bucket_ref[...]).astype(F32)


def _bias_table(rel_bias_t, bucket):
    nqk = bucket.shape[1]
    tc = _tile(nqk, 4096)

    def body(rb_ref, bk_ref, o_ref):
        o_ref[...] = jnp.dot(rb_ref[...], _one_hot_buckets(bk_ref, tc), precision=lax.Precision.HIGHEST,
                             preferred_element_type=F32)

    return pl.pallas_call(
        body, name="bias_table", grid=(nqk // tc,),
        in_specs=[pl.BlockSpec((H_B, NUM_BUCKETS), lambda i: (0, 0)), pl.BlockSpec((1, tc), lambda i: (0, i))],
        out_specs=pl.BlockSpec((H_B, tc), lambda i: (0, i)),
        out_shape=jax.ShapeDtypeStruct((H_B, nqk), F32),
        compiler_params=_cparams(("parallel",)),
    )(rel_bias_t, bucket)


def _bias_table_bwd(dbias, bucket):
    nqk = bucket.shape[1]
    tc = _tile(nqk, 4096)
    n = nqk // tc

    def body(db_ref, bk_ref, o_ref):
        i = pl.program_id(0)

        @pl.when(i == 0)
        def _():
            o_ref[...] = jnp.zeros_like(o_ref)
        o_ref[...] += lax.dot_general(db_ref[...], _one_hot_buckets(bk_ref, tc), (((1,), (1,)), ((), ())),
                                      precision=lax.Precision.HIGHEST, preferred_element_type=F32)

    return pl.pallas_call(
        body, name="bias_table_bwd", grid=(n,),
        in_specs=[pl.BlockSpec((H_B, tc), lambda i: (0, i)), pl.BlockSpec((1, tc), lambda i: (0, i))],
        out_specs=pl.BlockSpec((H_B, NUM_BUCKETS), lambda i: (0, 0)),
        out_shape=jax.ShapeDtypeStruct((H_B, NUM_BUCKETS), F32),
        compiler_params=_cparams(("arbitrary",)),
    )(dbias, bucket)


def _lane_lo():
    return lax.broadcasted_iota(jnp.int32, (1, LANE), 1) < _HALF


def _dup_kv_head(cat, hk):
    xcol = cat[:, (hk // 2) * LANE:(hk // 2 + 1) * LANE].astype(F32)
    swapped = pltpu.roll(xcol, _HALF, 1)
    lo = _lane_lo()
    return (jnp.where(lo, xcol, swapped) if hk % 2 == 0 else jnp.where(lo, swapped, xcol)).astype(BF16)


def _band_mask(first_block):
    ql = lax.broadcasted_iota(jnp.int32, (BLOCK, 2 * BLOCK), 0)
    kl = lax.broadcasted_iota(jnp.int32, (BLOCK, 2 * BLOCK), 1)
    dist = ql + BLOCK - kl
    ok = jnp.logical_and(dist >= 0, dist < WINDOW)
    return jnp.logical_and(ok, jnp.logical_or(jnp.logical_not(first_block), kl >= BLOCK))


def _swa_probs(qh, kdup, bias_h, sink, mask):
    sc = _dot_nt(qh, kdup) * _B_SCALE + bias_h
    sc = jnp.where(mask, sc, -jnp.inf)
    m = jnp.maximum(jnp.max(sc, axis=1, keepdims=True), sink)
    p = jnp.exp(sc - m)
    e_sink = jnp.exp(sink - m)
    inv = 1.0 / (jnp.sum(p, axis=1, keepdims=True) + e_sink)
    return p * inv, e_sink * inv


def _split_pair(ref, col):
    x = ref[:, col * LANE:(col + 1) * LANE].astype(F32)
    lo = _lane_lo()
    return jnp.where(lo, x, 0.0).astype(BF16), jnp.where(lo, 0.0, x).astype(BF16)


def _swa_fwd(qkv, bias, sinks, jobs=()):
    s = qkv.shape[0]
    nb = s // BLOCK
    qb, kb, vb = OFF_QB // W_QB, OFF_KB // W_KB, OFF_VB // W_KB
    assert OFF_QB % W_QB == 0 and OFF_KB % W_KB == 0 and OFF_VB % W_KB == 0 and W_KB % LANE == 0 and G_B % 2 == 0

    def body(q_ref, kp_ref, kc_ref, vp_ref, vc_ref, bias_ref, sink_ref, o_ref):
        i = pl.program_id(0)
        mask = _band_mask(i == 0)
        kcat = jnp.concatenate([kp_ref[...], kc_ref[...]], axis=0)
        vcat = jnp.concatenate([vp_ref[...], vc_ref[...]], axis=0)
        lo = _lane_lo()
        for hk in range(HKV_B):
            kdup, vdup = _dup_kv_head(kcat, hk), _dup_kv_head(vcat, hk)
            for pr in range(G_B // 2):
                h0 = hk * G_B + 2 * pr
                q0, q1 = _split_pair(q_ref, h0 // 2)
                p0, _ = _swa_probs(q0, kdup, bias_ref[h0], sink_ref[0, h0], mask)
                p1, _ = _swa_probs(q1, kdup, bias_ref[h0 + 1], sink_ref[0, h0 + 1], mask)
                o0 = _dot(p0.astype(BF16), vdup)
                o1 = _dot(p1.astype(BF16), vdup)
                o_ref[:, (h0 // 2) * LANE:(h0 // 2 + 1) * LANE] = jnp.where(lo, o0, o1).astype(o_ref.dtype)

    prev = lambda i: jnp.maximum(i - 1, 0)
    return _pcall(
        body, (qkv, qkv, qkv, qkv, qkv, bias, sinks), jobs, name="swa_fwd", grid=(nb,),
        in_specs=[pl.BlockSpec((BLOCK, W_QB), lambda i: (i, qb)),
                  pl.BlockSpec((BLOCK, W_KB), lambda i: (prev(i), kb)),
                  pl.BlockSpec((BLOCK, W_KB), lambda i: (i, kb)),
                  pl.BlockSpec((BLOCK, W_KB), lambda i: (prev(i), vb)),
                  pl.BlockSpec((BLOCK, W_KB), lambda i: (i, vb)),
                  pl.BlockSpec((H_B, BLOCK, 2 * BLOCK), lambda i: (0, 0, 0)),
                  pl.BlockSpec(memory_space=pltpu.SMEM)],
        out_specs=pl.BlockSpec((BLOCK, W_QB), lambda i: (i, 0)),
        out_shape=jax.ShapeDtypeStruct((s, W_QB), BF16),
        sem=("parallel",))


def _swa_bwd(qkv, do, bias, sinks, jobs=()):
    s = qkv.shape[0]
    nb = s // BLOCK
    qb, kb, vb = OFF_QB // W_QB, OFF_KB // W_KB, OFF_VB // W_KB

    def body(q_ref, kp_ref, kc_ref, vp_ref, vc_ref, do_ref, bias_ref, sink_ref,
             dq_ref, dk_ref, dv_ref, dbias_ref, dsink_ref, carry_k, carry_v):
        i = pl.program_id(0)
        lo = _lane_lo()

        @pl.when(i == 0)
        def _():
            dbias_ref[...] = jnp.zeros_like(dbias_ref)
            dsink_ref[...] = jnp.zeros_like(dsink_ref)
            carry_k[...] = jnp.zeros_like(carry_k)
            carry_v[...] = jnp.zeros_like(carry_v)

        @pl.when(i < nb)
        def _():
            mask = _band_mask(i == 0)
            kcat = jnp.concatenate([kp_ref[...], kc_ref[...]], axis=0)
            vcat = jnp.concatenate([vp_ref[...], vc_ref[...]], axis=0)
            lane = lax.broadcasted_iota(jnp.int32, (1, LANE), 1)
            dsink = jnp.zeros((1, LANE), F32)
            dk_cols = [jnp.zeros((2 * BLOCK, LANE), F32) for _ in range(W_KB // LANE)]
            dv_cols = [jnp.zeros((2 * BLOCK, LANE), F32) for _ in range(W_KB // LANE)]
            for hk in range(HKV_B):
                kdup, vdup = _dup_kv_head(kcat, hk), _dup_kv_head(vcat, hk)
                dk_acc = jnp.zeros((2 * BLOCK, LANE), F32)
                dv_acc = jnp.zeros((2 * BLOCK, LANE), F32)
                for pr in range(G_B // 2):
                    h0 = hk * G_B + 2 * pr
                    col = h0 // 2
                    qs = _split_pair(q_ref, col)
                    dos = _split_pair(do_ref, col)
                    dqs = []
                    for e in range(2):
                        h = h0 + e
                        p, p_sink = _swa_probs(qs[e], kdup, bias_ref[h], sink_ref[0, h], mask)
                        dp = _dot_nt(dos[e], vdup)
                        delta = jnp.sum(p * dp, axis=1, keepdims=True)
                        ds = p * (dp - delta)
                        dbias_ref[h] += ds
                        dsink = dsink - jnp.where(lane == h, jnp.sum(p_sink * delta, axis=0, keepdims=True), 0.0)
                        dqs.append(_dot(ds.astype(BF16), kdup))
                        dk_acc = dk_acc + _dot(ds.T.astype(BF16), qs[e])
                        dv_acc = dv_acc + _dot(p.T.astype(BF16), dos[e])
                    dq_ref[:, col * LANE:(col + 1) * LANE] = (jnp.where(lo, dqs[0], dqs[1]) * _B_SCALE).astype(dq_ref.dtype)
                dk_tot = (dk_acc + pltpu.roll(dk_acc, _HALF, 1)) * _B_SCALE
                dv_tot = dv_acc + pltpu.roll(dv_acc, _HALF, 1)
                mine = lo if hk % 2 == 0 else jnp.logical_not(lo)
                dk_cols[hk // 2] = jnp.where(mine, dk_tot, dk_cols[hk // 2])
                dv_cols[hk // 2] = jnp.where(mine, dv_tot, dv_cols[hk // 2])
            dsink_ref[...] += dsink
            dk_cat = jnp.concatenate(dk_cols, axis=1)
            dv_cat = jnp.concatenate(dv_cols, axis=1)
            dk_ref[...] = (carry_k[...] + dk_cat[:BLOCK]).astype(dk_ref.dtype)
            dv_ref[...] = (carry_v[...] + dv_cat[:BLOCK]).astype(dv_ref.dtype)
            carry_k[...] = dk_cat[BLOCK:]
            carry_v[...] = dv_cat[BLOCK:]

        @pl.when(i == nb)
        def _():
            dk_ref[...] = carry_k[...].astype(dk_ref.dtype)
            dv_ref[...] = carry_v[...].astype(dv_ref.dtype)

    cur = lambda i: jnp.minimum(i, nb - 1)
    prev = lambda i: jnp.clip(i - 1, 0, nb - 1)
    kv_out = pl.BlockSpec((BLOCK, W_KB), lambda i: (prev(i), 0))
    return _pcall(
        body, (qkv, qkv, qkv, qkv, qkv, do, bias, sinks), jobs, name="swa_bwd", grid=(nb + 1,),
        in_specs=[pl.BlockSpec((BLOCK, W_QB), lambda i: (cur(i), qb)),
                  pl.BlockSpec((BLOCK, W_KB), lambda i: (prev(i), kb)),
                  pl.BlockSpec((BLOCK, W_KB), lambda i: (cur(i), kb)),
                  pl.BlockSpec((BLOCK, W_KB), lambda i: (prev(i), vb)),
                  pl.BlockSpec((BLOCK, W_KB), lambda i: (cur(i), vb)),
                  pl.BlockSpec((BLOCK, W_QB), lambda i: (cur(i), 0)),
                  pl.BlockSpec((H_B, BLOCK, 2 * BLOCK), lambda i: (0, 0, 0)),
                  pl.BlockSpec(memory_space=pltpu.SMEM)],
        out_specs=[pl.BlockSpec((BLOCK, W_QB), lambda i: (cur(i), 0)), kv_out, kv_out,
                   pl.BlockSpec((H_B, BLOCK, 2 * BLOCK), lambda i: (0, 0, 0)),
                   pl.BlockSpec((1, LANE), lambda i: (0, 0))],
        out_shape=[jax.ShapeDtypeStruct((s, W_QB), BF16), jax.ShapeDtypeStruct((s, W_KB), BF16),
                   jax.ShapeDtypeStruct((s, W_KB), BF16),
                   jax.ShapeDtypeStruct((H_B, BLOCK, 2 * BLOCK), F32), jax.ShapeDtypeStruct((1, LANE), F32)],
        scratch_shapes=[pltpu.VMEM((BLOCK, W_KB), F32), pltpu.VMEM((BLOCK, W_KB), F32)],
        sem=("arbitrary",))


_RELS = ((1, 0), (0, 1), (1, 1))


def _place():
    x, y, c = lax.axis_index("x"), lax.axis_index("y"), lax.axis_index("c")
    return x, y, c


def _rel_chip(x, y, rel):
    px = 1 - x if rel[0] else x
    py = 1 - y if rel[1] else y
    return px, py, 2 * px + py


def _half_rows(ref_shape, c):
    half = ref_shape[-2] // 2
    return pl.ds(pl.multiple_of(c * half, 16), half)


def _dma_sems(*shape):
    return pltpu.SemaphoreType.DMA(shape)


def _gather_ici_job(srcs):
    nt = len(srcs)

    def copies(src, dst, sems):
        send, recv, lsem = sems
        x, y, c = _place()
        me_chip = 2 * x + y
        local, sends, recvs = [], [], []
        for t in range(nt):
            ns = src[t].shape[0]
            rows = _half_rows(src[t].shape, c)
            local.append(pltpu.make_async_copy(src[t], dst[t].at[pl.ds(me_chip * ns, ns)], lsem.at[t]))
            for k, rel in enumerate(_RELS):
                px, py, chip = _rel_chip(x, y, rel)
                for into, lst in ((me_chip, sends), (chip, recvs)):
                    lst.append(pltpu.make_async_remote_copy(
                        src_ref=src[t].at[:, rows], dst_ref=dst[t].at[pl.ds(into * ns, ns), rows],
                        send_sem=send.at[t, k], recv_sem=recv.at[t, k], device_id=(px, py, c), device_id_type=MESH))
        return local, sends, recvs

    def start(src, dst, sems):
        local, sends, _ = copies(src, dst, sems)
        for cp in local + sends:
            cp.start()

    def wait(src, dst, sems):
        local, sends, recvs = copies(src, dst, sems)
        for cp in recvs:
            cp.wait_recv()
        for cp in sends:
            cp.wait_send()
        for cp in local:
            cp.wait()

    return _Job(srcs, [jax.ShapeDtypeStruct((N_CHIP * a.shape[0],) + a.shape[1:], a.dtype) for a in srcs],
                [_dma_sems(nt, 3), _dma_sems(nt, 3), _dma_sems(nt)], start, wait)


def _gather_d2d_job(gathered):
    nt = len(gathered)

    def copies(dst, sems):
        send, recv = sems
        x, y, c = _place()
        sends, recvs = [], []
        for t in range(nt):
            ns = dst[t].shape[0] // N_CHIP
            for k, rel in enumerate(_RELS):
                _, _, chip = _rel_chip(x, y, rel)
                for half, lst in ((c, sends), (1 - c, recvs)):
                    part = dst[t].at[pl.ds(chip * ns, ns), _half_rows(dst[t].shape, half)]
                    lst.append(pltpu.make_async_remote_copy(
                        src_ref=part, dst_ref=part, send_sem=send.at[t, k], recv_sem=recv.at[t, k],
                        device_id=(x, y, 1 - c), device_id_type=MESH))
        return sends, recvs

    def start(_, dst, sems):
        for cp in copies(dst, sems)[0]:
            cp.start()

    def wait(_, dst, sems):
        sends, recvs = copies(dst, sems)
        for cp in recvs:
            cp.wait_recv()
        for cp in sends:
            cp.wait_send()

    return _Job(gathered, [jax.ShapeDtypeStruct(a.shape, a.dtype) for a in gathered],
                [_dma_sems(nt, 3), _dma_sems(nt, 3)], start, wait, alias={t: t for t in range(nt)})


def _pair_job(grads):
    nt = len(grads)

    def copies(g, got, sems):
        send, recv = sems
        x, y, c = _place()
        return [pltpu.make_async_remote_copy(
            src_ref=g[t].at[:, _half_rows(g[t].shape, 1 - c)], dst_ref=got[t], send_sem=send.at[t],
            recv_sem=recv.at[t], device_id=(x, y, 1 - c), device_id_type=MESH) for t in range(nt)]

    def start(g, got, sems):
        for cp in copies(g, got, sems):
            cp.start()

    def wait(g, got, sems):
        for cp in copies(g, got, sems):
            cp.wait()

    return _Job(grads, [jax.ShapeDtypeStruct((a.shape[0], a.shape[1] // 2, a.shape[2]), a.dtype) for a in grads],
                [_dma_sems(nt), _dma_sems(nt)], start, wait)


def _ew_rows(rows, cols, itemsize=4):
    tr = 16
    while tr * 2 <= 128 and rows % (tr * 2) == 0 and tr * 2 * cols * itemsize <= (1 << 20):
        tr *= 2
    assert rows % tr == 0, (rows, tr)
    return tr


def _pair_sum(name, grad, got, place):
    ns, rows, cols = grad.shape
    half = rows // 2
    tr = _ew_rows(half, cols)
    nh = half // tr

    def body(p_ref, g_ref, r_ref, o_ref):
        o_ref[...] = (g_ref[...] + r_ref[...]).astype(o_ref.dtype)

    return pl.pallas_call(
        body, name=name,
        grid_spec=pltpu.PrefetchScalarGridSpec(
            num_scalar_prefetch=1, grid=(ns, nh),
            in_specs=[pl.BlockSpec((None, tr, cols), lambda s, i, p: (s, p[1] * nh + i, 0)),
                      pl.BlockSpec((None, tr, cols), lambda s, i, p: (s, i, 0))],
            out_specs=pl.BlockSpec((None, tr, cols), lambda s, i, p: (s, i, 0))),
        out_shape=jax.ShapeDtypeStruct((ns, half, cols), BF16),
        compiler_params=_cparams(("parallel", "parallel")),
    )(place, grad, got)


def _chip_job(psums):
    nt = len(psums)

    def copies(p, got, sems):
        send, recv = sems
        x, y, c = _place()
        cps = []
        for t in range(nt):
            ns = p[t].shape[0] // N_CHIP
            for k, rel in enumerate(_RELS):
                px, py, chip = _rel_chip(x, y, rel)
                cps.append(pltpu.make_async_remote_copy(
                    src_ref=p[t].at[pl.ds(chip * ns, ns)], dst_ref=got[t].at[k],
                    send_sem=send.at[t, k], recv_sem=recv.at[t, k], device_id=(px, py, c), device_id_type=MESH))
        return cps

    def start(p, got, sems):
        for cp in copies(p, got, sems):
            cp.start()

    def wait(p, got, sems):
        for cp in copies(p, got, sems):
            cp.wait()

    return _Job(psums, [jax.ShapeDtypeStruct((3, a.shape[0] // N_CHIP) + a.shape[1:], a.dtype) for a in psums],
                [_dma_sems(nt, 3), _dma_sems(nt, 3)], start, wait)


def _chip_sum(name, psum, got, place):
    ns4, half, cols = psum.shape
    ns = ns4 // N_CHIP
    tr = _ew_rows(half, cols)
    nh = half // tr

    def body(p_ref, mine_ref, got_ref, o_ref):
        acc = mine_ref[...].astype(F32)
        for k in range(3):
            acc = acc + got_ref[k].astype(F32)
        o_ref[...] = acc

    return pl.pallas_call(
        body, name=name,
        grid_spec=pltpu.PrefetchScalarGridSpec(
            num_scalar_prefetch=1, grid=(ns, nh),
            in_specs=[pl.BlockSpec((None, tr, cols), lambda s, i, p: (p[0] * ns + s, i, 0)),
                      pl.BlockSpec((3, None, tr, cols), lambda s, i, p: (0, s, i, 0))],
            out_specs=pl.BlockSpec((None, tr, cols), lambda s, i, p: (s, p[1] * nh + i, 0))),
        out_shape=jax.ShapeDtypeStruct((ns, 2 * half, cols), F32),
        compiler_params=_cparams(("parallel", "parallel")),
    )(place, psum, got)


def _share_job(halves):
    nt = len(halves)

    def copies(full, sems):
        send, recv = sems
        x, y, c = _place()
        sends, recvs = [], []
        for t in range(nt):
            for half, lst in ((c, sends), (1 - c, recvs)):
                part = full[t].at[:, _half_rows(full[t].shape, half)]
                lst.append(pltpu.make_async_remote_copy(
                    src_ref=part, dst_ref=part, send_sem=send.at[t], recv_sem=recv.at[t],
                    device_id=(x, y, 1 - c), device_id_type=MESH))
        return sends, recvs

    def start(_, full, sems):
        for cp in copies(full, sems)[0]:
            cp.start()

    def wait(_, full, sems):
        sends, recvs = copies(full, sems)
        for cp in sends:
            cp.wait_send()
        for cp in recvs:
            cp.wait_recv()

    return _Job(halves, [jax.ShapeDtypeStruct(a.shape, a.dtype) for a in halves],
                [_dma_sems(nt), _dma_sems(nt)], start, wait, alias={t: t for t in range(nt)})


def _small_all_reduce(pack):
    rows, d = pack.shape

    def body(x_ref, o_ref, land, send, recv):
        x, y, c = _place()
        me = 4 * x + 2 * y + c
        land[me] = x_ref[...]
        cps = []
        for k in range(1, 8):
            to = (1 - x if k & 4 else x, 1 - y if k & 2 else y, 1 - c if k & 1 else c)
            cps.append(pltpu.make_async_remote_copy(
                src_ref=x_ref, dst_ref=land.at[me], send_sem=send.at[k - 1], recv_sem=recv.at[k - 1],
                device_id=to, device_id_type=MESH))
        for cp in cps:
            cp.start()
        for cp in cps:
            cp.wait()
        acc = land[0]
        for dev in range(1, 8):
            acc = acc + land[dev]
        o_ref[...] = acc

    vm = pl.BlockSpec(memory_space=pltpu.VMEM)
    return pl.pallas_call(
        body, name="small_all_reduce", in_specs=[vm], out_specs=vm,
        out_shape=jax.ShapeDtypeStruct((rows, d), F32),
        scratch_shapes=[pltpu.VMEM((8, rows, d), F32), pltpu.SemaphoreType.DMA((7,)), pltpu.SemaphoreType.DMA((7,))],
    )(pack)


def _adamw(name, w, g, m, v, jobs=()):
    lead, rows, cols = w.shape
    tr = 8
    while tr * 2 <= 256 and rows % (tr * 2) == 0 and tr * 2 * cols * 4 <= (1 << 20):
        tr *= 2
    if rows % tr:
        tr = rows

    def body(w_ref, g_ref, m_ref, v_ref, d_ref, nm_ref, nv_ref):
        gv = g_ref[...]
        nm = ADAM_B1 * m_ref[...] + (1.0 - ADAM_B1) * gv
        nv = ADAM_B2 * v_ref[...] + (1.0 - ADAM_B2) * (gv * gv)
        m_hat = nm / (1.0 - ADAM_B1 ** ADAM_STEP)
        v_hat = nv / (1.0 - ADAM_B2 ** ADAM_STEP)
        d_ref[...] = -ADAM_LR * (m_hat / (jnp.sqrt(v_hat) + ADAM_EPS) + ADAM_WD * w_ref[...])
        nm_ref[...] = nm
        nv_ref[...] = nv

    blk = pl.BlockSpec((None, tr, cols), lambda l, i: (l, i, 0))
    out = jax.ShapeDtypeStruct(w.shape, F32)
    return _pcall(
        body, (w, g, m, v), jobs, name=name, grid=(lead, rows // tr), in_specs=[blk] * 4, out_specs=[blk] * 3,
        out_shape=[out, out, out], sem=("parallel", "parallel"))


def _pad_to(a, shape):
    return jnp.pad(a, [(0, t - s) for s, t in zip(a.shape, shape)])


def _pack_small(n1, n2, fg, bf, sk, rb, extra=None):
    rows = [n1.reshape(1, D_MODEL), n2.reshape(1, D_MODEL), fg.reshape(1, D_MODEL),
            _pad_to(bf.reshape(1, H_A), (1, D_MODEL)), _pad_to(sk.reshape(1, H_B), (1, D_MODEL)),
            jnp.zeros((1, D_MODEL), F32) if extra is None else _pad_to(extra.reshape(1, 1), (1, D_MODEL)),
            _pad_to(rb.reshape(1, NUM_BUCKETS * H_B), (1, RB_ROWS * D_MODEL)).reshape(RB_ROWS, D_MODEL)]
    return _pad_to(jnp.concatenate(rows, axis=0), (PACK_ROWS, D_MODEL))


def _unpack_small(p):
    return (p[0:1], p[1:2], p[2], p[3:4, :H_A], p[4:5, :H_B],
            p[6:6 + RB_ROWS].reshape(-1)[:NUM_BUCKETS * H_B].reshape(NUM_BUCKETS, H_B))


def kernel(x, norm1_g, w_in, b_forget, attn_sinks, rel_bias, w_branch_a, w_branch_b, w_out, norm2_g, w_ffn_gate, w_ffn_up, w_ffn_down, final_g, loss_target, m_norm1_g, m_w_in, m_b_forget, m_attn_sinks, m_rel_bias, m_w_branch_a, m_w_branch_b, m_w_out, m_norm2_g, m_w_ffn_gate, m_w_ffn_up, m_w_ffn_down, m_final_g, v_norm1_g, v_w_in, v_b_forget, v_attn_sinks, v_rel_bias, v_w_branch_a, v_w_branch_b, v_w_out, v_norm2_g, v_w_ffn_gate, v_w_ffn_up, v_w_ffn_down, v_final_g):
    s, d = SEQ, D_MODEL
    assert x.shape == (1, s, d) and w_in.shape == (1, d, W_IN_SH)
    xs = x[0]
    place = jnp.stack([2 * lax.axis_index("x") + lax.axis_index("y"), lax.axis_index("c")]).astype(jnp.int32)

    w_gu_l = jnp.stack([_pad_to(w_ffn_gate[0], (d, FF_P)), _pad_to(w_ffn_up[0], (d, FF_P))]).astype(BF16)
    w_dn_l = _pad_to(w_ffn_down, (1, FF_P, d)).astype(BF16)
    j_in = _gather_ici_job([w_in.astype(BF16)])
    _comm_now("gather_w_in_ici", [j_in])
    j_in2 = _gather_d2d_job(j_in.out)
    _comm_now("gather_w_in_d2d", [j_in2])
    w_in4 = j_in2.out[0]
    j_abo = _gather_ici_job([w_branch_a.astype(BF16), w_branch_b.astype(BF16), w_out.astype(BF16)])
    j_dn = _gather_ici_job([w_dn_l])
    j_gu = _gather_ici_job([w_gu_l])
    w_full = jnp.concatenate([w_in4[j] for j in range(N_CHIP)], axis=1)
    n_qkv_a = 3 * W_A
    w_p = jnp.concatenate([w_full[:, :n_qkv_a], w_full[:, n_qkv_a + H_A:], w_full[:, n_qkv_a:n_qkv_a + H_A],
                           jnp.zeros((d, LANE - H_A), BF16)], axis=1)

    h1 = _rms_fwd("norm1_fwd", xs, norm1_g)
    qkv = _matmul("proj_qkv", h1, w_p, m=s, n=W_QKV, k=d, out_shape=(s, W_QKV), out_dtype=BF16, jobs=[j_abo])
    j_abo2 = _gather_d2d_job(j_abo.out)
    proj_g = _matmul("proj_gates", h1, w_p, m=s, n=2 * d, k=d, b_noff=OFF_GA, out_shape=(s, 2 * d), out_dtype=F32,
                     jobs=[j_abo2, j_dn])
    w_a, w_b, w_o = j_abo2.out
    w_o = w_o.reshape(d, d)
    fa = _matmul("proj_forget", h1, w_p, m=s, n=LANE, k=d, b_noff=OFF_FA, tn=LANE, out_shape=(s, LANE), out_dtype=F32)
    b_f = _pad_to(b_forget, (1, LANE))
    c_cum = _forget_fwd(fa, b_f)
    c_t = c_cum[:, :H_A].T
    c_col = jnp.broadcast_to(c_t[:, :, None], (H_A, s, LANE))
    c_row = c_t[:, None, :]
    j_dn2 = _gather_d2d_job(j_dn.out)
    attn_a, lse_col = _fox_fwd(qkv, c_col, c_row, jobs=[j_dn2, j_gu])
    w_dn = j_dn2.out[0].reshape(FP, d)

    bucket = jnp.asarray(_t5_bucket_table())
    bias = _bias_table(rel_bias.T, bucket).reshape(H_B, BLOCK, 2 * BLOCK)
    j_gu2 = _gather_d2d_job(j_gu.out)
    attn_b = _swa_fwd(qkv, bias, attn_sinks, jobs=[j_gu2])
    w_gu = j_gu2.out[0]

    ya = _matmul("branch_a", attn_a, w_a, m=s, n=d, k=W_A, b_kind="col", tn=_tile(A_SH, TN),
                 out_shape=(s, d), out_dtype=F32)
    yb = _matmul("branch_b", attn_b, w_b, m=s, n=d, k=W_QB, b_kind="col", tn=_tile(A_SH, TN),
                 out_shape=(s, d), out_dtype=F32)
    mixed = _gate_fwd(ya, yb, proj_g)
    x1 = _matmul("out_proj", mixed, w_o, m=s, n=d, k=d, res=xs, out_shape=(s, d), out_dtype=F32)

    h2 = _rms_fwd("norm2_fwd", x1, norm2_g)
    gu = _matmul("ffn_gate_up", h2, w_gu, m=s, n=2 * FP, k=d, b_kind="col", tn=_tile(FF_P, 1408),
                 out_shape=(s, 2 * FP), out_dtype=BF16)
    hidden = _swiglu_fwd(gu)
    x2 = _matmul("ffn_down", hidden, w_dn, m=s, n=d, k=FP, tk=_tile(FP, 5632), res=x1, out_shape=(s, d), out_dtype=F32)

    dx2, dx2_b, d_fg, loss_tile = _loss_head(x2, loss_target[0], final_g.reshape(1, d))
    dhidden = _matmul("ffn_down_dx", dx2_b, w_dn, m=s, n=FP, k=d, nt=True, tn=_tile(FF_P, 1408),
                      out_shape=(s, FP), out_dtype=BF16)
    dgu = _swiglu_bwd(dhidden, gu)
    g_dn = _matmul("ffn_down_dw", hidden.T, dx2_b, m=FP, n=d, k=s, tm=_tile(FF_P, 1408, 16),
                   out_shape=(FP, d), out_dtype=F32).reshape(N_CHIP, FF_P, d)
    j_p_dn = _pair_job([g_dn])
    g_gu = _matmul("ffn_gate_up_dw", h2.T, dgu, m=d, n=2 * FP, k=s, o_kind="col", tm=_tile(d, 512, 16),
                   tn=_tile(FF_P, 1408), out_shape=(2 * N_CHIP, d, FF_P), out_dtype=F32, jobs=[j_p_dn])
    ps_dn = _pair_sum("pair_sum_w_ffn_down", g_dn, j_p_dn.out[0], place)
    j_c_dn = _chip_job([ps_dn])
    j_p_gu = _pair_job([g_gu])
    dh2 = _matmul("ffn_gate_up_dx", dgu, w_gu, m=s, n=d, k=2 * FP, nt=True, b_kind="col", tn=_tile(d, 1024),
                  tk=_tile(FF_P, TK), out_shape=(s, d), out_dtype=F32, jobs=[j_c_dn, j_p_gu])
    h_dn = _chip_sum("chip_sum_w_ffn_down", ps_dn, j_c_dn.out[0], place)
    ps_gu = _pair_sum("pair_sum_w_ffn_gate_up", g_gu, j_p_gu.out[0], place)
    dx1, dx1_b, d_n2 = _rms_bwd("norm2_bwd", dh2, x1, norm2_g, dx2)

    dmixed = _matmul("out_proj_dx", dx1_b, w_o, m=s, n=d, k=d, nt=True, out_shape=(s, d), out_dtype=BF16)
    g_o = _matmul("out_proj_dw", mixed.T, dx1_b, m=d, n=d, k=s, out_shape=(d, d),
                  out_dtype=F32).reshape(N_CHIP, A_SH, d)
    dya, dyb, dga, dgb = _gate_bwd(dmixed, ya, yb, proj_g)
    g_a = _matmul("branch_a_dw", attn_a.T, dya, m=W_A, n=d, k=s, o_kind="col", tn=_tile(A_SH, TN),
                  out_shape=(N_CHIP, W_A, A_SH), out_dtype=F32)
    g_b = _matmul("branch_b_dw", attn_b.T, dyb, m=W_QB, n=d, k=s, o_kind="col", tn=_tile(A_SH, TN),
                  out_shape=(N_CHIP, W_QB, A_SH), out_dtype=F32)
    j_p_oab = _pair_job([g_o, g_a, g_b])
    dattn_a = _matmul("branch_a_dx", dya, w_a, m=s, n=W_A, k=d, nt=True, b_kind="col", tn=_tile(W_A, 1024),
                      tk=_tile(A_SH, TK), out_shape=(s, W_A), out_dtype=BF16, jobs=[j_p_oab])
    dattn_b = _matmul("branch_b_dx", dyb, w_b, m=s, n=W_QB, k=d, nt=True, b_kind="col", tn=_tile(W_QB, 1024),
                      tk=_tile(A_SH, TK), out_shape=(s, W_QB), out_dtype=BF16)
    ps_o, ps_a, ps_b = (_pair_sum("pair_sum_" + n, g, r, place) for n, g, r in
                        zip(("w_out", "w_branch_a", "w_branch_b"), (g_o, g_a, g_b), j_p_oab.out))

    j_c_gu = _chip_job([ps_gu])
    dq_a, delta_col = _fox_dq(qkv, dattn_a, c_col, c_row, lse_col, jobs=[j_c_gu])
    lse_row = lse_col[:, :, 0][:, None, :]
    delta_row = delta_col[:, :, 0][:, None, :]
    j_c_oab = _chip_job([ps_o, ps_a, ps_b])
    j_s_dn = _share_job([h_dn])
    dk_a, dv_a, dc_col = _fox_dkv(qkv, dattn_a, c_col, c_row, lse_row, delta_row, jobs=[j_c_oab, j_s_dn])
    dc = _pad_to(dc_col[:, :, 0].T, (s, LANE))
    df, d_bf = _forget_bwd(dc, fa, b_f)
    h_gu = _chip_sum("chip_sum_w_ffn_gate_up", ps_gu, j_c_gu.out[0], place)
    h_o, h_a, h_b = (_chip_sum("chip_sum_" + n, p, r, place) for n, p, r in
                     zip(("w_out", "w_branch_a", "w_branch_b"), (ps_o, ps_a, ps_b), j_c_oab.out))

    j_s_rest = _share_job([h_gu, h_o, h_a, h_b])
    dq_b, dk_b, dv_b, dbias, d_sk = _swa_bwd(qkv, dattn_b, bias, attn_sinks, jobs=[j_s_rest])
    r_dn = j_s_dn.out[0]
    r_gu, r_o, r_a, r_b = j_s_rest.out
    d_rb = _bias_table_bwd(dbias.reshape(H_B, BLOCK * 2 * BLOCK), bucket).T

    dproj = jnp.concatenate([dq_a, dk_a, dv_a, dq_b, dk_b, dv_b, dga, dgb, df], axis=1)
    g_in_p = _matmul("in_proj_dw", h1.T, dproj, m=d, n=PW, k=s, tn=_tile(PW, 1024), out_shape=(d, PW), out_dtype=F32)
    g_in_full = jnp.concatenate([g_in_p[:, :n_qkv_a], g_in_p[:, OFF_FA:OFF_FA + H_A], g_in_p[:, n_qkv_a:OFF_FA]], axis=1)
    g_in = jnp.stack([g_in_full[:, j * W_IN_SH:(j + 1) * W_IN_SH] for j in range(N_CHIP)])
    j_p_in = _pair_job([g_in])
    dh1 = _matmul("in_proj_dx", dproj, w_p, m=s, n=d, k=PW, nt=True, tn=_tile(d, 1024), tk=_tile(PW, 2560),
                  out_shape=(s, d), out_dtype=F32, jobs=[j_p_in])
    ps_in = _pair_sum("pair_sum_w_in", g_in, j_p_in.out[0], place)
    grad_x, _, d_n1 = _rms_bwd("norm1_bwd", dh1, xs, norm1_g, dx1)

    small = _small_all_reduce(_pack_small(d_n1, d_n2, d_fg, d_bf[:, :H_A], d_sk[:, :H_B], d_rb, loss_tile[0:1, 0:1]))
    loss = small[5, 0]

    grads = {
        "w_branch_a": r_a, "w_branch_b": r_b, "w_out": r_o,
        "w_ffn_gate": r_gu[0:1, :, :FF_SH], "w_ffn_up": r_gu[1:2, :, :FF_SH], "w_ffn_down": r_dn[:, :FF_SH, :],
    }
    given = dict(w_in=(w_in, m_w_in, v_w_in), w_branch_a=(w_branch_a, m_w_branch_a, v_w_branch_a),
                 w_branch_b=(w_branch_b, m_w_branch_b, v_w_branch_b), w_out=(w_out, m_w_out, v_w_out),
                 w_ffn_gate=(w_ffn_gate, m_w_ffn_gate, v_w_ffn_gate), w_ffn_up=(w_ffn_up, m_w_ffn_up, v_w_ffn_up),
                 w_ffn_down=(w_ffn_down, m_w_ffn_down, v_w_ffn_down))

    def adamw(n, jobs=()):
        return _adamw("adamw_" + n, given[n][0], grads[n], given[n][1], given[n][2], jobs=jobs)

    j_c_in = _chip_job([ps_in])
    upd = {"w_ffn_gate": adamw("w_ffn_gate", [j_c_in])}
    h_in = _chip_sum("chip_sum_w_in", ps_in, j_c_in.out[0], place)
    j_s_in = _share_job([h_in])
    upd["w_ffn_up"] = adamw("w_ffn_up", [j_s_in])
    grads["w_in"] = j_s_in.out[0]
    for n in ("w_ffn_down", "w_out", "w_branch_a", "w_branch_b", "w_in"):
        upd[n] = adamw(n)
    sm = _adamw("adamw_small",
                _pack_small(norm1_g, norm2_g, final_g, b_forget, attn_sinks, rel_bias)[None],
                small.at[5].set(0.0)[None],
                _pack_small(m_norm1_g, m_norm2_g, m_final_g, m_b_forget, m_attn_sinks, m_rel_bias)[None],
                _pack_small(v_norm1_g, v_norm2_g, v_final_g, v_b_forget, v_attn_sinks, v_rel_bias)[None])
    g_small = _unpack_small(small)
    d_small, m_small, v_small = (_unpack_small(t[0]) for t in sm)

    order = ["norm1_g", "w_in", "b_forget", "attn_sinks", "rel_bias", "w_branch_a", "w_branch_b", "w_out",
             "norm2_g", "w_ffn_gate", "w_ffn_up", "w_ffn_down", "final_g"]
    small_at = {"norm1_g": 0, "norm2_g": 1, "final_g": 2, "b_forget": 3, "attn_sinks": 4, "rel_bias": 5}

    def pick(big_idx, small_src, n):
        return small_src[small_at[n]] if n in small_at else (grads[n] if big_idx is None else upd[n][big_idx])

    return (loss, grad_x[None],
            *[pick(None, g_small, n) for n in order], *[pick(0, d_small, n) for n in order],
            *[pick(1, m_small, n) for n in order], *[pick(2, v_small, n) for n in order])
```

```python
import functools
import math

import numpy as np
import jax
import jax.numpy as jnp
from jax import lax
from jax.experimental import pallas as pl
from jax.experimental.pallas import tpu as pltpu

F32 = jnp.float32
BF16 = jnp.bfloat16
MESH = pl.DeviceIdType.MESH

D_MODEL = 4096
SEQ = 4096
H_A = 16
DH_A = 128
H_B = 32
HKV_B = 4
G_B = H_B // HKV_B
DH_B = 64
WINDOW = 128
NUM_BUCKETS = 32
MAX_DISTANCE = 128
BLOCK = 128
D_FF = ((8 * D_MODEL // 3 + 255) // 256) * 256
EPS = 1e-6
ADAM_LR = 0.001
ADAM_B1 = 0.9
ADAM_B2 = 0.999
ADAM_EPS = 1e-08
ADAM_WD = 0.01
ADAM_STEP = 10

N_CHIP = 4
LANE = 128
VMEM_LIMIT = 56 * 1024 * 1024

TM = 1024
TN = 512
TK = 4096
TQ_A = 512
TR_EW = 256

W_A = H_A * DH_A
W_QB = H_B * DH_B
W_KB = HKV_B * DH_B
OFF_QA = 0
OFF_KA = W_A
OFF_VA = 2 * W_A
OFF_QB = 3 * W_A
OFF_KB = OFF_QB + W_QB
OFF_VB = OFF_KB + W_KB
OFF_GA = OFF_VB + W_KB
OFF_GB = OFF_GA + D_MODEL
OFF_FA = OFF_GB + D_MODEL
PW = OFF_FA + LANE
W_QKV = OFF_GA
W_IN = 3 * W_A + H_A + W_QB + 2 * W_KB + 2 * D_MODEL
W_IN_SH = W_IN // N_CHIP
A_SH = D_MODEL // N_CHIP
FF_SH = D_FF // N_CHIP
FF_P = -(-FF_SH // LANE) * LANE
FP = N_CHIP * FF_P
PACK_ROWS = 16
RB_ROWS = -(-(NUM_BUCKETS * H_B) // D_MODEL)


def _tile(n, target, mult=LANE):
    t = min(target, n) // mult * mult
    while t > mult and n % t:
        t -= mult
    assert t > 0 and n % t == 0, (n, target, mult)
    return t


def _cparams(sem):
    return pltpu.CompilerParams(dimension_semantics=sem, vmem_limit_bytes=VMEM_LIMIT)


def _sigmoid(x):
    return 1.0 / (1.0 + jnp.exp(-x))


def _dot(a, b):
    return jnp.dot(a, b, preferred_element_type=F32)


def _dot_nt(a, b):
    return lax.dot_general(a, b, (((1,), (1,)), ((), ())), preferred_element_type=F32)


ANY = pl.BlockSpec(memory_space=pl.ANY)


class _Job:
    def __init__(self, ins, outs, sems, start, wait, alias=None):
        self.ins, self.outs, self.sems = list(ins), list(outs), list(sems)
        self.start, self.wait, self.alias = start, wait, dict(alias or {})
        self.out = None


def _split(seq, sizes):
    parts, p = [], 0
    for n in sizes:
        parts.append(seq[p:p + n])
        p += n
    return parts


def _pcall(body, args, jobs, *, name, grid, in_specs, out_specs, out_shape, scratch_shapes=(), sem):
    if not jobs:
        return pl.pallas_call(body, name=name, grid=grid, in_specs=in_specs, out_specs=out_specs, out_shape=out_shape,
                              scratch_shapes=list(scratch_shapes), compiler_params=_cparams(sem))(*args)
    single = not isinstance(out_shape, (list, tuple))
    out_specs_l = [out_specs] if single else list(out_specs)
    out_shape_l = [out_shape] if single else list(out_shape)
    cin = [a for j in jobs for a in j.ins]
    cout = [o for j in jobs for o in j.outs]
    csem = [s for j in jobs for s in j.sems]
    sizes = [len(args), len(cin), len(out_shape_l), len(cout), len(scratch_shapes), len(csem)]
    aliases, io, oo = {}, len(args), len(out_shape_l)
    for j in jobs:
        for a, b in j.alias.items():
            aliases[io + a] = oo + b
        io, oo = io + len(j.ins), oo + len(j.outs)

    def wrapped(*refs):
        ins, cins, outs, couts, scr, sems = _split(refs, sizes)
        ids = [pl.program_id(a) for a in range(len(grid))]
        first = functools.reduce(jnp.logical_and, [i == 0 for i in ids])
        last = functools.reduce(jnp.logical_and, [i == g - 1 for i, g in zip(ids, grid)])
        per_job = list(zip(jobs, _split(cins, [len(j.ins) for j in jobs]), _split(couts, [len(j.outs) for j in jobs]),
                           _split(sems, [len(j.sems) for j in jobs])))

        @pl.when(first)
        def _():
            for j, ji, jo, js in per_job:
                j.start(ji, jo, js)

        body(*ins, *outs, *scr)

        @pl.when(last)
        def _():
            for j, ji, jo, js in per_job:
                j.wait(ji, jo, js)

    res = pl.pallas_call(
        wrapped, name=name, grid=grid, in_specs=list(in_specs) + [ANY] * len(cin),
        out_specs=out_specs_l + [ANY] * len(cout), out_shape=out_shape_l + cout,
        scratch_shapes=list(scratch_shapes) + csem, input_output_aliases=aliases,
        compiler_params=_cparams(("arbitrary",) * len(grid)))(*args, *cin)
    main, rest = res[:len(out_shape_l)], res[len(out_shape_l):]
    for j, o in zip(jobs, _split(rest, [len(j.outs) for j in jobs])):
        j.out = list(o)
    return main[0] if single else list(main)


def _comm_now(name, jobs):
    cin = [a for j in jobs for a in j.ins]
    cout = [o for j in jobs for o in j.outs]
    csem = [s for j in jobs for s in j.sems]
    sizes = [len(cin), len(cout), len(csem)]
    aliases, io, oo = {}, 0, 0
    for j in jobs:
        for a, b in j.alias.items():
            aliases[io + a] = oo + b
        io, oo = io + len(j.ins), oo + len(j.outs)

    def body(*refs):
        cins, couts, sems = _split(refs, sizes)
        per_job = list(zip(jobs, _split(cins, [len(j.ins) for j in jobs]), _split(couts, [len(j.outs) for j in jobs]),
                           _split(sems, [len(j.sems) for j in jobs])))
        for j, ji, jo, js in per_job:
            j.start(ji, jo, js)
        for j, ji, jo, js in per_job:
            j.wait(ji, jo, js)

    res = pl.pallas_call(body, name=name, in_specs=[ANY] * len(cin), out_specs=[ANY] * len(cout), out_shape=cout,
                         scratch_shapes=csem, input_output_aliases=aliases)(*cin)
    for j, o in zip(jobs, _split(res, [len(j.outs) for j in jobs])):
        j.out = list(o)


def _slot_spec(shape, kind, br, bc, rc):
    if kind == "2d":
        return pl.BlockSpec((br, bc), lambda *g: rc(*g))
    if kind == "col":
        assert shape[2] % bc == 0, (shape, bc)
        per = shape[2] // bc

        def im_col(*g):
            rb, cb = rc(*g)
            return (cb // per, rb, cb % per)
        return pl.BlockSpec((None, br, bc), im_col)
    assert kind == "row" and shape[1] % br == 0, (shape, kind, br)
    per = shape[1] // br

    def im_row(*g):
        rb, cb = rc(*g)
        return (rb // per, rb % per, cb)
    return pl.BlockSpec((None, br, bc), im_row)


def _matmul(name, a, b, *, m, n, k, nt=False, b_kind="2d", o_kind="2d", out_shape, out_dtype,
            tm=None, tn=None, tk=None, b_noff=0, res=None, b_outer=False, jobs=()):
    tm = tm or _tile(m, TM, 16)
    tn = tn or _tile(n, TN)
    tk = tk or _tile(k, TK)
    assert m % tm == 0 and n % tn == 0 and k % tk == 0 and b_noff % tn == 0
    nk = k // tk
    noff = b_noff // tn
    if b_outer:
        grid = (n // tn, m // tm, nk)
        ij = lambda g0, g1: (g1, g0)
    else:
        grid = (m // tm, n // tn, nk)
        ij = lambda g0, g1: (g0, g1)

    a_spec = pl.BlockSpec((tm, tk), lambda g0, g1, kk: (ij(g0, g1)[0], kk))
    if nt:
        b_spec = _slot_spec(b.shape, b_kind, tn, tk, lambda g0, g1, kk: (ij(g0, g1)[1] + noff, kk))
    else:
        b_spec = _slot_spec(b.shape, b_kind, tk, tn, lambda g0, g1, kk: (kk, ij(g0, g1)[1] + noff))
    o_spec = _slot_spec(out_shape, o_kind, tm, tn, lambda g0, g1, kk: ij(g0, g1))
    in_specs = [a_spec, b_spec]
    args = [a, b]
    if res is not None:
        in_specs.append(pl.BlockSpec((tm, tn), lambda g0, g1, kk: ij(g0, g1)))
        args.append(res)

    def body(*refs):
        a_ref, b_ref = refs[0], refs[1]
        r_ref = refs[2] if res is not None else None
        o_ref = refs[3] if res is not None else refs[2]

        def prod():
            return _dot_nt(a_ref[...], b_ref[...]) if nt else _dot(a_ref[...], b_ref[...])

        def finish(acc):
            if r_ref is not None:
                acc = acc + r_ref[...].astype(F32)
            o_ref[...] = acc.astype(o_ref.dtype)

        if nk == 1:
            finish(prod())
            return
        acc_ref = refs[-1]
        kk = pl.program_id(2)

        @pl.when(kk == 0)
        def _():
            acc_ref[...] = prod()

        @pl.when(jnp.logical_and(kk > 0, kk < nk - 1))
        def _():
            acc_ref[...] += prod()

        @pl.when(kk == nk - 1)
        def _():
            finish(acc_ref[...] + prod())

    return _pcall(
        body, args, jobs, name=name, grid=grid, in_specs=in_specs, out_specs=o_spec,
        out_shape=jax.ShapeDtypeStruct(out_shape, out_dtype),
        scratch_shapes=[pltpu.VMEM((tm, tn), F32)] if nk > 1 else [],
        sem=("parallel", "parallel", "arbitrary"))


def _rms_fwd(name, x, g):
    s, d = x.shape
    tr = _tile(s, TR_EW, 16)

    def body(x_ref, g_ref, h_ref):
        xf = x_ref[...]
        rstd = lax.rsqrt(jnp.mean(xf * xf, axis=-1, keepdims=True) + EPS)
        h_ref[...] = (xf * rstd * g_ref[...]).astype(h_ref.dtype)

    return pl.pallas_call(
        body, name=name, grid=(s // tr,),
        in_specs=[pl.BlockSpec((tr, d), lambda i: (i, 0)), pl.BlockSpec((1, d), lambda i: (0, 0))],
        out_specs=pl.BlockSpec((tr, d), lambda i: (i, 0)),
        out_shape=jax.ShapeDtypeStruct((s, d), BF16),
        compiler_params=_cparams(("parallel",)),
    )(x, g)


def _rms_bwd_rows(dh, xf, g):
    rstd = lax.rsqrt(jnp.mean(xf * xf, axis=-1, keepdims=True) + EPS)
    xhat = xf * rstd
    dxhat = dh * g
    dx = rstd * (dxhat - xhat * jnp.mean(dxhat * xhat, axis=-1, keepdims=True))
    return dx, dh * xhat


def _fold8(v):
    return jnp.sum(v.reshape(v.shape[0] // 8, 8, v.shape[1]), axis=0)


def _rms_bwd(name, dh, x, g, dres, jobs=()):
    s, d = x.shape
    tr = _tile(s, TR_EW, 16)
    n = s // tr

    def body(dh_ref, x_ref, g_ref, r_ref, dx_ref, dxb_ref, dg_ref, acc_ref):
        i = pl.program_id(0)
        dx, dgrows = _rms_bwd_rows(dh_ref[...].astype(F32), x_ref[...], g_ref[...])
        dx = dx + r_ref[...]
        dx_ref[...] = dx
        dxb_ref[...] = dx.astype(BF16)

        @pl.when(i == 0)
        def _():
            acc_ref[...] = jnp.zeros_like(acc_ref)
        acc_ref[...] += _fold8(dgrows)

        @pl.when(i == n - 1)
        def _():
            dg_ref[...] = jnp.sum(acc_ref[...], axis=0, keepdims=True)

    row = pl.BlockSpec((tr, d), lambda i: (i, 0))
    vec = pl.BlockSpec((1, d), lambda i: (0, 0))
    return _pcall(
        body, (dh, x, g, dres), jobs, name=name, grid=(n,), in_specs=[row, row, vec, row], out_specs=[row, row, vec],
        out_shape=[jax.ShapeDtypeStruct((s, d), F32), jax.ShapeDtypeStruct((s, d), BF16),
                   jax.ShapeDtypeStruct((1, d), F32)],
        scratch_shapes=[pltpu.VMEM((8, d), F32)],
        sem=("arbitrary",))


def _loss_head(x2, target, g):
    s, d = x2.shape
    tr = _tile(s, TR_EW, 16)
    n = s // tr

    def body(x_ref, t_ref, g_ref, dx_ref, dxb_ref, dg_ref, loss_ref, acc_ref):
        i = pl.program_id(0)
        xf = x_ref[...]
        gv = g_ref[...]
        rstd = lax.rsqrt(jnp.mean(xf * xf, axis=-1, keepdims=True) + EPS)
        err = xf * rstd * gv - t_ref[...]
        row_loss = jnp.mean(err * err, axis=-1, keepdims=True)
        dx, dgrows = _rms_bwd_rows(err / d, xf, gv)
        dx_ref[...] = dx
        dxb_ref[...] = dx.astype(BF16)

        @pl.when(i == 0)
        def _():
            acc_ref[...] = jnp.zeros_like(acc_ref)
            loss_ref[...] = jnp.zeros_like(loss_ref)
        acc_ref[...] += _fold8(dgrows)
        loss_ref[...] += jnp.broadcast_to(0.5 * jnp.sum(row_loss, axis=0, keepdims=True), (8, LANE))

        @pl.when(i == n - 1)
        def _():
            dg_ref[...] = jnp.sum(acc_ref[...], axis=0, keepdims=True)

    row = pl.BlockSpec((tr, d), lambda i: (i, 0))
    vec = pl.BlockSpec((1, d), lambda i: (0, 0))
    return pl.pallas_call(
        body, name="loss_head", grid=(n,), in_specs=[row, row, vec],
        out_specs=[row, row, vec, pl.BlockSpec((8, LANE), lambda i: (0, 0))],
        out_shape=[jax.ShapeDtypeStruct((s, d), F32), jax.ShapeDtypeStruct((s, d), BF16),
                   jax.ShapeDtypeStruct((1, d), F32), jax.ShapeDtypeStruct((8, LANE), F32)],
        scratch_shapes=[pltpu.VMEM((8, d), F32)],
        compiler_params=_cparams(("arbitrary",)),
    )(x2, target, g)


def _gate_fwd(ya, yb, proj_g):
    s, d = ya.shape
    tr = _tile(s, TR_EW, 16)
    tc = _tile(d, 1024)
    nc = d // tc

    def body(ya_ref, yb_ref, ga_ref, gb_ref, o_ref):
        o_ref[...] = (_sigmoid(ga_ref[...]) * ya_ref[...] + _sigmoid(gb_ref[...]) * yb_ref[...]).astype(o_ref.dtype)

    blk = pl.BlockSpec((tr, tc), lambda i, j: (i, j))
    return pl.pallas_call(
        body, name="gate_fwd", grid=(s // tr, nc),
        in_specs=[blk, blk, blk, pl.BlockSpec((tr, tc), lambda i, j: (i, j + nc))],
        out_specs=blk, out_shape=jax.ShapeDtypeStruct((s, d), BF16),
        compiler_params=_cparams(("parallel", "parallel")),
    )(ya, yb, proj_g, proj_g)


def _gate_bwd(dmixed, ya, yb, proj_g):
    s, d = ya.shape
    tr = _tile(s, TR_EW, 16)
    tc = _tile(d, 1024)
    nc = d // tc

    def body(dm_ref, ya_ref, yb_ref, ga_ref, gb_ref, dya_ref, dyb_ref, dga_ref, dgb_ref):
        dm = dm_ref[...].astype(F32)
        sa = _sigmoid(ga_ref[...])
        sb = _sigmoid(gb_ref[...])
        dya_ref[...] = (dm * sa).astype(BF16)
        dyb_ref[...] = (dm * sb).astype(BF16)
        dga_ref[...] = (dm * ya_ref[...] * sa * (1.0 - sa)).astype(BF16)
        dgb_ref[...] = (dm * yb_ref[...] * sb * (1.0 - sb)).astype(BF16)

    blk = pl.BlockSpec((tr, tc), lambda i, j: (i, j))
    out = jax.ShapeDtypeStruct((s, d), BF16)
    return pl.pallas_call(
        body, name="gate_bwd", grid=(s // tr, nc),
        in_specs=[blk, blk, blk, blk, pl.BlockSpec((tr, tc), lambda i, j: (i, j + nc))],
        out_specs=[blk, blk, blk, blk], out_shape=[out, out, out, out],
        compiler_params=_cparams(("parallel", "parallel")),
    )(dmixed, ya, yb, proj_g, proj_g)


def _swiglu_fwd(gu, jobs=()):
    s = gu.shape[0]
    tr = _tile(s, 128, 16)

    def body(gu_ref, h_ref):
        gate = gu_ref[:, :FF_P].astype(F32)
        up = gu_ref[:, FF_P:].astype(F32)
        h_ref[...] = (gate * _sigmoid(gate) * up).astype(h_ref.dtype)

    return _pcall(
        body, (gu,), jobs, name="swiglu_fwd", grid=(s // tr, N_CHIP),
        in_specs=[pl.BlockSpec((tr, 2 * FF_P), lambda i, j: (i, j))],
        out_specs=pl.BlockSpec((tr, FF_P), lambda i, j: (i, j)),
        out_shape=jax.ShapeDtypeStruct((s, FP), BF16),
        sem=("parallel", "parallel"))


def _swiglu_bwd(dhidden, gu):
    s = gu.shape[0]
    tr = _tile(s, 128, 16)

    def body(dh_ref, gu_ref, dgu_ref):
        gate = gu_ref[:, :FF_P].astype(F32)
        up = gu_ref[:, FF_P:].astype(F32)
        dh = dh_ref[...].astype(F32)
        sg = _sigmoid(gate)
        dgu_ref[:, :FF_P] = (dh * up * sg * (1.0 + gate * (1.0 - sg))).astype(BF16)
        dgu_ref[:, FF_P:] = (dh * gate * sg).astype(BF16)

    return pl.pallas_call(
        body, name="swiglu_bwd", grid=(s // tr, N_CHIP),
        in_specs=[pl.BlockSpec((tr, FF_P), lambda i, j: (i, j)), pl.BlockSpec((tr, 2 * FF_P), lambda i, j: (i, j))],
        out_specs=pl.BlockSpec((tr, 2 * FF_P), lambda i, j: (i, j)),
        out_shape=jax.ShapeDtypeStruct((s, 2 * FP), BF16),
        compiler_params=_cparams(("parallel", "parallel")),
    )(dhidden, gu)


def _tri(n, upper):
    r = lax.broadcasted_iota(jnp.int32, (n, n), 0)
    c = lax.broadcasted_iota(jnp.int32, (n, n), 1)
    return (c >= r if upper else c <= r).astype(F32)


def _forget_fwd(fa, bias):
    s = fa.shape[0]
    tb = _tile(s, 512, 8)

    def body(f_ref, b_ref, c_ref, carry_ref):
        i = pl.program_id(0)

        @pl.when(i == 0)
        def _():
            carry_ref[...] = jnp.zeros_like(carry_ref)
        z = f_ref[...] + b_ref[...]
        logf = jnp.minimum(z, 0.0) - jnp.log(1.0 + jnp.exp(-jnp.abs(z)))
        c = jnp.dot(_tri(tb, False), logf, precision=lax.Precision.HIGHEST, preferred_element_type=F32)
        c_ref[...] = c + carry_ref[0:1, :]
        carry_ref[...] = jnp.broadcast_to(c_ref[tb - 1:tb, :], carry_ref.shape)

    return pl.pallas_call(
        body, name="forget_fwd", grid=(s // tb,),
        in_specs=[pl.BlockSpec((tb, LANE), lambda i: (i, 0)), pl.BlockSpec((1, LANE), lambda i: (0, 0))],
        out_specs=pl.BlockSpec((tb, LANE), lambda i: (i, 0)),
        out_shape=jax.ShapeDtypeStruct((s, LANE), F32),
        scratch_shapes=[pltpu.VMEM((8, LANE), F32)],
        compiler_params=_cparams(("arbitrary",)),
    )(fa, bias)


def _forget_bwd(dc, fa, bias):
    s = fa.shape[0]
    tb = _tile(s, 512, 16)
    n = s // tb

    def body(dc_ref, f_ref, b_ref, df_ref, db_ref, carry_ref, acc_ref, tmp_ref):
        i = pl.program_id(0)

        @pl.when(i == 0)
        def _():
            carry_ref[...] = jnp.zeros_like(carry_ref)
            acc_ref[...] = jnp.zeros_like(acc_ref)
        dlogf = jnp.dot(_tri(tb, True), dc_ref[...], precision=lax.Precision.HIGHEST, preferred_element_type=F32)
        tmp_ref[...] = dlogf + carry_ref[0:1, :]
        carry_ref[...] = jnp.broadcast_to(tmp_ref[0:1, :], carry_ref.shape)
        df = tmp_ref[...] * _sigmoid(-(f_ref[...] + b_ref[...]))
        df_ref[...] = df.astype(BF16)
        acc_ref[...] += _fold8(df)

        @pl.when(i == n - 1)
        def _():
            db_ref[...] = jnp.sum(acc_ref[...], axis=0, keepdims=True)

    rev = pl.BlockSpec((tb, LANE), lambda i: (n - 1 - i, 0))
    vec = pl.BlockSpec((1, LANE), lambda i: (0, 0))
    return pl.pallas_call(
        body, name="forget_bwd", grid=(n,), in_specs=[rev, rev, vec], out_specs=[rev, vec],
        out_shape=[jax.ShapeDtypeStruct((s, LANE), BF16), jax.ShapeDtypeStruct((1, LANE), F32)],
        scratch_shapes=[pltpu.VMEM((8, LANE), F32), pltpu.VMEM((8, LANE), F32), pltpu.VMEM((tb, LANE), F32)],
        compiler_params=_cparams(("arbitrary",)),
    )(dc, fa, bias)


_A_SCALE = DH_A ** -0.5


def _causal(tq, tk, transposed):
    r = lax.broadcasted_iota(jnp.int32, (tq, tk), 0)
    c = lax.broadcasted_iota(jnp.int32, (tq, tk), 1)
    return c >= r if transposed else r >= c


def _fox_fwd(qkv, c_col, c_row, jobs=()):
    s = qkv.shape[0]
    t = _tile(s, TQ_A)
    n = s // t
    kb, vb = OFF_KA // DH_A, OFF_VA // DH_A

    def body(q_ref, k_ref, v_ref, cq_ref, ck_ref, o_ref, lse_ref, m_s, l_s, acc_s):
        i, j = pl.program_id(1), pl.program_id(2)

        @pl.when(j == 0)
        def _():
            m_s[...] = jnp.full_like(m_s, -jnp.inf)
            l_s[...] = jnp.zeros_like(l_s)
            acc_s[...] = jnp.zeros_like(acc_s)

        def step(diag):
            sc = _dot_nt(q_ref[...], k_ref[...]) * _A_SCALE + (cq_ref[:, 0:1] - ck_ref[...])
            if diag:
                sc = jnp.where(_causal(t, t, False), sc, -jnp.inf)
            m_prev = m_s[...]
            m_new = jnp.maximum(m_prev, jnp.max(sc, axis=1, keepdims=True))
            alpha = jnp.exp(m_prev - m_new)
            p = jnp.exp(sc - m_new[:, 0:1])
            l_s[...] = alpha * l_s[...] + jnp.sum(p, axis=1, keepdims=True)
            acc_s[...] = alpha * acc_s[...] + _dot(p.astype(BF16), v_ref[...])
            m_s[...] = m_new

        @pl.when(j < i)
        def _():
            step(False)

        @pl.when(j == i)
        def _():
            step(True)
            o_ref[...] = (acc_s[...] / l_s[...]).astype(o_ref.dtype)
            lse_ref[...] = m_s[...] + jnp.log(l_s[...])

    jc = lambda i, j: jnp.minimum(j, i)
    return _pcall(
        body, (qkv, qkv, qkv, c_col, c_row), jobs, name="fox_fwd", grid=(H_A, n, n),
        in_specs=[pl.BlockSpec((t, DH_A), lambda h, i, j: (i, h)),
                  pl.BlockSpec((t, DH_A), lambda h, i, j: (jc(i, j), kb + h)),
                  pl.BlockSpec((t, DH_A), lambda h, i, j: (jc(i, j), vb + h)),
                  pl.BlockSpec((None, t, LANE), lambda h, i, j: (h, i, 0)),
                  pl.BlockSpec((None, 1, t), lambda h, i, j: (h, 0, jc(i, j)))],
        out_specs=[pl.BlockSpec((t, DH_A), lambda h, i, j: (i, h)),
                   pl.BlockSpec((None, t, LANE), lambda h, i, j: (h, i, 0))],
        out_shape=[jax.ShapeDtypeStruct((s, W_A), BF16), jax.ShapeDtypeStruct((H_A, s, LANE), F32)],
        scratch_shapes=[pltpu.VMEM((t, LANE), F32)] * 3,
        sem=("parallel", "parallel", "arbitrary"))


def _fox_dq(qkv, do, c_col, c_row, lse_col, jobs=()):
    s = qkv.shape[0]
    t = _tile(s, TQ_A)
    n = s // t
    kb, vb = OFF_KA // DH_A, OFF_VA // DH_A

    def body(q_ref, k_ref, v_ref, do_ref, cq_ref, ck_ref, lse_ref, dq_ref, dl_ref, pdk_s, pk_s, dl_s):
        i, j = pl.program_id(1), pl.program_id(2)

        @pl.when(j == 0)
        def _():
            pdk_s[...] = jnp.zeros_like(pdk_s)
            pk_s[...] = jnp.zeros_like(pk_s)
            dl_s[...] = jnp.zeros_like(dl_s)

        def step(diag):
            sc = _dot_nt(q_ref[...], k_ref[...]) * _A_SCALE + (cq_ref[:, 0:1] - ck_ref[...])
            if diag:
                sc = jnp.where(_causal(t, t, False), sc, -jnp.inf)
            p = jnp.exp(sc - lse_ref[:, 0:1])
            pdp = p * _dot_nt(do_ref[...], v_ref[...])
            dl_s[...] += jnp.sum(pdp, axis=1, keepdims=True)
            pdk_s[...] += _dot(pdp.astype(BF16), k_ref[...])
            pk_s[...] += _dot(p.astype(BF16), k_ref[...])

        @pl.when(j < i)
        def _():
            step(False)

        @pl.when(j == i)
        def _():
            step(True)
            dq_ref[...] = ((pdk_s[...] - dl_s[...] * pk_s[...]) * _A_SCALE).astype(dq_ref.dtype)
            dl_ref[...] = dl_s[...]

    jc = lambda i, j: jnp.minimum(j, i)
    col = pl.BlockSpec((None, t, LANE), lambda h, i, j: (h, i, 0))
    return _pcall(
        body, (qkv, qkv, qkv, do, c_col, c_row, lse_col), jobs, name="fox_dq", grid=(H_A, n, n),
        in_specs=[pl.BlockSpec((t, DH_A), lambda h, i, j: (i, h)),
                  pl.BlockSpec((t, DH_A), lambda h, i, j: (jc(i, j), kb + h)),
                  pl.BlockSpec((t, DH_A), lambda h, i, j: (jc(i, j), vb + h)),
                  pl.BlockSpec((t, DH_A), lambda h, i, j: (i, h)),
                  col,
                  pl.BlockSpec((None, 1, t), lambda h, i, j: (h, 0, jc(i, j))),
                  col],
        out_specs=[pl.BlockSpec((t, DH_A), lambda h, i, j: (i, h)), col],
        out_shape=[jax.ShapeDtypeStruct((s, W_A), BF16), jax.ShapeDtypeStruct((H_A, s, LANE), F32)],
        scratch_shapes=[pltpu.VMEM((t, DH_A), F32), pltpu.VMEM((t, DH_A), F32), pltpu.VMEM((t, LANE), F32)],
        sem=("parallel", "parallel", "arbitrary"))


def _fox_dkv(qkv, do, c_col, c_row, lse_row, delta_row, jobs=()):
    s = qkv.shape[0]
    t = _tile(s, TQ_A)
    n = s // t
    kb, vb = OFF_KA // DH_A, OFF_VA // DH_A

    def body(q_ref, k_ref, v_ref, do_ref, cq_ref, ck_ref, lse_ref, dl_ref, dk_ref, dv_ref, dc_ref,
             dk_s, dv_s, dc_s):
        j, i = pl.program_id(1), pl.program_id(2)

        @pl.when(i == 0)
        def _():
            dk_s[...] = jnp.zeros_like(dk_s)
            dv_s[...] = jnp.zeros_like(dv_s)
            dc_s[...] = jnp.zeros_like(dc_s)

        def step(diag):
            st = _dot_nt(k_ref[...], q_ref[...]) * _A_SCALE + (cq_ref[...] - ck_ref[:, 0:1])
            if diag:
                st = jnp.where(_causal(t, t, True), st, -jnp.inf)
            pt = jnp.exp(st - lse_ref[...])
            dv_s[...] += _dot(pt.astype(BF16), do_ref[...])
            dpt = _dot_nt(v_ref[...], do_ref[...])
            dst = pt * (dpt - dl_ref[...])
            dk_s[...] += _dot(dst.astype(BF16), q_ref[...])
            dc_s[...] -= jnp.sum(dst, axis=1, keepdims=True)

        @pl.when(i == j)
        def _():
            step(True)

        @pl.when(i > j)
        def _():
            step(False)

        @pl.when(i == n - 1)
        def _():
            dk_ref[...] = (dk_s[...] * _A_SCALE).astype(dk_ref.dtype)
            dv_ref[...] = dv_s[...].astype(dv_ref.dtype)
            dc_ref[...] = dc_s[...]

    ic = lambda j, i: jnp.maximum(i, j)
    rowq = pl.BlockSpec((None, 1, t), lambda h, j, i: (h, 0, ic(j, i)))
    kv_out = pl.BlockSpec((t, DH_A), lambda h, j, i: (j, h))
    return _pcall(
        body, (qkv, qkv, qkv, do, c_row, c_col, lse_row, delta_row), jobs, name="fox_dkv", grid=(H_A, n, n),
        in_specs=[pl.BlockSpec((t, DH_A), lambda h, j, i: (ic(j, i), h)),
                  pl.BlockSpec((t, DH_A), lambda h, j, i: (j, kb + h)),
                  pl.BlockSpec((t, DH_A), lambda h, j, i: (j, vb + h)),
                  pl.BlockSpec((t, DH_A), lambda h, j, i: (ic(j, i), h)),
                  rowq,
                  pl.BlockSpec((None, t, LANE), lambda h, j, i: (h, j, 0)),
                  rowq, rowq],
        out_specs=[kv_out, kv_out, pl.BlockSpec((None, t, LANE), lambda h, j, i: (h, j, 0))],
        out_shape=[jax.ShapeDtypeStruct((s, W_A), BF16), jax.ShapeDtypeStruct((s, W_A), BF16),
                   jax.ShapeDtypeStruct((H_A, s, LANE), F32)],
        scratch_shapes=[pltpu.VMEM((t, DH_A), F32), pltpu.VMEM((t, DH_A), F32), pltpu.VMEM((t, LANE), F32)],
        sem=("parallel", "parallel", "arbitrary"))


_B_SCALE = DH_B ** -0.5
_HALF = LANE // 2


def _t5_bucket_table():
    ql = np.arange(BLOCK)[:, None]
    kl = np.arange(2 * BLOCK)[None, :]
    dist = np.clip(ql + BLOCK - kl, 0, None)
    max_exact = NUM_BUCKETS // 2
    large = max_exact + (np.log(np.maximum(dist, 1) / max_exact) / np.log(MAX_DISTANCE / max_exact)
                         * (NUM_BUCKETS - max_exact)).astype(np.int64)
    large = np.minimum(large, NUM_BUCKETS - 1)
    return np.where(dist < max_exact, dist, large).astype(np.int32).reshape(1, BLOCK * 2 * BLOCK)


def _one_hot_buckets(bucket_ref, n):
    return (lax.broadcasted_iota(jnp.int32, (NUM_BUCKETS, n), 0) == bucket_ref[...]).astype(F32)


def _bias_table(rel_bias_t, bucket):
    nqk = bucket.shape[1]
    tc = _tile(nqk, 4096)

    def body(rb_ref, bk_ref, o_ref):
        o_ref[...] = jnp.dot(rb_ref[...], _one_hot_buckets(bk_ref, tc), precision=lax.Precision.HIGHEST,
                             preferred_element_type=F32)

    return pl.pallas_call(
        body, name="bias_table", grid=(nqk // tc,),
        in_specs=[pl.BlockSpec((H_B, NUM_BUCKETS), lambda i: (0, 0)), pl.BlockSpec((1, tc), lambda i: (0, i))],
        out_specs=pl.BlockSpec((H_B, tc), lambda i: (0, i)),
        out_shape=jax.ShapeDtypeStruct((H_B, nqk), F32),
        compiler_params=_cparams(("parallel",)),
    )(rel_bias_t, bucket)


def _bias_table_bwd(dbias, bucket):
    nqk = bucket.shape[1]
    tc = _tile(nqk, 4096)
    n = nqk // tc

    def body(db_ref, bk_ref, o_ref):
        i = pl.program_id(0)

        @pl.when(i == 0)
        def _():
            o_ref[...] = jnp.zeros_like(o_ref)
        o_ref[...] += lax.dot_general(db_ref[...], _one_hot_buckets(bk_ref, tc), (((1,), (1,)), ((), ())),
                                      precision=lax.Precision.HIGHEST, preferred_element_type=F32)

    return pl.pallas_call(
        body, name="bias_table_bwd", grid=(n,),
        in_specs=[pl.BlockSpec((H_B, tc), lambda i: (0, i)), pl.BlockSpec((1, tc), lambda i: (0, i))],
        out_specs=pl.BlockSpec((H_B, NUM_BUCKETS), lambda i: (0, 0)),
        out_shape=jax.ShapeDtypeStruct((H_B, NUM_BUCKETS), F32),
        compiler_params=_cparams(("arbitrary",)),
    )(dbias, bucket)


def _lane_lo():
    return lax.broadcasted_iota(jnp.int32, (1, LANE), 1) < _HALF


def _dup_kv_head(cat, hk):
    xcol = cat[:, (hk // 2) * LANE:(hk // 2 + 1) * LANE].astype(F32)
    swapped = pltpu.roll(xcol, _HALF, 1)
    lo = _lane_lo()
    return (jnp.where(lo, xcol, swapped) if hk % 2 == 0 else jnp.where(lo, swapped, xcol)).astype(BF16)


def _band_mask(first_block):
    ql = lax.broadcasted_iota(jnp.int32, (BLOCK, 2 * BLOCK), 0)
    kl = lax.broadcasted_iota(jnp.int32, (BLOCK, 2 * BLOCK), 1)
    dist = ql + BLOCK - kl
    ok = jnp.logical_and(dist >= 0, dist < WINDOW)
    return jnp.logical_and(ok, jnp.logical_or(jnp.logical_not(first_block), kl >= BLOCK))


def _swa_probs(qh, kdup, bias_h, sink, mask):
    sc = _dot_nt(qh, kdup) * _B_SCALE + bias_h
    sc = jnp.where(mask, sc, -jnp.inf)
    m = jnp.maximum(jnp.max(sc, axis=1, keepdims=True), sink)
    p = jnp.exp(sc - m)
    e_sink = jnp.exp(sink - m)
    inv = 1.0 / (jnp.sum(p, axis=1, keepdims=True) + e_sink)
    return p * inv, e_sink * inv


def _split_pair(ref, col):
    x = ref[:, col * LANE:(col + 1) * LANE].astype(F32)
    lo = _lane_lo()
    return jnp.where(lo, x, 0.0).astype(BF16), jnp.where(lo, 0.0, x).astype(BF16)


def _swa_fwd(qkv, bias, sinks, jobs=()):
    s = qkv.shape[0]
    nb = s // BLOCK
    qb, kb, vb = OFF_QB // W_QB, OFF_KB // W_KB, OFF_VB // W_KB
    assert OFF_QB % W_QB == 0 and OFF_KB % W_KB == 0 and OFF_VB % W_KB == 0 and W_KB % LANE == 0 and G_B % 2 == 0

    def body(q_ref, kp_ref, kc_ref, vp_ref, vc_ref, bias_ref, sink_ref, o_ref):
        i = pl.program_id(0)
        mask = _band_mask(i == 0)
        kcat = jnp.concatenate([kp_ref[...], kc_ref[...]], axis=0)
        vcat = jnp.concatenate([vp_ref[...], vc_ref[...]], axis=0)
        lo = _lane_lo()
        for hk in range(HKV_B):
            kdup, vdup = _dup_kv_head(kcat, hk), _dup_kv_head(vcat, hk)
            for pr in range(G_B // 2):
                h0 = hk * G_B + 2 * pr
                q0, q1 = _split_pair(q_ref, h0 // 2)
                p0, _ = _swa_probs(q0, kdup, bias_ref[h0], sink_ref[0, h0], mask)
                p1, _ = _swa_probs(q1, kdup, bias_ref[h0 + 1], sink_ref[0, h0 + 1], mask)
                o0 = _dot(p0.astype(BF16), vdup)
                o1 = _dot(p1.astype(BF16), vdup)
                o_ref[:, (h0 // 2) * LANE:(h0 // 2 + 1) * LANE] = jnp.where(lo, o0, o1).astype(o_ref.dtype)

    prev = lambda i: jnp.maximum(i - 1, 0)
    return _pcall(
        body, (qkv, qkv, qkv, qkv, qkv, bias, sinks), jobs, name="swa_fwd", grid=(nb,),
        in_specs=[pl.BlockSpec((BLOCK, W_QB), lambda i: (i, qb)),
                  pl.BlockSpec((BLOCK, W_KB), lambda i: (prev(i), kb)),
                  pl.BlockSpec((BLOCK, W_KB), lambda i: (i, kb)),
                  pl.BlockSpec((BLOCK, W_KB), lambda i: (prev(i), vb)),
                  pl.BlockSpec((BLOCK, W_KB), lambda i: (i, vb)),
                  pl.BlockSpec((H_B, BLOCK, 2 * BLOCK), lambda i: (0, 0, 0)),
                  pl.BlockSpec(memory_space=pltpu.SMEM)],
        out_specs=pl.BlockSpec((BLOCK, W_QB), lambda i: (i, 0)),
        out_shape=jax.ShapeDtypeStruct((s, W_QB), BF16),
        sem=("parallel",))


def _swa_bwd(qkv, do, bias, sinks, jobs=()):
    s = qkv.shape[0]
    nb = s // BLOCK
    qb, kb, vb = OFF_QB // W_QB, OFF_KB // W_KB, OFF_VB // W_KB

    def body(q_ref, kp_ref, kc_ref, vp_ref, vc_ref, do_ref, bias_ref, sink_ref,
             dq_ref, dk_ref, dv_ref, dbias_ref, dsink_ref, carry_k, carry_v):
        i = pl.program_id(0)
        lo = _lane_lo()

        @pl.when(i == 0)
        def _():
            dbias_ref[...] = jnp.zeros_like(dbias_ref)
            dsink_ref[...] = jnp.zeros_like(dsink_ref)
            carry_k[...] = jnp.zeros_like(carry_k)
            carry_v[...] = jnp.zeros_like(carry_v)

        @pl.when(i < nb)
        def _():
            mask = _band_mask(i == 0)
            kcat = jnp.concatenate([kp_ref[...], kc_ref[...]], axis=0)
            vcat = jnp.concatenate([vp_ref[...], vc_ref[...]], axis=0)
            lane = lax.broadcasted_iota(jnp.int32, (1, LANE), 1)
            dsink = jnp.zeros((1, LANE), F32)
            dk_cols = [jnp.zeros((2 * BLOCK, LANE), F32) for _ in range(W_KB // LANE)]
            dv_cols = [jnp.zeros((2 * BLOCK, LANE), F32) for _ in range(W_KB // LANE)]
            for hk in range(HKV_B):
                kdup, vdup = _dup_kv_head(kcat, hk), _dup_kv_head(vcat, hk)
                dk_acc = jnp.zeros((2 * BLOCK, LANE), F32)
                dv_acc = jnp.zeros((2 * BLOCK, LANE), F32)
                for pr in range(G_B // 2):
                    h0 = hk * G_B + 2 * pr
                    col = h0 // 2
                    qs = _split_pair(q_ref, col)
                    dos = _split_pair(do_ref, col)
                    dqs = []
                    for e in range(2):
                        h = h0 + e
                        p, p_sink = _swa_probs(qs[e], kdup, bias_ref[h], sink_ref[0, h], mask)
                        dp = _dot_nt(dos[e], vdup)
                        delta = jnp.sum(p * dp, axis=1, keepdims=True)
                        ds = p * (dp - delta)
                        dbias_ref[h] += ds
                        dsink = dsink - jnp.where(lane == h, jnp.sum(p_sink * delta, axis=0, keepdims=True), 0.0)
                        dqs.append(_dot(ds.astype(BF16), kdup))
                        dk_acc = dk_acc + _dot(ds.T.astype(BF16), qs[e])
                        dv_acc = dv_acc + _dot(p.T.astype(BF16), dos[e])
                    dq_ref[:, col * LANE:(col + 1) * LANE] = (jnp.where(lo, dqs[0], dqs[1]) * _B_SCALE).astype(dq_ref.dtype)
                dk_tot = (dk_acc + pltpu.roll(dk_acc, _HALF, 1)) * _B_SCALE
                dv_tot = dv_acc + pltpu.roll(dv_acc, _HALF, 1)
                mine = lo if hk % 2 == 0 else jnp.logical_not(lo)
                dk_cols[hk // 2] = jnp.where(mine, dk_tot, dk_cols[hk // 2])
                dv_cols[hk // 2] = jnp.where(mine, dv_tot, dv_cols[hk // 2])
            dsink_ref[...] += dsink
            dk_cat = jnp.concatenate(dk_cols, axis=1)
            dv_cat = jnp.concatenate(dv_cols, axis=1)
            dk_ref[...] = (carry_k[...] + dk_cat[:BLOCK]).astype(dk_ref.dtype)
            dv_ref[...] = (carry_v[...] + dv_cat[:BLOCK]).astype(dv_ref.dtype)
            carry_k[...] = dk_cat[BLOCK:]
            carry_v[...] = dv_cat[BLOCK:]

        @pl.when(i == nb)
        def _():
            dk_ref[...] = carry_k[...].astype(dk_ref.dtype)
            dv_ref[...] = carry_v[...].astype(dv_ref.dtype)

    cur = lambda i: jnp.minimum(i, nb - 1)
    prev = lambda i: jnp.clip(i - 1, 0, nb - 1)
    kv_out = pl.BlockSpec((BLOCK, W_KB), lambda i: (prev(i), 0))
    return _pcall(
        body, (qkv, qkv, qkv, qkv, qkv, do, bias, sinks), jobs, name="swa_bwd", grid=(nb + 1,),
        in_specs=[pl.BlockSpec((BLOCK, W_QB), lambda i: (cur(i), qb)),
                  pl.BlockSpec((BLOCK, W_KB), lambda i: (prev(i), kb)),
                  pl.BlockSpec((BLOCK, W_KB), lambda i: (cur(i), kb)),
                  pl.BlockSpec((BLOCK, W_KB), lambda i: (prev(i), vb)),
                  pl.BlockSpec((BLOCK, W_KB), lambda i: (cur(i), vb)),
                  pl.BlockSpec((BLOCK, W_QB), lambda i: (cur(i), 0)),
                  pl.BlockSpec((H_B, BLOCK, 2 * BLOCK), lambda i: (0, 0, 0)),
                  pl.BlockSpec(memory_space=pltpu.SMEM)],
        out_specs=[pl.BlockSpec((BLOCK, W_QB), lambda i: (cur(i), 0)), kv_out, kv_out,
                   pl.BlockSpec((H_B, BLOCK, 2 * BLOCK), lambda i: (0, 0, 0)),
                   pl.BlockSpec((1, LANE), lambda i: (0, 0))],
        out_shape=[jax.ShapeDtypeStruct((s, W_QB), BF16), jax.ShapeDtypeStruct((s, W_KB), BF16),
                   jax.ShapeDtypeStruct((s, W_KB), BF16),
                   jax.ShapeDtypeStruct((H_B, BLOCK, 2 * BLOCK), F32), jax.ShapeDtypeStruct((1, LANE), F32)],
        scratch_shapes=[pltpu.VMEM((BLOCK, W_KB), F32), pltpu.VMEM((BLOCK, W_KB), F32)],
        sem=("arbitrary",))


_RELS = ((1, 0), (0, 1), (1, 1))


def _place():
    x, y, c = lax.axis_index("x"), lax.axis_index("y"), lax.axis_index("c")
    return x, y, c


def _rel_chip(x, y, rel):
    px = 1 - x if rel[0] else x
    py = 1 - y if rel[1] else y
    return px, py, 2 * px + py


def _half_rows(ref_shape, c):
    half = ref_shape[-2] // 2
    return pl.ds(pl.multiple_of(c * half, 16), half)


def _dma_sems(*shape):
    return pltpu.SemaphoreType.DMA(shape)


def _gather_ici_job(srcs):
    nt = len(srcs)

    def copies(src, dst, sems):
        send, recv, lsem = sems
        x, y, c = _place()
        me_chip = 2 * x + y
        local, sends, recvs = [], [], []
        for t in range(nt):
            ns = src[t].shape[0]
            rows = _half_rows(src[t].shape, c)
            local.append(pltpu.make_async_copy(src[t], dst[t].at[pl.ds(me_chip * ns, ns)], lsem.at[t]))
            for k, rel in enumerate(_RELS):
                px, py, chip = _rel_chip(x, y, rel)
                for into, lst in ((me_chip, sends), (chip, recvs)):
                    lst.append(pltpu.make_async_remote_copy(
                        src_ref=src[t].at[:, rows], dst_ref=dst[t].at[pl.ds(into * ns, ns), rows],
                        send_sem=send.at[t, k], recv_sem=recv.at[t, k], device_id=(px, py, c), device_id_type=MESH))
        return local, sends, recvs

    def start(src, dst, sems):
        local, sends, _ = copies(src, dst, sems)
        for cp in local + sends:
            cp.start()

    def wait(src, dst, sems):
        local, sends, recvs = copies(src, dst, sems)
        for cp in recvs:
            cp.wait_recv()
        for cp in sends:
            cp.wait_send()
        for cp in local:
            cp.wait()

    return _Job(srcs, [jax.ShapeDtypeStruct((N_CHIP * a.shape[0],) + a.shape[1:], a.dtype) for a in srcs],
                [_dma_sems(nt, 3), _dma_sems(nt, 3), _dma_sems(nt)], start, wait)


def _gather_d2d_job(gathered):
    nt = len(gathered)

    def copies(dst, sems):
        send, recv = sems
        x, y, c = _place()
        sends, recvs = [], []
        for t in range(nt):
            ns = dst[t].shape[0] // N_CHIP
            for k, rel in enumerate(_RELS):
                _, _, chip = _rel_chip(x, y, rel)
                for half, lst in ((c, sends), (1 - c, recvs)):
                    part = dst[t].at[pl.ds(chip * ns, ns), _half_rows(dst[t].shape, half)]
                    lst.append(pltpu.make_async_remote_copy(
                        src_ref=part, dst_ref=part, send_sem=send.at[t, k], recv_sem=recv.at[t, k],
                        device_id=(x, y, 1 - c), device_id_type=MESH))
        return sends, recvs

    def start(_, dst, sems):
        for cp in copies(dst, sems)[0]:
            cp.start()

    def wait(_, dst, sems):
        sends, recvs = copies(dst, sems)
        for cp in recvs:
            cp.wait_recv()
        for cp in sends:
            cp.wait_send()

    return _Job(gathered, [jax.ShapeDtypeStruct(a.shape, a.dtype) for a in gathered],
                [_dma_sems(nt, 3), _dma_sems(nt, 3)], start, wait, alias={t: t for t in range(nt)})


def _pair_job(grads):
    nt = len(grads)

    def copies(g, got, sems):
        send, recv = sems
        x, y, c = _place()
        return [pltpu.make_async_remote_copy(
            src_ref=g[t].at[:, _half_rows(g[t].shape, 1 - c)], dst_ref=got[t], send_sem=send.at[t],
            recv_sem=recv.at[t], device_id=(x, y, 1 - c), device_id_type=MESH) for t in range(nt)]

    def start(g, got, sems):
        for cp in copies(g, got, sems):
            cp.start()

    def wait(g, got, sems):
        for cp in copies(g, got, sems):
            cp.wait()

    return _Job(grads, [jax.ShapeDtypeStruct((a.shape[0], a.shape[1] // 2, a.shape[2]), a.dtype) for a in grads],
                [_dma_sems(nt), _dma_sems(nt)], start, wait)


def _ew_rows(rows, cols, itemsize=4):
    tr = 16
    while tr * 2 <= 128 and rows % (tr * 2) == 0 and tr * 2 * cols * itemsize <= (1 << 20):
        tr *= 2
    assert rows % tr == 0, (rows, tr)
    return tr


def _pair_sum(name, grad, got, place):
    ns, rows, cols = grad.shape
    half = rows // 2
    tr = _ew_rows(half, cols)
    nh = half // tr

    def body(p_ref, g_ref, r_ref, o_ref):
        o_ref[...] = (g_ref[...].astype(F32) + r_ref[...].astype(F32)).astype(o_ref.dtype)

    return pl.pallas_call(
        body, name=name,
        grid_spec=pltpu.PrefetchScalarGridSpec(
            num_scalar_prefetch=1, grid=(ns, nh),
            in_specs=[pl.BlockSpec((None, tr, cols), lambda s, i, p: (s, p[1] * nh + i, 0)),
                      pl.BlockSpec((None, tr, cols), lambda s, i, p: (s, i, 0))],
            out_specs=pl.BlockSpec((None, tr, cols), lambda s, i, p: (s, i, 0))),
        out_shape=jax.ShapeDtypeStruct((ns, half, cols), BF16),
        compiler_params=_cparams(("parallel", "parallel")),
    )(place, grad, got)


def _chip_job(psums):
    nt = len(psums)

    def copies(p, got, sems):
        send, recv = sems
        x, y, c = _place()
        cps = []
        for t in range(nt):
            ns = p[t].shape[0] // N_CHIP
            for k, rel in enumerate(_RELS):
                px, py, chip = _rel_chip(x, y, rel)
                cps.append(pltpu.make_async_remote_copy(
                    src_ref=p[t].at[pl.ds(chip * ns, ns)], dst_ref=got[t].at[k],
                    send_sem=send.at[t, k], recv_sem=recv.at[t, k], device_id=(px, py, c), device_id_type=MESH))
        return cps

    def start(p, got, sems):
        for cp in copies(p, got, sems):
            cp.start()

    def wait(p, got, sems):
        for cp in copies(p, got, sems):
            cp.wait()

    return _Job(psums, [jax.ShapeDtypeStruct((3, a.shape[0] // N_CHIP) + a.shape[1:], a.dtype) for a in psums],
                [_dma_sems(nt, 3), _dma_sems(nt, 3)], start, wait)


def _chip_sum(name, psum, got, place):
    ns4, half, cols = psum.shape
    ns = ns4 // N_CHIP
    tr = _ew_rows(half, cols)
    nh = half // tr

    def body(p_ref, mine_ref, got_ref, o_ref):
        acc = mine_ref[...].astype(F32)
        for k in range(3):
            acc = acc + got_ref[k].astype(F32)
        o_ref[...] = acc

    return pl.pallas_call(
        body, name=name,
        grid_spec=pltpu.PrefetchScalarGridSpec(
            num_scalar_prefetch=1, grid=(ns, nh),
            in_specs=[pl.BlockSpec((None, tr, cols), lambda s, i, p: (p[0] * ns + s, i, 0)),
                      pl.BlockSpec((3, None, tr, cols), lambda s, i, p: (0, s, i, 0))],
            out_specs=pl.BlockSpec((None, tr, cols), lambda s, i, p: (s, p[1] * nh + i, 0))),
        out_shape=jax.ShapeDtypeStruct((ns, 2 * half, cols), F32),
        compiler_params=_cparams(("parallel", "parallel")),
    )(place, psum, got)


def _share_job(halves):
    nt = len(halves)

    def copies(full, sems):
        send, recv = sems
        x, y, c = _place()
        sends, recvs = [], []
        for t in range(nt):
            for half, lst in ((c, sends), (1 - c, recvs)):
                part = full[t].at[:, _half_rows(full[t].shape, half)]
                lst.append(pltpu.make_async_remote_copy(
                    src_ref=part, dst_ref=part, send_sem=send.at[t], recv_sem=recv.at[t],
                    device_id=(x, y, 1 - c), device_id_type=MESH))
        return sends, recvs

    def start(_, full, sems):
        for cp in copies(full, sems)[0]:
            cp.start()

    def wait(_, full, sems):
        sends, recvs = copies(full, sems)
        for cp in sends:
            cp.wait_send()
        for cp in recvs:
            cp.wait_recv()

    return _Job(halves, [jax.ShapeDtypeStruct(a.shape, a.dtype) for a in halves],
                [_dma_sems(nt), _dma_sems(nt)], start, wait, alias={t: t for t in range(nt)})


def _small_all_reduce(pack):
    rows, d = pack.shape

    def body(x_ref, o_ref, land, send, recv):
        x, y, c = _place()
        me = 4 * x + 2 * y + c
        land[me] = x_ref[...]
        cps = []
        for k in range(1, 8):
            to = (1 - x if k & 4 else x, 1 - y if k & 2 else y, 1 - c if k & 1 else c)
            cps.append(pltpu.make_async_remote_copy(
                src_ref=x_ref, dst_ref=land.at[me], send_sem=send.at[k - 1], recv_sem=recv.at[k - 1],
                device_id=to, device_id_type=MESH))
        for cp in cps:
            cp.start()
        for cp in cps:
            cp.wait()
        acc = land[0]
        for dev in range(1, 8):
            acc = acc + land[dev]
        o_ref[...] = acc

    vm = pl.BlockSpec(memory_space=pltpu.VMEM)
    return pl.pallas_call(
        body, name="small_all_reduce", in_specs=[vm], out_specs=vm,
        out_shape=jax.ShapeDtypeStruct((rows, d), F32),
        scratch_shapes=[pltpu.VMEM((8, rows, d), F32), pltpu.SemaphoreType.DMA((7,)), pltpu.SemaphoreType.DMA((7,))],
    )(pack)


def _adamw(name, w, g, m, v, jobs=()):
    lead, rows, cols = w.shape
    fits = [t for t in range(8, rows + 1, 8) if rows % t == 0 and t * cols * 4 <= (1 << 20)]
    tr = fits[-1] if fits else rows
    tl = 1
    if tr == rows:
        tl = max(t for t in range(1, lead + 1) if lead % t == 0 and t * rows * cols * 4 <= (1 << 20))

    def body(w_ref, g_ref, m_ref, v_ref, d_ref, nm_ref, nv_ref):
        gv = g_ref[...]
        nm = ADAM_B1 * m_ref[...] + (1.0 - ADAM_B1) * gv
        nv = ADAM_B2 * v_ref[...] + (1.0 - ADAM_B2) * (gv * gv)
        m_hat = nm / (1.0 - ADAM_B1 ** ADAM_STEP)
        v_hat = nv / (1.0 - ADAM_B2 ** ADAM_STEP)
        d_ref[...] = -ADAM_LR * (m_hat / (jnp.sqrt(v_hat) + ADAM_EPS) + ADAM_WD * w_ref[...])
        nm_ref[...] = nm
        nv_ref[...] = nv

    blk = pl.BlockSpec((tl, tr, cols), lambda l, i: (l, i, 0))
    out = jax.ShapeDtypeStruct(w.shape, F32)
    return _pcall(
        body, (w, g, m, v), jobs, name=name, grid=(lead // tl, rows // tr), in_specs=[blk] * 4, out_specs=[blk] * 3,
        out_shape=[out, out, out], sem=("parallel", "parallel"))


def _pad_to(a, shape):
    return jnp.pad(a, [(0, t - s) for s, t in zip(a.shape, shape)])


def _pack_small(n1, n2, fg, bf, sk, rb, extra=None):
    rows = [n1.reshape(1, D_MODEL), n2.reshape(1, D_MODEL), fg.reshape(1, D_MODEL),
            _pad_to(bf.reshape(1, H_A), (1, D_MODEL)), _pad_to(sk.reshape(1, H_B), (1, D_MODEL)),
            jnp.zeros((1, D_MODEL), F32) if extra is None else _pad_to(extra.reshape(1, 1), (1, D_MODEL)),
            _pad_to(rb.reshape(1, NUM_BUCKETS * H_B), (1, RB_ROWS * D_MODEL)).reshape(RB_ROWS, D_MODEL)]
    return _pad_to(jnp.concatenate(rows, axis=0), (PACK_ROWS, D_MODEL))


def _unpack_small(p):
    return (p[0:1], p[1:2], p[2], p[3:4, :H_A], p[4:5, :H_B],
            p[6:6 + RB_ROWS].reshape(-1)[:NUM_BUCKETS * H_B].reshape(NUM_BUCKETS, H_B))


def kernel(x, norm1_g, w_in, b_forget, attn_sinks, rel_bias, w_branch_a, w_branch_b, w_out, norm2_g, w_ffn_gate, w_ffn_up, w_ffn_down, final_g, loss_target, m_norm1_g, m_w_in, m_b_forget, m_attn_sinks, m_rel_bias, m_w_branch_a, m_w_branch_b, m_w_out, m_norm2_g, m_w_ffn_gate, m_w_ffn_up, m_w_ffn_down, m_final_g, v_norm1_g, v_w_in, v_b_forget, v_attn_sinks, v_rel_bias, v_w_branch_a, v_w_branch_b, v_w_out, v_norm2_g, v_w_ffn_gate, v_w_ffn_up, v_w_ffn_down, v_final_g):
    s, d = SEQ, D_MODEL
    assert x.shape == (1, s, d) and w_in.shape == (1, d, W_IN_SH)
    xs = x[0]
    place = jnp.stack([2 * lax.axis_index("x") + lax.axis_index("y"), lax.axis_index("c")]).astype(jnp.int32)

    w_gu_l = jnp.stack([_pad_to(w_ffn_gate[0], (d, FF_P)), _pad_to(w_ffn_up[0], (d, FF_P))]).astype(BF16)
    w_dn_l = _pad_to(w_ffn_down, (1, FF_P, d)).astype(BF16)
    j_in = _gather_ici_job([w_in.astype(BF16)])
    _comm_now("gather_w_in_ici", [j_in])
    j_in2 = _gather_d2d_job(j_in.out)
    _comm_now("gather_w_in_d2d", [j_in2])
    w_in4 = j_in2.out[0]
    j_abo = _gather_ici_job([w_branch_a.astype(BF16), w_branch_b.astype(BF16), w_out.astype(BF16)])
    j_dn = _gather_ici_job([w_dn_l])
    j_gu = _gather_ici_job([w_gu_l])
    w_full = jnp.concatenate([w_in4[j] for j in range(N_CHIP)], axis=1)
    n_qkv_a = 3 * W_A
    w_p = jnp.concatenate([w_full[:, :n_qkv_a], w_full[:, n_qkv_a + H_A:], w_full[:, n_qkv_a:n_qkv_a + H_A],
                           jnp.zeros((d, LANE - H_A), BF16)], axis=1)

    h1 = _rms_fwd("norm1_fwd", xs, norm1_g)
    qkv = _matmul("proj_qkv", h1, w_p, m=s, n=W_QKV, k=d, out_shape=(s, W_QKV), out_dtype=BF16, jobs=[j_abo])
    j_abo2 = _gather_d2d_job(j_abo.out)
    proj_g = _matmul("proj_gates", h1, w_p, m=s, n=2 * d, k=d, b_noff=OFF_GA, out_shape=(s, 2 * d), out_dtype=F32,
                     jobs=[j_abo2])
    w_a, w_b, w_o = j_abo2.out
    w_o = w_o.reshape(d, d)
    fa = _matmul("proj_forget", h1, w_p, m=s, n=LANE, k=d, b_noff=OFF_FA, tn=LANE, out_shape=(s, LANE), out_dtype=F32)
    b_f = _pad_to(b_forget, (1, LANE))
    c_cum = _forget_fwd(fa, b_f)
    c_t = c_cum[:, :H_A].T
    c_col = jnp.broadcast_to(c_t[:, :, None], (H_A, s, LANE))
    c_row = c_t[:, None, :]
    attn_a, lse_col = _fox_fwd(qkv, c_col, c_row, jobs=[j_gu])

    bucket = jnp.asarray(_t5_bucket_table())
    bias = _bias_table(rel_bias.T, bucket).reshape(H_B, BLOCK, 2 * BLOCK)
    j_gu2 = _gather_d2d_job(j_gu.out)
    attn_b = _swa_fwd(qkv, bias, attn_sinks, jobs=[j_gu2])
    w_gu = j_gu2.out[0]

    ya = _matmul("branch_a", attn_a, w_a, m=s, n=d, k=W_A, b_kind="col", tn=_tile(A_SH, TN),
                 out_shape=(s, d), out_dtype=F32)
    yb = _matmul("branch_b", attn_b, w_b, m=s, n=d, k=W_QB, b_kind="col", tn=_tile(A_SH, TN),
                 out_shape=(s, d), out_dtype=F32)
    mixed = _gate_fwd(ya, yb, proj_g)
    x1 = _matmul("out_proj", mixed, w_o, m=s, n=d, k=d, res=xs, out_shape=(s, d), out_dtype=F32)

    h2 = _rms_fwd("norm2_fwd", x1, norm2_g)
    gu = _matmul("ffn_gate_up", h2, w_gu, m=s, n=2 * FP, k=d, b_kind="col", tn=_tile(FF_P, 1408),
                 out_shape=(s, 2 * FP), out_dtype=BF16, jobs=[j_dn])
    j_dn2 = _gather_d2d_job(j_dn.out)
    hidden = _swiglu_fwd(gu, jobs=[j_dn2])
    w_dn = j_dn2.out[0].reshape(FP, d)
    x2 = _matmul("ffn_down", hidden, w_dn, m=s, n=d, k=FP, tk=_tile(FP, 5632), res=x1, out_shape=(s, d), out_dtype=F32)

    dx2, dx2_b, d_fg, loss_tile = _loss_head(x2, loss_target[0], final_g.reshape(1, d))
    dhidden = _matmul("ffn_down_dx", dx2_b, w_dn, m=s, n=FP, k=d, nt=True, tn=_tile(FF_P, 1408),
                      out_shape=(s, FP), out_dtype=BF16)
    dgu = _swiglu_bwd(dhidden, gu)
    g_dn = _matmul("ffn_down_dw", hidden.T, dx2_b, m=FP, n=d, k=s, tm=_tile(FF_P, 1408, 16),
                   out_shape=(FP, d), out_dtype=BF16).reshape(N_CHIP, FF_P, d)
    j_p_dn = _pair_job([g_dn])
    g_gu = _matmul("ffn_gate_up_dw", h2.T, dgu, m=d, n=2 * FP, k=s, o_kind="col", tm=_tile(d, 512, 16),
                   tn=_tile(FF_P, 1408), out_shape=(2 * N_CHIP, d, FF_P), out_dtype=BF16, jobs=[j_p_dn])
    ps_dn = _pair_sum("pair_sum_w_ffn_down", g_dn, j_p_dn.out[0], place)
    j_c_dn = _chip_job([ps_dn])
    j_p_gu = _pair_job([g_gu])
    dh2 = _matmul("ffn_gate_up_dx", dgu, w_gu, m=s, n=d, k=2 * FP, nt=True, b_kind="col", tn=_tile(d, 1024),
                  tk=_tile(FF_P, TK), out_shape=(s, d), out_dtype=F32, jobs=[j_c_dn, j_p_gu])
    h_dn = _chip_sum("chip_sum_w_ffn_down", ps_dn, j_c_dn.out[0], place)
    ps_gu = _pair_sum("pair_sum_w_ffn_gate_up", g_gu, j_p_gu.out[0], place)
    dx1, dx1_b, d_n2 = _rms_bwd("norm2_bwd", dh2, x1, norm2_g, dx2)

    dmixed = _matmul("out_proj_dx", dx1_b, w_o, m=s, n=d, k=d, nt=True, out_shape=(s, d), out_dtype=BF16)
    dya, dyb, dga, dgb = _gate_bwd(dmixed, ya, yb, proj_g)
    g_a = _matmul("branch_a_dw", attn_a.T, dya, m=W_A, n=d, k=s, o_kind="col", tn=_tile(A_SH, TN),
                  out_shape=(N_CHIP, W_A, A_SH), out_dtype=BF16)
    g_b = _matmul("branch_b_dw", attn_b.T, dyb, m=W_QB, n=d, k=s, o_kind="col", tn=_tile(A_SH, TN),
                  out_shape=(N_CHIP, W_QB, A_SH), out_dtype=BF16)
    j_p_ab = _pair_job([g_a, g_b])
    dattn_a = _matmul("branch_a_dx", dya, w_a, m=s, n=W_A, k=d, nt=True, b_kind="col", tn=_tile(W_A, 1024),
                      tk=_tile(A_SH, TK), out_shape=(s, W_A), out_dtype=BF16, jobs=[j_p_ab])
    dattn_b = _matmul("branch_b_dx", dyb, w_b, m=s, n=W_QB, k=d, nt=True, b_kind="col", tn=_tile(W_QB, 1024),
                      tk=_tile(A_SH, TK), out_shape=(s, W_QB), out_dtype=BF16)
    ps_a, ps_b = (_pair_sum("pair_sum_" + n, g, r, place) for n, g, r in
                  zip(("w_branch_a", "w_branch_b"), (g_a, g_b), j_p_ab.out))

    j_c_gu = _chip_job([ps_gu])
    dq_a, delta_col = _fox_dq(qkv, dattn_a, c_col, c_row, lse_col, jobs=[j_c_gu])
    lse_row = lse_col[:, :, 0][:, None, :]
    delta_row = delta_col[:, :, 0][:, None, :]
    j_c_ab = _chip_job([ps_a, ps_b])
    j_s_dn = _share_job([h_dn])
    dk_a, dv_a, dc_col = _fox_dkv(qkv, dattn_a, c_col, c_row, lse_row, delta_row, jobs=[j_c_ab, j_s_dn])
    dc = _pad_to(dc_col[:, :, 0].T, (s, LANE))
    df, d_bf = _forget_bwd(dc, fa, b_f)
    h_gu = _chip_sum("chip_sum_w_ffn_gate_up", ps_gu, j_c_gu.out[0], place)
    h_a, h_b = (_chip_sum("chip_sum_" + n, p, r, place) for n, p, r in
                zip(("w_branch_a", "w_branch_b"), (ps_a, ps_b), j_c_ab.out))

    j_s_rest = _share_job([h_gu, h_a, h_b])
    dq_b, dk_b, dv_b, dbias, d_sk = _swa_bwd(qkv, dattn_b, bias, attn_sinks, jobs=[j_s_rest])
    r_dn = j_s_dn.out[0]
    r_gu, r_a, r_b = j_s_rest.out
    d_rb = _bias_table_bwd(dbias.reshape(H_B, BLOCK * 2 * BLOCK), bucket).T

    dproj = jnp.concatenate([dq_a, dk_a, dv_a, dq_b, dk_b, dv_b, dga, dgb, df], axis=1)
    g_in_p = _matmul("in_proj_dw", h1.T, dproj, m=d, n=PW, k=s, tn=_tile(PW, 1024), out_shape=(d, PW), out_dtype=BF16)
    g_in_full = jnp.concatenate([g_in_p[:, :n_qkv_a], g_in_p[:, OFF_FA:OFF_FA + H_A], g_in_p[:, n_qkv_a:OFF_FA]], axis=1)
    g_in = jnp.stack([g_in_full[:, j * W_IN_SH:(j + 1) * W_IN_SH] for j in range(N_CHIP)])
    j_p_in = _pair_job([g_in])
    g_o = _matmul("out_proj_dw", mixed.T, dx1_b, m=d, n=d, k=s, out_shape=(d, d), out_dtype=BF16,
                  jobs=[j_p_in]).reshape(N_CHIP, A_SH, d)
    ps_in = _pair_sum("pair_sum_w_in", g_in, j_p_in.out[0], place)
    j_c_in = _chip_job([ps_in])
    j_p_o = _pair_job([g_o])
    dh1 = _matmul("in_proj_dx", dproj, w_p, m=s, n=d, k=PW, nt=True, tn=_tile(d, 1024), tk=_tile(PW, 2560),
                  out_shape=(s, d), out_dtype=F32, jobs=[j_c_in, j_p_o])
    h_in = _chip_sum("chip_sum_w_in", ps_in, j_c_in.out[0], place)
    ps_o = _pair_sum("pair_sum_w_out", g_o, j_p_o.out[0], place)
    j_c_o = _chip_job([ps_o])
    j_s_in = _share_job([h_in])
    grad_x, _, d_n1 = _rms_bwd("norm1_bwd", dh1, xs, norm1_g, dx1, jobs=[j_c_o, j_s_in])
    h_o = _chip_sum("chip_sum_w_out", ps_o, j_c_o.out[0], place)
    j_s_o = _share_job([h_o])

    small = _small_all_reduce(_pack_small(d_n1, d_n2, d_fg, d_bf[:, :H_A], d_sk[:, :H_B], d_rb, loss_tile[0:1, 0:1]))
    loss = small[5, 0]

    grads = {
        "w_branch_a": r_a, "w_branch_b": r_b,
        "w_ffn_gate": r_gu[0:1, :, :FF_SH], "w_ffn_up": r_gu[1:2, :, :FF_SH], "w_ffn_down": r_dn[:, :FF_SH, :],
    }
    given = dict(w_in=(w_in, m_w_in, v_w_in), w_branch_a=(w_branch_a, m_w_branch_a, v_w_branch_a),
                 w_branch_b=(w_branch_b, m_w_branch_b, v_w_branch_b), w_out=(w_out, m_w_out, v_w_out),
                 w_ffn_gate=(w_ffn_gate, m_w_ffn_gate, v_w_ffn_gate), w_ffn_up=(w_ffn_up, m_w_ffn_up, v_w_ffn_up),
                 w_ffn_down=(w_ffn_down, m_w_ffn_down, v_w_ffn_down))

    def col_major(a):
        return jnp.transpose(a[0])[None]

    views = {n: (lambda a: a, lambda a: a) for n in given}
    views["w_ffn_gate"] = views["w_ffn_up"] = (col_major, col_major)
    views["w_in"] = (lambda a: jnp.transpose(a, (2, 0, 1)).reshape(W_IN_SH, d // LANE, LANE),
                     lambda a: jnp.transpose(a.reshape(W_IN_SH, 1, d), (1, 2, 0)))

    def adamw(n, jobs=()):
        to_view, from_view = views[n]
        g_view = to_view(grads[n])
        outs = _adamw("adamw_" + n, to_view(given[n][0]), g_view, to_view(given[n][1]), to_view(given[n][2]), jobs=jobs)
        grads[n] = from_view(g_view)
        return [from_view(o) for o in outs]

    grads["w_in"] = j_s_in.out[0]
    upd = {n: adamw(n) for n in ("w_ffn_gate", "w_ffn_up", "w_ffn_down", "w_branch_a", "w_branch_b", "w_in")}
    sm = _adamw("adamw_small",
                _pack_small(norm1_g, norm2_g, final_g, b_forget, attn_sinks, rel_bias)[None],
                small.at[5].set(0.0)[None],
                _pack_small(m_norm1_g, m_norm2_g, m_final_g, m_b_forget, m_attn_sinks, m_rel_bias)[None],
                _pack_small(v_norm1_g, v_norm2_g, v_final_g, v_b_forget, v_attn_sinks, v_rel_bias)[None],
                jobs=[j_s_o])
    grads["w_out"] = j_s_o.out[0]
    upd["w_out"] = adamw("w_out")
    g_small = _unpack_small(small)
    d_small, m_small, v_small = (_unpack_small(t[0]) for t in sm)

    order = ["norm1_g", "w_in", "b_forget", "attn_sinks", "rel_bias", "w_branch_a", "w_branch_b", "w_out",
             "norm2_g", "w_ffn_gate", "w_ffn_up", "w_ffn_down", "final_g"]
    small_at = {"norm1_g": 0, "norm2_g": 1, "final_g": 2, "b_forget": 3, "attn_sinks": 4, "rel_bias": 5}

    def pick(big_idx, small_src, n):
        return small_src[small_at[n]] if n in small_at else (grads[n] if big_idx is None else upd[n][big_idx])

    return (loss, grad_x[None],
            *[pick(None, g_small, n) for n in order], *[pick(0, d_small, n) for n in order],
            *[pick(1, m_small, n) for n in order], *[pick(2, v_small, n) for n in order])
```

```python
import functools
import math

import numpy as np
import jax
import jax.numpy as jnp
from jax import lax
from jax.experimental import pallas as pl
from jax.experimental.pallas import tpu as pltpu

F32 = jnp.float32
BF16 = jnp.bfloat16
MESH = pl.DeviceIdType.MESH

D_MODEL = 4096
SEQ = 4096
H_A = 16
DH_A = 128
H_B = 32
HKV_B = 4
G_B = H_B // HKV_B
DH_B = 64
WINDOW = 128
NUM_BUCKETS = 32
MAX_DISTANCE = 128
BLOCK = 128
D_FF = ((8 * D_MODEL // 3 + 255) // 256) * 256
EPS = 1e-6
ADAM_LR = 0.001
ADAM_B1 = 0.9
ADAM_B2 = 0.999
ADAM_EPS = 1e-08
ADAM_WD = 0.01
ADAM_STEP = 10

N_CHIP = 4
LANE = 128
VMEM_LIMIT = 56 * 1024 * 1024

TM = 1024
TN = 512
TK = 4096
TQ_A = 512
TR_EW = 256

W_A = H_A * DH_A
W_QB = H_B * DH_B
W_KB = HKV_B * DH_B
OFF_QA = 0
OFF_KA = W_A
OFF_VA = 2 * W_A
OFF_QB = 3 * W_A
OFF_KB = OFF_QB + W_QB
OFF_VB = OFF_KB + W_KB
OFF_GA = OFF_VB + W_KB
OFF_GB = OFF_GA + D_MODEL
OFF_FA = OFF_GB + D_MODEL
PW = OFF_FA + LANE
W_QKV = OFF_GA
W_IN = 3 * W_A + H_A + W_QB + 2 * W_KB + 2 * D_MODEL
W_IN_SH = W_IN // N_CHIP
A_SH = D_MODEL // N_CHIP
FF_SH = D_FF // N_CHIP
FF_P = -(-FF_SH // LANE) * LANE
FP = N_CHIP * FF_P
PACK_ROWS = 16
RB_ROWS = -(-(NUM_BUCKETS * H_B) // D_MODEL)


def _tile(n, target, mult=LANE):
    t = min(target, n) // mult * mult
    while t > mult and n % t:
        t -= mult
    assert t > 0 and n % t == 0, (n, target, mult)
    return t


def _cparams(sem):
    return pltpu.CompilerParams(dimension_semantics=sem, vmem_limit_bytes=VMEM_LIMIT)


def _sigmoid(x):
    return 1.0 / (1.0 + jnp.exp(-x))


def _dot(a, b):
    return jnp.dot(a, b, preferred_element_type=F32)


def _dot_nt(a, b):
    return lax.dot_general(a, b, (((1,), (1,)), ((), ())), preferred_element_type=F32)


ANY = pl.BlockSpec(memory_space=pl.ANY)


class _Job:
    def __init__(self, ins, outs, sems, start, wait, alias=None):
        self.ins, self.outs, self.sems = list(ins), list(outs), list(sems)
        self.start, self.wait, self.alias = start, wait, dict(alias or {})
        self.out = None


def _split(seq, sizes):
    parts, p = [], 0
    for n in sizes:
        parts.append(seq[p:p + n])
        p += n
    return parts


def _pcall(body, args, jobs, *, name, grid, in_specs, out_specs, out_shape, scratch_shapes=(), sem):
    if not jobs:
        return pl.pallas_call(body, name=name, grid=grid, in_specs=in_specs, out_specs=out_specs, out_shape=out_shape,
                              scratch_shapes=list(scratch_shapes), compiler_params=_cparams(sem))(*args)
    single = not isinstance(out_shape, (list, tuple))
    out_specs_l = [out_specs] if single else list(out_specs)
    out_shape_l = [out_shape] if single else list(out_shape)
    cin = [a for j in jobs for a in j.ins]
    cout = [o for j in jobs for o in j.outs]
    csem = [s for j in jobs for s in j.sems]
    sizes = [len(args), len(cin), len(out_shape_l), len(cout), len(scratch_shapes), len(csem)]
    aliases, io, oo = {}, len(args), len(out_shape_l)
    for j in jobs:
        for a, b in j.alias.items():
            aliases[io + a] = oo + b
        io, oo = io + len(j.ins), oo + len(j.outs)

    def wrapped(*refs):
        ins, cins, outs, couts, scr, sems = _split(refs, sizes)
        ids = [pl.program_id(a) for a in range(len(grid))]
        first = functools.reduce(jnp.logical_and, [i == 0 for i in ids])
        last = functools.reduce(jnp.logical_and, [i == g - 1 for i, g in zip(ids, grid)])
        per_job = list(zip(jobs, _split(cins, [len(j.ins) for j in jobs]), _split(couts, [len(j.outs) for j in jobs]),
                           _split(sems, [len(j.sems) for j in jobs])))

        @pl.when(first)
        def _():
            for j, ji, jo, js in per_job:
                j.start(ji, jo, js)

        body(*ins, *outs, *scr)

        @pl.when(last)
        def _():
            for j, ji, jo, js in per_job:
                j.wait(ji, jo, js)

    res = pl.pallas_call(
        wrapped, name=name, grid=grid, in_specs=list(in_specs) + [ANY] * len(cin),
        out_specs=out_specs_l + [ANY] * len(cout), out_shape=out_shape_l + cout,
        scratch_shapes=list(scratch_shapes) + csem, input_output_aliases=aliases,
        compiler_params=_cparams(("arbitrary",) * len(grid)))(*args, *cin)
    main, rest = res[:len(out_shape_l)], res[len(out_shape_l):]
    for j, o in zip(jobs, _split(rest, [len(j.outs) for j in jobs])):
        j.out = list(o)
    return main[0] if single else list(main)


def _comm_now(name, jobs):
    cin = [a for j in jobs for a in j.ins]
    cout = [o for j in jobs for o in j.outs]
    csem = [s for j in jobs for s in j.sems]
    sizes = [len(cin), len(cout), len(csem)]
    aliases, io, oo = {}, 0, 0
    for j in jobs:
        for a, b in j.alias.items():
            aliases[io + a] = oo + b
        io, oo = io + len(j.ins), oo + len(j.outs)

    def body(*refs):
        cins, couts, sems = _split(refs, sizes)
        per_job = list(zip(jobs, _split(cins, [len(j.ins) for j in jobs]), _split(couts, [len(j.outs) for j in jobs]),
                           _split(sems, [len(j.sems) for j in jobs])))
        for j, ji, jo, js in per_job:
            j.start(ji, jo, js)
        for j, ji, jo, js in per_job:
            j.wait(ji, jo, js)

    res = pl.pallas_call(body, name=name, in_specs=[ANY] * len(cin), out_specs=[ANY] * len(cout), out_shape=cout,
                         scratch_shapes=csem, input_output_aliases=aliases)(*cin)
    for j, o in zip(jobs, _split(res, [len(j.outs) for j in jobs])):
        j.out = list(o)


def _slot_spec(shape, kind, br, bc, rc):
    if kind == "2d":
        return pl.BlockSpec((br, bc), lambda *g: rc(*g))
    if kind == "col":
        assert shape[2] % bc == 0, (shape, bc)
        per = shape[2] // bc

        def im_col(*g):
            rb, cb = rc(*g)
            return (cb // per, rb, cb % per)
        return pl.BlockSpec((None, br, bc), im_col)
    assert kind == "row" and shape[1] % br == 0, (shape, kind, br)
    per = shape[1] // br

    def im_row(*g):
        rb, cb = rc(*g)
        return (rb // per, rb % per, cb)
    return pl.BlockSpec((None, br, bc), im_row)


def _matmul(name, a, b, *, m, n, k, nt=False, b_kind="2d", o_kind="2d", out_shape, out_dtype,
            tm=None, tn=None, tk=None, b_noff=0, res=None, b_outer=False, jobs=()):
    tm = tm or _tile(m, TM, 16)
    tn = tn or _tile(n, TN)
    tk = tk or _tile(k, TK)
    assert m % tm == 0 and n % tn == 0 and k % tk == 0 and b_noff % tn == 0
    nk = k // tk
    noff = b_noff // tn
    if b_outer:
        grid = (n // tn, m // tm, nk)
        ij = lambda g0, g1: (g1, g0)
    else:
        grid = (m // tm, n // tn, nk)
        ij = lambda g0, g1: (g0, g1)

    a_spec = pl.BlockSpec((tm, tk), lambda g0, g1, kk: (ij(g0, g1)[0], kk))
    if nt:
        b_spec = _slot_spec(b.shape, b_kind, tn, tk, lambda g0, g1, kk: (ij(g0, g1)[1] + noff, kk))
    else:
        b_spec = _slot_spec(b.shape, b_kind, tk, tn, lambda g0, g1, kk: (kk, ij(g0, g1)[1] + noff))
    o_spec = _slot_spec(out_shape, o_kind, tm, tn, lambda g0, g1, kk: ij(g0, g1))
    in_specs = [a_spec, b_spec]
    args = [a, b]
    if res is not None:
        in_specs.append(pl.BlockSpec((tm, tn), lambda g0, g1, kk: ij(g0, g1)))
        args.append(res)

    def body(*refs):
        a_ref, b_ref = refs[0], refs[1]
        r_ref = refs[2] if res is not None else None
        o_ref = refs[3] if res is not None else refs[2]

        def prod():
            return _dot_nt(a_ref[...], b_ref[...]) if nt else _dot(a_ref[...], b_ref[...])

        def finish(acc):
            if r_ref is not None:
                acc = acc + r_ref[...].astype(F32)
            o_ref[...] = acc.astype(o_ref.dtype)

        if nk == 1:
            finish(prod())
            return
        acc_ref = refs[-1]
        kk = pl.program_id(2)

        @pl.when(kk == 0)
        def _():
            acc_ref[...] = prod()

        @pl.when(jnp.logical_and(kk > 0, kk < nk - 1))
        def _():
            acc_ref[...] += prod()

        @pl.when(kk == nk - 1)
        def _():
            finish(acc_ref[...] + prod())

    return _pcall(
        body, args, jobs, name=name, grid=grid, in_specs=in_specs, out_specs=o_spec,
        out_shape=jax.ShapeDtypeStruct(out_shape, out_dtype),
        scratch_shapes=[pltpu.VMEM((tm, tn), F32)] if nk > 1 else [],
        sem=("parallel", "parallel", "arbitrary"))


def _rms_fwd(name, x, g):
    s, d = x.shape
    tr = _tile(s, TR_EW, 16)

    def body(x_ref, g_ref, h_ref):
        xf = x_ref[...]
        rstd = lax.rsqrt(jnp.mean(xf * xf, axis=-1, keepdims=True) + EPS)
        h_ref[...] = (xf * rstd * g_ref[...]).astype(h_ref.dtype)

    return pl.pallas_call(
        body, name=name, grid=(s // tr,),
        in_specs=[pl.BlockSpec((tr, d), lambda i: (i, 0)), pl.BlockSpec((1, d), lambda i: (0, 0))],
        out_specs=pl.BlockSpec((tr, d), lambda i: (i, 0)),
        out_shape=jax.ShapeDtypeStruct((s, d), BF16),
        compiler_params=_cparams(("parallel",)),
    )(x, g)


def _rms_bwd_rows(dh, xf, g):
    rstd = lax.rsqrt(jnp.mean(xf * xf, axis=-1, keepdims=True) + EPS)
    xhat = xf * rstd
    dxhat = dh * g
    dx = rstd * (dxhat - xhat * jnp.mean(dxhat * xhat, axis=-1, keepdims=True))
    return dx, dh * xhat


def _fold8(v):
    return jnp.sum(v.reshape(v.shape[0] // 8, 8, v.shape[1]), axis=0)


def _rms_bwd(name, dh, x, g, dres, jobs=()):
    s, d = x.shape
    tr = _tile(s, TR_EW, 16)
    n = s // tr

    def body(dh_ref, x_ref, g_ref, r_ref, dx_ref, dxb_ref, dg_ref, acc_ref):
        i = pl.program_id(0)
        dx, dgrows = _rms_bwd_rows(dh_ref[...].astype(F32), x_ref[...], g_ref[...])
        dx = dx + r_ref[...]
        dx_ref[...] = dx
        dxb_ref[...] = dx.astype(BF16)

        @pl.when(i == 0)
        def _():
            acc_ref[...] = jnp.zeros_like(acc_ref)
        acc_ref[...] += _fold8(dgrows)

        @pl.when(i == n - 1)
        def _():
            dg_ref[...] = jnp.sum(acc_ref[...], axis=0, keepdims=True)

    row = pl.BlockSpec((tr, d), lambda i: (i, 0))
    vec = pl.BlockSpec((1, d), lambda i: (0, 0))
    return _pcall(
        body, (dh, x, g, dres), jobs, name=name, grid=(n,), in_specs=[row, row, vec, row], out_specs=[row, row, vec],
        out_shape=[jax.ShapeDtypeStruct((s, d), F32), jax.ShapeDtypeStruct((s, d), BF16),
                   jax.ShapeDtypeStruct((1, d), F32)],
        scratch_shapes=[pltpu.VMEM((8, d), F32)],
        sem=("arbitrary",))


def _loss_head(x2, target, g):
    s, d = x2.shape
    tr = _tile(s, TR_EW, 16)
    n = s // tr

    def body(x_ref, t_ref, g_ref, dx_ref, dxb_ref, dg_ref, loss_ref, acc_ref):
        i = pl.program_id(0)
        xf = x_ref[...]
        gv = g_ref[...]
        rstd = lax.rsqrt(jnp.mean(xf * xf, axis=-1, keepdims=True) + EPS)
        err = xf * rstd * gv - t_ref[...]
        row_loss = jnp.mean(err * err, axis=-1, keepdims=True)
        dx, dgrows = _rms_bwd_rows(err / d, xf, gv)
        dx_ref[...] = dx
        dxb_ref[...] = dx.astype(BF16)

        @pl.when(i == 0)
        def _():
            acc_ref[...] = jnp.zeros_like(acc_ref)
            loss_ref[...] = jnp.zeros_like(loss_ref)
        acc_ref[...] += _fold8(dgrows)
        loss_ref[...] += jnp.broadcast_to(0.5 * jnp.sum(row_loss, axis=0, keepdims=True), (8, LANE))

        @pl.when(i == n - 1)
        def _():
            dg_ref[...] = jnp.sum(acc_ref[...], axis=0, keepdims=True)

    row = pl.BlockSpec((tr, d), lambda i: (i, 0))
    vec = pl.BlockSpec((1, d), lambda i: (0, 0))
    return pl.pallas_call(
        body, name="loss_head", grid=(n,), in_specs=[row, row, vec],
        out_specs=[row, row, vec, pl.BlockSpec((8, LANE), lambda i: (0, 0))],
        out_shape=[jax.ShapeDtypeStruct((s, d), F32), jax.ShapeDtypeStruct((s, d), BF16),
                   jax.ShapeDtypeStruct((1, d), F32), jax.ShapeDtypeStruct((8, LANE), F32)],
        scratch_shapes=[pltpu.VMEM((8, d), F32)],
        compiler_params=_cparams(("arbitrary",)),
    )(x2, target, g)


def _gate_fwd(ya, yb, proj_g):
    s, d = ya.shape
    tr = _tile(s, TR_EW, 16)
    tc = _tile(d, 1024)
    nc = d // tc

    def body(ya_ref, yb_ref, ga_ref, gb_ref, o_ref):
        o_ref[...] = (_sigmoid(ga_ref[...]) * ya_ref[...] + _sigmoid(gb_ref[...]) * yb_ref[...]).astype(o_ref.dtype)

    blk = pl.BlockSpec((tr, tc), lambda i, j: (i, j))
    return pl.pallas_call(
        body, name="gate_fwd", grid=(s // tr, nc),
        in_specs=[blk, blk, blk, pl.BlockSpec((tr, tc), lambda i, j: (i, j + nc))],
        out_specs=blk, out_shape=jax.ShapeDtypeStruct((s, d), BF16),
        compiler_params=_cparams(("parallel", "parallel")),
    )(ya, yb, proj_g, proj_g)


def _gate_bwd(dmixed, ya, yb, proj_g):
    s, d = ya.shape
    tr = _tile(s, TR_EW, 16)
    tc = _tile(d, 1024)
    nc = d // tc

    def body(dm_ref, ya_ref, yb_ref, ga_ref, gb_ref, dya_ref, dyb_ref, dga_ref, dgb_ref):
        dm = dm_ref[...].astype(F32)
        sa = _sigmoid(ga_ref[...])
        sb = _sigmoid(gb_ref[...])
        dya_ref[...] = (dm * sa).astype(BF16)
        dyb_ref[...] = (dm * sb).astype(BF16)
        dga_ref[...] = (dm * ya_ref[...] * sa * (1.0 - sa)).astype(BF16)
        dgb_ref[...] = (dm * yb_ref[...] * sb * (1.0 - sb)).astype(BF16)

    blk = pl.BlockSpec((tr, tc), lambda i, j: (i, j))
    out = jax.ShapeDtypeStruct((s, d), BF16)
    return pl.pallas_call(
        body, name="gate_bwd", grid=(s // tr, nc),
        in_specs=[blk, blk, blk, blk, pl.BlockSpec((tr, tc), lambda i, j: (i, j + nc))],
        out_specs=[blk, blk, blk, blk], out_shape=[out, out, out, out],
        compiler_params=_cparams(("parallel", "parallel")),
    )(dmixed, ya, yb, proj_g, proj_g)


def _swiglu_fwd(gu, jobs=()):
    s = gu.shape[0]
    tr = _tile(s, 128, 16)

    def body(gu_ref, h_ref):
        gate = gu_ref[:, :FF_P].astype(F32)
        up = gu_ref[:, FF_P:].astype(F32)
        h_ref[...] = (gate * _sigmoid(gate) * up).astype(h_ref.dtype)

    return _pcall(
        body, (gu,), jobs, name="swiglu_fwd", grid=(s // tr, N_CHIP),
        in_specs=[pl.BlockSpec((tr, 2 * FF_P), lambda i, j: (i, j))],
        out_specs=pl.BlockSpec((tr, FF_P), lambda i, j: (i, j)),
        out_shape=jax.ShapeDtypeStruct((s, FP), BF16),
        sem=("parallel", "parallel"))


def _swiglu_bwd(dhidden, gu):
    s = gu.shape[0]
    tr = _tile(s, 128, 16)

    def body(dh_ref, gu_ref, dgu_ref):
        gate = gu_ref[:, :FF_P].astype(F32)
        up = gu_ref[:, FF_P:].astype(F32)
        dh = dh_ref[...].astype(F32)
        sg = _sigmoid(gate)
        dgu_ref[:, :FF_P] = (dh * up * sg * (1.0 + gate * (1.0 - sg))).astype(BF16)
        dgu_ref[:, FF_P:] = (dh * gate * sg).astype(BF16)

    return pl.pallas_call(
        body, name="swiglu_bwd", grid=(s // tr, N_CHIP),
        in_specs=[pl.BlockSpec((tr, FF_P), lambda i, j: (i, j)), pl.BlockSpec((tr, 2 * FF_P), lambda i, j: (i, j))],
        out_specs=pl.BlockSpec((tr, 2 * FF_P), lambda i, j: (i, j)),
        out_shape=jax.ShapeDtypeStruct((s, 2 * FP), BF16),
        compiler_params=_cparams(("parallel", "parallel")),
    )(dhidden, gu)


def _tri(n, upper):
    r = lax.broadcasted_iota(jnp.int32, (n, n), 0)
    c = lax.broadcasted_iota(jnp.int32, (n, n), 1)
    return (c >= r if upper else c <= r).astype(F32)


def _forget_fwd(fa, bias):
    s = fa.shape[0]
    tb = _tile(s, 512, 8)

    def body(f_ref, b_ref, c_ref, carry_ref):
        i = pl.program_id(0)

        @pl.when(i == 0)
        def _():
            carry_ref[...] = jnp.zeros_like(carry_ref)
        z = f_ref[...] + b_ref[...]
        logf = jnp.minimum(z, 0.0) - jnp.log(1.0 + jnp.exp(-jnp.abs(z)))
        c = jnp.dot(_tri(tb, False), logf, precision=lax.Precision.HIGHEST, preferred_element_type=F32)
        c_ref[...] = c + carry_ref[0:1, :]
        carry_ref[...] = jnp.broadcast_to(c_ref[tb - 1:tb, :], carry_ref.shape)

    return pl.pallas_call(
        body, name="forget_fwd", grid=(s // tb,),
        in_specs=[pl.BlockSpec((tb, LANE), lambda i: (i, 0)), pl.BlockSpec((1, LANE), lambda i: (0, 0))],
        out_specs=pl.BlockSpec((tb, LANE), lambda i: (i, 0)),
        out_shape=jax.ShapeDtypeStruct((s, LANE), F32),
        scratch_shapes=[pltpu.VMEM((8, LANE), F32)],
        compiler_params=_cparams(("arbitrary",)),
    )(fa, bias)


def _forget_bwd(dc, fa, bias):
    s = fa.shape[0]
    tb = _tile(s, 512, 16)
    n = s // tb

    def body(dc_ref, f_ref, b_ref, df_ref, db_ref, carry_ref, acc_ref, tmp_ref):
        i = pl.program_id(0)

        @pl.when(i == 0)
        def _():
            carry_ref[...] = jnp.zeros_like(carry_ref)
            acc_ref[...] = jnp.zeros_like(acc_ref)
        dlogf = jnp.dot(_tri(tb, True), dc_ref[...], precision=lax.Precision.HIGHEST, preferred_element_type=F32)
        tmp_ref[...] = dlogf + carry_ref[0:1, :]
        carry_ref[...] = jnp.broadcast_to(tmp_ref[0:1, :], carry_ref.shape)
        df = tmp_ref[...] * _sigmoid(-(f_ref[...] + b_ref[...]))
        df_ref[...] = df.astype(BF16)
        acc_ref[...] += _fold8(df)

        @pl.when(i == n - 1)
        def _():
            db_ref[...] = jnp.sum(acc_ref[...], axis=0, keepdims=True)

    rev = pl.BlockSpec((tb, LANE), lambda i: (n - 1 - i, 0))
    vec = pl.BlockSpec((1, LANE), lambda i: (0, 0))
    return pl.pallas_call(
        body, name="forget_bwd", grid=(n,), in_specs=[rev, rev, vec], out_specs=[rev, vec],
        out_shape=[jax.ShapeDtypeStruct((s, LANE), BF16), jax.ShapeDtypeStruct((1, LANE), F32)],
        scratch_shapes=[pltpu.VMEM((8, LANE), F32), pltpu.VMEM((8, LANE), F32), pltpu.VMEM((tb, LANE), F32)],
        compiler_params=_cparams(("arbitrary",)),
    )(dc, fa, bias)


_A_SCALE = DH_A ** -0.5


def _causal(tq, tk, transposed):
    r = lax.broadcasted_iota(jnp.int32, (tq, tk), 0)
    c = lax.broadcasted_iota(jnp.int32, (tq, tk), 1)
    return c >= r if transposed else r >= c


def _fox_fwd(qkv, c_col, c_row, jobs=()):
    s = qkv.shape[0]
    t = _tile(s, TQ_A)
    n = s // t
    kb, vb = OFF_KA // DH_A, OFF_VA // DH_A

    def body(q_ref, k_ref, v_ref, cq_ref, ck_ref, o_ref, lse_ref, m_s, l_s, acc_s):
        i, j = pl.program_id(1), pl.program_id(2)

        @pl.when(j == 0)
        def _():
            m_s[...] = jnp.full_like(m_s, -jnp.inf)
            l_s[...] = jnp.zeros_like(l_s)
            acc_s[...] = jnp.zeros_like(acc_s)

        def step(diag):
            sc = _dot_nt(q_ref[...], k_ref[...]) * _A_SCALE + (cq_ref[:, 0:1] - ck_ref[...])
            if diag:
                sc = jnp.where(_causal(t, t, False), sc, -jnp.inf)
            m_prev = m_s[...]
            m_new = jnp.maximum(m_prev, jnp.max(sc, axis=1, keepdims=True))
            alpha = jnp.exp(m_prev - m_new)
            p = jnp.exp(sc - m_new[:, 0:1])
            l_s[...] = alpha * l_s[...] + jnp.sum(p, axis=1, keepdims=True)
            acc_s[...] = alpha * acc_s[...] + _dot(p.astype(BF16), v_ref[...])
            m_s[...] = m_new

        @pl.when(j < i)
        def _():
            step(False)

        @pl.when(j == i)
        def _():
            step(True)
            o_ref[...] = (acc_s[...] / l_s[...]).astype(o_ref.dtype)
            lse_ref[...] = m_s[...] + jnp.log(l_s[...])

    jc = lambda i, j: jnp.minimum(j, i)
    return _pcall(
        body, (qkv, qkv, qkv, c_col, c_row), jobs, name="fox_fwd", grid=(H_A, n, n),
        in_specs=[pl.BlockSpec((t, DH_A), lambda h, i, j: (i, h)),
                  pl.BlockSpec((t, DH_A), lambda h, i, j: (jc(i, j), kb + h)),
                  pl.BlockSpec((t, DH_A), lambda h, i, j: (jc(i, j), vb + h)),
                  pl.BlockSpec((None, t, LANE), lambda h, i, j: (h, i, 0)),
                  pl.BlockSpec((None, 1, t), lambda h, i, j: (h, 0, jc(i, j)))],
        out_specs=[pl.BlockSpec((t, DH_A), lambda h, i, j: (i, h)),
                   pl.BlockSpec((None, t, LANE), lambda h, i, j: (h, i, 0))],
        out_shape=[jax.ShapeDtypeStruct((s, W_A), BF16), jax.ShapeDtypeStruct((H_A, s, LANE), F32)],
        scratch_shapes=[pltpu.VMEM((t, LANE), F32)] * 3,
        sem=("parallel", "parallel", "arbitrary"))


def _fox_dq(qkv, do, c_col, c_row, lse_col, jobs=()):
    s = qkv.shape[0]
    t = _tile(s, TQ_A)
    n = s // t
    kb, vb = OFF_KA // DH_A, OFF_VA // DH_A

    def body(q_ref, k_ref, v_ref, do_ref, cq_ref, ck_ref, lse_ref, dq_ref, dl_ref, pdk_s, pk_s, dl_s):
        i, j = pl.program_id(1), pl.program_id(2)

        @pl.when(j == 0)
        def _():
            pdk_s[...] = jnp.zeros_like(pdk_s)
            pk_s[...] = jnp.zeros_like(pk_s)
            dl_s[...] = jnp.zeros_like(dl_s)

        def step(diag):
            sc = _dot_nt(q_ref[...], k_ref[...]) * _A_SCALE + (cq_ref[:, 0:1] - ck_ref[...])
            if diag:
                sc = jnp.where(_causal(t, t, False), sc, -jnp.inf)
            p = jnp.exp(sc - lse_ref[:, 0:1])
            pdp = p * _dot_nt(do_ref[...], v_ref[...])
            dl_s[...] += jnp.sum(pdp, axis=1, keepdims=True)
            pdk_s[...] += _dot(pdp.astype(BF16), k_ref[...])
            pk_s[...] += _dot(p.astype(BF16), k_ref[...])

        @pl.when(j < i)
        def _():
            step(False)

        @pl.when(j == i)
        def _():
            step(True)
            dq_ref[...] = ((pdk_s[...] - dl_s[...] * pk_s[...]) * _A_SCALE).astype(dq_ref.dtype)
            dl_ref[...] = dl_s[...]

    jc = lambda i, j: jnp.minimum(j, i)
    col = pl.BlockSpec((None, t, LANE), lambda h, i, j: (h, i, 0))
    return _pcall(
        body, (qkv, qkv, qkv, do, c_col, c_row, lse_col), jobs, name="fox_dq", grid=(H_A, n, n),
        in_specs=[pl.BlockSpec((t, DH_A), lambda h, i, j: (i, h)),
                  pl.BlockSpec((t, DH_A), lambda h, i, j: (jc(i, j), kb + h)),
                  pl.BlockSpec((t, DH_A), lambda h, i, j: (jc(i, j), vb + h)),
                  pl.BlockSpec((t, DH_A), lambda h, i, j: (i, h)),
                  col,
                  pl.BlockSpec((None, 1, t), lambda h, i, j: (h, 0, jc(i, j))),
                  col],
        out_specs=[pl.BlockSpec((t, DH_A), lambda h, i, j: (i, h)), col],
        out_shape=[jax.ShapeDtypeStruct((s, W_A), BF16), jax.ShapeDtypeStruct((H_A, s, LANE), F32)],
        scratch_shapes=[pltpu.VMEM((t, DH_A), F32), pltpu.VMEM((t, DH_A), F32), pltpu.VMEM((t, LANE), F32)],
        sem=("parallel", "parallel", "arbitrary"))


def _fox_dkv(qkv, do, c_col, c_row, lse_row, delta_row, jobs=()):
    s = qkv.shape[0]
    t = _tile(s, TQ_A)
    n = s // t
    kb, vb = OFF_KA // DH_A, OFF_VA // DH_A

    def body(q_ref, k_ref, v_ref, do_ref, cq_ref, ck_ref, lse_ref, dl_ref, dk_ref, dv_ref, dc_ref,
             dk_s, dv_s, dc_s):
        j, i = pl.program_id(1), pl.program_id(2)

        @pl.when(i == 0)
        def _():
            dk_s[...] = jnp.zeros_like(dk_s)
            dv_s[...] = jnp.zeros_like(dv_s)
            dc_s[...] = jnp.zeros_like(dc_s)

        def step(diag):
            st = _dot_nt(k_ref[...], q_ref[...]) * _A_SCALE + (cq_ref[...] - ck_ref[:, 0:1])
            if diag:
                st = jnp.where(_causal(t, t, True), st, -jnp.inf)
            pt = jnp.exp(st - lse_ref[...])
            dv_s[...] += _dot(pt.astype(BF16), do_ref[...])
            dpt = _dot_nt(v_ref[...], do_ref[...])
            dst = pt * (dpt - dl_ref[...])
            dk_s[...] += _dot(dst.astype(BF16), q_ref[...])
            dc_s[...] -= jnp.sum(dst, axis=1, keepdims=True)

        @pl.when(i == j)
        def _():
            step(True)

        @pl.when(i > j)
        def _():
            step(False)

        @pl.when(i == n - 1)
        def _():
            dk_ref[...] = (dk_s[...] * _A_SCALE).astype(dk_ref.dtype)
            dv_ref[...] = dv_s[...].astype(dv_ref.dtype)
            dc_ref[...] = dc_s[...]

    ic = lambda j, i: jnp.maximum(i, j)
    rowq = pl.BlockSpec((None, 1, t), lambda h, j, i: (h, 0, ic(j, i)))
    kv_out = pl.BlockSpec((t, DH_A), lambda h, j, i: (j, h))
    return _pcall(
        body, (qkv, qkv, qkv, do, c_row, c_col, lse_row, delta_row), jobs, name="fox_dkv", grid=(H_A, n, n),
        in_specs=[pl.BlockSpec((t, DH_A), lambda h, j, i: (ic(j, i), h)),
                  pl.BlockSpec((t, DH_A), lambda h, j, i: (j, kb + h)),
                  pl.BlockSpec((t, DH_A), lambda h, j, i: (j, vb + h)),
                  pl.BlockSpec((t, DH_A), lambda h, j, i: (ic(j, i), h)),
                  rowq,
                  pl.BlockSpec((None, t, LANE), lambda h, j, i: (h, j, 0)),
                  rowq, rowq],
        out_specs=[kv_out, kv_out, pl.BlockSpec((None, t, LANE), lambda h, j, i: (h, j, 0))],
        out_shape=[jax.ShapeDtypeStruct((s, W_A), BF16), jax.ShapeDtypeStruct((s, W_A), BF16),
                   jax.ShapeDtypeStruct((H_A, s, LANE), F32)],
        scratch_shapes=[pltpu.VMEM((t, DH_A), F32), pltpu.VMEM((t, DH_A), F32), pltpu.VMEM((t, LANE), F32)],
        sem=("parallel", "parallel", "arbitrary"))


_B_SCALE = DH_B ** -0.5
_HALF = LANE // 2


def _t5_bucket_table():
    ql = np.arange(BLOCK)[:, None]
    kl = np.arange(2 * BLOCK)[None, :]
    dist = np.clip(ql + BLOCK - kl, 0, None)
    max_exact = NUM_BUCKETS // 2
    large = max_exact + (np.log(np.maximum(dist, 1) / max_exact) / np.log(MAX_DISTANCE / max_exact)
                         * (NUM_BUCKETS - max_exact)).astype(np.int64)
    large = np.minimum(large, NUM_BUCKETS - 1)
    return np.where(dist < max_exact, dist, large).astype(np.int32).reshape(1, BLOCK * 2 * BLOCK)


def _one_hot_buckets(bucket_ref, n):
    return (lax.broadcasted_iota(jnp.int32, (NUM_BUCKETS, n), 0) == bucket_ref[...]).astype(F32)


def _bias_table(rel_bias_t, bucket):
    nqk = bucket.shape[1]
    tc = _tile(nqk, 4096)

    def body(rb_ref, bk_ref, o_ref):
        o_ref[...] = jnp.dot(rb_ref[...], _one_hot_buckets(bk_ref, tc), precision=lax.Precision.HIGHEST,
                             preferred_element_type=F32)

    return pl.pallas_call(
        body, name="bias_table", grid=(nqk // tc,),
        in_specs=[pl.BlockSpec((H_B, NUM_BUCKETS), lambda i: (0, 0)), pl.BlockSpec((1, tc), lambda i: (0, i))],
        out_specs=pl.BlockSpec((H_B, tc), lambda i: (0, i)),
        out_shape=jax.ShapeDtypeStruct((H_B, nqk), F32),
        compiler_params=_cparams(("parallel",)),
    )(rel_bias_t, bucket)


def _bias_table_bwd(dbias, bucket):
    nqk = bucket.shape[1]
    tc = _tile(nqk, 4096)
    n = nqk // tc

    def body(db_ref, bk_ref, o_ref):
        i = pl.program_id(0)

        @pl.when(i == 0)
        def _():
            o_ref[...] = jnp.zeros_like(o_ref)
        o_ref[...] += lax.dot_general(db_ref[...], _one_hot_buckets(bk_ref, tc), (((1,), (1,)), ((), ())),
                                      precision=lax.Precision.HIGHEST, preferred_element_type=F32)

    return pl.pallas_call(
        body, name="bias_table_bwd", grid=(n,),
        in_specs=[pl.BlockSpec((H_B, tc), lambda i: (0, i)), pl.BlockSpec((1, tc), lambda i: (0, i))],
        out_specs=pl.BlockSpec((H_B, NUM_BUCKETS), lambda i: (0, 0)),
        out_shape=jax.ShapeDtypeStruct((H_B, NUM_BUCKETS), F32),
        compiler_params=_cparams(("arbitrary",)),
    )(dbias, bucket)


def _lane_lo():
    return lax.broadcasted_iota(jnp.int32, (1, LANE), 1) < _HALF


def _dup_kv_head(cat, hk):
    xcol = cat[:, (hk // 2) * LANE:(hk // 2 + 1) * LANE].astype(F32)
    swapped = pltpu.roll(xcol, _HALF, 1)
    lo = _lane_lo()
    return (jnp.where(lo, xcol, swapped) if hk % 2 == 0 else jnp.where(lo, swapped, xcol)).astype(BF16)


def _band_mask(first_block):
    ql = lax.broadcasted_iota(jnp.int32, (BLOCK, 2 * BLOCK), 0)
    kl = lax.broadcasted_iota(jnp.int32, (BLOCK, 2 * BLOCK), 1)
    dist = ql + BLOCK - kl
    ok = jnp.logical_and(dist >= 0, dist < WINDOW)
    return jnp.logical_and(ok, jnp.logical_or(jnp.logical_not(first_block), kl >= BLOCK))


def _swa_probs(qh, kdup, bias_h, sink, mask):
    sc = _dot_nt(qh, kdup) * _B_SCALE + bias_h
    sc = jnp.where(mask, sc, -jnp.inf)
    m = jnp.maximum(jnp.max(sc, axis=1, keepdims=True), sink)
    p = jnp.exp(sc - m)
    e_sink = jnp.exp(sink - m)
    inv = 1.0 / (jnp.sum(p, axis=1, keepdims=True) + e_sink)
    return p * inv, e_sink * inv


def _split_pair(ref, col):
    x = ref[:, col * LANE:(col + 1) * LANE].astype(F32)
    lo = _lane_lo()
    return jnp.where(lo, x, 0.0).astype(BF16), jnp.where(lo, 0.0, x).astype(BF16)


def _swa_fwd(qkv, bias, sinks, jobs=()):
    s = qkv.shape[0]
    nb = s // BLOCK
    qb, kb, vb = OFF_QB // W_QB, OFF_KB // W_KB, OFF_VB // W_KB
    assert OFF_QB % W_QB == 0 and OFF_KB % W_KB == 0 and OFF_VB % W_KB == 0 and W_KB % LANE == 0 and G_B % 2 == 0

    def body(q_ref, kp_ref, kc_ref, vp_ref, vc_ref, bias_ref, sink_ref, o_ref):
        i = pl.program_id(0)
        mask = _band_mask(i == 0)
        kcat = jnp.concatenate([kp_ref[...], kc_ref[...]], axis=0)
        vcat = jnp.concatenate([vp_ref[...], vc_ref[...]], axis=0)
        lo = _lane_lo()
        for hk in range(HKV_B):
            kdup, vdup = _dup_kv_head(kcat, hk), _dup_kv_head(vcat, hk)
            for pr in range(G_B // 2):
                h0 = hk * G_B + 2 * pr
                q0, q1 = _split_pair(q_ref, h0 // 2)
                p0, _ = _swa_probs(q0, kdup, bias_ref[h0], sink_ref[0, h0], mask)
                p1, _ = _swa_probs(q1, kdup, bias_ref[h0 + 1], sink_ref[0, h0 + 1], mask)
                o0 = _dot(p0.astype(BF16), vdup)
                o1 = _dot(p1.astype(BF16), vdup)
                o_ref[:, (h0 // 2) * LANE:(h0 // 2 + 1) * LANE] = jnp.where(lo, o0, o1).astype(o_ref.dtype)

    prev = lambda i: jnp.maximum(i - 1, 0)
    return _pcall(
        body, (qkv, qkv, qkv, qkv, qkv, bias, sinks), jobs, name="swa_fwd", grid=(nb,),
        in_specs=[pl.BlockSpec((BLOCK, W_QB), lambda i: (i, qb)),
                  pl.BlockSpec((BLOCK, W_KB), lambda i: (prev(i), kb)),
                  pl.BlockSpec((BLOCK, W_KB), lambda i: (i, kb)),
                  pl.BlockSpec((BLOCK, W_KB), lambda i: (prev(i), vb)),
                  pl.BlockSpec((BLOCK, W_KB), lambda i: (i, vb)),
                  pl.BlockSpec((H_B, BLOCK, 2 * BLOCK), lambda i: (0, 0, 0)),
                  pl.BlockSpec(memory_space=pltpu.SMEM)],
        out_specs=pl.BlockSpec((BLOCK, W_QB), lambda i: (i, 0)),
        out_shape=jax.ShapeDtypeStruct((s, W_QB), BF16),
        sem=("parallel",))


def _swa_bwd(qkv, do, bias, sinks, jobs=()):
    s = qkv.shape[0]
    nb = s // BLOCK
    qb, kb, vb = OFF_QB // W_QB, OFF_KB // W_KB, OFF_VB // W_KB

    def body(q_ref, kp_ref, kc_ref, vp_ref, vc_ref, do_ref, bias_ref, sink_ref,
             dq_ref, dk_ref, dv_ref, dbias_ref, dsink_ref, carry_k, carry_v):
        i = pl.program_id(0)
        lo = _lane_lo()

        @pl.when(i == 0)
        def _():
            dbias_ref[...] = jnp.zeros_like(dbias_ref)
            dsink_ref[...] = jnp.zeros_like(dsink_ref)
            carry_k[...] = jnp.zeros_like(carry_k)
            carry_v[...] = jnp.zeros_like(carry_v)

        @pl.when(i < nb)
        def _():
            mask = _band_mask(i == 0)
            kcat = jnp.concatenate([kp_ref[...], kc_ref[...]], axis=0)
            vcat = jnp.concatenate([vp_ref[...], vc_ref[...]], axis=0)
            lane = lax.broadcasted_iota(jnp.int32, (1, LANE), 1)
            dsink = jnp.zeros((1, LANE), F32)
            dk_cols = [jnp.zeros((2 * BLOCK, LANE), F32) for _ in range(W_KB // LANE)]
            dv_cols = [jnp.zeros((2 * BLOCK, LANE), F32) for _ in range(W_KB // LANE)]
            for hk in range(HKV_B):
                kdup, vdup = _dup_kv_head(kcat, hk), _dup_kv_head(vcat, hk)
                dk_acc = jnp.zeros((2 * BLOCK, LANE), F32)
                dv_acc = jnp.zeros((2 * BLOCK, LANE), F32)
                for pr in range(G_B // 2):
                    h0 = hk * G_B + 2 * pr
                    col = h0 // 2
                    qs = _split_pair(q_ref, col)
                    dos = _split_pair(do_ref, col)
                    dqs = []
                    for e in range(2):
                        h = h0 + e
                        p, p_sink = _swa_probs(qs[e], kdup, bias_ref[h], sink_ref[0, h], mask)
                        dp = _dot_nt(dos[e], vdup)
                        delta = jnp.sum(p * dp, axis=1, keepdims=True)
                        ds = p * (dp - delta)
                        dbias_ref[h] += ds
                        dsink = dsink - jnp.where(lane == h, jnp.sum(p_sink * delta, axis=0, keepdims=True), 0.0)
                        dqs.append(_dot(ds.astype(BF16), kdup))
                        dk_acc = dk_acc + _dot(ds.T.astype(BF16), qs[e])
                        dv_acc = dv_acc + _dot(p.T.astype(BF16), dos[e])
                    dq_ref[:, col * LANE:(col + 1) * LANE] = (jnp.where(lo, dqs[0], dqs[1]) * _B_SCALE).astype(dq_ref.dtype)
                dk_tot = (dk_acc + pltpu.roll(dk_acc, _HALF, 1)) * _B_SCALE
                dv_tot = dv_acc + pltpu.roll(dv_acc, _HALF, 1)
                mine = lo if hk % 2 == 0 else jnp.logical_not(lo)
                dk_cols[hk // 2] = jnp.where(mine, dk_tot, dk_cols[hk // 2])
                dv_cols[hk // 2] = jnp.where(mine, dv_tot, dv_cols[hk // 2])
            dsink_ref[...] += dsink
            dk_cat = jnp.concatenate(dk_cols, axis=1)
            dv_cat = jnp.concatenate(dv_cols, axis=1)
            dk_ref[...] = (carry_k[...] + dk_cat[:BLOCK]).astype(dk_ref.dtype)
            dv_ref[...] = (carry_v[...] + dv_cat[:BLOCK]).astype(dv_ref.dtype)
            carry_k[...] = dk_cat[BLOCK:]
            carry_v[...] = dv_cat[BLOCK:]

        @pl.when(i == nb)
        def _():
            dk_ref[...] = carry_k[...].astype(dk_ref.dtype)
            dv_ref[...] = carry_v[...].astype(dv_ref.dtype)

    cur = lambda i: jnp.minimum(i, nb - 1)
    prev = lambda i: jnp.clip(i - 1, 0, nb - 1)
    kv_out = pl.BlockSpec((BLOCK, W_KB), lambda i: (prev(i), 0))
    return _pcall(
        body, (qkv, qkv, qkv, qkv, qkv, do, bias, sinks), jobs, name="swa_bwd", grid=(nb + 1,),
        in_specs=[pl.BlockSpec((BLOCK, W_QB), lambda i: (cur(i), qb)),
                  pl.BlockSpec((BLOCK, W_KB), lambda i: (prev(i), kb)),
                  pl.BlockSpec((BLOCK, W_KB), lambda i: (cur(i), kb)),
                  pl.BlockSpec((BLOCK, W_KB), lambda i: (prev(i), vb)),
                  pl.BlockSpec((BLOCK, W_KB), lambda i: (cur(i), vb)),
                  pl.BlockSpec((BLOCK, W_QB), lambda i: (cur(i), 0)),
                  pl.BlockSpec((H_B, BLOCK, 2 * BLOCK), lambda i: (0, 0, 0)),
                  pl.BlockSpec(memory_space=pltpu.SMEM)],
        out_specs=[pl.BlockSpec((BLOCK, W_QB), lambda i: (cur(i), 0)), kv_out, kv_out,
                   pl.BlockSpec((H_B, BLOCK, 2 * BLOCK), lambda i: (0, 0, 0)),
                   pl.BlockSpec((1, LANE), lambda i: (0, 0))],
        out_shape=[jax.ShapeDtypeStruct((s, W_QB), BF16), jax.ShapeDtypeStruct((s, W_KB), BF16),
                   jax.ShapeDtypeStruct((s, W_KB), BF16),
                   jax.ShapeDtypeStruct((H_B, BLOCK, 2 * BLOCK), F32), jax.ShapeDtypeStruct((1, LANE), F32)],
        scratch_shapes=[pltpu.VMEM((BLOCK, W_KB), F32), pltpu.VMEM((BLOCK, W_KB), F32)],
        sem=("arbitrary",))


_RELS = ((1, 0), (0, 1), (1, 1))


def _place():
    x, y, c = lax.axis_index("x"), lax.axis_index("y"), lax.axis_index("c")
    return x, y, c


def _rel_chip(x, y, rel):
    px = 1 - x if rel[0] else x
    py = 1 - y if rel[1] else y
    return px, py, 2 * px + py


def _half_rows(ref_shape, c):
    half = ref_shape[-2] // 2
    return pl.ds(pl.multiple_of(c * half, 16), half)


def _dma_sems(*shape):
    return pltpu.SemaphoreType.DMA(shape)


def _gather_ici_job(srcs):
    nt = len(srcs)

    def copies(src, dst, sems):
        send, recv, lsem = sems
        x, y, c = _place()
        me_chip = 2 * x + y
        local, sends, recvs = [], [], []
        for t in range(nt):
            ns = src[t].shape[0]
            rows = _half_rows(src[t].shape, c)
            local.append(pltpu.make_async_copy(src[t], dst[t].at[pl.ds(me_chip * ns, ns)], lsem.at[t]))
            for k, rel in enumerate(_RELS[:2]):
                px, py, chip = _rel_chip(x, y, rel)
                for into, lst in ((me_chip, sends), (chip, recvs)):
                    lst.append(pltpu.make_async_remote_copy(
                        src_ref=src[t].at[:, rows], dst_ref=dst[t].at[pl.ds(into * ns, ns), rows],
                        send_sem=send.at[t, k], recv_sem=recv.at[t, k], device_id=(px, py, c), device_id_type=MESH))
        return local, sends, recvs

    def start(src, dst, sems):
        local, sends, _ = copies(src, dst, sems)
        for cp in local + sends:
            cp.start()

    def wait(src, dst, sems):
        local, sends, recvs = copies(src, dst, sems)
        for cp in recvs:
            cp.wait_recv()
        for cp in sends:
            cp.wait_send()
        for cp in local:
            cp.wait()

    return _Job(srcs, [jax.ShapeDtypeStruct((N_CHIP * a.shape[0],) + a.shape[1:], a.dtype) for a in srcs],
                [_dma_sems(nt, 2), _dma_sems(nt, 2), _dma_sems(nt)], start, wait)


def _gather_relay_job(gathered):
    nt = len(gathered)

    def copies(dst, sems):
        send, recv = sems
        x, y, c = _place()
        diag = 2 * (1 - x) + (1 - y)
        sends, recvs = [], []
        for t in range(nt):
            ns = dst[t].shape[0] // N_CHIP
            quarter = dst[t].shape[-2] // 4
            for k, rel in enumerate(_RELS[:2]):
                px, py, _ = _rel_chip(x, y, rel)
                _, _, origin = _rel_chip(x, y, _RELS[1 - k])
                rows = pl.ds(pl.multiple_of((2 * c + k) * quarter, 16), quarter)
                for chip, lst in ((origin, sends), (diag, recvs)):
                    part = dst[t].at[pl.ds(chip * ns, ns), rows]
                    lst.append(pltpu.make_async_remote_copy(
                        src_ref=part, dst_ref=part, send_sem=send.at[t, k], recv_sem=recv.at[t, k],
                        device_id=(px, py, c), device_id_type=MESH))
        return sends, recvs

    def start(_, dst, sems):
        for cp in copies(dst, sems)[0]:
            cp.start()

    def wait(_, dst, sems):
        sends, recvs = copies(dst, sems)
        for cp in recvs:
            cp.wait_recv()
        for cp in sends:
            cp.wait_send()

    return _Job(gathered, [jax.ShapeDtypeStruct(a.shape, a.dtype) for a in gathered],
                [_dma_sems(nt, 2), _dma_sems(nt, 2)], start, wait, alias={t: t for t in range(nt)})


def _gather_d2d_job(gathered):
    nt = len(gathered)

    def copies(dst, sems):
        send, recv = sems
        x, y, c = _place()
        sends, recvs = [], []
        for t in range(nt):
            ns = dst[t].shape[0] // N_CHIP
            for k, rel in enumerate(_RELS):
                _, _, chip = _rel_chip(x, y, rel)
                for half, lst in ((c, sends), (1 - c, recvs)):
                    part = dst[t].at[pl.ds(chip * ns, ns), _half_rows(dst[t].shape, half)]
                    lst.append(pltpu.make_async_remote_copy(
                        src_ref=part, dst_ref=part, send_sem=send.at[t, k], recv_sem=recv.at[t, k],
                        device_id=(x, y, 1 - c), device_id_type=MESH))
        return sends, recvs

    def start(_, dst, sems):
        for cp in copies(dst, sems)[0]:
            cp.start()

    def wait(_, dst, sems):
        sends, recvs = copies(dst, sems)
        for cp in recvs:
            cp.wait_recv()
        for cp in sends:
            cp.wait_send()

    return _Job(gathered, [jax.ShapeDtypeStruct(a.shape, a.dtype) for a in gathered],
                [_dma_sems(nt, 3), _dma_sems(nt, 3)], start, wait, alias={t: t for t in range(nt)})


def _pair_job(grads):
    nt = len(grads)

    def copies(g, got, sems):
        send, recv = sems
        x, y, c = _place()
        return [pltpu.make_async_remote_copy(
            src_ref=g[t].at[:, _half_rows(g[t].shape, 1 - c)], dst_ref=got[t], send_sem=send.at[t],
            recv_sem=recv.at[t], device_id=(x, y, 1 - c), device_id_type=MESH) for t in range(nt)]

    def start(g, got, sems):
        for cp in copies(g, got, sems):
            cp.start()

    def wait(g, got, sems):
        for cp in copies(g, got, sems):
            cp.wait()

    return _Job(grads, [jax.ShapeDtypeStruct((a.shape[0], a.shape[1] // 2, a.shape[2]), a.dtype) for a in grads],
                [_dma_sems(nt), _dma_sems(nt)], start, wait)


def _ew_rows(rows, cols, itemsize=4):
    tr = 16
    while tr * 2 <= 128 and rows % (tr * 2) == 0 and tr * 2 * cols * itemsize <= (1 << 20):
        tr *= 2
    assert rows % tr == 0, (rows, tr)
    return tr


def _pair_sum(name, grad, got, place):
    ns, rows, cols = grad.shape
    half = rows // 2
    tr = _ew_rows(half, cols)
    nh = half // tr

    def body(p_ref, g_ref, r_ref, o_ref):
        o_ref[...] = (g_ref[...].astype(F32) + r_ref[...].astype(F32)).astype(o_ref.dtype)

    return pl.pallas_call(
        body, name=name,
        grid_spec=pltpu.PrefetchScalarGridSpec(
            num_scalar_prefetch=1, grid=(ns, nh),
            in_specs=[pl.BlockSpec((None, tr, cols), lambda s, i, p: (s, p[1] * nh + i, 0)),
                      pl.BlockSpec((None, tr, cols), lambda s, i, p: (s, i, 0))],
            out_specs=pl.BlockSpec((None, tr, cols), lambda s, i, p: (s, i, 0))),
        out_shape=jax.ShapeDtypeStruct((ns, half, cols), BF16),
        compiler_params=_cparams(("parallel", "parallel")),
    )(place, grad, got)


def _chip_job(psums):
    nt = len(psums)

    def copies(p, got, sems):
        send, recv = sems
        x, y, c = _place()
        cps = []
        for t in range(nt):
            ns = p[t].shape[0] // N_CHIP
            for k, rel in enumerate(_RELS):
                px, py, chip = _rel_chip(x, y, rel)
                cps.append(pltpu.make_async_remote_copy(
                    src_ref=p[t].at[pl.ds(chip * ns, ns)], dst_ref=got[t].at[k],
                    send_sem=send.at[t, k], recv_sem=recv.at[t, k], device_id=(px, py, c), device_id_type=MESH))
        return cps

    def start(p, got, sems):
        for cp in copies(p, got, sems):
            cp.start()

    def wait(p, got, sems):
        for cp in copies(p, got, sems):
            cp.wait()

    return _Job(psums, [jax.ShapeDtypeStruct((3, a.shape[0] // N_CHIP) + a.shape[1:], a.dtype) for a in psums],
                [_dma_sems(nt, 3), _dma_sems(nt, 3)], start, wait)


def _chip_sum(name, psum, got, place):
    ns4, half, cols = psum.shape
    ns = ns4 // N_CHIP
    tr = _ew_rows(half, cols)
    nh = half // tr

    def body(p_ref, mine_ref, got_ref, o_ref):
        acc = mine_ref[...].astype(F32)
        for k in range(3):
            acc = acc + got_ref[k].astype(F32)
        o_ref[...] = acc

    return pl.pallas_call(
        body, name=name,
        grid_spec=pltpu.PrefetchScalarGridSpec(
            num_scalar_prefetch=1, grid=(ns, nh),
            in_specs=[pl.BlockSpec((None, tr, cols), lambda s, i, p: (p[0] * ns + s, i, 0)),
                      pl.BlockSpec((3, None, tr, cols), lambda s, i, p: (0, s, i, 0))],
            out_specs=pl.BlockSpec((None, tr, cols), lambda s, i, p: (s, p[1] * nh + i, 0))),
        out_shape=jax.ShapeDtypeStruct((ns, 2 * half, cols), F32),
        compiler_params=_cparams(("parallel", "parallel")),
    )(place, psum, got)


def _share_job(halves):
    nt = len(halves)

    def copies(full, sems):
        send, recv = sems
        x, y, c = _place()
        sends, recvs = [], []
        for t in range(nt):
            for half, lst in ((c, sends), (1 - c, recvs)):
                part = full[t].at[:, _half_rows(full[t].shape, half)]
                lst.append(pltpu.make_async_remote_copy(
                    src_ref=part, dst_ref=part, send_sem=send.at[t], recv_sem=recv.at[t],
                    device_id=(x, y, 1 - c), device_id_type=MESH))
        return sends, recvs

    def start(_, full, sems):
        for cp in copies(full, sems)[0]:
            cp.start()

    def wait(_, full, sems):
        sends, recvs = copies(full, sems)
        for cp in sends:
            cp.wait_send()
        for cp in recvs:
            cp.wait_recv()

    return _Job(halves, [jax.ShapeDtypeStruct(a.shape, a.dtype) for a in halves],
                [_dma_sems(nt), _dma_sems(nt)], start, wait, alias={t: t for t in range(nt)})


def _small_all_reduce(pack):
    rows, d = pack.shape

    def body(x_ref, o_ref, land, send, recv):
        x, y, c = _place()
        me = 4 * x + 2 * y + c
        land[me] = x_ref[...]
        cps = []
        for k in range(1, 8):
            to = (1 - x if k & 4 else x, 1 - y if k & 2 else y, 1 - c if k & 1 else c)
            cps.append(pltpu.make_async_remote_copy(
                src_ref=x_ref, dst_ref=land.at[me], send_sem=send.at[k - 1], recv_sem=recv.at[k - 1],
                device_id=to, device_id_type=MESH))
        for cp in cps:
            cp.start()
        for cp in cps:
            cp.wait()
        acc = land[0]
        for dev in range(1, 8):
            acc = acc + land[dev]
        o_ref[...] = acc

    vm = pl.BlockSpec(memory_space=pltpu.VMEM)
    return pl.pallas_call(
        body, name="small_all_reduce", in_specs=[vm], out_specs=vm,
        out_shape=jax.ShapeDtypeStruct((rows, d), F32),
        scratch_shapes=[pltpu.VMEM((8, rows, d), F32), pltpu.SemaphoreType.DMA((7,)), pltpu.SemaphoreType.DMA((7,))],
    )(pack)


def _adamw(name, w, g, m, v, jobs=()):
    lead, rows, cols = w.shape
    fits = [t for t in range(8, rows + 1, 8) if rows % t == 0 and t * cols * 4 <= (1 << 20)]
    tr = fits[-1] if fits else rows
    tl = 1
    if tr == rows:
        tl = max(t for t in range(1, lead + 1) if lead % t == 0 and t * rows * cols * 4 <= (1 << 20))

    def body(w_ref, g_ref, m_ref, v_ref, d_ref, nm_ref, nv_ref):
        gv = g_ref[...]
        nm = ADAM_B1 * m_ref[...] + (1.0 - ADAM_B1) * gv
        nv = ADAM_B2 * v_ref[...] + (1.0 - ADAM_B2) * (gv * gv)
        m_hat = nm / (1.0 - ADAM_B1 ** ADAM_STEP)
        v_hat = nv / (1.0 - ADAM_B2 ** ADAM_STEP)
        d_ref[...] = -ADAM_LR * (m_hat / (jnp.sqrt(v_hat) + ADAM_EPS) + ADAM_WD * w_ref[...])
        nm_ref[...] = nm
        nv_ref[...] = nv

    blk = pl.BlockSpec((tl, tr, cols), lambda l, i: (l, i, 0))
    out = jax.ShapeDtypeStruct(w.shape, F32)
    return _pcall(
        body, (w, g, m, v), jobs, name=name, grid=(lead // tl, rows // tr), in_specs=[blk] * 4, out_specs=[blk] * 3,
        out_shape=[out, out, out], sem=("parallel", "parallel"))


def _pad_to(a, shape):
    return jnp.pad(a, [(0, t - s) for s, t in zip(a.shape, shape)])


def _pack_small(n1, n2, fg, bf, sk, rb, extra=None):
    rows = [n1.reshape(1, D_MODEL), n2.reshape(1, D_MODEL), fg.reshape(1, D_MODEL),
            _pad_to(bf.reshape(1, H_A), (1, D_MODEL)), _pad_to(sk.reshape(1, H_B), (1, D_MODEL)),
            jnp.zeros((1, D_MODEL), F32) if extra is None else _pad_to(extra.reshape(1, 1), (1, D_MODEL)),
            _pad_to(rb.reshape(1, NUM_BUCKETS * H_B), (1, RB_ROWS * D_MODEL)).reshape(RB_ROWS, D_MODEL)]
    return _pad_to(jnp.concatenate(rows, axis=0), (PACK_ROWS, D_MODEL))


def _unpack_small(p):
    return (p[0:1], p[1:2], p[2], p[3:4, :H_A], p[4:5, :H_B],
            p[6:6 + RB_ROWS].reshape(-1)[:NUM_BUCKETS * H_B].reshape(NUM_BUCKETS, H_B))


def kernel(x, norm1_g, w_in, b_forget, attn_sinks, rel_bias, w_branch_a, w_branch_b, w_out, norm2_g, w_ffn_gate, w_ffn_up, w_ffn_down, final_g, loss_target, m_norm1_g, m_w_in, m_b_forget, m_attn_sinks, m_rel_bias, m_w_branch_a, m_w_branch_b, m_w_out, m_norm2_g, m_w_ffn_gate, m_w_ffn_up, m_w_ffn_down, m_final_g, v_norm1_g, v_w_in, v_b_forget, v_attn_sinks, v_rel_bias, v_w_branch_a, v_w_branch_b, v_w_out, v_norm2_g, v_w_ffn_gate, v_w_ffn_up, v_w_ffn_down, v_final_g):
    s, d = SEQ, D_MODEL
    assert x.shape == (1, s, d) and w_in.shape == (1, d, W_IN_SH)
    xs = x[0]
    place = jnp.stack([2 * lax.axis_index("x") + lax.axis_index("y"), lax.axis_index("c")]).astype(jnp.int32)

    w_gu_l = jnp.stack([_pad_to(w_ffn_gate[0], (d, FF_P)), _pad_to(w_ffn_up[0], (d, FF_P))]).astype(BF16)
    w_dn_l = _pad_to(w_ffn_down, (1, FF_P, d)).astype(BF16)
    j_in = _gather_ici_job([w_in.astype(BF16)])
    _comm_now("gather_w_in_ici", [j_in])
    j_in1 = _gather_relay_job(j_in.out)
    _comm_now("gather_w_in_relay", [j_in1])
    j_in2 = _gather_d2d_job(j_in1.out)
    _comm_now("gather_w_in_d2d", [j_in2])
    w_in4 = j_in2.out[0]
    j_abo = _gather_ici_job([w_branch_a.astype(BF16), w_branch_b.astype(BF16), w_out.astype(BF16)])
    j_dn = _gather_ici_job([w_dn_l])
    j_gu = _gather_ici_job([w_gu_l])
    w_full = jnp.concatenate([w_in4[j] for j in range(N_CHIP)], axis=1)
    n_qkv_a = 3 * W_A
    w_p = jnp.concatenate([w_full[:, :n_qkv_a], w_full[:, n_qkv_a + H_A:], w_full[:, n_qkv_a:n_qkv_a + H_A],
                           jnp.zeros((d, LANE - H_A), BF16)], axis=1)

    h1 = _rms_fwd("norm1_fwd", xs, norm1_g)
    qkv = _matmul("proj_qkv", h1, w_p, m=s, n=W_QKV, k=d, out_shape=(s, W_QKV), out_dtype=BF16, jobs=[j_abo])
    j_abo1 = _gather_relay_job(j_abo.out)
    proj_g = _matmul("proj_gates", h1, w_p, m=s, n=2 * d, k=d, b_noff=OFF_GA, out_shape=(s, 2 * d), out_dtype=F32,
                     jobs=[j_abo1])
    fa = _matmul("proj_forget", h1, w_p, m=s, n=LANE, k=d, b_noff=OFF_FA, tn=LANE, out_shape=(s, LANE), out_dtype=F32)
    b_f = _pad_to(b_forget, (1, LANE))
    c_cum = _forget_fwd(fa, b_f)
    c_t = c_cum[:, :H_A].T
    c_col = jnp.broadcast_to(c_t[:, :, None], (H_A, s, LANE))
    c_row = c_t[:, None, :]
    j_abo2 = _gather_d2d_job(j_abo1.out)
    attn_a, lse_col = _fox_fwd(qkv, c_col, c_row, jobs=[j_abo2, j_gu])
    w_a, w_b, w_o = j_abo2.out
    w_o = w_o.reshape(d, d)

    bucket = jnp.asarray(_t5_bucket_table())
    bias = _bias_table(rel_bias.T, bucket).reshape(H_B, BLOCK, 2 * BLOCK)
    j_gu1 = _gather_relay_job(j_gu.out)
    attn_b = _swa_fwd(qkv, bias, attn_sinks, jobs=[j_gu1])

    j_gu2 = _gather_d2d_job(j_gu1.out)
    ya = _matmul("branch_a", attn_a, w_a, m=s, n=d, k=W_A, b_kind="col", tn=_tile(A_SH, TN),
                 out_shape=(s, d), out_dtype=F32, jobs=[j_gu2])
    w_gu = j_gu2.out[0]
    yb = _matmul("branch_b", attn_b, w_b, m=s, n=d, k=W_QB, b_kind="col", tn=_tile(A_SH, TN),
                 out_shape=(s, d), out_dtype=F32)
    mixed = _gate_fwd(ya, yb, proj_g)
    x1 = _matmul("out_proj", mixed, w_o, m=s, n=d, k=d, res=xs, out_shape=(s, d), out_dtype=F32)

    h2 = _rms_fwd("norm2_fwd", x1, norm2_g)
    gu = _matmul("ffn_gate_up", h2, w_gu, m=s, n=2 * FP, k=d, b_kind="col", tn=_tile(FF_P, 1408),
                 out_shape=(s, 2 * FP), out_dtype=BF16, jobs=[j_dn])
    j_dn1 = _gather_relay_job(j_dn.out)
    hidden = _swiglu_fwd(gu, jobs=[j_dn1])
    j_dn2 = _gather_d2d_job(j_dn1.out)
    _comm_now("gather_w_down_d2d", [j_dn2])
    w_dn = j_dn2.out[0].reshape(FP, d)
    x2 = _matmul("ffn_down", hidden, w_dn, m=s, n=d, k=FP, tk=_tile(FP, 5632), res=x1, out_shape=(s, d), out_dtype=F32)

    dx2, dx2_b, d_fg, loss_tile = _loss_head(x2, loss_target[0], final_g.reshape(1, d))
    dhidden = _matmul("ffn_down_dx", dx2_b, w_dn, m=s, n=FP, k=d, nt=True, tn=_tile(FF_P, 1408),
                      out_shape=(s, FP), out_dtype=BF16)
    dgu = _swiglu_bwd(dhidden, gu)
    g_dn = _matmul("ffn_down_dw", hidden.T, dx2_b, m=FP, n=d, k=s, tm=_tile(FF_P, 1408, 16),
                   out_shape=(FP, d), out_dtype=BF16).reshape(N_CHIP, FF_P, d)
    j_p_dn = _pair_job([g_dn])
    g_gu = _matmul("ffn_gate_up_dw", h2.T, dgu, m=d, n=2 * FP, k=s, o_kind="col", tm=_tile(d, 512, 16),
                   tn=_tile(FF_P, 1408), out_shape=(2 * N_CHIP, d, FF_P), out_dtype=BF16, jobs=[j_p_dn])
    ps_dn = _pair_sum("pair_sum_w_ffn_down", g_dn, j_p_dn.out[0], place)
    j_c_dn = _chip_job([ps_dn])
    j_p_gu = _pair_job([g_gu])
    dh2 = _matmul("ffn_gate_up_dx", dgu, w_gu, m=s, n=d, k=2 * FP, nt=True, b_kind="col", tn=_tile(d, 1024),
                  tk=_tile(FF_P, TK), out_shape=(s, d), out_dtype=F32, jobs=[j_c_dn, j_p_gu])
    h_dn = _chip_sum("chip_sum_w_ffn_down", ps_dn, j_c_dn.out[0], place)
    ps_gu = _pair_sum("pair_sum_w_ffn_gate_up", g_gu, j_p_gu.out[0], place)
    dx1, dx1_b, d_n2 = _rms_bwd("norm2_bwd", dh2, x1, norm2_g, dx2)

    dmixed = _matmul("out_proj_dx", dx1_b, w_o, m=s, n=d, k=d, nt=True, out_shape=(s, d), out_dtype=BF16)
    dya, dyb, dga, dgb = _gate_bwd(dmixed, ya, yb, proj_g)
    g_a = _matmul("branch_a_dw", attn_a.T, dya, m=W_A, n=d, k=s, o_kind="col", tn=_tile(A_SH, TN),
                  out_shape=(N_CHIP, W_A, A_SH), out_dtype=BF16)
    g_b = _matmul("branch_b_dw", attn_b.T, dyb, m=W_QB, n=d, k=s, o_kind="col", tn=_tile(A_SH, TN),
                  out_shape=(N_CHIP, W_QB, A_SH), out_dtype=BF16)
    j_p_ab = _pair_job([g_a, g_b])
    dattn_a = _matmul("branch_a_dx", dya, w_a, m=s, n=W_A, k=d, nt=True, b_kind="col", tn=_tile(W_A, 1024),
                      tk=_tile(A_SH, TK), out_shape=(s, W_A), out_dtype=BF16, jobs=[j_p_ab])
    dattn_b = _matmul("branch_b_dx", dyb, w_b, m=s, n=W_QB, k=d, nt=True, b_kind="col", tn=_tile(W_QB, 1024),
                      tk=_tile(A_SH, TK), out_shape=(s, W_QB), out_dtype=BF16)
    ps_a, ps_b = (_pair_sum("pair_sum_" + n, g, r, place) for n, g, r in
                  zip(("w_branch_a", "w_branch_b"), (g_a, g_b), j_p_ab.out))

    j_c_gu = _chip_job([ps_gu])
    dq_a, delta_col = _fox_dq(qkv, dattn_a, c_col, c_row, lse_col, jobs=[j_c_gu])
    lse_row = lse_col[:, :, 0][:, None, :]
    delta_row = delta_col[:, :, 0][:, None, :]
    j_c_ab = _chip_job([ps_a, ps_b])
    j_s_dn = _share_job([h_dn])
    dk_a, dv_a, dc_col = _fox_dkv(qkv, dattn_a, c_col, c_row, lse_row, delta_row, jobs=[j_c_ab, j_s_dn])
    dc = _pad_to(dc_col[:, :, 0].T, (s, LANE))
    df, d_bf = _forget_bwd(dc, fa, b_f)
    h_gu = _chip_sum("chip_sum_w_ffn_gate_up", ps_gu, j_c_gu.out[0], place)
    h_a, h_b = (_chip_sum("chip_sum_" + n, p, r, place) for n, p, r in
                zip(("w_branch_a", "w_branch_b"), (ps_a, ps_b), j_c_ab.out))

    j_s_rest = _share_job([h_gu, h_a, h_b])
    dq_b, dk_b, dv_b, dbias, d_sk = _swa_bwd(qkv, dattn_b, bias, attn_sinks, jobs=[j_s_rest])
    r_dn = j_s_dn.out[0]
    r_gu, r_a, r_b = j_s_rest.out
    d_rb = _bias_table_bwd(dbias.reshape(H_B, BLOCK * 2 * BLOCK), bucket).T

    dproj = jnp.concatenate([dq_a, dk_a, dv_a, dq_b, dk_b, dv_b, dga, dgb, df], axis=1)
    g_in_p = _matmul("in_proj_dw", h1.T, dproj, m=d, n=PW, k=s, tn=_tile(PW, 1024), out_shape=(d, PW), out_dtype=BF16)
    g_in_full = jnp.concatenate([g_in_p[:, :n_qkv_a], g_in_p[:, OFF_FA:OFF_FA + H_A], g_in_p[:, n_qkv_a:OFF_FA]], axis=1)
    g_in = jnp.stack([g_in_full[:, j * W_IN_SH:(j + 1) * W_IN_SH] for j in range(N_CHIP)])
    j_p_in = _pair_job([g_in])
    g_o = _matmul("out_proj_dw", mixed.T, dx1_b, m=d, n=d, k=s, out_shape=(d, d), out_dtype=BF16,
                  jobs=[j_p_in]).reshape(N_CHIP, A_SH, d)
    ps_in = _pair_sum("pair_sum_w_in", g_in, j_p_in.out[0], place)
    j_c_in = _chip_job([ps_in])
    j_p_o = _pair_job([g_o])
    dh1 = _matmul("in_proj_dx", dproj, w_p, m=s, n=d, k=PW, nt=True, tn=_tile(d, 1024), tk=_tile(PW, 2560),
                  out_shape=(s, d), out_dtype=F32, jobs=[j_c_in, j_p_o])
    h_in = _chip_sum("chip_sum_w_in", ps_in, j_c_in.out[0], place)
    ps_o = _pair_sum("pair_sum_w_out", g_o, j_p_o.out[0], place)
    j_c_o = _chip_job([ps_o])
    j_s_in = _share_job([h_in])
    grad_x, _, d_n1 = _rms_bwd("norm1_bwd", dh1, xs, norm1_g, dx1, jobs=[j_c_o, j_s_in])
    h_o = _chip_sum("chip_sum_w_out", ps_o, j_c_o.out[0], place)
    j_s_o = _share_job([h_o])

    small = _small_all_reduce(_pack_small(d_n1, d_n2, d_fg, d_bf[:, :H_A], d_sk[:, :H_B], d_rb, loss_tile[0:1, 0:1]))
    loss = small[5, 0]

    grads = {
        "w_branch_a": r_a, "w_branch_b": r_b,
        "w_ffn_gate": r_gu[0:1, :, :FF_SH], "w_ffn_up": r_gu[1:2, :, :FF_SH], "w_ffn_down": r_dn[:, :FF_SH, :],
    }
    given = dict(w_in=(w_in, m_w_in, v_w_in), w_branch_a=(w_branch_a, m_w_branch_a, v_w_branch_a),
                 w_branch_b=(w_branch_b, m_w_branch_b, v_w_branch_b), w_out=(w_out, m_w_out, v_w_out),
                 w_ffn_gate=(w_ffn_gate, m_w_ffn_gate, v_w_ffn_gate), w_ffn_up=(w_ffn_up, m_w_ffn_up, v_w_ffn_up),
                 w_ffn_down=(w_ffn_down, m_w_ffn_down, v_w_ffn_down))

    def col_major(a):
        return jnp.transpose(a[0])[None]

    views = {n: (lambda a: a, lambda a: a) for n in given}
    views["w_ffn_gate"] = views["w_ffn_up"] = (col_major, col_major)
    views["w_in"] = (lambda a: jnp.transpose(a, (2, 0, 1)).reshape(W_IN_SH, d // LANE, LANE),
                     lambda a: jnp.transpose(a.reshape(W_IN_SH, 1, d), (1, 2, 0)))

    def adamw(n, jobs=()):
        to_view, from_view = views[n]
        g_view = to_view(grads[n])
        outs = _adamw("adamw_" + n, to_view(given[n][0]), g_view, to_view(given[n][1]), to_view(given[n][2]), jobs=jobs)
        grads[n] = from_view(g_view)
        return [from_view(o) for o in outs]

    grads["w_in"] = j_s_in.out[0]
    upd = {n: adamw(n) for n in ("w_ffn_gate", "w_ffn_up", "w_ffn_down", "w_branch_a", "w_branch_b", "w_in")}
    sm = _adamw("adamw_small",
                _pack_small(norm1_g, norm2_g, final_g, b_forget, attn_sinks, rel_bias)[None],
                small.at[5].set(0.0)[None],
                _pack_small(m_norm1_g, m_norm2_g, m_final_g, m_b_forget, m_attn_sinks, m_rel_bias)[None],
                _pack_small(v_norm1_g, v_norm2_g, v_final_g, v_b_forget, v_attn_sinks, v_rel_bias)[None],
                jobs=[j_s_o])
    grads["w_out"] = j_s_o.out[0]
    upd["w_out"] = adamw("w_out")
    g_small = _unpack_small(small)
    d_small, m_small, v_small = (_unpack_small(t[0]) for t in sm)

    order = ["norm1_g", "w_in", "b_forget", "attn_sinks", "rel_bias", "w_branch_a", "w_branch_b", "w_out",
             "norm2_g", "w_ffn_gate", "w_ffn_up", "w_ffn_down", "final_g"]
    small_at = {"norm1_g": 0, "norm2_g": 1, "final_g": 2, "b_forget": 3, "attn_sinks": 4, "rel_bias": 5}

    def pick(big_idx, small_src, n):
        return small_src[small_at[n]] if n in small_at else (grads[n] if big_idx is None else upd[n][big_idx])

    return (loss, grad_x[None],
            *[pick(None, g_small, n) for n in order], *[pick(0, d_small, n) for n in order],
            *[pick(1, m_small, n) for n in order], *[pick(2, v_small, n) for n in order])
```

```python
import functools
import math

import numpy as np
import jax
import jax.numpy as jnp
from jax import lax
from jax.experimental import pallas as pl
from jax.experimental.pallas import tpu as pltpu

F32 = jnp.float32
BF16 = jnp.bfloat16
MESH = pl.DeviceIdType.MESH

D_MODEL = 4096
SEQ = 4096
H_A = 16
DH_A = 128
H_B = 32
HKV_B = 4
G_B = H_B // HKV_B
DH_B = 64
WINDOW = 128
NUM_BUCKETS = 32
MAX_DISTANCE = 128
BLOCK = 128
D_FF = ((8 * D_MODEL // 3 + 255) // 256) * 256
EPS = 1e-6
ADAM_LR = 0.001
ADAM_B1 = 0.9
ADAM_B2 = 0.999
ADAM_EPS = 1e-08
ADAM_WD = 0.01
ADAM_STEP = 10

N_CHIP = 4
LANE = 128
VMEM_LIMIT = 56 * 1024 * 1024

TM = 1024
TN = 512
TK = 4096
TQ_A = 512
TR_EW = 256

W_A = H_A * DH_A
W_QB = H_B * DH_B
W_KB = HKV_B * DH_B
OFF_QA = 0
OFF_KA = W_A
OFF_VA = 2 * W_A
OFF_QB = 3 * W_A
OFF_KB = OFF_QB + W_QB
OFF_VB = OFF_KB + W_KB
OFF_GA = OFF_VB + W_KB
OFF_GB = OFF_GA + D_MODEL
OFF_FA = OFF_GB + D_MODEL
PW = OFF_FA + LANE
W_QKV = OFF_GA
W_IN = 3 * W_A + H_A + W_QB + 2 * W_KB + 2 * D_MODEL
W_IN_SH = W_IN // N_CHIP
A_SH = D_MODEL // N_CHIP
FF_SH = D_FF // N_CHIP
FF_P = -(-FF_SH // LANE) * LANE
FP = N_CHIP * FF_P
PACK_ROWS = 16
RB_ROWS = -(-(NUM_BUCKETS * H_B) // D_MODEL)


def _tile(n, target, mult=LANE):
    t = min(target, n) // mult * mult
    while t > mult and n % t:
        t -= mult
    assert t > 0 and n % t == 0, (n, target, mult)
    return t


def _cparams(sem):
    return pltpu.CompilerParams(dimension_semantics=sem, vmem_limit_bytes=VMEM_LIMIT)


def _sigmoid(x):
    return 1.0 / (1.0 + jnp.exp(-x))


def _dot(a, b):
    return jnp.dot(a, b, preferred_element_type=F32)


def _dot_nt(a, b):
    return lax.dot_general(a, b, (((1,), (1,)), ((), ())), preferred_element_type=F32)


ANY = pl.BlockSpec(memory_space=pl.ANY)


class _Job:
    def __init__(self, ins, outs, sems, start, wait, alias=None):
        self.ins, self.outs, self.sems = list(ins), list(outs), list(sems)
        self.start, self.wait, self.alias = start, wait, dict(alias or {})
        self.out = None


def _split(seq, sizes):
    parts, p = [], 0
    for n in sizes:
        parts.append(seq[p:p + n])
        p += n
    return parts


def _pcall(body, args, jobs, *, name, grid, in_specs, out_specs, out_shape, scratch_shapes=(), sem):
    if not jobs:
        return pl.pallas_call(body, name=name, grid=grid, in_specs=in_specs, out_specs=out_specs, out_shape=out_shape,
                              scratch_shapes=list(scratch_shapes), compiler_params=_cparams(sem))(*args)
    single = not isinstance(out_shape, (list, tuple))
    out_specs_l = [out_specs] if single else list(out_specs)
    out_shape_l = [out_shape] if single else list(out_shape)
    cin = [a for j in jobs for a in j.ins]
    cout = [o for j in jobs for o in j.outs]
    csem = [s for j in jobs for s in j.sems]
    sizes = [len(args), len(cin), len(out_shape_l), len(cout), len(scratch_shapes), len(csem)]
    aliases, io, oo = {}, len(args), len(out_shape_l)
    for j in jobs:
        for a, b in j.alias.items():
            aliases[io + a] = oo + b
        io, oo = io + len(j.ins), oo + len(j.outs)

    def wrapped(*refs):
        ins, cins, outs, couts, scr, sems = _split(refs, sizes)
        ids = [pl.program_id(a) for a in range(len(grid))]
        first = functools.reduce(jnp.logical_and, [i == 0 for i in ids])
        last = functools.reduce(jnp.logical_and, [i == g - 1 for i, g in zip(ids, grid)])
        per_job = list(zip(jobs, _split(cins, [len(j.ins) for j in jobs]), _split(couts, [len(j.outs) for j in jobs]),
                           _split(sems, [len(j.sems) for j in jobs])))

        @pl.when(first)
        def _():
            for j, ji, jo, js in per_job:
                j.start(ji, jo, js)

        body(*ins, *outs, *scr)

        @pl.when(last)
        def _():
            for j, ji, jo, js in per_job:
                j.wait(ji, jo, js)

    res = pl.pallas_call(
        wrapped, name=name, grid=grid, in_specs=list(in_specs) + [ANY] * len(cin),
        out_specs=out_specs_l + [ANY] * len(cout), out_shape=out_shape_l + cout,
        scratch_shapes=list(scratch_shapes) + csem, input_output_aliases=aliases,
        compiler_params=_cparams(("arbitrary",) * len(grid)))(*args, *cin)
    main, rest = res[:len(out_shape_l)], res[len(out_shape_l):]
    for j, o in zip(jobs, _split(rest, [len(j.outs) for j in jobs])):
        j.out = list(o)
    return main[0] if single else list(main)


def _comm_now(name, jobs):
    cin = [a for j in jobs for a in j.ins]
    cout = [o for j in jobs for o in j.outs]
    csem = [s for j in jobs for s in j.sems]
    sizes = [len(cin), len(cout), len(csem)]
    aliases, io, oo = {}, 0, 0
    for j in jobs:
        for a, b in j.alias.items():
            aliases[io + a] = oo + b
        io, oo = io + len(j.ins), oo + len(j.outs)

    def body(*refs):
        cins, couts, sems = _split(refs, sizes)
        per_job = list(zip(jobs, _split(cins, [len(j.ins) for j in jobs]), _split(couts, [len(j.outs) for j in jobs]),
                           _split(sems, [len(j.sems) for j in jobs])))
        for j, ji, jo, js in per_job:
            j.start(ji, jo, js)
        for j, ji, jo, js in per_job:
            j.wait(ji, jo, js)

    res = pl.pallas_call(body, name=name, in_specs=[ANY] * len(cin), out_specs=[ANY] * len(cout), out_shape=cout,
                         scratch_shapes=csem, input_output_aliases=aliases)(*cin)
    for j, o in zip(jobs, _split(res, [len(j.outs) for j in jobs])):
        j.out = list(o)


def _slot_spec(shape, kind, br, bc, rc):
    if kind == "2d":
        return pl.BlockSpec((br, bc), lambda *g: rc(*g))
    if kind == "col":
        assert shape[2] % bc == 0, (shape, bc)
        per = shape[2] // bc

        def im_col(*g):
            rb, cb = rc(*g)
            return (cb // per, rb, cb % per)
        return pl.BlockSpec((None, br, bc), im_col)
    assert kind == "row" and shape[1] % br == 0, (shape, kind, br)
    per = shape[1] // br

    def im_row(*g):
        rb, cb = rc(*g)
        return (rb // per, rb % per, cb)
    return pl.BlockSpec((None, br, bc), im_row)


def _matmul(name, a, b, *, m, n, k, nt=False, b_kind="2d", o_kind="2d", out_shape, out_dtype,
            tm=None, tn=None, tk=None, b_noff=0, res=None, b_outer=False, jobs=()):
    tm = tm or _tile(m, TM, 16)
    tn = tn or _tile(n, TN)
    tk = tk or _tile(k, TK)
    assert m % tm == 0 and n % tn == 0 and k % tk == 0 and b_noff % tn == 0
    nk = k // tk
    noff = b_noff // tn
    if b_outer:
        grid = (n // tn, m // tm, nk)
        ij = lambda g0, g1: (g1, g0)
    else:
        grid = (m // tm, n // tn, nk)
        ij = lambda g0, g1: (g0, g1)

    a_spec = pl.BlockSpec((tm, tk), lambda g0, g1, kk: (ij(g0, g1)[0], kk))
    if nt:
        b_spec = _slot_spec(b.shape, b_kind, tn, tk, lambda g0, g1, kk: (ij(g0, g1)[1] + noff, kk))
    else:
        b_spec = _slot_spec(b.shape, b_kind, tk, tn, lambda g0, g1, kk: (kk, ij(g0, g1)[1] + noff))
    o_spec = _slot_spec(out_shape, o_kind, tm, tn, lambda g0, g1, kk: ij(g0, g1))
    in_specs = [a_spec, b_spec]
    args = [a, b]
    if res is not None:
        in_specs.append(pl.BlockSpec((tm, tn), lambda g0, g1, kk: ij(g0, g1)))
        args.append(res)

    def body(*refs):
        a_ref, b_ref = refs[0], refs[1]
        r_ref = refs[2] if res is not None else None
        o_ref = refs[3] if res is not None else refs[2]

        def prod():
            return _dot_nt(a_ref[...], b_ref[...]) if nt else _dot(a_ref[...], b_ref[...])

        def finish(acc):
            if r_ref is not None:
                acc = acc + r_ref[...].astype(F32)
            o_ref[...] = acc.astype(o_ref.dtype)

        if nk == 1:
            finish(prod())
            return
        acc_ref = refs[-1]
        kk = pl.program_id(2)

        @pl.when(kk == 0)
        def _():
            acc_ref[...] = prod()

        @pl.when(jnp.logical_and(kk > 0, kk < nk - 1))
        def _():
            acc_ref[...] += prod()

        @pl.when(kk == nk - 1)
        def _():
            finish(acc_ref[...] + prod())

    return _pcall(
        body, args, jobs, name=name, grid=grid, in_specs=in_specs, out_specs=o_spec,
        out_shape=jax.ShapeDtypeStruct(out_shape, out_dtype),
        scratch_shapes=[pltpu.VMEM((tm, tn), F32)] if nk > 1 else [],
        sem=("parallel", "parallel", "arbitrary"))


def _rms_fwd(name, x, g):
    s, d = x.shape
    tr = _tile(s, TR_EW, 16)

    def body(x_ref, g_ref, h_ref):
        xf = x_ref[...]
        rstd = lax.rsqrt(jnp.mean(xf * xf, axis=-1, keepdims=True) + EPS)
        h_ref[...] = (xf * rstd * g_ref[...]).astype(h_ref.dtype)

    return pl.pallas_call(
        body, name=name, grid=(s // tr,),
        in_specs=[pl.BlockSpec((tr, d), lambda i: (i, 0)), pl.BlockSpec((1, d), lambda i: (0, 0))],
        out_specs=pl.BlockSpec((tr, d), lambda i: (i, 0)),
        out_shape=jax.ShapeDtypeStruct((s, d), BF16),
        compiler_params=_cparams(("parallel",)),
    )(x, g)


def _rms_bwd_rows(dh, xf, g):
    rstd = lax.rsqrt(jnp.mean(xf * xf, axis=-1, keepdims=True) + EPS)
    xhat = xf * rstd
    dxhat = dh * g
    dx = rstd * (dxhat - xhat * jnp.mean(dxhat * xhat, axis=-1, keepdims=True))
    return dx, dh * xhat


def _fold8(v):
    return jnp.sum(v.reshape(v.shape[0] // 8, 8, v.shape[1]), axis=0)


def _rms_bwd(name, dh, x, g, dres, jobs=()):
    s, d = x.shape
    tr = _tile(s, TR_EW, 16)
    n = s // tr

    def body(dh_ref, x_ref, g_ref, r_ref, dx_ref, dxb_ref, dg_ref, acc_ref):
        i = pl.program_id(0)
        dx, dgrows = _rms_bwd_rows(dh_ref[...].astype(F32), x_ref[...], g_ref[...])
        dx = dx + r_ref[...]
        dx_ref[...] = dx
        dxb_ref[...] = dx.astype(BF16)

        @pl.when(i == 0)
        def _():
            acc_ref[...] = jnp.zeros_like(acc_ref)
        acc_ref[...] += _fold8(dgrows)

        @pl.when(i == n - 1)
        def _():
            dg_ref[...] = jnp.sum(acc_ref[...], axis=0, keepdims=True)

    row = pl.BlockSpec((tr, d), lambda i: (i, 0))
    vec = pl.BlockSpec((1, d), lambda i: (0, 0))
    return _pcall(
        body, (dh, x, g, dres), jobs, name=name, grid=(n,), in_specs=[row, row, vec, row], out_specs=[row, row, vec],
        out_shape=[jax.ShapeDtypeStruct((s, d), F32), jax.ShapeDtypeStruct((s, d), BF16),
                   jax.ShapeDtypeStruct((1, d), F32)],
        scratch_shapes=[pltpu.VMEM((8, d), F32)],
        sem=("arbitrary",))


def _loss_head(x2, target, g):
    s, d = x2.shape
    tr = _tile(s, TR_EW, 16)
    n = s // tr

    def body(x_ref, t_ref, g_ref, dx_ref, dxb_ref, dg_ref, loss_ref, acc_ref):
        i = pl.program_id(0)
        xf = x_ref[...]
        gv = g_ref[...]
        rstd = lax.rsqrt(jnp.mean(xf * xf, axis=-1, keepdims=True) + EPS)
        err = xf * rstd * gv - t_ref[...]
        row_loss = jnp.mean(err * err, axis=-1, keepdims=True)
        dx, dgrows = _rms_bwd_rows(err / d, xf, gv)
        dx_ref[...] = dx
        dxb_ref[...] = dx.astype(BF16)

        @pl.when(i == 0)
        def _():
            acc_ref[...] = jnp.zeros_like(acc_ref)
            loss_ref[...] = jnp.zeros_like(loss_ref)
        acc_ref[...] += _fold8(dgrows)
        loss_ref[...] += jnp.broadcast_to(0.5 * jnp.sum(row_loss, axis=0, keepdims=True), (8, LANE))

        @pl.when(i == n - 1)
        def _():
            dg_ref[...] = jnp.sum(acc_ref[...], axis=0, keepdims=True)

    row = pl.BlockSpec((tr, d), lambda i: (i, 0))
    vec = pl.BlockSpec((1, d), lambda i: (0, 0))
    return pl.pallas_call(
        body, name="loss_head", grid=(n,), in_specs=[row, row, vec],
        out_specs=[row, row, vec, pl.BlockSpec((8, LANE), lambda i: (0, 0))],
        out_shape=[jax.ShapeDtypeStruct((s, d), F32), jax.ShapeDtypeStruct((s, d), BF16),
                   jax.ShapeDtypeStruct((1, d), F32), jax.ShapeDtypeStruct((8, LANE), F32)],
        scratch_shapes=[pltpu.VMEM((8, d), F32)],
        compiler_params=_cparams(("arbitrary",)),
    )(x2, target, g)


def _gate_fwd(ya, yb, proj_g):
    s, d = ya.shape
    tr = _tile(s, TR_EW, 16)
    tc = _tile(d, 1024)
    nc = d // tc

    def body(ya_ref, yb_ref, ga_ref, gb_ref, o_ref):
        o_ref[...] = (_sigmoid(ga_ref[...]) * ya_ref[...] + _sigmoid(gb_ref[...]) * yb_ref[...]).astype(o_ref.dtype)

    blk = pl.BlockSpec((tr, tc), lambda i, j: (i, j))
    return pl.pallas_call(
        body, name="gate_fwd", grid=(s // tr, nc),
        in_specs=[blk, blk, blk, pl.BlockSpec((tr, tc), lambda i, j: (i, j + nc))],
        out_specs=blk, out_shape=jax.ShapeDtypeStruct((s, d), BF16),
        compiler_params=_cparams(("parallel", "parallel")),
    )(ya, yb, proj_g, proj_g)


def _gate_bwd(dmixed, ya, yb, proj_g):
    s, d = ya.shape
    tr = _tile(s, TR_EW, 16)
    tc = _tile(d, 1024)
    nc = d // tc

    def body(dm_ref, ya_ref, yb_ref, ga_ref, gb_ref, dya_ref, dyb_ref, dga_ref, dgb_ref):
        dm = dm_ref[...].astype(F32)
        sa = _sigmoid(ga_ref[...])
        sb = _sigmoid(gb_ref[...])
        dya_ref[...] = (dm * sa).astype(BF16)
        dyb_ref[...] = (dm * sb).astype(BF16)
        dga_ref[...] = (dm * ya_ref[...] * sa * (1.0 - sa)).astype(BF16)
        dgb_ref[...] = (dm * yb_ref[...] * sb * (1.0 - sb)).astype(BF16)

    blk = pl.BlockSpec((tr, tc), lambda i, j: (i, j))
    out = jax.ShapeDtypeStruct((s, d), BF16)
    return pl.pallas_call(
        body, name="gate_bwd", grid=(s // tr, nc),
        in_specs=[blk, blk, blk, blk, pl.BlockSpec((tr, tc), lambda i, j: (i, j + nc))],
        out_specs=[blk, blk, blk, blk], out_shape=[out, out, out, out],
        compiler_params=_cparams(("parallel", "parallel")),
    )(dmixed, ya, yb, proj_g, proj_g)


def _swiglu_fwd(gu, jobs=()):
    s = gu.shape[0]
    tr = _tile(s, 128, 16)

    def body(gu_ref, h_ref):
        gate = gu_ref[:, :FF_P].astype(F32)
        up = gu_ref[:, FF_P:].astype(F32)
        h_ref[...] = (gate * _sigmoid(gate) * up).astype(h_ref.dtype)

    return _pcall(
        body, (gu,), jobs, name="swiglu_fwd", grid=(s // tr, N_CHIP),
        in_specs=[pl.BlockSpec((tr, 2 * FF_P), lambda i, j: (i, j))],
        out_specs=pl.BlockSpec((tr, FF_P), lambda i, j: (i, j)),
        out_shape=jax.ShapeDtypeStruct((s, FP), BF16),
        sem=("parallel", "parallel"))


def _swiglu_bwd(dhidden, gu):
    s = gu.shape[0]
    tr = _tile(s, 128, 16)

    def body(dh_ref, gu_ref, dgu_ref):
        gate = gu_ref[:, :FF_P].astype(F32)
        up = gu_ref[:, FF_P:].astype(F32)
        dh = dh_ref[...].astype(F32)
        sg = _sigmoid(gate)
        dgu_ref[:, :FF_P] = (dh * up * sg * (1.0 + gate * (1.0 - sg))).astype(BF16)
        dgu_ref[:, FF_P:] = (dh * gate * sg).astype(BF16)

    return pl.pallas_call(
        body, name="swiglu_bwd", grid=(s // tr, N_CHIP),
        in_specs=[pl.BlockSpec((tr, FF_P), lambda i, j: (i, j)), pl.BlockSpec((tr, 2 * FF_P), lambda i, j: (i, j))],
        out_specs=pl.BlockSpec((tr, 2 * FF_P), lambda i, j: (i, j)),
        out_shape=jax.ShapeDtypeStruct((s, 2 * FP), BF16),
        compiler_params=_cparams(("parallel", "parallel")),
    )(dhidden, gu)


def _tri(n, upper):
    r = lax.broadcasted_iota(jnp.int32, (n, n), 0)
    c = lax.broadcasted_iota(jnp.int32, (n, n), 1)
    return (c >= r if upper else c <= r).astype(F32)


def _forget_fwd(fa, bias):
    s = fa.shape[0]
    tb = _tile(s, 512, 8)

    def body(f_ref, b_ref, c_ref, carry_ref):
        i = pl.program_id(0)

        @pl.when(i == 0)
        def _():
            carry_ref[...] = jnp.zeros_like(carry_ref)
        z = f_ref[...] + b_ref[...]
        logf = jnp.minimum(z, 0.0) - jnp.log(1.0 + jnp.exp(-jnp.abs(z)))
        c = jnp.dot(_tri(tb, False), logf, precision=lax.Precision.HIGHEST, preferred_element_type=F32)
        c_ref[...] = c + carry_ref[0:1, :]
        carry_ref[...] = jnp.broadcast_to(c_ref[tb - 1:tb, :], carry_ref.shape)

    return pl.pallas_call(
        body, name="forget_fwd", grid=(s // tb,),
        in_specs=[pl.BlockSpec((tb, LANE), lambda i: (i, 0)), pl.BlockSpec((1, LANE), lambda i: (0, 0))],
        out_specs=pl.BlockSpec((tb, LANE), lambda i: (i, 0)),
        out_shape=jax.ShapeDtypeStruct((s, LANE), F32),
        scratch_shapes=[pltpu.VMEM((8, LANE), F32)],
        compiler_params=_cparams(("arbitrary",)),
    )(fa, bias)


def _forget_bwd(dc, fa, bias):
    s = fa.shape[0]
    tb = _tile(s, 512, 16)
    n = s // tb

    def body(dc_ref, f_ref, b_ref, df_ref, db_ref, carry_ref, acc_ref, tmp_ref):
        i = pl.program_id(0)

        @pl.when(i == 0)
        def _():
            carry_ref[...] = jnp.zeros_like(carry_ref)
            acc_ref[...] = jnp.zeros_like(acc_ref)
        dlogf = jnp.dot(_tri(tb, True), dc_ref[...], precision=lax.Precision.HIGHEST, preferred_element_type=F32)
        tmp_ref[...] = dlogf + carry_ref[0:1, :]
        carry_ref[...] = jnp.broadcast_to(tmp_ref[0:1, :], carry_ref.shape)
        df = tmp_ref[...] * _sigmoid(-(f_ref[...] + b_ref[...]))
        df_ref[...] = df.astype(BF16)
        acc_ref[...] += _fold8(df)

        @pl.when(i == n - 1)
        def _():
            db_ref[...] = jnp.sum(acc_ref[...], axis=0, keepdims=True)

    rev = pl.BlockSpec((tb, LANE), lambda i: (n - 1 - i, 0))
    vec = pl.BlockSpec((1, LANE), lambda i: (0, 0))
    return pl.pallas_call(
        body, name="forget_bwd", grid=(n,), in_specs=[rev, rev, vec], out_specs=[rev, vec],
        out_shape=[jax.ShapeDtypeStruct((s, LANE), BF16), jax.ShapeDtypeStruct((1, LANE), F32)],
        scratch_shapes=[pltpu.VMEM((8, LANE), F32), pltpu.VMEM((8, LANE), F32), pltpu.VMEM((tb, LANE), F32)],
        compiler_params=_cparams(("arbitrary",)),
    )(dc, fa, bias)


_A_SCALE = DH_A ** -0.5


def _causal(tq, tk, transposed):
    r = lax.broadcasted_iota(jnp.int32, (tq, tk), 0)
    c = lax.broadcasted_iota(jnp.int32, (tq, tk), 1)
    return c >= r if transposed else r >= c


def _fox_fwd(qkv, c_col, c_row, jobs=()):
    s = qkv.shape[0]
    t = _tile(s, TQ_A)
    n = s // t
    kb, vb = OFF_KA // DH_A, OFF_VA // DH_A

    def body(q_ref, k_ref, v_ref, cq_ref, ck_ref, o_ref, lse_ref, m_s, l_s, acc_s):
        i, j = pl.program_id(1), pl.program_id(2)

        @pl.when(j == 0)
        def _():
            m_s[...] = jnp.full_like(m_s, -jnp.inf)
            l_s[...] = jnp.zeros_like(l_s)
            acc_s[...] = jnp.zeros_like(acc_s)

        def step(diag):
            sc = _dot_nt(q_ref[...], k_ref[...]) * _A_SCALE + (cq_ref[:, 0:1] - ck_ref[...])
            if diag:
                sc = jnp.where(_causal(t, t, False), sc, -jnp.inf)
            m_prev = m_s[...]
            m_new = jnp.maximum(m_prev, jnp.max(sc, axis=1, keepdims=True))
            alpha = jnp.exp(m_prev - m_new)
            p = jnp.exp(sc - m_new[:, 0:1])
            l_s[...] = alpha * l_s[...] + jnp.sum(p, axis=1, keepdims=True)
            acc_s[...] = alpha * acc_s[...] + _dot(p.astype(BF16), v_ref[...])
            m_s[...] = m_new

        @pl.when(j < i)
        def _():
            step(False)

        @pl.when(j == i)
        def _():
            step(True)
            o_ref[...] = (acc_s[...] / l_s[...]).astype(o_ref.dtype)
            lse_ref[...] = m_s[...] + jnp.log(l_s[...])

    jc = lambda i, j: jnp.minimum(j, i)
    return _pcall(
        body, (qkv, qkv, qkv, c_col, c_row), jobs, name="fox_fwd", grid=(H_A, n, n),
        in_specs=[pl.BlockSpec((t, DH_A), lambda h, i, j: (i, h)),
                  pl.BlockSpec((t, DH_A), lambda h, i, j: (jc(i, j), kb + h)),
                  pl.BlockSpec((t, DH_A), lambda h, i, j: (jc(i, j), vb + h)),
                  pl.BlockSpec((None, t, LANE), lambda h, i, j: (h, i, 0)),
                  pl.BlockSpec((None, 1, t), lambda h, i, j: (h, 0, jc(i, j)))],
        out_specs=[pl.BlockSpec((t, DH_A), lambda h, i, j: (i, h)),
                   pl.BlockSpec((None, t, LANE), lambda h, i, j: (h, i, 0))],
        out_shape=[jax.ShapeDtypeStruct((s, W_A), BF16), jax.ShapeDtypeStruct((H_A, s, LANE), F32)],
        scratch_shapes=[pltpu.VMEM((t, LANE), F32)] * 3,
        sem=("parallel", "parallel", "arbitrary"))


def _fox_dq(qkv, do, c_col, c_row, lse_col, jobs=()):
    s = qkv.shape[0]
    t = _tile(s, TQ_A)
    n = s // t
    kb, vb = OFF_KA // DH_A, OFF_VA // DH_A

    def body(q_ref, k_ref, v_ref, do_ref, cq_ref, ck_ref, lse_ref, dq_ref, dl_ref, pdk_s, pk_s, dl_s):
        i, j = pl.program_id(1), pl.program_id(2)

        @pl.when(j == 0)
        def _():
            pdk_s[...] = jnp.zeros_like(pdk_s)
            pk_s[...] = jnp.zeros_like(pk_s)
            dl_s[...] = jnp.zeros_like(dl_s)

        def step(diag):
            sc = _dot_nt(q_ref[...], k_ref[...]) * _A_SCALE + (cq_ref[:, 0:1] - ck_ref[...])
            if diag:
                sc = jnp.where(_causal(t, t, False), sc, -jnp.inf)
            p = jnp.exp(sc - lse_ref[:, 0:1])
            pdp = p * _dot_nt(do_ref[...], v_ref[...])
            dl_s[...] += jnp.sum(pdp, axis=1, keepdims=True)
            pdk_s[...] += _dot(pdp.astype(BF16), k_ref[...])
            pk_s[...] += _dot(p.astype(BF16), k_ref[...])

        @pl.when(j < i)
        def _():
            step(False)

        @pl.when(j == i)
        def _():
            step(True)
            dq_ref[...] = ((pdk_s[...] - dl_s[...] * pk_s[...]) * _A_SCALE).astype(dq_ref.dtype)
            dl_ref[...] = dl_s[...]

    jc = lambda i, j: jnp.minimum(j, i)
    col = pl.BlockSpec((None, t, LANE), lambda h, i, j: (h, i, 0))
    return _pcall(
        body, (qkv, qkv, qkv, do, c_col, c_row, lse_col), jobs, name="fox_dq", grid=(H_A, n, n),
        in_specs=[pl.BlockSpec((t, DH_A), lambda h, i, j: (i, h)),
                  pl.BlockSpec((t, DH_A), lambda h, i, j: (jc(i, j), kb + h)),
                  pl.BlockSpec((t, DH_A), lambda h, i, j: (jc(i, j), vb + h)),
                  pl.BlockSpec((t, DH_A), lambda h, i, j: (i, h)),
                  col,
                  pl.BlockSpec((None, 1, t), lambda h, i, j: (h, 0, jc(i, j))),
                  col],
        out_specs=[pl.BlockSpec((t, DH_A), lambda h, i, j: (i, h)), col],
        out_shape=[jax.ShapeDtypeStruct((s, W_A), BF16), jax.ShapeDtypeStruct((H_A, s, LANE), F32)],
        scratch_shapes=[pltpu.VMEM((t, DH_A), F32), pltpu.VMEM((t, DH_A), F32), pltpu.VMEM((t, LANE), F32)],
        sem=("parallel", "parallel", "arbitrary"))


def _fox_dkv(qkv, do, c_col, c_row, lse_row, delta_row, jobs=()):
    s = qkv.shape[0]
    t = _tile(s, TQ_A)
    n = s // t
    kb, vb = OFF_KA // DH_A, OFF_VA // DH_A

    def body(q_ref, k_ref, v_ref, do_ref, cq_ref, ck_ref, lse_ref, dl_ref, dk_ref, dv_ref, dc_ref,
             dk_s, dv_s, dc_s):
        j, i = pl.program_id(1), pl.program_id(2)

        @pl.when(i == 0)
        def _():
            dk_s[...] = jnp.zeros_like(dk_s)
            dv_s[...] = jnp.zeros_like(dv_s)
            dc_s[...] = jnp.zeros_like(dc_s)

        def step(diag):
            st = _dot_nt(k_ref[...], q_ref[...]) * _A_SCALE + (cq_ref[...] - ck_ref[:, 0:1])
            if diag:
                st = jnp.where(_causal(t, t, True), st, -jnp.inf)
            pt = jnp.exp(st - lse_ref[...])
            dv_s[...] += _dot(pt.astype(BF16), do_ref[...])
            dpt = _dot_nt(v_ref[...], do_ref[...])
            dst = pt * (dpt - dl_ref[...])
            dk_s[...] += _dot(dst.astype(BF16), q_ref[...])
            dc_s[...] -= jnp.sum(dst, axis=1, keepdims=True)

        @pl.when(i == j)
        def _():
            step(True)

        @pl.when(i > j)
        def _():
            step(False)

        @pl.when(i == n - 1)
        def _():
            dk_ref[...] = (dk_s[...] * _A_SCALE).astype(dk_ref.dtype)
            dv_ref[...] = dv_s[...].astype(dv_ref.dtype)
            dc_ref[...] = dc_s[...]

    ic = lambda j, i: jnp.maximum(i, j)
    rowq = pl.BlockSpec((None, 1, t), lambda h, j, i: (h, 0, ic(j, i)))
    kv_out = pl.BlockSpec((t, DH_A), lambda h, j, i: (j, h))
    return _pcall(
        body, (qkv, qkv, qkv, do, c_row, c_col, lse_row, delta_row), jobs, name="fox_dkv", grid=(H_A, n, n),
        in_specs=[pl.BlockSpec((t, DH_A), lambda h, j, i: (ic(j, i), h)),
                  pl.BlockSpec((t, DH_A), lambda h, j, i: (j, kb + h)),
                  pl.BlockSpec((t, DH_A), lambda h, j, i: (j, vb + h)),
                  pl.BlockSpec((t, DH_A), lambda h, j, i: (ic(j, i), h)),
                  rowq,
                  pl.BlockSpec((None, t, LANE), lambda h, j, i: (h, j, 0)),
                  rowq, rowq],
        out_specs=[kv_out, kv_out, pl.BlockSpec((None, t, LANE), lambda h, j, i: (h, j, 0))],
        out_shape=[jax.ShapeDtypeStruct((s, W_A), BF16), jax.ShapeDtypeStruct((s, W_A), BF16),
                   jax.ShapeDtypeStruct((H_A, s, LANE), F32)],
        scratch_shapes=[pltpu.VMEM((t, DH_A), F32), pltpu.VMEM((t, DH_A), F32), pltpu.VMEM((t, LANE), F32)],
        sem=("parallel", "parallel", "arbitrary"))


_B_SCALE = DH_B ** -0.5
_HALF = LANE // 2


def _t5_bucket_table():
    ql = np.arange(BLOCK)[:, None]
    kl = np.arange(2 * BLOCK)[None, :]
    dist = np.clip(ql + BLOCK - kl, 0, None)
    max_exact = NUM_BUCKETS // 2
    large = max_exact + (np.log(np.maximum(dist, 1) / max_exact) / np.log(MAX_DISTANCE / max_exact)
                         * (NUM_BUCKETS - max_exact)).astype(np.int64)
    large = np.minimum(large, NUM_BUCKETS - 1)
    return np.where(dist < max_exact, dist, large).astype(np.int32).reshape(1, BLOCK * 2 * BLOCK)


def _one_hot_buckets(bucket_ref, n):
    return (lax.broadcasted_iota(jnp.int32, (NUM_BUCKETS, n), 0) == bucket_ref[...]).astype(F32)


def _bias_table(rel_bias_t, bucket):
    nqk = bucket.shape[1]
    tc = _tile(nqk, 4096)

    def body(rb_ref, bk_ref, o_ref):
        o_ref[...] = jnp.dot(rb_ref[...], _one_hot_buckets(bk_ref, tc), precision=lax.Precision.HIGHEST,
                             preferred_element_type=F32)

    return pl.pallas_call(
        body, name="bias_table", grid=(nqk // tc,),
        in_specs=[pl.BlockSpec((H_B, NUM_BUCKETS), lambda i: (0, 0)), pl.BlockSpec((1, tc), lambda i: (0, i))],
        out_specs=pl.BlockSpec((H_B, tc), lambda i: (0, i)),
        out_shape=jax.ShapeDtypeStruct((H_B, nqk), F32),
        compiler_params=_cparams(("parallel",)),
    )(rel_bias_t, bucket)


def _bias_table_bwd(dbias, bucket):
    nqk = bucket.shape[1]
    tc = _tile(nqk, 4096)
    n = nqk // tc

    def body(db_ref, bk_ref, o_ref):
        i = pl.program_id(0)

        @pl.when(i == 0)
        def _():
            o_ref[...] = jnp.zeros_like(o_ref)
        o_ref[...] += lax.dot_general(db_ref[...], _one_hot_buckets(bk_ref, tc), (((1,), (1,)), ((), ())),
                                      precision=lax.Precision.HIGHEST, preferred_element_type=F32)

    return pl.pallas_call(
        body, name="bias_table_bwd", grid=(n,),
        in_specs=[pl.BlockSpec((H_B, tc), lambda i: (0, i)), pl.BlockSpec((1, tc), lambda i: (0, i))],
        out_specs=pl.BlockSpec((H_B, NUM_BUCKETS), lambda i: (0, 0)),
        out_shape=jax.ShapeDtypeStruct((H_B, NUM_BUCKETS), F32),
        compiler_params=_cparams(("arbitrary",)),
    )(dbias, bucket)


def _lane_lo():
    return lax.broadcasted_iota(jnp.int32, (1, LANE), 1) < _HALF


def _dup_kv_head(cat, hk):
    xcol = cat[:, (hk // 2) * LANE:(hk // 2 + 1) * LANE].astype(F32)
    swapped = pltpu.roll(xcol, _HALF, 1)
    lo = _lane_lo()
    return (jnp.where(lo, xcol, swapped) if hk % 2 == 0 else jnp.where(lo, swapped, xcol)).astype(BF16)


def _band_mask(first_block):
    ql = lax.broadcasted_iota(jnp.int32, (BLOCK, 2 * BLOCK), 0)
    kl = lax.broadcasted_iota(jnp.int32, (BLOCK, 2 * BLOCK), 1)
    dist = ql + BLOCK - kl
    ok = jnp.logical_and(dist >= 0, dist < WINDOW)
    return jnp.logical_and(ok, jnp.logical_or(jnp.logical_not(first_block), kl >= BLOCK))


def _swa_probs(qh, kdup, bias_h, sink, mask):
    sc = _dot_nt(qh, kdup) * _B_SCALE + bias_h
    sc = jnp.where(mask, sc, -jnp.inf)
    m = jnp.maximum(jnp.max(sc, axis=1, keepdims=True), sink)
    p = jnp.exp(sc - m)
    e_sink = jnp.exp(sink - m)
    inv = 1.0 / (jnp.sum(p, axis=1, keepdims=True) + e_sink)
    return p * inv, e_sink * inv


def _split_pair(ref, col):
    x = ref[:, col * LANE:(col + 1) * LANE].astype(F32)
    lo = _lane_lo()
    return jnp.where(lo, x, 0.0).astype(BF16), jnp.where(lo, 0.0, x).astype(BF16)


def _swa_fwd(qkv, bias, sinks, jobs=()):
    s = qkv.shape[0]
    nb = s // BLOCK
    qb, kb, vb = OFF_QB // W_QB, OFF_KB // W_KB, OFF_VB // W_KB
    assert OFF_QB % W_QB == 0 and OFF_KB % W_KB == 0 and OFF_VB % W_KB == 0 and W_KB % LANE == 0 and G_B % 2 == 0

    def body(q_ref, kp_ref, kc_ref, vp_ref, vc_ref, bias_ref, sink_ref, o_ref):
        i = pl.program_id(0)
        mask = _band_mask(i == 0)
        kcat = jnp.concatenate([kp_ref[...], kc_ref[...]], axis=0)
        vcat = jnp.concatenate([vp_ref[...], vc_ref[...]], axis=0)
        lo = _lane_lo()
        for hk in range(HKV_B):
            kdup, vdup = _dup_kv_head(kcat, hk), _dup_kv_head(vcat, hk)
            for pr in range(G_B // 2):
                h0 = hk * G_B + 2 * pr
                q0, q1 = _split_pair(q_ref, h0 // 2)
                p0, _ = _swa_probs(q0, kdup, bias_ref[h0], sink_ref[0, h0], mask)
                p1, _ = _swa_probs(q1, kdup, bias_ref[h0 + 1], sink_ref[0, h0 + 1], mask)
                o0 = _dot(p0.astype(BF16), vdup)
                o1 = _dot(p1.astype(BF16), vdup)
                o_ref[:, (h0 // 2) * LANE:(h0 // 2 + 1) * LANE] = jnp.where(lo, o0, o1).astype(o_ref.dtype)

    prev = lambda i: jnp.maximum(i - 1, 0)
    return _pcall(
        body, (qkv, qkv, qkv, qkv, qkv, bias, sinks), jobs, name="swa_fwd", grid=(nb,),
        in_specs=[pl.BlockSpec((BLOCK, W_QB), lambda i: (i, qb)),
                  pl.BlockSpec((BLOCK, W_KB), lambda i: (prev(i), kb)),
                  pl.BlockSpec((BLOCK, W_KB), lambda i: (i, kb)),
                  pl.BlockSpec((BLOCK, W_KB), lambda i: (prev(i), vb)),
                  pl.BlockSpec((BLOCK, W_KB), lambda i: (i, vb)),
                  pl.BlockSpec((H_B, BLOCK, 2 * BLOCK), lambda i: (0, 0, 0)),
                  pl.BlockSpec(memory_space=pltpu.SMEM)],
        out_specs=pl.BlockSpec((BLOCK, W_QB), lambda i: (i, 0)),
        out_shape=jax.ShapeDtypeStruct((s, W_QB), BF16),
        sem=("parallel",))


def _swa_bwd(qkv, do, bias, sinks, jobs=()):
    s = qkv.shape[0]
    nb = s // BLOCK
    qb, kb, vb = OFF_QB // W_QB, OFF_KB // W_KB, OFF_VB // W_KB

    def body(q_ref, kp_ref, kc_ref, vp_ref, vc_ref, do_ref, bias_ref, sink_ref,
             dq_ref, dk_ref, dv_ref, dbias_ref, dsink_ref, carry_k, carry_v):
        i = pl.program_id(0)
        lo = _lane_lo()

        @pl.when(i == 0)
        def _():
            dbias_ref[...] = jnp.zeros_like(dbias_ref)
            dsink_ref[...] = jnp.zeros_like(dsink_ref)
            carry_k[...] = jnp.zeros_like(carry_k)
            carry_v[...] = jnp.zeros_like(carry_v)

        @pl.when(i < nb)
        def _():
            mask = _band_mask(i == 0)
            kcat = jnp.concatenate([kp_ref[...], kc_ref[...]], axis=0)
            vcat = jnp.concatenate([vp_ref[...], vc_ref[...]], axis=0)
            lane = lax.broadcasted_iota(jnp.int32, (1, LANE), 1)
            dsink = jnp.zeros((1, LANE), F32)
            dk_cols = [jnp.zeros((2 * BLOCK, LANE), F32) for _ in range(W_KB // LANE)]
            dv_cols = [jnp.zeros((2 * BLOCK, LANE), F32) for _ in range(W_KB // LANE)]
            for hk in range(HKV_B):
                kdup, vdup = _dup_kv_head(kcat, hk), _dup_kv_head(vcat, hk)
                dk_acc = jnp.zeros((2 * BLOCK, LANE), F32)
                dv_acc = jnp.zeros((2 * BLOCK, LANE), F32)
                for pr in range(G_B // 2):
                    h0 = hk * G_B + 2 * pr
                    col = h0 // 2
                    qs = _split_pair(q_ref, col)
                    dos = _split_pair(do_ref, col)
                    dqs = []
                    for e in range(2):
                        h = h0 + e
                        p, p_sink = _swa_probs(qs[e], kdup, bias_ref[h], sink_ref[0, h], mask)
                        dp = _dot_nt(dos[e], vdup)
                        delta = jnp.sum(p * dp, axis=1, keepdims=True)
                        ds = p * (dp - delta)
                        dbias_ref[h] += ds
                        dsink = dsink - jnp.where(lane == h, jnp.sum(p_sink * delta, axis=0, keepdims=True), 0.0)
                        dqs.append(_dot(ds.astype(BF16), kdup))
                        dk_acc = dk_acc + _dot(ds.T.astype(BF16), qs[e])
                        dv_acc = dv_acc + _dot(p.T.astype(BF16), dos[e])
                    dq_ref[:, col * LANE:(col + 1) * LANE] = (jnp.where(lo, dqs[0], dqs[1]) * _B_SCALE).astype(dq_ref.dtype)
                dk_tot = (dk_acc + pltpu.roll(dk_acc, _HALF, 1)) * _B_SCALE
                dv_tot = dv_acc + pltpu.roll(dv_acc, _HALF, 1)
                mine = lo if hk % 2 == 0 else jnp.logical_not(lo)
                dk_cols[hk // 2] = jnp.where(mine, dk_tot, dk_cols[hk // 2])
                dv_cols[hk // 2] = jnp.where(mine, dv_tot, dv_cols[hk // 2])
            dsink_ref[...] += dsink
            dk_cat = jnp.concatenate(dk_cols, axis=1)
            dv_cat = jnp.concatenate(dv_cols, axis=1)
            dk_ref[...] = (carry_k[...] + dk_cat[:BLOCK]).astype(dk_ref.dtype)
            dv_ref[...] = (carry_v[...] + dv_cat[:BLOCK]).astype(dv_ref.dtype)
            carry_k[...] = dk_cat[BLOCK:]
            carry_v[...] = dv_cat[BLOCK:]

        @pl.when(i == nb)
        def _():
            dk_ref[...] = carry_k[...].astype(dk_ref.dtype)
            dv_ref[...] = carry_v[...].astype(dv_ref.dtype)

    cur = lambda i: jnp.minimum(i, nb - 1)
    prev = lambda i: jnp.clip(i - 1, 0, nb - 1)
    kv_out = pl.BlockSpec((BLOCK, W_KB), lambda i: (prev(i), 0))
    return _pcall(
        body, (qkv, qkv, qkv, qkv, qkv, do, bias, sinks), jobs, name="swa_bwd", grid=(nb + 1,),
        in_specs=[pl.BlockSpec((BLOCK, W_QB), lambda i: (cur(i), qb)),
                  pl.BlockSpec((BLOCK, W_KB), lambda i: (prev(i), kb)),
                  pl.BlockSpec((BLOCK, W_KB), lambda i: (cur(i), kb)),
                  pl.BlockSpec((BLOCK, W_KB), lambda i: (prev(i), vb)),
                  pl.BlockSpec((BLOCK, W_KB), lambda i: (cur(i), vb)),
                  pl.BlockSpec((BLOCK, W_QB), lambda i: (cur(i), 0)),
                  pl.BlockSpec((H_B, BLOCK, 2 * BLOCK), lambda i: (0, 0, 0)),
                  pl.BlockSpec(memory_space=pltpu.SMEM)],
        out_specs=[pl.BlockSpec((BLOCK, W_QB), lambda i: (cur(i), 0)), kv_out, kv_out,
                   pl.BlockSpec((H_B, BLOCK, 2 * BLOCK), lambda i: (0, 0, 0)),
                   pl.BlockSpec((1, LANE), lambda i: (0, 0))],
        out_shape=[jax.ShapeDtypeStruct((s, W_QB), BF16), jax.ShapeDtypeStruct((s, W_KB), BF16),
                   jax.ShapeDtypeStruct((s, W_KB), BF16),
                   jax.ShapeDtypeStruct((H_B, BLOCK, 2 * BLOCK), F32), jax.ShapeDtypeStruct((1, LANE), F32)],
        scratch_shapes=[pltpu.VMEM((BLOCK, W_KB), F32), pltpu.VMEM((BLOCK, W_KB), F32)],
        sem=("arbitrary",))


_RELS = ((1, 0), (0, 1), (1, 1))


def _place():
    x, y, c = lax.axis_index("x"), lax.axis_index("y"), lax.axis_index("c")
    return x, y, c


def _rel_chip(x, y, rel):
    px = 1 - x if rel[0] else x
    py = 1 - y if rel[1] else y
    return px, py, 2 * px + py


def _half_rows(ref_shape, c):
    half = ref_shape[-2] // 2
    return pl.ds(pl.multiple_of(c * half, 16), half)


def _row_chunks(start, size):
    n = next(n for n in (8, 4, 2, 1) if size % (16 * n) == 0)
    step = size // n
    if isinstance(start, int):
        return [pl.ds(start + i * step, step) for i in range(n)]
    return [pl.ds(pl.multiple_of(start + i * step, 16), step) for i in range(n)]


def _dma_sems(*shape):
    return pltpu.SemaphoreType.DMA(shape)


def _gather_ici_job(srcs):
    nt = len(srcs)

    def copies(src, dst, sems):
        send, recv = sems
        x, y, c = _place()
        me_chip = 2 * x + y
        chunks, sends, recvs = [], [], []
        for t in range(nt):
            ns = src[t].shape[0]
            half = src[t].shape[-2] // 2
            for k, rel in enumerate(_RELS[:2]):
                px, py, chip = _rel_chip(x, y, rel)

                def mk(into, rows):
                    return pltpu.make_async_remote_copy(
                        src_ref=src[t].at[:, rows], dst_ref=dst[t].at[pl.ds(into * ns, ns), rows],
                        send_sem=send.at[t, k], recv_sem=recv.at[t, k], device_id=(px, py, c), device_id_type=MESH)
                chunks += [mk(me_chip, r) for r in _row_chunks(c * half, half)]
                sends.append(mk(me_chip, _half_rows(src[t].shape, c)))
                recvs.append(mk(chip, _half_rows(src[t].shape, c)))
        return chunks, sends, recvs

    def start(src, dst, sems):
        for cp in copies(src, dst, sems)[0]:
            cp.start()

    def wait(src, dst, sems):
        _, sends, recvs = copies(src, dst, sems)
        for cp in recvs:
            cp.wait_recv()
        for cp in sends:
            cp.wait_send()

    return _Job(srcs, [jax.ShapeDtypeStruct((N_CHIP * a.shape[0],) + a.shape[1:], a.dtype) for a in srcs],
                [_dma_sems(nt, 2), _dma_sems(nt, 2)], start, wait)


def _with_own(gathered, srcs):
    chip = 2 * lax.axis_index("x") + lax.axis_index("y")
    return [lax.dynamic_update_slice(g, s, (chip * s.shape[0], 0, 0)) for g, s in zip(gathered, srcs)]


def _gather_relay_job(gathered):
    nt = len(gathered)

    def copies(dst, sems):
        send, recv = sems
        x, y, c = _place()
        diag = 2 * (1 - x) + (1 - y)
        chunks, sends, recvs = [], [], []
        for t in range(nt):
            ns = dst[t].shape[0] // N_CHIP
            quarter = dst[t].shape[-2] // 4
            for k, rel in enumerate(_RELS[:2]):
                px, py, _ = _rel_chip(x, y, rel)
                _, _, origin = _rel_chip(x, y, _RELS[1 - k])

                def mk(chip, rows):
                    part = dst[t].at[pl.ds(chip * ns, ns), rows]
                    return pltpu.make_async_remote_copy(
                        src_ref=part, dst_ref=part, send_sem=send.at[t, k], recv_sem=recv.at[t, k],
                        device_id=(px, py, c), device_id_type=MESH)
                whole = pl.ds(pl.multiple_of((2 * c + k) * quarter, 16), quarter)
                chunks += [mk(origin, r) for r in _row_chunks((2 * c + k) * quarter, quarter)]
                sends.append(mk(origin, whole))
                recvs.append(mk(diag, whole))
        return chunks, sends, recvs

    def start(_, dst, sems):
        for cp in copies(dst, sems)[0]:
            cp.start()

    def wait(_, dst, sems):
        _, sends, recvs = copies(dst, sems)
        for cp in recvs:
            cp.wait_recv()
        for cp in sends:
            cp.wait_send()

    return _Job(gathered, [jax.ShapeDtypeStruct(a.shape, a.dtype) for a in gathered],
                [_dma_sems(nt, 2), _dma_sems(nt, 2)], start, wait, alias={t: t for t in range(nt)})


def _gather_d2d_job(gathered):
    nt = len(gathered)

    def copies(dst, sems):
        send, recv = sems
        x, y, c = _place()
        sends, recvs = [], []
        for t in range(nt):
            ns = dst[t].shape[0] // N_CHIP
            for k, rel in enumerate(_RELS):
                _, _, chip = _rel_chip(x, y, rel)
                for half, lst in ((c, sends), (1 - c, recvs)):
                    part = dst[t].at[pl.ds(chip * ns, ns), _half_rows(dst[t].shape, half)]
                    lst.append(pltpu.make_async_remote_copy(
                        src_ref=part, dst_ref=part, send_sem=send.at[t, k], recv_sem=recv.at[t, k],
                        device_id=(x, y, 1 - c), device_id_type=MESH))
        return sends, recvs

    def start(_, dst, sems):
        for cp in copies(dst, sems)[0]:
            cp.start()

    def wait(_, dst, sems):
        sends, recvs = copies(dst, sems)
        for cp in recvs:
            cp.wait_recv()
        for cp in sends:
            cp.wait_send()

    return _Job(gathered, [jax.ShapeDtypeStruct(a.shape, a.dtype) for a in gathered],
                [_dma_sems(nt, 3), _dma_sems(nt, 3)], start, wait, alias={t: t for t in range(nt)})


def _pair_job(grads):
    nt = len(grads)

    def copies(g, got, sems):
        send, recv = sems
        x, y, c = _place()
        return [pltpu.make_async_remote_copy(
            src_ref=g[t].at[:, _half_rows(g[t].shape, 1 - c)], dst_ref=got[t], send_sem=send.at[t],
            recv_sem=recv.at[t], device_id=(x, y, 1 - c), device_id_type=MESH) for t in range(nt)]

    def start(g, got, sems):
        for cp in copies(g, got, sems):
            cp.start()

    def wait(g, got, sems):
        for cp in copies(g, got, sems):
            cp.wait()

    return _Job(grads, [jax.ShapeDtypeStruct((a.shape[0], a.shape[1] // 2, a.shape[2]), a.dtype) for a in grads],
                [_dma_sems(nt), _dma_sems(nt)], start, wait)


def _ew_rows(rows, cols, itemsize=4):
    tr = 16
    while tr * 2 <= 128 and rows % (tr * 2) == 0 and tr * 2 * cols * itemsize <= (1 << 20):
        tr *= 2
    assert rows % tr == 0, (rows, tr)
    return tr


def _pair_sum(name, grad, got, place):
    ns, rows, cols = grad.shape
    half = rows // 2
    tr = _ew_rows(half, cols)
    nh = half // tr

    def body(p_ref, g_ref, r_ref, o_ref):
        o_ref[...] = (g_ref[...].astype(F32) + r_ref[...].astype(F32)).astype(o_ref.dtype)

    return pl.pallas_call(
        body, name=name,
        grid_spec=pltpu.PrefetchScalarGridSpec(
            num_scalar_prefetch=1, grid=(ns, nh),
            in_specs=[pl.BlockSpec((None, tr, cols), lambda s, i, p: (s, p[1] * nh + i, 0)),
                      pl.BlockSpec((None, tr, cols), lambda s, i, p: (s, i, 0))],
            out_specs=pl.BlockSpec((None, tr, cols), lambda s, i, p: (s, i, 0))),
        out_shape=jax.ShapeDtypeStruct((ns, half, cols), BF16),
        compiler_params=_cparams(("parallel", "parallel")),
    )(place, grad, got)


def _chip_job(psums):
    nt = len(psums)

    def copies(p, got, sems):
        send, recv = sems
        x, y, c = _place()
        chunks, whole = [], []
        for t in range(nt):
            ns = p[t].shape[0] // N_CHIP
            for k, rel in enumerate(_RELS):
                px, py, chip = _rel_chip(x, y, rel)

                def mk(rows):
                    return pltpu.make_async_remote_copy(
                        src_ref=p[t].at[pl.ds(chip * ns, ns), rows], dst_ref=got[t].at[k, :, rows],
                        send_sem=send.at[t, k], recv_sem=recv.at[t, k], device_id=(px, py, c), device_id_type=MESH)
                chunks += [mk(r) for r in _row_chunks(0, p[t].shape[1])]
                whole.append(mk(pl.ds(0, p[t].shape[1])))
        return chunks, whole

    def start(p, got, sems):
        for cp in copies(p, got, sems)[0]:
            cp.start()

    def wait(p, got, sems):
        for cp in copies(p, got, sems)[1]:
            cp.wait()

    return _Job(psums, [jax.ShapeDtypeStruct((3, a.shape[0] // N_CHIP) + a.shape[1:], a.dtype) for a in psums],
                [_dma_sems(nt, 3), _dma_sems(nt, 3)], start, wait)


def _chip_sum(name, psum, got, place):
    ns4, half, cols = psum.shape
    ns = ns4 // N_CHIP
    tr = _ew_rows(half, cols)
    nh = half // tr

    def body(p_ref, mine_ref, got_ref, o_ref):
        acc = mine_ref[...].astype(F32)
        for k in range(3):
            acc = acc + got_ref[k].astype(F32)
        o_ref[...] = acc

    return pl.pallas_call(
        body, name=name,
        grid_spec=pltpu.PrefetchScalarGridSpec(
            num_scalar_prefetch=1, grid=(ns, nh),
            in_specs=[pl.BlockSpec((None, tr, cols), lambda s, i, p: (p[0] * ns + s, i, 0)),
                      pl.BlockSpec((3, None, tr, cols), lambda s, i, p: (0, s, i, 0))],
            out_specs=pl.BlockSpec((None, tr, cols), lambda s, i, p: (s, p[1] * nh + i, 0))),
        out_shape=jax.ShapeDtypeStruct((ns, 2 * half, cols), F32),
        compiler_params=_cparams(("parallel", "parallel")),
    )(place, psum, got)


def _share_job(halves):
    nt = len(halves)

    def copies(full, sems):
        send, recv = sems
        x, y, c = _place()
        sends, recvs = [], []
        for t in range(nt):
            for half, lst in ((c, sends), (1 - c, recvs)):
                part = full[t].at[:, _half_rows(full[t].shape, half)]
                lst.append(pltpu.make_async_remote_copy(
                    src_ref=part, dst_ref=part, send_sem=send.at[t], recv_sem=recv.at[t],
                    device_id=(x, y, 1 - c), device_id_type=MESH))
        return sends, recvs

    def start(_, full, sems):
        for cp in copies(full, sems)[0]:
            cp.start()

    def wait(_, full, sems):
        sends, recvs = copies(full, sems)
        for cp in sends:
            cp.wait_send()
        for cp in recvs:
            cp.wait_recv()

    return _Job(halves, [jax.ShapeDtypeStruct(a.shape, a.dtype) for a in halves],
                [_dma_sems(nt), _dma_sems(nt)], start, wait, alias={t: t for t in range(nt)})


def _small_all_reduce(pack):
    rows, d = pack.shape

    def body(x_ref, o_ref, land, send, recv):
        x, y, c = _place()
        me = 4 * x + 2 * y + c
        land[me] = x_ref[...]
        cps = []
        for k in range(1, 8):
            to = (1 - x if k & 4 else x, 1 - y if k & 2 else y, 1 - c if k & 1 else c)
            cps.append(pltpu.make_async_remote_copy(
                src_ref=x_ref, dst_ref=land.at[me], send_sem=send.at[k - 1], recv_sem=recv.at[k - 1],
                device_id=to, device_id_type=MESH))
        for cp in cps:
            cp.start()
        for cp in cps:
            cp.wait()
        acc = land[0]
        for dev in range(1, 8):
            acc = acc + land[dev]
        o_ref[...] = acc

    vm = pl.BlockSpec(memory_space=pltpu.VMEM)
    return pl.pallas_call(
        body, name="small_all_reduce", in_specs=[vm], out_specs=vm,
        out_shape=jax.ShapeDtypeStruct((rows, d), F32),
        scratch_shapes=[pltpu.VMEM((8, rows, d), F32), pltpu.SemaphoreType.DMA((7,)), pltpu.SemaphoreType.DMA((7,))],
    )(pack)


def _adamw(name, w, g, m, v, jobs=()):
    lead, rows, cols = w.shape
    fits = [t for t in range(8, rows + 1, 8) if rows % t == 0 and t * cols * 4 <= (1 << 20)]
    tr = fits[-1] if fits else rows
    tl = 1
    if tr == rows:
        tl = max(t for t in range(1, lead + 1) if lead % t == 0 and t * rows * cols * 4 <= (1 << 20))

    def body(w_ref, g_ref, m_ref, v_ref, d_ref, nm_ref, nv_ref):
        gv = g_ref[...]
        nm = ADAM_B1 * m_ref[...] + (1.0 - ADAM_B1) * gv
        nv = ADAM_B2 * v_ref[...] + (1.0 - ADAM_B2) * (gv * gv)
        m_hat = nm / (1.0 - ADAM_B1 ** ADAM_STEP)
        v_hat = nv / (1.0 - ADAM_B2 ** ADAM_STEP)
        d_ref[...] = -ADAM_LR * (m_hat / (jnp.sqrt(v_hat) + ADAM_EPS) + ADAM_WD * w_ref[...])
        nm_ref[...] = nm
        nv_ref[...] = nv

    blk = pl.BlockSpec((tl, tr, cols), lambda l, i: (l, i, 0))
    out = jax.ShapeDtypeStruct(w.shape, F32)
    return _pcall(
        body, (w, g, m, v), jobs, name=name, grid=(lead // tl, rows // tr), in_specs=[blk] * 4, out_specs=[blk] * 3,
        out_shape=[out, out, out], sem=("parallel", "parallel"))


def _pad_to(a, shape):
    return jnp.pad(a, [(0, t - s) for s, t in zip(a.shape, shape)])


def _pack_small(n1, n2, fg, bf, sk, rb, extra=None):
    rows = [n1.reshape(1, D_MODEL), n2.reshape(1, D_MODEL), fg.reshape(1, D_MODEL),
            _pad_to(bf.reshape(1, H_A), (1, D_MODEL)), _pad_to(sk.reshape(1, H_B), (1, D_MODEL)),
            jnp.zeros((1, D_MODEL), F32) if extra is None else _pad_to(extra.reshape(1, 1), (1, D_MODEL)),
            _pad_to(rb.reshape(1, NUM_BUCKETS * H_B), (1, RB_ROWS * D_MODEL)).reshape(RB_ROWS, D_MODEL)]
    return _pad_to(jnp.concatenate(rows, axis=0), (PACK_ROWS, D_MODEL))


def _unpack_small(p):
    return (p[0:1], p[1:2], p[2], p[3:4, :H_A], p[4:5, :H_B],
            p[6:6 + RB_ROWS].reshape(-1)[:NUM_BUCKETS * H_B].reshape(NUM_BUCKETS, H_B))


def kernel(x, norm1_g, w_in, b_forget, attn_sinks, rel_bias, w_branch_a, w_branch_b, w_out, norm2_g, w_ffn_gate, w_ffn_up, w_ffn_down, final_g, loss_target, m_norm1_g, m_w_in, m_b_forget, m_attn_sinks, m_rel_bias, m_w_branch_a, m_w_branch_b, m_w_out, m_norm2_g, m_w_ffn_gate, m_w_ffn_up, m_w_ffn_down, m_final_g, v_norm1_g, v_w_in, v_b_forget, v_attn_sinks, v_rel_bias, v_w_branch_a, v_w_branch_b, v_w_out, v_norm2_g, v_w_ffn_gate, v_w_ffn_up, v_w_ffn_down, v_final_g):
    s, d = SEQ, D_MODEL
    assert x.shape == (1, s, d) and w_in.shape == (1, d, W_IN_SH)
    xs = x[0]
    place = jnp.stack([2 * lax.axis_index("x") + lax.axis_index("y"), lax.axis_index("c")]).astype(jnp.int32)

    w_gu_l = jnp.stack([_pad_to(w_ffn_gate[0], (d, FF_P)), _pad_to(w_ffn_up[0], (d, FF_P))]).astype(BF16)
    w_dn_l = _pad_to(w_ffn_down, (1, FF_P, d)).astype(BF16)
    w_in_l, w_abo_l = w_in.astype(BF16), [w_branch_a.astype(BF16), w_branch_b.astype(BF16), w_out.astype(BF16)]
    j_in = _gather_ici_job([w_in_l])
    _comm_now("gather_w_in_ici", [j_in])
    j_in1 = _gather_relay_job(j_in.out)
    _comm_now("gather_w_in_relay", [j_in1])
    j_in2 = _gather_d2d_job(j_in1.out)
    _comm_now("gather_w_in_d2d", [j_in2])
    w_in4 = _with_own(j_in2.out, [w_in_l])[0]
    j_abo = _gather_ici_job(w_abo_l)
    j_dn = _gather_ici_job([w_dn_l])
    j_gu = _gather_ici_job([w_gu_l])
    w_full = jnp.concatenate([w_in4[j] for j in range(N_CHIP)], axis=1)
    n_qkv_a = 3 * W_A
    w_p = jnp.concatenate([w_full[:, :n_qkv_a], w_full[:, n_qkv_a + H_A:], w_full[:, n_qkv_a:n_qkv_a + H_A],
                           jnp.zeros((d, LANE - H_A), BF16)], axis=1)

    h1 = _rms_fwd("norm1_fwd", xs, norm1_g)
    qkv = _matmul("proj_qkv", h1, w_p, m=s, n=W_QKV, k=d, out_shape=(s, W_QKV), out_dtype=BF16, jobs=[j_abo])
    j_abo1 = _gather_relay_job(j_abo.out)
    proj_g = _matmul("proj_gates", h1, w_p, m=s, n=2 * d, k=d, b_noff=OFF_GA, out_shape=(s, 2 * d), out_dtype=F32,
                     jobs=[j_abo1])
    fa = _matmul("proj_forget", h1, w_p, m=s, n=LANE, k=d, b_noff=OFF_FA, tn=LANE, out_shape=(s, LANE), out_dtype=F32)
    b_f = _pad_to(b_forget, (1, LANE))
    c_cum = _forget_fwd(fa, b_f)
    c_t = c_cum[:, :H_A].T
    c_col = jnp.broadcast_to(c_t[:, :, None], (H_A, s, LANE))
    c_row = c_t[:, None, :]
    j_abo2 = _gather_d2d_job(j_abo1.out)
    attn_a, lse_col = _fox_fwd(qkv, c_col, c_row, jobs=[j_abo2, j_gu])
    w_a, w_b, w_o = _with_own(j_abo2.out, w_abo_l)
    w_o = w_o.reshape(d, d)

    bucket = jnp.asarray(_t5_bucket_table())
    bias = _bias_table(rel_bias.T, bucket).reshape(H_B, BLOCK, 2 * BLOCK)
    j_gu1 = _gather_relay_job(j_gu.out)
    attn_b = _swa_fwd(qkv, bias, attn_sinks, jobs=[j_gu1])

    j_gu2 = _gather_d2d_job(j_gu1.out)
    ya = _matmul("branch_a", attn_a, w_a, m=s, n=d, k=W_A, b_kind="col", tn=_tile(A_SH, TN),
                 out_shape=(s, d), out_dtype=F32, jobs=[j_gu2])
    w_gu = _with_own(j_gu2.out, [w_gu_l])[0]
    yb = _matmul("branch_b", attn_b, w_b, m=s, n=d, k=W_QB, b_kind="col", tn=_tile(A_SH, TN),
                 out_shape=(s, d), out_dtype=F32)
    mixed = _gate_fwd(ya, yb, proj_g)
    x1 = _matmul("out_proj", mixed, w_o, m=s, n=d, k=d, res=xs, out_shape=(s, d), out_dtype=F32)

    h2 = _rms_fwd("norm2_fwd", x1, norm2_g)
    gu = _matmul("ffn_gate_up", h2, w_gu, m=s, n=2 * FP, k=d, b_kind="col", tn=_tile(FF_P, 1408),
                 out_shape=(s, 2 * FP), out_dtype=BF16, jobs=[j_dn])
    j_dn1 = _gather_relay_job(j_dn.out)
    hidden = _swiglu_fwd(gu, jobs=[j_dn1])
    j_dn2 = _gather_d2d_job(j_dn1.out)
    _comm_now("gather_w_down_d2d", [j_dn2])
    w_dn = _with_own(j_dn2.out, [w_dn_l])[0].reshape(FP, d)
    x2 = _matmul("ffn_down", hidden, w_dn, m=s, n=d, k=FP, tk=_tile(FP, 5632), res=x1, out_shape=(s, d), out_dtype=F32)

    dx2, dx2_b, d_fg, loss_tile = _loss_head(x2, loss_target[0], final_g.reshape(1, d))
    dhidden = _matmul("ffn_down_dx", dx2_b, w_dn, m=s, n=FP, k=d, nt=True, tn=_tile(FF_P, 1408),
                      out_shape=(s, FP), out_dtype=BF16)
    dgu = _swiglu_bwd(dhidden, gu)
    g_dn = _matmul("ffn_down_dw", hidden.T, dx2_b, m=FP, n=d, k=s, tm=_tile(FF_P, 1408, 16),
                   out_shape=(FP, d), out_dtype=BF16).reshape(N_CHIP, FF_P, d)
    j_p_dn = _pair_job([g_dn])
    g_gu = _matmul("ffn_gate_up_dw", h2.T, dgu, m=d, n=2 * FP, k=s, o_kind="col", tm=_tile(d, 512, 16),
                   tn=_tile(FF_P, 1408), out_shape=(2 * N_CHIP, d, FF_P), out_dtype=BF16, jobs=[j_p_dn])
    ps_dn = _pair_sum("pair_sum_w_ffn_down", g_dn, j_p_dn.out[0], place)
    j_c_dn = _chip_job([ps_dn])
    j_p_gu = _pair_job([g_gu])
    dh2 = _matmul("ffn_gate_up_dx", dgu, w_gu, m=s, n=d, k=2 * FP, nt=True, b_kind="col", tn=_tile(d, 1024),
                  tk=_tile(FF_P, TK), out_shape=(s, d), out_dtype=F32, jobs=[j_c_dn, j_p_gu])
    h_dn = _chip_sum("chip_sum_w_ffn_down", ps_dn, j_c_dn.out[0], place)
    ps_gu = _pair_sum("pair_sum_w_ffn_gate_up", g_gu, j_p_gu.out[0], place)
    dx1, dx1_b, d_n2 = _rms_bwd("norm2_bwd", dh2, x1, norm2_g, dx2)

    dmixed = _matmul("out_proj_dx", dx1_b, w_o, m=s, n=d, k=d, nt=True, out_shape=(s, d), out_dtype=BF16)
    dya, dyb, dga, dgb = _gate_bwd(dmixed, ya, yb, proj_g)
    g_a = _matmul("branch_a_dw", attn_a.T, dya, m=W_A, n=d, k=s, o_kind="col", tn=_tile(A_SH, TN),
                  out_shape=(N_CHIP, W_A, A_SH), out_dtype=BF16)
    g_b = _matmul("branch_b_dw", attn_b.T, dyb, m=W_QB, n=d, k=s, o_kind="col", tn=_tile(A_SH, TN),
                  out_shape=(N_CHIP, W_QB, A_SH), out_dtype=BF16)
    j_p_ab = _pair_job([g_a, g_b])
    dattn_a = _matmul("branch_a_dx", dya, w_a, m=s, n=W_A, k=d, nt=True, b_kind="col", tn=_tile(W_A, 1024),
                      tk=_tile(A_SH, TK), out_shape=(s, W_A), out_dtype=BF16, jobs=[j_p_ab])
    dattn_b = _matmul("branch_b_dx", dyb, w_b, m=s, n=W_QB, k=d, nt=True, b_kind="col", tn=_tile(W_QB, 1024),
                      tk=_tile(A_SH, TK), out_shape=(s, W_QB), out_dtype=BF16)
    ps_a, ps_b = (_pair_sum("pair_sum_" + n, g, r, place) for n, g, r in
                  zip(("w_branch_a", "w_branch_b"), (g_a, g_b), j_p_ab.out))

    j_c_gu = _chip_job([ps_gu])
    dq_a, delta_col = _fox_dq(qkv, dattn_a, c_col, c_row, lse_col, jobs=[j_c_gu])
    lse_row = lse_col[:, :, 0][:, None, :]
    delta_row = delta_col[:, :, 0][:, None, :]
    j_c_ab = _chip_job([ps_a, ps_b])
    j_s_dn = _share_job([h_dn])
    dk_a, dv_a, dc_col = _fox_dkv(qkv, dattn_a, c_col, c_row, lse_row, delta_row, jobs=[j_c_ab, j_s_dn])
    dc = _pad_to(dc_col[:, :, 0].T, (s, LANE))
    df, d_bf = _forget_bwd(dc, fa, b_f)
    h_gu = _chip_sum("chip_sum_w_ffn_gate_up", ps_gu, j_c_gu.out[0], place)
    h_a, h_b = (_chip_sum("chip_sum_" + n, p, r, place) for n, p, r in
                zip(("w_branch_a", "w_branch_b"), (ps_a, ps_b), j_c_ab.out))

    j_s_rest = _share_job([h_gu, h_a, h_b])
    dq_b, dk_b, dv_b, dbias, d_sk = _swa_bwd(qkv, dattn_b, bias, attn_sinks, jobs=[j_s_rest])
    r_dn = j_s_dn.out[0]
    r_gu, r_a, r_b = j_s_rest.out
    d_rb = _bias_table_bwd(dbias.reshape(H_B, BLOCK * 2 * BLOCK), bucket).T

    dproj = jnp.concatenate([dq_a, dk_a, dv_a, dq_b, dk_b, dv_b, dga, dgb, df], axis=1)
    g_in_p = _matmul("in_proj_dw", h1.T, dproj, m=d, n=PW, k=s, tn=_tile(PW, 1024), out_shape=(d, PW), out_dtype=BF16)
    g_in_full = jnp.concatenate([g_in_p[:, :n_qkv_a], g_in_p[:, OFF_FA:OFF_FA + H_A], g_in_p[:, n_qkv_a:OFF_FA]], axis=1)
    g_in = jnp.stack([g_in_full[:, j * W_IN_SH:(j + 1) * W_IN_SH] for j in range(N_CHIP)])
    j_p_in = _pair_job([g_in])
    g_o = _matmul("out_proj_dw", mixed.T, dx1_b, m=d, n=d, k=s, out_shape=(d, d), out_dtype=BF16,
                  jobs=[j_p_in]).reshape(N_CHIP, A_SH, d)
    ps_in = _pair_sum("pair_sum_w_in", g_in, j_p_in.out[0], place)
    j_c_in = _chip_job([ps_in])
    j_p_o = _pair_job([g_o])
    dh1 = _matmul("in_proj_dx", dproj, w_p, m=s, n=d, k=PW, nt=True, tn=_tile(d, 1024), tk=_tile(PW, 2560),
                  out_shape=(s, d), out_dtype=F32, jobs=[j_c_in, j_p_o])
    h_in = _chip_sum("chip_sum_w_in", ps_in, j_c_in.out[0], place)
    ps_o = _pair_sum("pair_sum_w_out", g_o, j_p_o.out[0], place)
    j_c_o = _chip_job([ps_o])
    j_s_in = _share_job([h_in])
    grad_x, _, d_n1 = _rms_bwd("norm1_bwd", dh1, xs, norm1_g, dx1, jobs=[j_c_o, j_s_in])
    h_o = _chip_sum("chip_sum_w_out", ps_o, j_c_o.out[0], place)
    j_s_o = _share_job([h_o])

    small = _small_all_reduce(_pack_small(d_n1, d_n2, d_fg, d_bf[:, :H_A], d_sk[:, :H_B], d_rb, loss_tile[0:1, 0:1]))
    loss = small[5, 0]

    grads = {
        "w_branch_a": r_a, "w_branch_b": r_b,
        "w_ffn_gate": r_gu[0:1, :, :FF_SH], "w_ffn_up": r_gu[1:2, :, :FF_SH], "w_ffn_down": r_dn[:, :FF_SH, :],
    }
    given = dict(w_in=(w_in, m_w_in, v_w_in), w_branch_a=(w_branch_a, m_w_branch_a, v_w_branch_a),
                 w_branch_b=(w_branch_b, m_w_branch_b, v_w_branch_b), w_out=(w_out, m_w_out, v_w_out),
                 w_ffn_gate=(w_ffn_gate, m_w_ffn_gate, v_w_ffn_gate), w_ffn_up=(w_ffn_up, m_w_ffn_up, v_w_ffn_up),
                 w_ffn_down=(w_ffn_down, m_w_ffn_down, v_w_ffn_down))

    def col_major(a):
        return jnp.transpose(a[0])[None]

    views = {n: (lambda a: a, lambda a: a) for n in given}
    views["w_ffn_gate"] = views["w_ffn_up"] = (col_major, col_major)
    views["w_in"] = (lambda a: jnp.transpose(a, (2, 0, 1)).reshape(W_IN_SH, d // LANE, LANE),
                     lambda a: jnp.transpose(a.reshape(W_IN_SH, 1, d), (1, 2, 0)))

    def adamw(n, jobs=()):
        to_view, from_view = views[n]
        g_view = to_view(grads[n])
        outs = _adamw("adamw_" + n, to_view(given[n][0]), g_view, to_view(given[n][1]), to_view(given[n][2]), jobs=jobs)
        grads[n] = from_view(g_view)
        return [from_view(o) for o in outs]

    grads["w_in"] = j_s_in.out[0]
    upd = {n: adamw(n) for n in ("w_ffn_gate", "w_ffn_up", "w_ffn_down", "w_branch_a", "w_branch_b", "w_in")}
    sm = _adamw("adamw_small",
                _pack_small(norm1_g, norm2_g, final_g, b_forget, attn_sinks, rel_bias)[None],
                small.at[5].set(0.0)[None],
                _pack_small(m_norm1_g, m_norm2_g, m_final_g, m_b_forget, m_attn_sinks, m_rel_bias)[None],
                _pack_small(v_norm1_g, v_norm2_g, v_final_g, v_b_forget, v_attn_sinks, v_rel_bias)[None],
                jobs=[j_s_o])
    grads["w_out"] = j_s_o.out[0]
    upd["w_out"] = adamw("w_out")
    g_small = _unpack_small(small)
    d_small, m_small, v_small = (_unpack_small(t[0]) for t in sm)

    order = ["norm1_g", "w_in", "b_forget", "attn_sinks", "rel_bias", "w_branch_a", "w_branch_b", "w_out",
             "norm2_g", "w_ffn_gate", "w_ffn_up", "w_ffn_down", "final_g"]
    small_at = {"norm1_g": 0, "norm2_g": 1, "final_g": 2, "b_forget": 3, "attn_sinks": 4, "rel_bias": 5}

    def pick(big_idx, small_src, n):
        return small_src[small_at[n]] if n in small_at else (grads[n] if big_idx is None else upd[n][big_idx])

    return (loss, grad_x[None],
            *[pick(None, g_small, n) for n in order], *[pick(0, d_small, n) for n in order],
            *[pick(1, m_small, n) for n in order], *[pick(2, v_small, n) for n in order])
```

```python
import functools
import math

import numpy as np
import jax
import jax.numpy as jnp
from jax import lax
from jax.experimental import pallas as pl
from jax.experimental.pallas import tpu as pltpu

F32 = jnp.float32
BF16 = jnp.bfloat16
MESH = pl.DeviceIdType.MESH

D_MODEL = 4096
SEQ = 4096
H_A = 16
DH_A = 128
H_B = 32
HKV_B = 4
G_B = H_B // HKV_B
DH_B = 64
WINDOW = 128
NUM_BUCKETS = 32
MAX_DISTANCE = 128
BLOCK = 128
D_FF = ((8 * D_MODEL // 3 + 255) // 256) * 256
EPS = 1e-6
ADAM_LR = 0.001
ADAM_B1 = 0.9
ADAM_B2 = 0.999
ADAM_EPS = 1e-08
ADAM_WD = 0.01
ADAM_STEP = 10

N_CHIP = 4
LANE = 128
VMEM_LIMIT = 56 * 1024 * 1024

TM = 1024
TN = 512
TK = 4096
TQ_A = 512
TR_EW = 256

W_A = H_A * DH_A
W_QB = H_B * DH_B
W_KB = HKV_B * DH_B
OFF_QA = 0
OFF_KA = W_A
OFF_VA = 2 * W_A
OFF_QB = 3 * W_A
OFF_KB = OFF_QB + W_QB
OFF_VB = OFF_KB + W_KB
OFF_GA = OFF_VB + W_KB
OFF_GB = OFF_GA + D_MODEL
OFF_FA = OFF_GB + D_MODEL
PW = OFF_FA + LANE
W_QKV = OFF_GA
W_IN = 3 * W_A + H_A + W_QB + 2 * W_KB + 2 * D_MODEL
W_IN_SH = W_IN // N_CHIP
A_SH = D_MODEL // N_CHIP
FF_SH = D_FF // N_CHIP
FF_P = -(-FF_SH // LANE) * LANE
FP = N_CHIP * FF_P
PACK_ROWS = 16
RB_ROWS = -(-(NUM_BUCKETS * H_B) // D_MODEL)


def _tile(n, target, mult=LANE):
    t = min(target, n) // mult * mult
    while t > mult and n % t:
        t -= mult
    assert t > 0 and n % t == 0, (n, target, mult)
    return t


def _cparams(sem):
    return pltpu.CompilerParams(dimension_semantics=sem, vmem_limit_bytes=VMEM_LIMIT)


def _sigmoid(x):
    return 1.0 / (1.0 + jnp.exp(-x))


def _dot(a, b):
    return jnp.dot(a, b, preferred_element_type=F32)


def _dot_nt(a, b):
    return lax.dot_general(a, b, (((1,), (1,)), ((), ())), preferred_element_type=F32)


ANY = pl.BlockSpec(memory_space=pl.ANY)


class _Job:
    def __init__(self, ins, outs, sems, start, wait, alias=None):
        self.ins, self.outs, self.sems = list(ins), list(outs), list(sems)
        self.start, self.wait, self.alias = start, wait, dict(alias or {})
        self.out = None


def _split(seq, sizes):
    parts, p = [], 0
    for n in sizes:
        parts.append(seq[p:p + n])
        p += n
    return parts


def _pcall(body, args, jobs, *, name, grid, in_specs, out_specs, out_shape, scratch_shapes=(), sem):
    if not jobs:
        return pl.pallas_call(body, name=name, grid=grid, in_specs=in_specs, out_specs=out_specs, out_shape=out_shape,
                              scratch_shapes=list(scratch_shapes), compiler_params=_cparams(sem))(*args)
    single = not isinstance(out_shape, (list, tuple))
    out_specs_l = [out_specs] if single else list(out_specs)
    out_shape_l = [out_shape] if single else list(out_shape)
    cin = [a for j in jobs for a in j.ins]
    cout = [o for j in jobs for o in j.outs]
    csem = [s for j in jobs for s in j.sems]
    sizes = [len(args), len(cin), len(out_shape_l), len(cout), len(scratch_shapes), len(csem)]
    aliases, io, oo = {}, len(args), len(out_shape_l)
    for j in jobs:
        for a, b in j.alias.items():
            aliases[io + a] = oo + b
        io, oo = io + len(j.ins), oo + len(j.outs)

    def wrapped(*refs):
        ins, cins, outs, couts, scr, sems = _split(refs, sizes)
        ids = [pl.program_id(a) for a in range(len(grid))]
        first = functools.reduce(jnp.logical_and, [i == 0 for i in ids])
        last = functools.reduce(jnp.logical_and, [i == g - 1 for i, g in zip(ids, grid)])
        per_job = list(zip(jobs, _split(cins, [len(j.ins) for j in jobs]), _split(couts, [len(j.outs) for j in jobs]),
                           _split(sems, [len(j.sems) for j in jobs])))

        @pl.when(first)
        def _():
            for j, ji, jo, js in per_job:
                j.start(ji, jo, js)

        body(*ins, *outs, *scr)

        @pl.when(last)
        def _():
            for j, ji, jo, js in per_job:
                j.wait(ji, jo, js)

    res = pl.pallas_call(
        wrapped, name=name, grid=grid, in_specs=list(in_specs) + [ANY] * len(cin),
        out_specs=out_specs_l + [ANY] * len(cout), out_shape=out_shape_l + cout,
        scratch_shapes=list(scratch_shapes) + csem, input_output_aliases=aliases,
        compiler_params=_cparams(("arbitrary",) * len(grid)))(*args, *cin)
    main, rest = res[:len(out_shape_l)], res[len(out_shape_l):]
    for j, o in zip(jobs, _split(rest, [len(j.outs) for j in jobs])):
        j.out = list(o)
    return main[0] if single else list(main)


def _comm_now(name, jobs):
    cin = [a for j in jobs for a in j.ins]
    cout = [o for j in jobs for o in j.outs]
    csem = [s for j in jobs for s in j.sems]
    sizes = [len(cin), len(cout), len(csem)]
    aliases, io, oo = {}, 0, 0
    for j in jobs:
        for a, b in j.alias.items():
            aliases[io + a] = oo + b
        io, oo = io + len(j.ins), oo + len(j.outs)

    def body(*refs):
        cins, couts, sems = _split(refs, sizes)
        per_job = list(zip(jobs, _split(cins, [len(j.ins) for j in jobs]), _split(couts, [len(j.outs) for j in jobs]),
                           _split(sems, [len(j.sems) for j in jobs])))
        for j, ji, jo, js in per_job:
            j.start(ji, jo, js)
        for j, ji, jo, js in per_job:
            j.wait(ji, jo, js)

    res = pl.pallas_call(body, name=name, in_specs=[ANY] * len(cin), out_specs=[ANY] * len(cout), out_shape=cout,
                         scratch_shapes=csem, input_output_aliases=aliases)(*cin)
    for j, o in zip(jobs, _split(res, [len(j.outs) for j in jobs])):
        j.out = list(o)


def _slot_spec(shape, kind, br, bc, rc):
    if kind == "2d":
        return pl.BlockSpec((br, bc), lambda *g: rc(*g))
    if kind == "col":
        assert shape[2] % bc == 0, (shape, bc)
        per = shape[2] // bc

        def im_col(*g):
            rb, cb = rc(*g)
            return (cb // per, rb, cb % per)
        return pl.BlockSpec((None, br, bc), im_col)
    assert kind == "row" and shape[1] % br == 0, (shape, kind, br)
    per = shape[1] // br

    def im_row(*g):
        rb, cb = rc(*g)
        return (rb // per, rb % per, cb)
    return pl.BlockSpec((None, br, bc), im_row)


def _matmul(name, a, b, *, m, n, k, nt=False, b_kind="2d", o_kind="2d", out_shape, out_dtype,
            tm=None, tn=None, tk=None, b_noff=0, res=None, b_outer=False, jobs=()):
    tm = tm or _tile(m, TM, 16)
    tn = tn or _tile(n, TN)
    tk = tk or _tile(k, TK)
    assert m % tm == 0 and n % tn == 0 and k % tk == 0 and b_noff % tn == 0
    nk = k // tk
    noff = b_noff // tn
    if b_outer:
        grid = (n // tn, m // tm, nk)
        ij = lambda g0, g1: (g1, g0)
    else:
        grid = (m // tm, n // tn, nk)
        ij = lambda g0, g1: (g0, g1)

    a_spec = pl.BlockSpec((tm, tk), lambda g0, g1, kk: (ij(g0, g1)[0], kk))
    if nt:
        b_spec = _slot_spec(b.shape, b_kind, tn, tk, lambda g0, g1, kk: (ij(g0, g1)[1] + noff, kk))
    else:
        b_spec = _slot_spec(b.shape, b_kind, tk, tn, lambda g0, g1, kk: (kk, ij(g0, g1)[1] + noff))
    o_spec = _slot_spec(out_shape, o_kind, tm, tn, lambda g0, g1, kk: ij(g0, g1))
    in_specs = [a_spec, b_spec]
    args = [a, b]
    if res is not None:
        in_specs.append(pl.BlockSpec((tm, tn), lambda g0, g1, kk: ij(g0, g1)))
        args.append(res)

    def body(*refs):
        a_ref, b_ref = refs[0], refs[1]
        r_ref = refs[2] if res is not None else None
        o_ref = refs[3] if res is not None else refs[2]

        def prod():
            return _dot_nt(a_ref[...], b_ref[...]) if nt else _dot(a_ref[...], b_ref[...])

        def finish(acc):
            if r_ref is not None:
                acc = acc + r_ref[...].astype(F32)
            o_ref[...] = acc.astype(o_ref.dtype)

        if nk == 1:
            finish(prod())
            return
        acc_ref = refs[-1]
        kk = pl.program_id(2)

        @pl.when(kk == 0)
        def _():
            acc_ref[...] = prod()

        @pl.when(jnp.logical_and(kk > 0, kk < nk - 1))
        def _():
            acc_ref[...] += prod()

        @pl.when(kk == nk - 1)
        def _():
            finish(acc_ref[...] + prod())

    return _pcall(
        body, args, jobs, name=name, grid=grid, in_specs=in_specs, out_specs=o_spec,
        out_shape=jax.ShapeDtypeStruct(out_shape, out_dtype),
        scratch_shapes=[pltpu.VMEM((tm, tn), F32)] if nk > 1 else [],
        sem=("parallel", "parallel", "arbitrary"))


def _rms_fwd(name, x, g):
    s, d = x.shape
    tr = _tile(s, TR_EW, 16)

    def body(x_ref, g_ref, h_ref):
        xf = x_ref[...]
        rstd = lax.rsqrt(jnp.mean(xf * xf, axis=-1, keepdims=True) + EPS)
        h_ref[...] = (xf * rstd * g_ref[...]).astype(h_ref.dtype)

    return pl.pallas_call(
        body, name=name, grid=(s // tr,),
        in_specs=[pl.BlockSpec((tr, d), lambda i: (i, 0)), pl.BlockSpec((1, d), lambda i: (0, 0))],
        out_specs=pl.BlockSpec((tr, d), lambda i: (i, 0)),
        out_shape=jax.ShapeDtypeStruct((s, d), BF16),
        compiler_params=_cparams(("parallel",)),
    )(x, g)


def _rms_bwd_rows(dh, xf, g):
    rstd = lax.rsqrt(jnp.mean(xf * xf, axis=-1, keepdims=True) + EPS)
    xhat = xf * rstd
    dxhat = dh * g
    dx = rstd * (dxhat - xhat * jnp.mean(dxhat * xhat, axis=-1, keepdims=True))
    return dx, dh * xhat


def _fold8(v):
    return jnp.sum(v.reshape(v.shape[0] // 8, 8, v.shape[1]), axis=0)


def _rms_bwd(name, dh, x, g, dres, jobs=()):
    s, d = x.shape
    tr = _tile(s, TR_EW, 16)
    n = s // tr

    def body(dh_ref, x_ref, g_ref, r_ref, dx_ref, dxb_ref, dg_ref, acc_ref):
        i = pl.program_id(0)
        dx, dgrows = _rms_bwd_rows(dh_ref[...].astype(F32), x_ref[...], g_ref[...])
        dx = dx + r_ref[...]
        dx_ref[...] = dx
        dxb_ref[...] = dx.astype(BF16)

        @pl.when(i == 0)
        def _():
            acc_ref[...] = jnp.zeros_like(acc_ref)
        acc_ref[...] += _fold8(dgrows)

        @pl.when(i == n - 1)
        def _():
            dg_ref[...] = jnp.sum(acc_ref[...], axis=0, keepdims=True)

    row = pl.BlockSpec((tr, d), lambda i: (i, 0))
    vec = pl.BlockSpec((1, d), lambda i: (0, 0))
    return _pcall(
        body, (dh, x, g, dres), jobs, name=name, grid=(n,), in_specs=[row, row, vec, row], out_specs=[row, row, vec],
        out_shape=[jax.ShapeDtypeStruct((s, d), F32), jax.ShapeDtypeStruct((s, d), BF16),
                   jax.ShapeDtypeStruct((1, d), F32)],
        scratch_shapes=[pltpu.VMEM((8, d), F32)],
        sem=("arbitrary",))


def _loss_head(x2, target, g):
    s, d = x2.shape
    tr = _tile(s, TR_EW, 16)
    n = s // tr

    def body(x_ref, t_ref, g_ref, dx_ref, dxb_ref, dg_ref, loss_ref, acc_ref):
        i = pl.program_id(0)
        xf = x_ref[...]
        gv = g_ref[...]
        rstd = lax.rsqrt(jnp.mean(xf * xf, axis=-1, keepdims=True) + EPS)
        err = xf * rstd * gv - t_ref[...]
        row_loss = jnp.mean(err * err, axis=-1, keepdims=True)
        dx, dgrows = _rms_bwd_rows(err / d, xf, gv)
        dx_ref[...] = dx
        dxb_ref[...] = dx.astype(BF16)

        @pl.when(i == 0)
        def _():
            acc_ref[...] = jnp.zeros_like(acc_ref)
            loss_ref[...] = jnp.zeros_like(loss_ref)
        acc_ref[...] += _fold8(dgrows)
        loss_ref[...] += jnp.broadcast_to(0.5 * jnp.sum(row_loss, axis=0, keepdims=True), (8, LANE))

        @pl.when(i == n - 1)
        def _():
            dg_ref[...] = jnp.sum(acc_ref[...], axis=0, keepdims=True)

    row = pl.BlockSpec((tr, d), lambda i: (i, 0))
    vec = pl.BlockSpec((1, d), lambda i: (0, 0))
    return pl.pallas_call(
        body, name="loss_head", grid=(n,), in_specs=[row, row, vec],
        out_specs=[row, row, vec, pl.BlockSpec((8, LANE), lambda i: (0, 0))],
        out_shape=[jax.ShapeDtypeStruct((s, d), F32), jax.ShapeDtypeStruct((s, d), BF16),
                   jax.ShapeDtypeStruct((1, d), F32), jax.ShapeDtypeStruct((8, LANE), F32)],
        scratch_shapes=[pltpu.VMEM((8, d), F32)],
        compiler_params=_cparams(("arbitrary",)),
    )(x2, target, g)


def _gate_fwd(ya, yb, proj_g):
    s, d = ya.shape
    tr = _tile(s, TR_EW, 16)
    tc = _tile(d, 1024)
    nc = d // tc

    def body(ya_ref, yb_ref, ga_ref, gb_ref, o_ref):
        o_ref[...] = (_sigmoid(ga_ref[...]) * ya_ref[...] + _sigmoid(gb_ref[...]) * yb_ref[...]).astype(o_ref.dtype)

    blk = pl.BlockSpec((tr, tc), lambda i, j: (i, j))
    return pl.pallas_call(
        body, name="gate_fwd", grid=(s // tr, nc),
        in_specs=[blk, blk, blk, pl.BlockSpec((tr, tc), lambda i, j: (i, j + nc))],
        out_specs=blk, out_shape=jax.ShapeDtypeStruct((s, d), BF16),
        compiler_params=_cparams(("parallel", "parallel")),
    )(ya, yb, proj_g, proj_g)


def _gate_bwd(dmixed, ya, yb, proj_g):
    s, d = ya.shape
    tr = _tile(s, TR_EW, 16)
    tc = _tile(d, 1024)
    nc = d // tc

    def body(dm_ref, ya_ref, yb_ref, ga_ref, gb_ref, dya_ref, dyb_ref, dga_ref, dgb_ref):
        dm = dm_ref[...].astype(F32)
        sa = _sigmoid(ga_ref[...])
        sb = _sigmoid(gb_ref[...])
        dya_ref[...] = (dm * sa).astype(BF16)
        dyb_ref[...] = (dm * sb).astype(BF16)
        dga_ref[...] = (dm * ya_ref[...] * sa * (1.0 - sa)).astype(BF16)
        dgb_ref[...] = (dm * yb_ref[...] * sb * (1.0 - sb)).astype(BF16)

    blk = pl.BlockSpec((tr, tc), lambda i, j: (i, j))
    out = jax.ShapeDtypeStruct((s, d), BF16)
    return pl.pallas_call(
        body, name="gate_bwd", grid=(s // tr, nc),
        in_specs=[blk, blk, blk, blk, pl.BlockSpec((tr, tc), lambda i, j: (i, j + nc))],
        out_specs=[blk, blk, blk, blk], out_shape=[out, out, out, out],
        compiler_params=_cparams(("parallel", "parallel")),
    )(dmixed, ya, yb, proj_g, proj_g)


def _swiglu_fwd(gu, jobs=()):
    s = gu.shape[0]
    tr = _tile(s, 128, 16)

    def body(gu_ref, h_ref):
        gate = gu_ref[:, :FF_P].astype(F32)
        up = gu_ref[:, FF_P:].astype(F32)
        h_ref[...] = (gate * _sigmoid(gate) * up).astype(h_ref.dtype)

    return _pcall(
        body, (gu,), jobs, name="swiglu_fwd", grid=(s // tr, N_CHIP),
        in_specs=[pl.BlockSpec((tr, 2 * FF_P), lambda i, j: (i, j))],
        out_specs=pl.BlockSpec((tr, FF_P), lambda i, j: (i, j)),
        out_shape=jax.ShapeDtypeStruct((s, FP), BF16),
        sem=("parallel", "parallel"))


def _swiglu_bwd(dhidden, gu):
    s = gu.shape[0]
    tr = _tile(s, 128, 16)

    def body(dh_ref, gu_ref, dgu_ref):
        gate = gu_ref[:, :FF_P].astype(F32)
        up = gu_ref[:, FF_P:].astype(F32)
        dh = dh_ref[...].astype(F32)
        sg = _sigmoid(gate)
        dgu_ref[:, :FF_P] = (dh * up * sg * (1.0 + gate * (1.0 - sg))).astype(BF16)
        dgu_ref[:, FF_P:] = (dh * gate * sg).astype(BF16)

    return pl.pallas_call(
        body, name="swiglu_bwd", grid=(s // tr, N_CHIP),
        in_specs=[pl.BlockSpec((tr, FF_P), lambda i, j: (i, j)), pl.BlockSpec((tr, 2 * FF_P), lambda i, j: (i, j))],
        out_specs=pl.BlockSpec((tr, 2 * FF_P), lambda i, j: (i, j)),
        out_shape=jax.ShapeDtypeStruct((s, 2 * FP), BF16),
        compiler_params=_cparams(("parallel", "parallel")),
    )(dhidden, gu)


def _tri(n, upper):
    r = lax.broadcasted_iota(jnp.int32, (n, n), 0)
    c = lax.broadcasted_iota(jnp.int32, (n, n), 1)
    return (c >= r if upper else c <= r).astype(F32)


def _forget_fwd(fa, bias):
    s = fa.shape[0]
    tb = _tile(s, 512, 8)

    def body(f_ref, b_ref, c_ref, carry_ref):
        i = pl.program_id(0)

        @pl.when(i == 0)
        def _():
            carry_ref[...] = jnp.zeros_like(carry_ref)
        z = f_ref[...] + b_ref[...]
        logf = jnp.minimum(z, 0.0) - jnp.log(1.0 + jnp.exp(-jnp.abs(z)))
        c = jnp.dot(_tri(tb, False), logf, precision=lax.Precision.HIGHEST, preferred_element_type=F32)
        c_ref[...] = c + carry_ref[0:1, :]
        carry_ref[...] = jnp.broadcast_to(c_ref[tb - 1:tb, :], carry_ref.shape)

    return pl.pallas_call(
        body, name="forget_fwd", grid=(s // tb,),
        in_specs=[pl.BlockSpec((tb, LANE), lambda i: (i, 0)), pl.BlockSpec((1, LANE), lambda i: (0, 0))],
        out_specs=pl.BlockSpec((tb, LANE), lambda i: (i, 0)),
        out_shape=jax.ShapeDtypeStruct((s, LANE), F32),
        scratch_shapes=[pltpu.VMEM((8, LANE), F32)],
        compiler_params=_cparams(("arbitrary",)),
    )(fa, bias)


def _forget_bwd(dc, fa, bias):
    s = fa.shape[0]
    tb = _tile(s, 512, 16)
    n = s // tb

    def body(dc_ref, f_ref, b_ref, df_ref, db_ref, carry_ref, acc_ref, tmp_ref):
        i = pl.program_id(0)

        @pl.when(i == 0)
        def _():
            carry_ref[...] = jnp.zeros_like(carry_ref)
            acc_ref[...] = jnp.zeros_like(acc_ref)
        dlogf = jnp.dot(_tri(tb, True), dc_ref[...], precision=lax.Precision.HIGHEST, preferred_element_type=F32)
        tmp_ref[...] = dlogf + carry_ref[0:1, :]
        carry_ref[...] = jnp.broadcast_to(tmp_ref[0:1, :], carry_ref.shape)
        df = tmp_ref[...] * _sigmoid(-(f_ref[...] + b_ref[...]))
        df_ref[...] = df.astype(BF16)
        acc_ref[...] += _fold8(df)

        @pl.when(i == n - 1)
        def _():
            db_ref[...] = jnp.sum(acc_ref[...], axis=0, keepdims=True)

    rev = pl.BlockSpec((tb, LANE), lambda i: (n - 1 - i, 0))
    vec = pl.BlockSpec((1, LANE), lambda i: (0, 0))
    return pl.pallas_call(
        body, name="forget_bwd", grid=(n,), in_specs=[rev, rev, vec], out_specs=[rev, vec],
        out_shape=[jax.ShapeDtypeStruct((s, LANE), BF16), jax.ShapeDtypeStruct((1, LANE), F32)],
        scratch_shapes=[pltpu.VMEM((8, LANE), F32), pltpu.VMEM((8, LANE), F32), pltpu.VMEM((tb, LANE), F32)],
        compiler_params=_cparams(("arbitrary",)),
    )(dc, fa, bias)


_A_SCALE = DH_A ** -0.5


def _causal(tq, tk, transposed):
    r = lax.broadcasted_iota(jnp.int32, (tq, tk), 0)
    c = lax.broadcasted_iota(jnp.int32, (tq, tk), 1)
    return c >= r if transposed else r >= c


def _fox_fwd(qkv, c_col, c_row, jobs=()):
    s = qkv.shape[0]
    t = _tile(s, TQ_A)
    n = s // t
    kb, vb = OFF_KA // DH_A, OFF_VA // DH_A

    def body(q_ref, k_ref, v_ref, cq_ref, ck_ref, o_ref, lse_ref, m_s, l_s, acc_s):
        i, j = pl.program_id(1), pl.program_id(2)

        @pl.when(j == 0)
        def _():
            m_s[...] = jnp.full_like(m_s, -jnp.inf)
            l_s[...] = jnp.zeros_like(l_s)
            acc_s[...] = jnp.zeros_like(acc_s)

        def step(diag):
            sc = _dot_nt(q_ref[...], k_ref[...]) * _A_SCALE + (cq_ref[:, 0:1] - ck_ref[...])
            if diag:
                sc = jnp.where(_causal(t, t, False), sc, -jnp.inf)
            m_prev = m_s[...]
            m_new = jnp.maximum(m_prev, jnp.max(sc, axis=1, keepdims=True))
            alpha = jnp.exp(m_prev - m_new)
            p = jnp.exp(sc - m_new[:, 0:1])
            l_s[...] = alpha * l_s[...] + jnp.sum(p, axis=1, keepdims=True)
            acc_s[...] = alpha * acc_s[...] + _dot(p.astype(BF16), v_ref[...])
            m_s[...] = m_new

        @pl.when(j < i)
        def _():
            step(False)

        @pl.when(j == i)
        def _():
            step(True)
            o_ref[...] = (acc_s[...] / l_s[...]).astype(o_ref.dtype)
            lse_ref[...] = m_s[...] + jnp.log(l_s[...])

    jc = lambda i, j: jnp.minimum(j, i)
    return _pcall(
        body, (qkv, qkv, qkv, c_col, c_row), jobs, name="fox_fwd", grid=(H_A, n, n),
        in_specs=[pl.BlockSpec((t, DH_A), lambda h, i, j: (i, h)),
                  pl.BlockSpec((t, DH_A), lambda h, i, j: (jc(i, j), kb + h)),
                  pl.BlockSpec((t, DH_A), lambda h, i, j: (jc(i, j), vb + h)),
                  pl.BlockSpec((None, t, LANE), lambda h, i, j: (h, i, 0)),
                  pl.BlockSpec((None, 1, t), lambda h, i, j: (h, 0, jc(i, j)))],
        out_specs=[pl.BlockSpec((t, DH_A), lambda h, i, j: (i, h)),
                   pl.BlockSpec((None, t, LANE), lambda h, i, j: (h, i, 0))],
        out_shape=[jax.ShapeDtypeStruct((s, W_A), BF16), jax.ShapeDtypeStruct((H_A, s, LANE), F32)],
        scratch_shapes=[pltpu.VMEM((t, LANE), F32)] * 3,
        sem=("parallel", "parallel", "arbitrary"))


def _fox_dq(qkv, do, c_col, c_row, lse_col, jobs=()):
    s = qkv.shape[0]
    t = _tile(s, TQ_A)
    n = s // t
    kb, vb = OFF_KA // DH_A, OFF_VA // DH_A

    def body(q_ref, k_ref, v_ref, do_ref, cq_ref, ck_ref, lse_ref, dq_ref, dl_ref, pdk_s, pk_s, dl_s):
        i, j = pl.program_id(1), pl.program_id(2)

        @pl.when(j == 0)
        def _():
            pdk_s[...] = jnp.zeros_like(pdk_s)
            pk_s[...] = jnp.zeros_like(pk_s)
            dl_s[...] = jnp.zeros_like(dl_s)

        def step(diag):
            sc = _dot_nt(q_ref[...], k_ref[...]) * _A_SCALE + (cq_ref[:, 0:1] - ck_ref[...])
            if diag:
                sc = jnp.where(_causal(t, t, False), sc, -jnp.inf)
            p = jnp.exp(sc - lse_ref[:, 0:1])
            pdp = p * _dot_nt(do_ref[...], v_ref[...])
            dl_s[...] += jnp.sum(pdp, axis=1, keepdims=True)
            pdk_s[...] += _dot(pdp.astype(BF16), k_ref[...])
            pk_s[...] += _dot(p.astype(BF16), k_ref[...])

        @pl.when(j < i)
        def _():
            step(False)

        @pl.when(j == i)
        def _():
            step(True)
            dq_ref[...] = ((pdk_s[...] - dl_s[...] * pk_s[...]) * _A_SCALE).astype(dq_ref.dtype)
            dl_ref[...] = dl_s[...]

    jc = lambda i, j: jnp.minimum(j, i)
    col = pl.BlockSpec((None, t, LANE), lambda h, i, j: (h, i, 0))
    return _pcall(
        body, (qkv, qkv, qkv, do, c_col, c_row, lse_col), jobs, name="fox_dq", grid=(H_A, n, n),
        in_specs=[pl.BlockSpec((t, DH_A), lambda h, i, j: (i, h)),
                  pl.BlockSpec((t, DH_A), lambda h, i, j: (jc(i, j), kb + h)),
                  pl.BlockSpec((t, DH_A), lambda h, i, j: (jc(i, j), vb + h)),
                  pl.BlockSpec((t, DH_A), lambda h, i, j: (i, h)),
                  col,
                  pl.BlockSpec((None, 1, t), lambda h, i, j: (h, 0, jc(i, j))),
                  col],
        out_specs=[pl.BlockSpec((t, DH_A), lambda h, i, j: (i, h)), col],
        out_shape=[jax.ShapeDtypeStruct((s, W_A), BF16), jax.ShapeDtypeStruct((H_A, s, LANE), F32)],
        scratch_shapes=[pltpu.VMEM((t, DH_A), F32), pltpu.VMEM((t, DH_A), F32), pltpu.VMEM((t, LANE), F32)],
        sem=("parallel", "parallel", "arbitrary"))


def _fox_dkv(qkv, do, c_col, c_row, lse_row, delta_row, jobs=()):
    s = qkv.shape[0]
    t = _tile(s, TQ_A)
    n = s // t
    kb, vb = OFF_KA // DH_A, OFF_VA // DH_A

    def body(q_ref, k_ref, v_ref, do_ref, cq_ref, ck_ref, lse_ref, dl_ref, dk_ref, dv_ref, dc_ref,
             dk_s, dv_s, dc_s):
        j, i = pl.program_id(1), pl.program_id(2)

        @pl.when(i == 0)
        def _():
            dk_s[...] = jnp.zeros_like(dk_s)
            dv_s[...] = jnp.zeros_like(dv_s)
            dc_s[...] = jnp.zeros_like(dc_s)

        def step(diag):
            st = _dot_nt(k_ref[...], q_ref[...]) * _A_SCALE + (cq_ref[...] - ck_ref[:, 0:1])
            if diag:
                st = jnp.where(_causal(t, t, True), st, -jnp.inf)
            pt = jnp.exp(st - lse_ref[...])
            dv_s[...] += _dot(pt.astype(BF16), do_ref[...])
            dpt = _dot_nt(v_ref[...], do_ref[...])
            dst = pt * (dpt - dl_ref[...])
            dk_s[...] += _dot(dst.astype(BF16), q_ref[...])
            dc_s[...] -= jnp.sum(dst, axis=1, keepdims=True)

        @pl.when(i == j)
        def _():
            step(True)

        @pl.when(i > j)
        def _():
            step(False)

        @pl.when(i == n - 1)
        def _():
            dk_ref[...] = (dk_s[...] * _A_SCALE).astype(dk_ref.dtype)
            dv_ref[...] = dv_s[...].astype(dv_ref.dtype)
            dc_ref[...] = dc_s[...]

    ic = lambda j, i: jnp.maximum(i, j)
    rowq = pl.BlockSpec((None, 1, t), lambda h, j, i: (h, 0, ic(j, i)))
    kv_out = pl.BlockSpec((t, DH_A), lambda h, j, i: (j, h))
    return _pcall(
        body, (qkv, qkv, qkv, do, c_row, c_col, lse_row, delta_row), jobs, name="fox_dkv", grid=(H_A, n, n),
        in_specs=[pl.BlockSpec((t, DH_A), lambda h, j, i: (ic(j, i), h)),
                  pl.BlockSpec((t, DH_A), lambda h, j, i: (j, kb + h)),
                  pl.BlockSpec((t, DH_A), lambda h, j, i: (j, vb + h)),
                  pl.BlockSpec((t, DH_A), lambda h, j, i: (ic(j, i), h)),
                  rowq,
                  pl.BlockSpec((None, t, LANE), lambda h, j, i: (h, j, 0)),
                  rowq, rowq],
        out_specs=[kv_out, kv_out, pl.BlockSpec((None, t, LANE), lambda h, j, i: (h, j, 0))],
        out_shape=[jax.ShapeDtypeStruct((s, W_A), BF16), jax.ShapeDtypeStruct((s, W_A), BF16),
                   jax.ShapeDtypeStruct((H_A, s, LANE), F32)],
        scratch_shapes=[pltpu.VMEM((t, DH_A), F32), pltpu.VMEM((t, DH_A), F32), pltpu.VMEM((t, LANE), F32)],
        sem=("parallel", "parallel", "arbitrary"))


_B_SCALE = DH_B ** -0.5
_HALF = LANE // 2


def _t5_bucket_table():
    ql = np.arange(BLOCK)[:, None]
    kl = np.arange(2 * BLOCK)[None, :]
    dist = np.clip(ql + BLOCK - kl, 0, None)
    max_exact = NUM_BUCKETS // 2
    large = max_exact + (np.log(np.maximum(dist, 1) / max_exact) / np.log(MAX_DISTANCE / max_exact)
                         * (NUM_BUCKETS - max_exact)).astype(np.int64)
    large = np.minimum(large, NUM_BUCKETS - 1)
    return np.where(dist < max_exact, dist, large).astype(np.int32).reshape(1, BLOCK * 2 * BLOCK)


def _one_hot_buckets(bucket_ref, n):
    return (lax.broadcasted_iota(jnp.int32, (NUM_BUCKETS, n), 0) == bucket_ref[...]).astype(F32)


def _bias_table(rel_bias_t, bucket):
    nqk = bucket.shape[1]
    tc = _tile(nqk, 4096)

    def body(rb_ref, bk_ref, o_ref):
        o_ref[...] = jnp.dot(rb_ref[...], _one_hot_buckets(bk_ref, tc), precision=lax.Precision.HIGHEST,
                             preferred_element_type=F32)

    return pl.pallas_call(
        body, name="bias_table", grid=(nqk // tc,),
        in_specs=[pl.BlockSpec((H_B, NUM_BUCKETS), lambda i: (0, 0)), pl.BlockSpec((1, tc), lambda i: (0, i))],
        out_specs=pl.BlockSpec((H_B, tc), lambda i: (0, i)),
        out_shape=jax.ShapeDtypeStruct((H_B, nqk), F32),
        compiler_params=_cparams(("parallel",)),
    )(rel_bias_t, bucket)


def _bias_table_bwd(dbias, bucket):
    nqk = bucket.shape[1]
    tc = _tile(nqk, 4096)
    n = nqk // tc

    def body(db_ref, bk_ref, o_ref):
        i = pl.program_id(0)

        @pl.when(i == 0)
        def _():
            o_ref[...] = jnp.zeros_like(o_ref)
        o_ref[...] += lax.dot_general(db_ref[...], _one_hot_buckets(bk_ref, tc), (((1,), (1,)), ((), ())),
                                      precision=lax.Precision.HIGHEST, preferred_element_type=F32)

    return pl.pallas_call(
        body, name="bias_table_bwd", grid=(n,),
        in_specs=[pl.BlockSpec((H_B, tc), lambda i: (0, i)), pl.BlockSpec((1, tc), lambda i: (0, i))],
        out_specs=pl.BlockSpec((H_B, NUM_BUCKETS), lambda i: (0, 0)),
        out_shape=jax.ShapeDtypeStruct((H_B, NUM_BUCKETS), F32),
        compiler_params=_cparams(("arbitrary",)),
    )(dbias, bucket)


def _lane_lo():
    return lax.broadcasted_iota(jnp.int32, (1, LANE), 1) < _HALF


def _dup_kv_head(cat, hk):
    xcol = cat[:, (hk // 2) * LANE:(hk // 2 + 1) * LANE].astype(F32)
    swapped = pltpu.roll(xcol, _HALF, 1)
    lo = _lane_lo()
    return (jnp.where(lo, xcol, swapped) if hk % 2 == 0 else jnp.where(lo, swapped, xcol)).astype(BF16)


def _band_mask(first_block):
    ql = lax.broadcasted_iota(jnp.int32, (BLOCK, 2 * BLOCK), 0)
    kl = lax.broadcasted_iota(jnp.int32, (BLOCK, 2 * BLOCK), 1)
    dist = ql + BLOCK - kl
    ok = jnp.logical_and(dist >= 0, dist < WINDOW)
    return jnp.logical_and(ok, jnp.logical_or(jnp.logical_not(first_block), kl >= BLOCK))


def _swa_probs(qh, kdup, bias_h, sink, mask):
    sc = _dot_nt(qh, kdup) * _B_SCALE + bias_h
    sc = jnp.where(mask, sc, -jnp.inf)
    m = jnp.maximum(jnp.max(sc, axis=1, keepdims=True), sink)
    p = jnp.exp(sc - m)
    e_sink = jnp.exp(sink - m)
    inv = 1.0 / (jnp.sum(p, axis=1, keepdims=True) + e_sink)
    return p * inv, e_sink * inv


def _split_pair(ref, col):
    x = ref[:, col * LANE:(col + 1) * LANE].astype(F32)
    lo = _lane_lo()
    return jnp.where(lo, x, 0.0).astype(BF16), jnp.where(lo, 0.0, x).astype(BF16)


def _swa_fwd(qkv, bias, sinks, jobs=()):
    s = qkv.shape[0]
    nb = s // BLOCK
    qb, kb, vb = OFF_QB // W_QB, OFF_KB // W_KB, OFF_VB // W_KB
    assert OFF_QB % W_QB == 0 and OFF_KB % W_KB == 0 and OFF_VB % W_KB == 0 and W_KB % LANE == 0 and G_B % 2 == 0

    def body(q_ref, kp_ref, kc_ref, vp_ref, vc_ref, bias_ref, sink_ref, o_ref):
        i = pl.program_id(0)
        mask = _band_mask(i == 0)
        kcat = jnp.concatenate([kp_ref[...], kc_ref[...]], axis=0)
        vcat = jnp.concatenate([vp_ref[...], vc_ref[...]], axis=0)
        lo = _lane_lo()
        for hk in range(HKV_B):
            kdup, vdup = _dup_kv_head(kcat, hk), _dup_kv_head(vcat, hk)
            for pr in range(G_B // 2):
                h0 = hk * G_B + 2 * pr
                q0, q1 = _split_pair(q_ref, h0 // 2)
                p0, _ = _swa_probs(q0, kdup, bias_ref[h0], sink_ref[0, h0], mask)
                p1, _ = _swa_probs(q1, kdup, bias_ref[h0 + 1], sink_ref[0, h0 + 1], mask)
                o0 = _dot(p0.astype(BF16), vdup)
                o1 = _dot(p1.astype(BF16), vdup)
                o_ref[:, (h0 // 2) * LANE:(h0 // 2 + 1) * LANE] = jnp.where(lo, o0, o1).astype(o_ref.dtype)

    prev = lambda i: jnp.maximum(i - 1, 0)
    return _pcall(
        body, (qkv, qkv, qkv, qkv, qkv, bias, sinks), jobs, name="swa_fwd", grid=(nb,),
        in_specs=[pl.BlockSpec((BLOCK, W_QB), lambda i: (i, qb)),
                  pl.BlockSpec((BLOCK, W_KB), lambda i: (prev(i), kb)),
                  pl.BlockSpec((BLOCK, W_KB), lambda i: (i, kb)),
                  pl.BlockSpec((BLOCK, W_KB), lambda i: (prev(i), vb)),
                  pl.BlockSpec((BLOCK, W_KB), lambda i: (i, vb)),
                  pl.BlockSpec((H_B, BLOCK, 2 * BLOCK), lambda i: (0, 0, 0)),
                  pl.BlockSpec(memory_space=pltpu.SMEM)],
        out_specs=pl.BlockSpec((BLOCK, W_QB), lambda i: (i, 0)),
        out_shape=jax.ShapeDtypeStruct((s, W_QB), BF16),
        sem=("parallel",))


def _swa_bwd(qkv, do, bias, sinks, jobs=()):
    s = qkv.shape[0]
    nb = s // BLOCK
    qb, kb, vb = OFF_QB // W_QB, OFF_KB // W_KB, OFF_VB // W_KB

    def body(q_ref, kp_ref, kc_ref, vp_ref, vc_ref, do_ref, bias_ref, sink_ref,
             dq_ref, dk_ref, dv_ref, dbias_ref, dsink_ref, carry_k, carry_v):
        i = pl.program_id(0)
        lo = _lane_lo()

        @pl.when(i == 0)
        def _():
            dbias_ref[...] = jnp.zeros_like(dbias_ref)
            dsink_ref[...] = jnp.zeros_like(dsink_ref)
            carry_k[...] = jnp.zeros_like(carry_k)
            carry_v[...] = jnp.zeros_like(carry_v)

        @pl.when(i < nb)
        def _():
            mask = _band_mask(i == 0)
            kcat = jnp.concatenate([kp_ref[...], kc_ref[...]], axis=0)
            vcat = jnp.concatenate([vp_ref[...], vc_ref[...]], axis=0)
            lane = lax.broadcasted_iota(jnp.int32, (1, LANE), 1)
            dsink = jnp.zeros((1, LANE), F32)
            dk_cols = [jnp.zeros((2 * BLOCK, LANE), F32) for _ in range(W_KB // LANE)]
            dv_cols = [jnp.zeros((2 * BLOCK, LANE), F32) for _ in range(W_KB // LANE)]
            for hk in range(HKV_B):
                kdup, vdup = _dup_kv_head(kcat, hk), _dup_kv_head(vcat, hk)
                dk_acc = jnp.zeros((2 * BLOCK, LANE), F32)
                dv_acc = jnp.zeros((2 * BLOCK, LANE), F32)
                for pr in range(G_B // 2):
                    h0 = hk * G_B + 2 * pr
                    col = h0 // 2
                    qs = _split_pair(q_ref, col)
                    dos = _split_pair(do_ref, col)
                    dqs = []
                    for e in range(2):
                        h = h0 + e
                        p, p_sink = _swa_probs(qs[e], kdup, bias_ref[h], sink_ref[0, h], mask)
                        dp = _dot_nt(dos[e], vdup)
                        delta = jnp.sum(p * dp, axis=1, keepdims=True)
                        ds = p * (dp - delta)
                        dbias_ref[h] += ds
                        dsink = dsink - jnp.where(lane == h, jnp.sum(p_sink * delta, axis=0, keepdims=True), 0.0)
                        dqs.append(_dot(ds.astype(BF16), kdup))
                        dk_acc = dk_acc + _dot(ds.T.astype(BF16), qs[e])
                        dv_acc = dv_acc + _dot(p.T.astype(BF16), dos[e])
                    dq_ref[:, col * LANE:(col + 1) * LANE] = (jnp.where(lo, dqs[0], dqs[1]) * _B_SCALE).astype(dq_ref.dtype)
                dk_tot = (dk_acc + pltpu.roll(dk_acc, _HALF, 1)) * _B_SCALE
                dv_tot = dv_acc + pltpu.roll(dv_acc, _HALF, 1)
                mine = lo if hk % 2 == 0 else jnp.logical_not(lo)
                dk_cols[hk // 2] = jnp.where(mine, dk_tot, dk_cols[hk // 2])
                dv_cols[hk // 2] = jnp.where(mine, dv_tot, dv_cols[hk // 2])
            dsink_ref[...] += dsink
            dk_cat = jnp.concatenate(dk_cols, axis=1)
            dv_cat = jnp.concatenate(dv_cols, axis=1)
            dk_ref[...] = (carry_k[...] + dk_cat[:BLOCK]).astype(dk_ref.dtype)
            dv_ref[...] = (carry_v[...] + dv_cat[:BLOCK]).astype(dv_ref.dtype)
            carry_k[...] = dk_cat[BLOCK:]
            carry_v[...] = dv_cat[BLOCK:]

        @pl.when(i == nb)
        def _():
            dk_ref[...] = carry_k[...].astype(dk_ref.dtype)
            dv_ref[...] = carry_v[...].astype(dv_ref.dtype)

    cur = lambda i: jnp.minimum(i, nb - 1)
    prev = lambda i: jnp.clip(i - 1, 0, nb - 1)
    kv_out = pl.BlockSpec((BLOCK, W_KB), lambda i: (prev(i), 0))
    return _pcall(
        body, (qkv, qkv, qkv, qkv, qkv, do, bias, sinks), jobs, name="swa_bwd", grid=(nb + 1,),
        in_specs=[pl.BlockSpec((BLOCK, W_QB), lambda i: (cur(i), qb)),
                  pl.BlockSpec((BLOCK, W_KB), lambda i: (prev(i), kb)),
                  pl.BlockSpec((BLOCK, W_KB), lambda i: (cur(i), kb)),
                  pl.BlockSpec((BLOCK, W_KB), lambda i: (prev(i), vb)),
                  pl.BlockSpec((BLOCK, W_KB), lambda i: (cur(i), vb)),
                  pl.BlockSpec((BLOCK, W_QB), lambda i: (cur(i), 0)),
                  pl.BlockSpec((H_B, BLOCK, 2 * BLOCK), lambda i: (0, 0, 0)),
                  pl.BlockSpec(memory_space=pltpu.SMEM)],
        out_specs=[pl.BlockSpec((BLOCK, W_QB), lambda i: (cur(i), 0)), kv_out, kv_out,
                   pl.BlockSpec((H_B, BLOCK, 2 * BLOCK), lambda i: (0, 0, 0)),
                   pl.BlockSpec((1, LANE), lambda i: (0, 0))],
        out_shape=[jax.ShapeDtypeStruct((s, W_QB), BF16), jax.ShapeDtypeStruct((s, W_KB), BF16),
                   jax.ShapeDtypeStruct((s, W_KB), BF16),
                   jax.ShapeDtypeStruct((H_B, BLOCK, 2 * BLOCK), F32), jax.ShapeDtypeStruct((1, LANE), F32)],
        scratch_shapes=[pltpu.VMEM((BLOCK, W_KB), F32), pltpu.VMEM((BLOCK, W_KB), F32)],
        sem=("arbitrary",))


_RELS = ((1, 0), (0, 1), (1, 1))


def _place():
    x, y, c = lax.axis_index("x"), lax.axis_index("y"), lax.axis_index("c")
    return x, y, c


def _rel_chip(x, y, rel):
    px = 1 - x if rel[0] else x
    py = 1 - y if rel[1] else y
    return px, py, 2 * px + py


def _half_rows(ref_shape, c):
    half = ref_shape[-2] // 2
    return pl.ds(pl.multiple_of(c * half, 16), half)


def _row_chunks(start, size):
    n = next(n for n in (8, 4, 2, 1) if size % (16 * n) == 0)
    step = size // n
    if isinstance(start, int):
        return [pl.ds(start + i * step, step) for i in range(n)]
    return [pl.ds(pl.multiple_of(start + i * step, 16), step) for i in range(n)]


def _dma_sems(*shape):
    return pltpu.SemaphoreType.DMA(shape)


def _gather_ici_job(srcs):
    nt = len(srcs)

    def copies(src, dst, sems):
        send, recv = sems
        x, y, c = _place()
        me_chip = 2 * x + y
        chunks, sends, recvs = [], [], []
        for t in range(nt):
            ns = src[t].shape[0]
            half = src[t].shape[-2] // 2
            for k, rel in enumerate(_RELS[:2]):
                px, py, chip = _rel_chip(x, y, rel)

                def mk(into, rows):
                    return pltpu.make_async_remote_copy(
                        src_ref=src[t].at[:, rows], dst_ref=dst[t].at[pl.ds(into * ns, ns), rows],
                        send_sem=send.at[t, k], recv_sem=recv.at[t, k], device_id=(px, py, c), device_id_type=MESH)
                chunks += [mk(me_chip, r) for r in _row_chunks(c * half, half)]
                sends.append(mk(me_chip, _half_rows(src[t].shape, c)))
                recvs.append(mk(chip, _half_rows(src[t].shape, c)))
        return chunks, sends, recvs

    def start(src, dst, sems):
        for cp in copies(src, dst, sems)[0]:
            cp.start()

    def wait(src, dst, sems):
        _, sends, recvs = copies(src, dst, sems)
        for cp in recvs:
            cp.wait_recv()
        for cp in sends:
            cp.wait_send()

    return _Job(srcs, [jax.ShapeDtypeStruct((N_CHIP * a.shape[0],) + a.shape[1:], a.dtype) for a in srcs],
                [_dma_sems(nt, 2), _dma_sems(nt, 2)], start, wait)


def _with_own(gathered, srcs):
    chip = 2 * lax.axis_index("x") + lax.axis_index("y")
    return [lax.dynamic_update_slice(g, s, (chip * s.shape[0], 0, 0)) for g, s in zip(gathered, srcs)]


def _gather_relay_job(gathered):
    nt = len(gathered)

    def copies(dst, sems):
        send, recv = sems
        x, y, c = _place()
        diag = 2 * (1 - x) + (1 - y)
        chunks, sends, recvs = [], [], []
        for t in range(nt):
            ns = dst[t].shape[0] // N_CHIP
            quarter = dst[t].shape[-2] // 4
            for k, rel in enumerate(_RELS[:2]):
                px, py, _ = _rel_chip(x, y, rel)
                _, _, origin = _rel_chip(x, y, _RELS[1 - k])

                def mk(chip, rows):
                    part = dst[t].at[pl.ds(chip * ns, ns), rows]
                    return pltpu.make_async_remote_copy(
                        src_ref=part, dst_ref=part, send_sem=send.at[t, k], recv_sem=recv.at[t, k],
                        device_id=(px, py, c), device_id_type=MESH)
                whole = pl.ds(pl.multiple_of((2 * c + k) * quarter, 16), quarter)
                chunks += [mk(origin, r) for r in _row_chunks((2 * c + k) * quarter, quarter)]
                sends.append(mk(origin, whole))
                recvs.append(mk(diag, whole))
        return chunks, sends, recvs

    def start(_, dst, sems):
        for cp in copies(dst, sems)[0]:
            cp.start()

    def wait(_, dst, sems):
        _, sends, recvs = copies(dst, sems)
        for cp in recvs:
            cp.wait_recv()
        for cp in sends:
            cp.wait_send()

    return _Job(gathered, [jax.ShapeDtypeStruct(a.shape, a.dtype) for a in gathered],
                [_dma_sems(nt, 2), _dma_sems(nt, 2)], start, wait, alias={t: t for t in range(nt)})


def _gather_d2d_job(gathered):
    nt = len(gathered)

    def copies(dst, sems):
        send, recv = sems
        x, y, c = _place()
        sends, recvs = [], []
        for t in range(nt):
            ns = dst[t].shape[0] // N_CHIP
            for k, rel in enumerate(_RELS):
                _, _, chip = _rel_chip(x, y, rel)
                for half, lst in ((c, sends), (1 - c, recvs)):
                    part = dst[t].at[pl.ds(chip * ns, ns), _half_rows(dst[t].shape, half)]
                    lst.append(pltpu.make_async_remote_copy(
                        src_ref=part, dst_ref=part, send_sem=send.at[t, k], recv_sem=recv.at[t, k],
                        device_id=(x, y, 1 - c), device_id_type=MESH))
        return sends, recvs

    def start(_, dst, sems):
        for cp in copies(dst, sems)[0]:
            cp.start()

    def wait(_, dst, sems):
        sends, recvs = copies(dst, sems)
        for cp in recvs:
            cp.wait_recv()
        for cp in sends:
            cp.wait_send()

    return _Job(gathered, [jax.ShapeDtypeStruct(a.shape, a.dtype) for a in gathered],
                [_dma_sems(nt, 3), _dma_sems(nt, 3)], start, wait, alias={t: t for t in range(nt)})


def _pair_job(grads):
    nt = len(grads)

    def copies(g, got, sems):
        send, recv = sems
        x, y, c = _place()
        return [pltpu.make_async_remote_copy(
            src_ref=g[t].at[:, _half_rows(g[t].shape, 1 - c)], dst_ref=got[t], send_sem=send.at[t],
            recv_sem=recv.at[t], device_id=(x, y, 1 - c), device_id_type=MESH) for t in range(nt)]

    def start(g, got, sems):
        for cp in copies(g, got, sems):
            cp.start()

    def wait(g, got, sems):
        for cp in copies(g, got, sems):
            cp.wait()

    return _Job(grads, [jax.ShapeDtypeStruct((a.shape[0], a.shape[1] // 2, a.shape[2]), a.dtype) for a in grads],
                [_dma_sems(nt), _dma_sems(nt)], start, wait)


def _ew_rows(rows, cols, itemsize=4):
    tr = 16
    while tr * 2 <= 1024 and rows % (tr * 2) == 0 and tr * 2 * cols * itemsize <= (1 << 22):
        tr *= 2
    assert rows % tr == 0, (rows, tr)
    return tr


def _pair_sum(name, grad, got, place):
    ns, rows, cols = grad.shape
    half = rows // 2
    tr = _ew_rows(half, cols)
    nh = half // tr

    def body(p_ref, g_ref, r_ref, o_ref):
        o_ref[...] = (g_ref[...].astype(F32) + r_ref[...].astype(F32)).astype(o_ref.dtype)

    return pl.pallas_call(
        body, name=name,
        grid_spec=pltpu.PrefetchScalarGridSpec(
            num_scalar_prefetch=1, grid=(ns, nh),
            in_specs=[pl.BlockSpec((None, tr, cols), lambda s, i, p: (s, p[1] * nh + i, 0)),
                      pl.BlockSpec((None, tr, cols), lambda s, i, p: (s, i, 0))],
            out_specs=pl.BlockSpec((None, tr, cols), lambda s, i, p: (s, i, 0))),
        out_shape=jax.ShapeDtypeStruct((ns, half, cols), BF16),
        compiler_params=_cparams(("parallel", "parallel")),
    )(place, grad, got)


def _chip_job(psums):
    nt = len(psums)

    def copies(p, got, sems):
        send, recv = sems
        x, y, c = _place()
        chunks, whole = [], []
        for t in range(nt):
            ns = p[t].shape[0] // N_CHIP
            for k, rel in enumerate(_RELS):
                px, py, chip = _rel_chip(x, y, rel)

                def mk(rows):
                    return pltpu.make_async_remote_copy(
                        src_ref=p[t].at[pl.ds(chip * ns, ns), rows], dst_ref=got[t].at[k, :, rows],
                        send_sem=send.at[t, k], recv_sem=recv.at[t, k], device_id=(px, py, c), device_id_type=MESH)
                chunks += [mk(r) for r in _row_chunks(0, p[t].shape[1])]
                whole.append(mk(pl.ds(0, p[t].shape[1])))
        return chunks, whole

    def start(p, got, sems):
        for cp in copies(p, got, sems)[0]:
            cp.start()

    def wait(p, got, sems):
        for cp in copies(p, got, sems)[1]:
            cp.wait()

    return _Job(psums, [jax.ShapeDtypeStruct((3, a.shape[0] // N_CHIP) + a.shape[1:], a.dtype) for a in psums],
                [_dma_sems(nt, 3), _dma_sems(nt, 3)], start, wait)


def _chip_sum(name, psum, got, place):
    ns4, half, cols = psum.shape
    ns = ns4 // N_CHIP
    tr = _ew_rows(half, cols)
    nh = half // tr

    def body(p_ref, mine_ref, got_ref, o_ref):
        acc = mine_ref[...].astype(F32)
        for k in range(3):
            acc = acc + got_ref[k].astype(F32)
        o_ref[...] = acc

    return pl.pallas_call(
        body, name=name,
        grid_spec=pltpu.PrefetchScalarGridSpec(
            num_scalar_prefetch=1, grid=(ns, nh),
            in_specs=[pl.BlockSpec((None, tr, cols), lambda s, i, p: (p[0] * ns + s, i, 0)),
                      pl.BlockSpec((3, None, tr, cols), lambda s, i, p: (0, s, i, 0))],
            out_specs=pl.BlockSpec((None, tr, cols), lambda s, i, p: (s, p[1] * nh + i, 0))),
        out_shape=jax.ShapeDtypeStruct((ns, 2 * half, cols), F32),
        compiler_params=_cparams(("parallel", "parallel")),
    )(place, psum, got)


def _share_job(halves):
    nt = len(halves)

    def copies(full, sems):
        send, recv = sems
        x, y, c = _place()
        sends, recvs = [], []
        for t in range(nt):
            for half, lst in ((c, sends), (1 - c, recvs)):
                part = full[t].at[:, _half_rows(full[t].shape, half)]
                lst.append(pltpu.make_async_remote_copy(
                    src_ref=part, dst_ref=part, send_sem=send.at[t], recv_sem=recv.at[t],
                    device_id=(x, y, 1 - c), device_id_type=MESH))
        return sends, recvs

    def start(_, full, sems):
        for cp in copies(full, sems)[0]:
            cp.start()

    def wait(_, full, sems):
        sends, recvs = copies(full, sems)
        for cp in sends:
            cp.wait_send()
        for cp in recvs:
            cp.wait_recv()

    return _Job(halves, [jax.ShapeDtypeStruct(a.shape, a.dtype) for a in halves],
                [_dma_sems(nt), _dma_sems(nt)], start, wait, alias={t: t for t in range(nt)})


def _small_all_reduce(pack):
    rows, d = pack.shape

    def body(x_ref, o_ref, land, send, recv):
        x, y, c = _place()
        me = 4 * x + 2 * y + c
        land[me] = x_ref[...]
        cps = []
        for k in range(1, 8):
            to = (1 - x if k & 4 else x, 1 - y if k & 2 else y, 1 - c if k & 1 else c)
            cps.append(pltpu.make_async_remote_copy(
                src_ref=x_ref, dst_ref=land.at[me], send_sem=send.at[k - 1], recv_sem=recv.at[k - 1],
                device_id=to, device_id_type=MESH))
        for cp in cps:
            cp.start()
        for cp in cps:
            cp.wait()
        acc = land[0]
        for dev in range(1, 8):
            acc = acc + land[dev]
        o_ref[...] = acc

    vm = pl.BlockSpec(memory_space=pltpu.VMEM)
    return pl.pallas_call(
        body, name="small_all_reduce", in_specs=[vm], out_specs=vm,
        out_shape=jax.ShapeDtypeStruct((rows, d), F32),
        scratch_shapes=[pltpu.VMEM((8, rows, d), F32), pltpu.SemaphoreType.DMA((7,)), pltpu.SemaphoreType.DMA((7,))],
    )(pack)


def _adamw(name, w, g, m, v, jobs=()):
    lead, rows, cols = w.shape
    fits = [t for t in range(8, rows + 1, 8) if rows % t == 0 and t * cols * 4 <= (1 << 20)]
    tr = fits[-1] if fits else rows
    tl = 1
    if tr == rows:
        tl = max(t for t in range(1, lead + 1) if lead % t == 0 and t * rows * cols * 4 <= (1 << 20))

    def body(w_ref, g_ref, m_ref, v_ref, d_ref, nm_ref, nv_ref):
        gv = g_ref[...]
        nm = ADAM_B1 * m_ref[...] + (1.0 - ADAM_B1) * gv
        nv = ADAM_B2 * v_ref[...] + (1.0 - ADAM_B2) * (gv * gv)
        m_hat = nm / (1.0 - ADAM_B1 ** ADAM_STEP)
        v_hat = nv / (1.0 - ADAM_B2 ** ADAM_STEP)
        d_ref[...] = -ADAM_LR * (m_hat / (jnp.sqrt(v_hat) + ADAM_EPS) + ADAM_WD * w_ref[...])
        nm_ref[...] = nm
        nv_ref[...] = nv

    blk = pl.BlockSpec((tl, tr, cols), lambda l, i: (l, i, 0))
    out = jax.ShapeDtypeStruct(w.shape, F32)
    return _pcall(
        body, (w, g, m, v), jobs, name=name, grid=(lead // tl, rows // tr), in_specs=[blk] * 4, out_specs=[blk] * 3,
        out_shape=[out, out, out], sem=("parallel", "parallel"))


def _pad_to(a, shape):
    return jnp.pad(a, [(0, t - s) for s, t in zip(a.shape, shape)])


def _pack_small(n1, n2, fg, bf, sk, rb, extra=None):
    rows = [n1.reshape(1, D_MODEL), n2.reshape(1, D_MODEL), fg.reshape(1, D_MODEL),
            _pad_to(bf.reshape(1, H_A), (1, D_MODEL)), _pad_to(sk.reshape(1, H_B), (1, D_MODEL)),
            jnp.zeros((1, D_MODEL), F32) if extra is None else _pad_to(extra.reshape(1, 1), (1, D_MODEL)),
            _pad_to(rb.reshape(1, NUM_BUCKETS * H_B), (1, RB_ROWS * D_MODEL)).reshape(RB_ROWS, D_MODEL)]
    return _pad_to(jnp.concatenate(rows, axis=0), (PACK_ROWS, D_MODEL))


def _unpack_small(p):
    return (p[0:1], p[1:2], p[2], p[3:4, :H_A], p[4:5, :H_B],
            p[6:6 + RB_ROWS].reshape(-1)[:NUM_BUCKETS * H_B].reshape(NUM_BUCKETS, H_B))


def kernel(x, norm1_g, w_in, b_forget, attn_sinks, rel_bias, w_branch_a, w_branch_b, w_out, norm2_g, w_ffn_gate, w_ffn_up, w_ffn_down, final_g, loss_target, m_norm1_g, m_w_in, m_b_forget, m_attn_sinks, m_rel_bias, m_w_branch_a, m_w_branch_b, m_w_out, m_norm2_g, m_w_ffn_gate, m_w_ffn_up, m_w_ffn_down, m_final_g, v_norm1_g, v_w_in, v_b_forget, v_attn_sinks, v_rel_bias, v_w_branch_a, v_w_branch_b, v_w_out, v_norm2_g, v_w_ffn_gate, v_w_ffn_up, v_w_ffn_down, v_final_g):
    s, d = SEQ, D_MODEL
    assert x.shape == (1, s, d) and w_in.shape == (1, d, W_IN_SH)
    xs = x[0]
    place = jnp.stack([2 * lax.axis_index("x") + lax.axis_index("y"), lax.axis_index("c")]).astype(jnp.int32)

    w_gu_l = jnp.stack([_pad_to(w_ffn_gate[0], (d, FF_P)), _pad_to(w_ffn_up[0], (d, FF_P))]).astype(BF16)
    w_dn_l = _pad_to(w_ffn_down, (1, FF_P, d)).astype(BF16)
    w_in_l, w_abo_l = w_in.astype(BF16), [w_branch_a.astype(BF16), w_branch_b.astype(BF16), w_out.astype(BF16)]
    j_in = _gather_ici_job([w_in_l])
    _comm_now("gather_w_in_ici", [j_in])
    j_in1 = _gather_relay_job(j_in.out)
    _comm_now("gather_w_in_relay", [j_in1])
    j_in2 = _gather_d2d_job(j_in1.out)
    _comm_now("gather_w_in_d2d", [j_in2])
    w_in4 = _with_own(j_in2.out, [w_in_l])[0]
    j_abo = _gather_ici_job(w_abo_l)
    j_dn = _gather_ici_job([w_dn_l])
    j_gu = _gather_ici_job([w_gu_l])
    w_full = jnp.concatenate([w_in4[j] for j in range(N_CHIP)], axis=1)
    n_qkv_a = 3 * W_A
    w_p = jnp.concatenate([w_full[:, :n_qkv_a], w_full[:, n_qkv_a + H_A:], w_full[:, n_qkv_a:n_qkv_a + H_A],
                           jnp.zeros((d, LANE - H_A), BF16)], axis=1)

    h1 = _rms_fwd("norm1_fwd", xs, norm1_g)
    qkv = _matmul("proj_qkv", h1, w_p, m=s, n=W_QKV, k=d, out_shape=(s, W_QKV), out_dtype=BF16, jobs=[j_abo])
    j_abo1 = _gather_relay_job(j_abo.out)
    proj_g = _matmul("proj_gates", h1, w_p, m=s, n=2 * d, k=d, b_noff=OFF_GA, out_shape=(s, 2 * d), out_dtype=F32,
                     jobs=[j_abo1])
    fa = _matmul("proj_forget", h1, w_p, m=s, n=LANE, k=d, b_noff=OFF_FA, tn=LANE, out_shape=(s, LANE), out_dtype=F32)
    b_f = _pad_to(b_forget, (1, LANE))
    c_cum = _forget_fwd(fa, b_f)
    c_t = c_cum[:, :H_A].T
    c_col = jnp.broadcast_to(c_t[:, :, None], (H_A, s, LANE))
    c_row = c_t[:, None, :]
    j_abo2 = _gather_d2d_job(j_abo1.out)
    attn_a, lse_col = _fox_fwd(qkv, c_col, c_row, jobs=[j_abo2, j_gu])
    w_a, w_b, w_o = _with_own(j_abo2.out, w_abo_l)
    w_o = w_o.reshape(d, d)

    bucket = jnp.asarray(_t5_bucket_table())
    bias = _bias_table(rel_bias.T, bucket).reshape(H_B, BLOCK, 2 * BLOCK)
    j_gu1 = _gather_relay_job(j_gu.out)
    attn_b = _swa_fwd(qkv, bias, attn_sinks, jobs=[j_gu1])

    j_gu2 = _gather_d2d_job(j_gu1.out)
    ya = _matmul("branch_a", attn_a, w_a, m=s, n=d, k=W_A, b_kind="col", tn=_tile(A_SH, TN),
                 out_shape=(s, d), out_dtype=F32, jobs=[j_gu2])
    w_gu = _with_own(j_gu2.out, [w_gu_l])[0]
    yb = _matmul("branch_b", attn_b, w_b, m=s, n=d, k=W_QB, b_kind="col", tn=_tile(A_SH, TN),
                 out_shape=(s, d), out_dtype=F32)
    mixed = _gate_fwd(ya, yb, proj_g)
    x1 = _matmul("out_proj", mixed, w_o, m=s, n=d, k=d, res=xs, out_shape=(s, d), out_dtype=F32)

    h2 = _rms_fwd("norm2_fwd", x1, norm2_g)
    gu = _matmul("ffn_gate_up", h2, w_gu, m=s, n=2 * FP, k=d, b_kind="col", tn=_tile(FF_P, 1408),
                 out_shape=(s, 2 * FP), out_dtype=BF16, jobs=[j_dn])
    j_dn1 = _gather_relay_job(j_dn.out)
    hidden = _swiglu_fwd(gu, jobs=[j_dn1])
    j_dn2 = _gather_d2d_job(j_dn1.out)
    _comm_now("gather_w_down_d2d", [j_dn2])
    w_dn = _with_own(j_dn2.out, [w_dn_l])[0].reshape(FP, d)
    x2 = _matmul("ffn_down", hidden, w_dn, m=s, n=d, k=FP, tk=_tile(FP, 5632), res=x1, out_shape=(s, d), out_dtype=F32)

    dx2, dx2_b, d_fg, loss_tile = _loss_head(x2, loss_target[0], final_g.reshape(1, d))
    dhidden = _matmul("ffn_down_dx", dx2_b, w_dn, m=s, n=FP, k=d, nt=True, tn=_tile(FF_P, 1408),
                      out_shape=(s, FP), out_dtype=BF16)
    dgu = _swiglu_bwd(dhidden, gu)
    g_dn = _matmul("ffn_down_dw", hidden.T, dx2_b, m=FP, n=d, k=s, tm=_tile(FF_P, 1408, 16),
                   out_shape=(FP, d), out_dtype=BF16).reshape(N_CHIP, FF_P, d)
    j_p_dn = _pair_job([g_dn])
    g_gu = _matmul("ffn_gate_up_dw", h2.T, dgu, m=d, n=2 * FP, k=s, o_kind="col", tm=_tile(d, 512, 16),
                   tn=_tile(FF_P, 1408), out_shape=(2 * N_CHIP, d, FF_P), out_dtype=BF16, jobs=[j_p_dn])
    ps_dn = _pair_sum("pair_sum_w_ffn_down", g_dn, j_p_dn.out[0], place)
    j_c_dn = _chip_job([ps_dn])
    j_p_gu = _pair_job([g_gu])
    dh2 = _matmul("ffn_gate_up_dx", dgu, w_gu, m=s, n=d, k=2 * FP, nt=True, b_kind="col", tn=_tile(d, 1024),
                  tk=_tile(FF_P, TK), out_shape=(s, d), out_dtype=F32, jobs=[j_c_dn, j_p_gu])
    h_dn = _chip_sum("chip_sum_w_ffn_down", ps_dn, j_c_dn.out[0], place)
    ps_gu = _pair_sum("pair_sum_w_ffn_gate_up", g_gu, j_p_gu.out[0], place)
    dx1, dx1_b, d_n2 = _rms_bwd("norm2_bwd", dh2, x1, norm2_g, dx2)

    dmixed = _matmul("out_proj_dx", dx1_b, w_o, m=s, n=d, k=d, nt=True, out_shape=(s, d), out_dtype=BF16)
    dya, dyb, dga, dgb = _gate_bwd(dmixed, ya, yb, proj_g)
    g_a = _matmul("branch_a_dw", attn_a.T, dya, m=W_A, n=d, k=s, o_kind="col", tn=_tile(A_SH, TN),
                  out_shape=(N_CHIP, W_A, A_SH), out_dtype=BF16)
    g_b = _matmul("branch_b_dw", attn_b.T, dyb, m=W_QB, n=d, k=s, o_kind="col", tn=_tile(A_SH, TN),
                  out_shape=(N_CHIP, W_QB, A_SH), out_dtype=BF16)
    j_p_ab = _pair_job([g_a, g_b])
    dattn_a = _matmul("branch_a_dx", dya, w_a, m=s, n=W_A, k=d, nt=True, b_kind="col", tn=_tile(W_A, 1024),
                      tk=_tile(A_SH, TK), out_shape=(s, W_A), out_dtype=BF16, jobs=[j_p_ab])
    dattn_b = _matmul("branch_b_dx", dyb, w_b, m=s, n=W_QB, k=d, nt=True, b_kind="col", tn=_tile(W_QB, 1024),
                      tk=_tile(A_SH, TK), out_shape=(s, W_QB), out_dtype=BF16)
    ps_a, ps_b = (_pair_sum("pair_sum_" + n, g, r, place) for n, g, r in
                  zip(("w_branch_a", "w_branch_b"), (g_a, g_b), j_p_ab.out))

    j_c_gu = _chip_job([ps_gu])
    dq_a, delta_col = _fox_dq(qkv, dattn_a, c_col, c_row, lse_col, jobs=[j_c_gu])
    lse_row = lse_col[:, :, 0][:, None, :]
    delta_row = delta_col[:, :, 0][:, None, :]
    j_c_ab = _chip_job([ps_a, ps_b])
    j_s_dn = _share_job([h_dn])
    dk_a, dv_a, dc_col = _fox_dkv(qkv, dattn_a, c_col, c_row, lse_row, delta_row, jobs=[j_c_ab, j_s_dn])
    dc = _pad_to(dc_col[:, :, 0].T, (s, LANE))
    df, d_bf = _forget_bwd(dc, fa, b_f)
    h_gu = _chip_sum("chip_sum_w_ffn_gate_up", ps_gu, j_c_gu.out[0], place)
    h_a, h_b = (_chip_sum("chip_sum_" + n, p, r, place) for n, p, r in
                zip(("w_branch_a", "w_branch_b"), (ps_a, ps_b), j_c_ab.out))

    j_s_rest = _share_job([h_gu, h_a, h_b])
    dq_b, dk_b, dv_b, dbias, d_sk = _swa_bwd(qkv, dattn_b, bias, attn_sinks, jobs=[j_s_rest])
    r_dn = j_s_dn.out[0]
    r_gu, r_a, r_b = j_s_rest.out
    d_rb = _bias_table_bwd(dbias.reshape(H_B, BLOCK * 2 * BLOCK), bucket).T

    dproj = jnp.concatenate([dq_a, dk_a, dv_a, dq_b, dk_b, dv_b, dga, dgb, df], axis=1)
    g_in_p = _matmul("in_proj_dw", h1.T, dproj, m=d, n=PW, k=s, tn=_tile(PW, 1024), out_shape=(d, PW), out_dtype=BF16)
    g_in_full = jnp.concatenate([g_in_p[:, :n_qkv_a], g_in_p[:, OFF_FA:OFF_FA + H_A], g_in_p[:, n_qkv_a:OFF_FA]], axis=1)
    g_in = jnp.stack([g_in_full[:, j * W_IN_SH:(j + 1) * W_IN_SH] for j in range(N_CHIP)])
    j_p_in = _pair_job([g_in])
    g_o = _matmul("out_proj_dw", mixed.T, dx1_b, m=d, n=d, k=s, out_shape=(d, d), out_dtype=BF16,
                  jobs=[j_p_in]).reshape(N_CHIP, A_SH, d)
    ps_in = _pair_sum("pair_sum_w_in", g_in, j_p_in.out[0], place)
    j_c_in = _chip_job([ps_in])
    j_p_o = _pair_job([g_o])
    dh1 = _matmul("in_proj_dx", dproj, w_p, m=s, n=d, k=PW, nt=True, tn=_tile(d, 1024), tk=_tile(PW, 2560),
                  out_shape=(s, d), out_dtype=F32, jobs=[j_c_in, j_p_o])
    h_in = _chip_sum("chip_sum_w_in", ps_in, j_c_in.out[0], place)
    ps_o = _pair_sum("pair_sum_w_out", g_o, j_p_o.out[0], place)
    j_c_o = _chip_job([ps_o])
    j_s_in = _share_job([h_in])
    grad_x, _, d_n1 = _rms_bwd("norm1_bwd", dh1, xs, norm1_g, dx1, jobs=[j_c_o, j_s_in])
    h_o = _chip_sum("chip_sum_w_out", ps_o, j_c_o.out[0], place)
    j_s_o = _share_job([h_o])

    small = _small_all_reduce(_pack_small(d_n1, d_n2, d_fg, d_bf[:, :H_A], d_sk[:, :H_B], d_rb, loss_tile[0:1, 0:1]))
    loss = small[5, 0]

    grads = {
        "w_branch_a": r_a, "w_branch_b": r_b,
        "w_ffn_gate": r_gu[0:1, :, :FF_SH], "w_ffn_up": r_gu[1:2, :, :FF_SH], "w_ffn_down": r_dn[:, :FF_SH, :],
    }
    given = dict(w_in=(w_in, m_w_in, v_w_in), w_branch_a=(w_branch_a, m_w_branch_a, v_w_branch_a),
                 w_branch_b=(w_branch_b, m_w_branch_b, v_w_branch_b), w_out=(w_out, m_w_out, v_w_out),
                 w_ffn_gate=(w_ffn_gate, m_w_ffn_gate, v_w_ffn_gate), w_ffn_up=(w_ffn_up, m_w_ffn_up, v_w_ffn_up),
                 w_ffn_down=(w_ffn_down, m_w_ffn_down, v_w_ffn_down))

    def col_major(a):
        return jnp.transpose(a[0])[None]

    views = {n: (lambda a: a, lambda a: a) for n in given}
    views["w_ffn_gate"] = views["w_ffn_up"] = (col_major, col_major)
    views["w_in"] = (lambda a: jnp.transpose(a, (2, 0, 1)).reshape(W_IN_SH, d // LANE, LANE),
                     lambda a: jnp.transpose(a.reshape(W_IN_SH, 1, d), (1, 2, 0)))

    def adamw(n, jobs=()):
        to_view, from_view = views[n]
        g_view = to_view(grads[n])
        outs = _adamw("adamw_" + n, to_view(given[n][0]), g_view, to_view(given[n][1]), to_view(given[n][2]), jobs=jobs)
        grads[n] = from_view(g_view)
        return [from_view(o) for o in outs]

    grads["w_in"] = j_s_in.out[0]
    upd = {n: adamw(n) for n in ("w_ffn_gate", "w_ffn_up", "w_ffn_down", "w_branch_a", "w_branch_b", "w_in")}
    sm = _adamw("adamw_small",
                _pack_small(norm1_g, norm2_g, final_g, b_forget, attn_sinks, rel_bias)[None],
                small.at[5].set(0.0)[None],
                _pack_small(m_norm1_g, m_norm2_g, m_final_g, m_b_forget, m_attn_sinks, m_rel_bias)[None],
                _pack_small(v_norm1_g, v_norm2_g, v_final_g, v_b_forget, v_attn_sinks, v_rel_bias)[None],
                jobs=[j_s_o])
    grads["w_out"] = j_s_o.out[0]
    upd["w_out"] = adamw("w_out")
    g_small = _unpack_small(small)
    d_small, m_small, v_small = (_unpack_small(t[0]) for t in sm)

    order = ["norm1_g", "w_in", "b_forget", "attn_sinks", "rel_bias", "w_branch_a", "w_branch_b", "w_out",
             "norm2_g", "w_ffn_gate", "w_ffn_up", "w_ffn_down", "final_g"]
    small_at = {"norm1_g": 0, "norm2_g": 1, "final_g": 2, "b_forget": 3, "attn_sinks": 4, "rel_bias": 5}

    def pick(big_idx, small_src, n):
        return small_src[small_at[n]] if n in small_at else (grads[n] if big_idx is None else upd[n][big_idx])

    return (loss, grad_x[None],
            *[pick(None, g_small, n) for n in order], *[pick(0, d_small, n) for n in order],
            *[pick(1, m_small, n) for n in order], *[pick(2, v_small, n) for n in order])
```

```python
import functools
import math

import numpy as np
import jax
import jax.numpy as jnp
from jax import lax
from jax.experimental import pallas as pl
from jax.experimental.pallas import tpu as pltpu

F32 = jnp.float32
BF16 = jnp.bfloat16
MESH = pl.DeviceIdType.MESH

D_MODEL = 4096
SEQ = 4096
H_A = 16
DH_A = 128
H_B = 32
HKV_B = 4
G_B = H_B // HKV_B
DH_B = 64
WINDOW = 128
NUM_BUCKETS = 32
MAX_DISTANCE = 128
BLOCK = 128
D_FF = ((8 * D_MODEL // 3 + 255) // 256) * 256
EPS = 1e-6
ADAM_LR = 0.001
ADAM_B1 = 0.9
ADAM_B2 = 0.999
ADAM_EPS = 1e-08
ADAM_WD = 0.01
ADAM_STEP = 10

N_CHIP = 4
LANE = 128
VMEM_LIMIT = 56 * 1024 * 1024

TM = 1024
TN = 512
TK = 4096
TQ_A = 1024
TR_EW = 256

W_A = H_A * DH_A
W_QB = H_B * DH_B
W_KB = HKV_B * DH_B
OFF_QA = 0
OFF_KA = W_A
OFF_VA = 2 * W_A
OFF_QB = 3 * W_A
OFF_KB = OFF_QB + W_QB
OFF_VB = OFF_KB + W_KB
OFF_GA = OFF_VB + W_KB
OFF_GB = OFF_GA + D_MODEL
OFF_FA = OFF_GB + D_MODEL
PW = OFF_FA + LANE
W_QKV = OFF_GA
W_IN = 3 * W_A + H_A + W_QB + 2 * W_KB + 2 * D_MODEL
W_IN_SH = W_IN // N_CHIP
A_SH = D_MODEL // N_CHIP
FF_SH = D_FF // N_CHIP
FF_P = -(-FF_SH // LANE) * LANE
FP = N_CHIP * FF_P
PACK_ROWS = 16
RB_ROWS = -(-(NUM_BUCKETS * H_B) // D_MODEL)


def _tile(n, target, mult=LANE):
    t = min(target, n) // mult * mult
    while t > mult and n % t:
        t -= mult
    assert t > 0 and n % t == 0, (n, target, mult)
    return t


def _cparams(sem):
    return pltpu.CompilerParams(dimension_semantics=sem, vmem_limit_bytes=VMEM_LIMIT)


def _sigmoid(x):
    return 1.0 / (1.0 + jnp.exp(-x))


def _dot(a, b):
    return jnp.dot(a, b, preferred_element_type=F32)


def _dot_nt(a, b):
    return lax.dot_general(a, b, (((1,), (1,)), ((), ())), preferred_element_type=F32)


ANY = pl.BlockSpec(memory_space=pl.ANY)


class _Job:
    def __init__(self, ins, outs, sems, start, wait, alias=None):
        self.ins, self.outs, self.sems = list(ins), list(outs), list(sems)
        self.start, self.wait, self.alias = start, wait, dict(alias or {})
        self.out = None


def _split(seq, sizes):
    parts, p = [], 0
    for n in sizes:
        parts.append(seq[p:p + n])
        p += n
    return parts


def _pcall(body, args, jobs, *, name, grid, in_specs, out_specs, out_shape, scratch_shapes=(), sem):
    if not jobs:
        return pl.pallas_call(body, name=name, grid=grid, in_specs=in_specs, out_specs=out_specs, out_shape=out_shape,
                              scratch_shapes=list(scratch_shapes), compiler_params=_cparams(sem))(*args)
    single = not isinstance(out_shape, (list, tuple))
    out_specs_l = [out_specs] if single else list(out_specs)
    out_shape_l = [out_shape] if single else list(out_shape)
    cin = [a for j in jobs for a in j.ins]
    cout = [o for j in jobs for o in j.outs]
    csem = [s for j in jobs for s in j.sems]
    sizes = [len(args), len(cin), len(out_shape_l), len(cout), len(scratch_shapes), len(csem)]
    aliases, io, oo = {}, len(args), len(out_shape_l)
    for j in jobs:
        for a, b in j.alias.items():
            aliases[io + a] = oo + b
        io, oo = io + len(j.ins), oo + len(j.outs)

    def wrapped(*refs):
        ins, cins, outs, couts, scr, sems = _split(refs, sizes)
        ids = [pl.program_id(a) for a in range(len(grid))]
        first = functools.reduce(jnp.logical_and, [i == 0 for i in ids])
        last = functools.reduce(jnp.logical_and, [i == g - 1 for i, g in zip(ids, grid)])
        per_job = list(zip(jobs, _split(cins, [len(j.ins) for j in jobs]), _split(couts, [len(j.outs) for j in jobs]),
                           _split(sems, [len(j.sems) for j in jobs])))

        @pl.when(first)
        def _():
            for j, ji, jo, js in per_job:
                j.start(ji, jo, js)

        body(*ins, *outs, *scr)

        @pl.when(last)
        def _():
            for j, ji, jo, js in per_job:
                j.wait(ji, jo, js)

    res = pl.pallas_call(
        wrapped, name=name, grid=grid, in_specs=list(in_specs) + [ANY] * len(cin),
        out_specs=out_specs_l + [ANY] * len(cout), out_shape=out_shape_l + cout,
        scratch_shapes=list(scratch_shapes) + csem, input_output_aliases=aliases,
        compiler_params=_cparams(("arbitrary",) * len(grid)))(*args, *cin)
    main, rest = res[:len(out_shape_l)], res[len(out_shape_l):]
    for j, o in zip(jobs, _split(rest, [len(j.outs) for j in jobs])):
        j.out = list(o)
    return main[0] if single else list(main)


def _comm_now(name, jobs):
    cin = [a for j in jobs for a in j.ins]
    cout = [o for j in jobs for o in j.outs]
    csem = [s for j in jobs for s in j.sems]
    sizes = [len(cin), len(cout), len(csem)]
    aliases, io, oo = {}, 0, 0
    for j in jobs:
        for a, b in j.alias.items():
            aliases[io + a] = oo + b
        io, oo = io + len(j.ins), oo + len(j.outs)

    def body(*refs):
        cins, couts, sems = _split(refs, sizes)
        per_job = list(zip(jobs, _split(cins, [len(j.ins) for j in jobs]), _split(couts, [len(j.outs) for j in jobs]),
                           _split(sems, [len(j.sems) for j in jobs])))
        for j, ji, jo, js in per_job:
            j.start(ji, jo, js)
        for j, ji, jo, js in per_job:
            j.wait(ji, jo, js)

    res = pl.pallas_call(body, name=name, in_specs=[ANY] * len(cin), out_specs=[ANY] * len(cout), out_shape=cout,
                         scratch_shapes=csem, input_output_aliases=aliases)(*cin)
    for j, o in zip(jobs, _split(res, [len(j.outs) for j in jobs])):
        j.out = list(o)


def _slot_spec(shape, kind, br, bc, rc):
    if kind == "2d":
        return pl.BlockSpec((br, bc), lambda *g: rc(*g))
    if kind == "col":
        assert shape[2] % bc == 0, (shape, bc)
        per = shape[2] // bc

        def im_col(*g):
            rb, cb = rc(*g)
            return (cb // per, rb, cb % per)
        return pl.BlockSpec((None, br, bc), im_col)
    assert kind == "row" and shape[1] % br == 0, (shape, kind, br)
    per = shape[1] // br

    def im_row(*g):
        rb, cb = rc(*g)
        return (rb // per, rb % per, cb)
    return pl.BlockSpec((None, br, bc), im_row)


def _matmul(name, a, b, *, m, n, k, nt=False, b_kind="2d", o_kind="2d", out_shape, out_dtype,
            tm=None, tn=None, tk=None, b_noff=0, res=None, b_outer=False, jobs=()):
    tm = tm or _tile(m, TM, 16)
    tn = tn or _tile(n, TN)
    tk = tk or _tile(k, TK)
    assert m % tm == 0 and n % tn == 0 and k % tk == 0 and b_noff % tn == 0
    nk = k // tk
    noff = b_noff // tn
    if b_outer:
        grid = (n // tn, m // tm, nk)
        ij = lambda g0, g1: (g1, g0)
    else:
        grid = (m // tm, n // tn, nk)
        ij = lambda g0, g1: (g0, g1)

    a_spec = pl.BlockSpec((tm, tk), lambda g0, g1, kk: (ij(g0, g1)[0], kk))
    if nt:
        b_spec = _slot_spec(b.shape, b_kind, tn, tk, lambda g0, g1, kk: (ij(g0, g1)[1] + noff, kk))
    else:
        b_spec = _slot_spec(b.shape, b_kind, tk, tn, lambda g0, g1, kk: (kk, ij(g0, g1)[1] + noff))
    o_spec = _slot_spec(out_shape, o_kind, tm, tn, lambda g0, g1, kk: ij(g0, g1))
    in_specs = [a_spec, b_spec]
    args = [a, b]
    if res is not None:
        in_specs.append(pl.BlockSpec((tm, tn), lambda g0, g1, kk: ij(g0, g1)))
        args.append(res)

    def body(*refs):
        a_ref, b_ref = refs[0], refs[1]
        r_ref = refs[2] if res is not None else None
        o_ref = refs[3] if res is not None else refs[2]

        def prod():
            return _dot_nt(a_ref[...], b_ref[...]) if nt else _dot(a_ref[...], b_ref[...])

        def finish(acc):
            if r_ref is not None:
                acc = acc + r_ref[...].astype(F32)
            o_ref[...] = acc.astype(o_ref.dtype)

        if nk == 1:
            finish(prod())
            return
        acc_ref = refs[-1]
        kk = pl.program_id(2)

        @pl.when(kk == 0)
        def _():
            acc_ref[...] = prod()

        @pl.when(jnp.logical_and(kk > 0, kk < nk - 1))
        def _():
            acc_ref[...] += prod()

        @pl.when(kk == nk - 1)
        def _():
            finish(acc_ref[...] + prod())

    return _pcall(
        body, args, jobs, name=name, grid=grid, in_specs=in_specs, out_specs=o_spec,
        out_shape=jax.ShapeDtypeStruct(out_shape, out_dtype),
        scratch_shapes=[pltpu.VMEM((tm, tn), F32)] if nk > 1 else [],
        sem=("parallel", "parallel", "arbitrary"))


def _rms_fwd(name, x, g):
    s, d = x.shape
    tr = _tile(s, TR_EW, 16)

    def body(x_ref, g_ref, h_ref):
        xf = x_ref[...]
        rstd = lax.rsqrt(jnp.mean(xf * xf, axis=-1, keepdims=True) + EPS)
        h_ref[...] = (xf * rstd * g_ref[...]).astype(h_ref.dtype)

    return pl.pallas_call(
        body, name=name, grid=(s // tr,),
        in_specs=[pl.BlockSpec((tr, d), lambda i: (i, 0)), pl.BlockSpec((1, d), lambda i: (0, 0))],
        out_specs=pl.BlockSpec((tr, d), lambda i: (i, 0)),
        out_shape=jax.ShapeDtypeStruct((s, d), BF16),
        compiler_params=_cparams(("parallel",)),
    )(x, g)


def _rms_bwd_rows(dh, xf, g):
    rstd = lax.rsqrt(jnp.mean(xf * xf, axis=-1, keepdims=True) + EPS)
    xhat = xf * rstd
    dxhat = dh * g
    dx = rstd * (dxhat - xhat * jnp.mean(dxhat * xhat, axis=-1, keepdims=True))
    return dx, dh * xhat


def _fold8(v):
    return jnp.sum(v.reshape(v.shape[0] // 8, 8, v.shape[1]), axis=0)


def _rms_bwd(name, dh, x, g, dres, jobs=()):
    s, d = x.shape
    tr = _tile(s, TR_EW, 16)
    n = s // tr

    def body(dh_ref, x_ref, g_ref, r_ref, dx_ref, dxb_ref, dg_ref, acc_ref):
        i = pl.program_id(0)
        dx, dgrows = _rms_bwd_rows(dh_ref[...].astype(F32), x_ref[...], g_ref[...])
        dx = dx + r_ref[...]
        dx_ref[...] = dx
        dxb_ref[...] = dx.astype(BF16)

        @pl.when(i == 0)
        def _():
            acc_ref[...] = jnp.zeros_like(acc_ref)
        acc_ref[...] += _fold8(dgrows)

        @pl.when(i == n - 1)
        def _():
            dg_ref[...] = jnp.sum(acc_ref[...], axis=0, keepdims=True)

    row = pl.BlockSpec((tr, d), lambda i: (i, 0))
    vec = pl.BlockSpec((1, d), lambda i: (0, 0))
    return _pcall(
        body, (dh, x, g, dres), jobs, name=name, grid=(n,), in_specs=[row, row, vec, row], out_specs=[row, row, vec],
        out_shape=[jax.ShapeDtypeStruct((s, d), F32), jax.ShapeDtypeStruct((s, d), BF16),
                   jax.ShapeDtypeStruct((1, d), F32)],
        scratch_shapes=[pltpu.VMEM((8, d), F32)],
        sem=("arbitrary",))


def _loss_head(x2, target, g):
    s, d = x2.shape
    tr = _tile(s, TR_EW, 16)
    n = s // tr

    def body(x_ref, t_ref, g_ref, dx_ref, dxb_ref, dg_ref, loss_ref, acc_ref):
        i = pl.program_id(0)
        xf = x_ref[...]
        gv = g_ref[...]
        rstd = lax.rsqrt(jnp.mean(xf * xf, axis=-1, keepdims=True) + EPS)
        err = xf * rstd * gv - t_ref[...]
        row_loss = jnp.mean(err * err, axis=-1, keepdims=True)
        dx, dgrows = _rms_bwd_rows(err / d, xf, gv)
        dx_ref[...] = dx
        dxb_ref[...] = dx.astype(BF16)

        @pl.when(i == 0)
        def _():
            acc_ref[...] = jnp.zeros_like(acc_ref)
            loss_ref[...] = jnp.zeros_like(loss_ref)
        acc_ref[...] += _fold8(dgrows)
        loss_ref[...] += jnp.broadcast_to(0.5 * jnp.sum(row_loss, axis=0, keepdims=True), (8, LANE))

        @pl.when(i == n - 1)
        def _():
            dg_ref[...] = jnp.sum(acc_ref[...], axis=0, keepdims=True)

    row = pl.BlockSpec((tr, d), lambda i: (i, 0))
    vec = pl.BlockSpec((1, d), lambda i: (0, 0))
    return pl.pallas_call(
        body, name="loss_head", grid=(n,), in_specs=[row, row, vec],
        out_specs=[row, row, vec, pl.BlockSpec((8, LANE), lambda i: (0, 0))],
        out_shape=[jax.ShapeDtypeStruct((s, d), F32), jax.ShapeDtypeStruct((s, d), BF16),
                   jax.ShapeDtypeStruct((1, d), F32), jax.ShapeDtypeStruct((8, LANE), F32)],
        scratch_shapes=[pltpu.VMEM((8, d), F32)],
        compiler_params=_cparams(("arbitrary",)),
    )(x2, target, g)


def _gate_fwd(ya, yb, proj_g):
    s, d = ya.shape
    tr = _tile(s, TR_EW, 16)
    tc = _tile(d, 1024)
    nc = d // tc

    def body(ya_ref, yb_ref, ga_ref, gb_ref, o_ref):
        o_ref[...] = (_sigmoid(ga_ref[...]) * ya_ref[...] + _sigmoid(gb_ref[...]) * yb_ref[...]).astype(o_ref.dtype)

    blk = pl.BlockSpec((tr, tc), lambda i, j: (i, j))
    return pl.pallas_call(
        body, name="gate_fwd", grid=(s // tr, nc),
        in_specs=[blk, blk, blk, pl.BlockSpec((tr, tc), lambda i, j: (i, j + nc))],
        out_specs=blk, out_shape=jax.ShapeDtypeStruct((s, d), BF16),
        compiler_params=_cparams(("parallel", "parallel")),
    )(ya, yb, proj_g, proj_g)


def _gate_bwd(dmixed, ya, yb, proj_g):
    s, d = ya.shape
    tr = _tile(s, TR_EW, 16)
    tc = _tile(d, 1024)
    nc = d // tc

    def body(dm_ref, ya_ref, yb_ref, ga_ref, gb_ref, dya_ref, dyb_ref, dga_ref, dgb_ref):
        dm = dm_ref[...].astype(F32)
        sa = _sigmoid(ga_ref[...])
        sb = _sigmoid(gb_ref[...])
        dya_ref[...] = (dm * sa).astype(BF16)
        dyb_ref[...] = (dm * sb).astype(BF16)
        dga_ref[...] = (dm * ya_ref[...] * sa * (1.0 - sa)).astype(BF16)
        dgb_ref[...] = (dm * yb_ref[...] * sb * (1.0 - sb)).astype(BF16)

    blk = pl.BlockSpec((tr, tc), lambda i, j: (i, j))
    out = jax.ShapeDtypeStruct((s, d), BF16)
    return pl.pallas_call(
        body, name="gate_bwd", grid=(s // tr, nc),
        in_specs=[blk, blk, blk, blk, pl.BlockSpec((tr, tc), lambda i, j: (i, j + nc))],
        out_specs=[blk, blk, blk, blk], out_shape=[out, out, out, out],
        compiler_params=_cparams(("parallel", "parallel")),
    )(dmixed, ya, yb, proj_g, proj_g)


def _swiglu_fwd(gu, jobs=()):
    s = gu.shape[0]
    tr = _tile(s, 128, 16)

    def body(gu_ref, h_ref):
        gate = gu_ref[:, :FF_P].astype(F32)
        up = gu_ref[:, FF_P:].astype(F32)
        h_ref[...] = (gate * _sigmoid(gate) * up).astype(h_ref.dtype)

    return _pcall(
        body, (gu,), jobs, name="swiglu_fwd", grid=(s // tr, N_CHIP),
        in_specs=[pl.BlockSpec((tr, 2 * FF_P), lambda i, j: (i, j))],
        out_specs=pl.BlockSpec((tr, FF_P), lambda i, j: (i, j)),
        out_shape=jax.ShapeDtypeStruct((s, FP), BF16),
        sem=("parallel", "parallel"))


def _swiglu_bwd(dhidden, gu):
    s = gu.shape[0]
    tr = _tile(s, 128, 16)

    def body(dh_ref, gu_ref, dgu_ref):
        gate = gu_ref[:, :FF_P].astype(F32)
        up = gu_ref[:, FF_P:].astype(F32)
        dh = dh_ref[...].astype(F32)
        sg = _sigmoid(gate)
        dgu_ref[:, :FF_P] = (dh * up * sg * (1.0 + gate * (1.0 - sg))).astype(BF16)
        dgu_ref[:, FF_P:] = (dh * gate * sg).astype(BF16)

    return pl.pallas_call(
        body, name="swiglu_bwd", grid=(s // tr, N_CHIP),
        in_specs=[pl.BlockSpec((tr, FF_P), lambda i, j: (i, j)), pl.BlockSpec((tr, 2 * FF_P), lambda i, j: (i, j))],
        out_specs=pl.BlockSpec((tr, 2 * FF_P), lambda i, j: (i, j)),
        out_shape=jax.ShapeDtypeStruct((s, 2 * FP), BF16),
        compiler_params=_cparams(("parallel", "parallel")),
    )(dhidden, gu)


def _tri(n, upper):
    r = lax.broadcasted_iota(jnp.int32, (n, n), 0)
    c = lax.broadcasted_iota(jnp.int32, (n, n), 1)
    return (c >= r if upper else c <= r).astype(F32)


def _forget_fwd(fa, bias):
    s = fa.shape[0]
    tb = _tile(s, 512, 8)

    def body(f_ref, b_ref, c_ref, carry_ref):
        i = pl.program_id(0)

        @pl.when(i == 0)
        def _():
            carry_ref[...] = jnp.zeros_like(carry_ref)
        z = f_ref[...] + b_ref[...]
        logf = jnp.minimum(z, 0.0) - jnp.log(1.0 + jnp.exp(-jnp.abs(z)))
        c = jnp.dot(_tri(tb, False), logf, precision=lax.Precision.HIGHEST, preferred_element_type=F32)
        c_ref[...] = c + carry_ref[0:1, :]
        carry_ref[...] = jnp.broadcast_to(c_ref[tb - 1:tb, :], carry_ref.shape)

    return pl.pallas_call(
        body, name="forget_fwd", grid=(s // tb,),
        in_specs=[pl.BlockSpec((tb, LANE), lambda i: (i, 0)), pl.BlockSpec((1, LANE), lambda i: (0, 0))],
        out_specs=pl.BlockSpec((tb, LANE), lambda i: (i, 0)),
        out_shape=jax.ShapeDtypeStruct((s, LANE), F32),
        scratch_shapes=[pltpu.VMEM((8, LANE), F32)],
        compiler_params=_cparams(("arbitrary",)),
    )(fa, bias)


def _forget_bwd(dc, fa, bias):
    s = fa.shape[0]
    tb = _tile(s, 512, 16)
    n = s // tb

    def body(dc_ref, f_ref, b_ref, df_ref, db_ref, carry_ref, acc_ref, tmp_ref):
        i = pl.program_id(0)

        @pl.when(i == 0)
        def _():
            carry_ref[...] = jnp.zeros_like(carry_ref)
            acc_ref[...] = jnp.zeros_like(acc_ref)
        dlogf = jnp.dot(_tri(tb, True), dc_ref[...], precision=lax.Precision.HIGHEST, preferred_element_type=F32)
        tmp_ref[...] = dlogf + carry_ref[0:1, :]
        carry_ref[...] = jnp.broadcast_to(tmp_ref[0:1, :], carry_ref.shape)
        df = tmp_ref[...] * _sigmoid(-(f_ref[...] + b_ref[...]))
        df_ref[...] = df.astype(BF16)
        acc_ref[...] += _fold8(df)

        @pl.when(i == n - 1)
        def _():
            db_ref[...] = jnp.sum(acc_ref[...], axis=0, keepdims=True)

    rev = pl.BlockSpec((tb, LANE), lambda i: (n - 1 - i, 0))
    vec = pl.BlockSpec((1, LANE), lambda i: (0, 0))
    return pl.pallas_call(
        body, name="forget_bwd", grid=(n,), in_specs=[rev, rev, vec], out_specs=[rev, vec],
        out_shape=[jax.ShapeDtypeStruct((s, LANE), BF16), jax.ShapeDtypeStruct((1, LANE), F32)],
        scratch_shapes=[pltpu.VMEM((8, LANE), F32), pltpu.VMEM((8, LANE), F32), pltpu.VMEM((tb, LANE), F32)],
        compiler_params=_cparams(("arbitrary",)),
    )(dc, fa, bias)


_A_SCALE = DH_A ** -0.5


def _causal(tq, tk, transposed):
    r = lax.broadcasted_iota(jnp.int32, (tq, tk), 0)
    c = lax.broadcasted_iota(jnp.int32, (tq, tk), 1)
    return c >= r if transposed else r >= c


def _fox_fwd(qkv, c_col, c_row, jobs=()):
    s = qkv.shape[0]
    t = _tile(s, TQ_A)
    n = s // t
    kb, vb = OFF_KA // DH_A, OFF_VA // DH_A

    def body(q_ref, k_ref, v_ref, cq_ref, ck_ref, o_ref, lse_ref, m_s, l_s, acc_s):
        i, j = pl.program_id(1), pl.program_id(2)

        @pl.when(j == 0)
        def _():
            m_s[...] = jnp.full_like(m_s, -jnp.inf)
            l_s[...] = jnp.zeros_like(l_s)
            acc_s[...] = jnp.zeros_like(acc_s)

        def step(diag):
            sc = _dot_nt(q_ref[...], k_ref[...]) * _A_SCALE + (cq_ref[:, 0:1] - ck_ref[...])
            if diag:
                sc = jnp.where(_causal(t, t, False), sc, -jnp.inf)
            m_prev = m_s[...]
            m_new = jnp.maximum(m_prev, jnp.max(sc, axis=1, keepdims=True))
            alpha = jnp.exp(m_prev - m_new)
            p = jnp.exp(sc - m_new[:, 0:1])
            l_s[...] = alpha * l_s[...] + jnp.sum(p, axis=1, keepdims=True)
            acc_s[...] = alpha * acc_s[...] + _dot(p.astype(BF16), v_ref[...])
            m_s[...] = m_new

        @pl.when(j < i)
        def _():
            step(False)

        @pl.when(j == i)
        def _():
            step(True)
            o_ref[...] = (acc_s[...] / l_s[...]).astype(o_ref.dtype)
            lse_ref[...] = m_s[...] + jnp.log(l_s[...])

    jc = lambda i, j: jnp.minimum(j, i)
    return _pcall(
        body, (qkv, qkv, qkv, c_col, c_row), jobs, name="fox_fwd", grid=(H_A, n, n),
        in_specs=[pl.BlockSpec((t, DH_A), lambda h, i, j: (i, h)),
                  pl.BlockSpec((t, DH_A), lambda h, i, j: (jc(i, j), kb + h)),
                  pl.BlockSpec((t, DH_A), lambda h, i, j: (jc(i, j), vb + h)),
                  pl.BlockSpec((None, t, LANE), lambda h, i, j: (h, i, 0)),
                  pl.BlockSpec((None, 1, t), lambda h, i, j: (h, 0, jc(i, j)))],
        out_specs=[pl.BlockSpec((t, DH_A), lambda h, i, j: (i, h)),
                   pl.BlockSpec((None, t, LANE), lambda h, i, j: (h, i, 0))],
        out_shape=[jax.ShapeDtypeStruct((s, W_A), BF16), jax.ShapeDtypeStruct((H_A, s, LANE), F32)],
        scratch_shapes=[pltpu.VMEM((t, LANE), F32)] * 3,
        sem=("parallel", "parallel", "arbitrary"))


def _fox_dq(qkv, do, c_col, c_row, lse_col, jobs=()):
    s = qkv.shape[0]
    t = _tile(s, TQ_A)
    n = s // t
    kb, vb = OFF_KA // DH_A, OFF_VA // DH_A

    def body(q_ref, k_ref, v_ref, do_ref, cq_ref, ck_ref, lse_ref, dq_ref, dl_ref, pdk_s, pk_s, dl_s):
        i, j = pl.program_id(1), pl.program_id(2)

        @pl.when(j == 0)
        def _():
            pdk_s[...] = jnp.zeros_like(pdk_s)
            pk_s[...] = jnp.zeros_like(pk_s)
            dl_s[...] = jnp.zeros_like(dl_s)

        def step(diag):
            sc = _dot_nt(q_ref[...], k_ref[...]) * _A_SCALE + (cq_ref[:, 0:1] - ck_ref[...])
            if diag:
                sc = jnp.where(_causal(t, t, False), sc, -jnp.inf)
            p = jnp.exp(sc - lse_ref[:, 0:1])
            pdp = p * _dot_nt(do_ref[...], v_ref[...])
            dl_s[...] += jnp.sum(pdp, axis=1, keepdims=True)
            pdk_s[...] += _dot(pdp.astype(BF16), k_ref[...])
            pk_s[...] += _dot(p.astype(BF16), k_ref[...])

        @pl.when(j < i)
        def _():
            step(False)

        @pl.when(j == i)
        def _():
            step(True)
            dq_ref[...] = ((pdk_s[...] - dl_s[...] * pk_s[...]) * _A_SCALE).astype(dq_ref.dtype)
            dl_ref[...] = dl_s[...]

    jc = lambda i, j: jnp.minimum(j, i)
    col = pl.BlockSpec((None, t, LANE), lambda h, i, j: (h, i, 0))
    return _pcall(
        body, (qkv, qkv, qkv, do, c_col, c_row, lse_col), jobs, name="fox_dq", grid=(H_A, n, n),
        in_specs=[pl.BlockSpec((t, DH_A), lambda h, i, j: (i, h)),
                  pl.BlockSpec((t, DH_A), lambda h, i, j: (jc(i, j), kb + h)),
                  pl.BlockSpec((t, DH_A), lambda h, i, j: (jc(i, j), vb + h)),
                  pl.BlockSpec((t, DH_A), lambda h, i, j: (i, h)),
                  col,
                  pl.BlockSpec((None, 1, t), lambda h, i, j: (h, 0, jc(i, j))),
                  col],
        out_specs=[pl.BlockSpec((t, DH_A), lambda h, i, j: (i, h)), col],
        out_shape=[jax.ShapeDtypeStruct((s, W_A), BF16), jax.ShapeDtypeStruct((H_A, s, LANE), F32)],
        scratch_shapes=[pltpu.VMEM((t, DH_A), F32), pltpu.VMEM((t, DH_A), F32), pltpu.VMEM((t, LANE), F32)],
        sem=("parallel", "parallel", "arbitrary"))


def _fox_dkv(qkv, do, c_col, c_row, lse_row, delta_row, jobs=()):
    s = qkv.shape[0]
    t = _tile(s, TQ_A)
    n = s // t
    kb, vb = OFF_KA // DH_A, OFF_VA // DH_A

    def body(q_ref, k_ref, v_ref, do_ref, cq_ref, ck_ref, lse_ref, dl_ref, dk_ref, dv_ref, dc_ref,
             dk_s, dv_s, dc_s):
        j, i = pl.program_id(1), pl.program_id(2)

        @pl.when(i == 0)
        def _():
            dk_s[...] = jnp.zeros_like(dk_s)
            dv_s[...] = jnp.zeros_like(dv_s)
            dc_s[...] = jnp.zeros_like(dc_s)

        def step(diag):
            st = _dot_nt(k_ref[...], q_ref[...]) * _A_SCALE + (cq_ref[...] - ck_ref[:, 0:1])
            if diag:
                st = jnp.where(_causal(t, t, True), st, -jnp.inf)
            pt = jnp.exp(st - lse_ref[...])
            dv_s[...] += _dot(pt.astype(BF16), do_ref[...])
            dpt = _dot_nt(v_ref[...], do_ref[...])
            dst = pt * (dpt - dl_ref[...])
            dk_s[...] += _dot(dst.astype(BF16), q_ref[...])
            dc_s[...] -= jnp.sum(dst, axis=1, keepdims=True)

        @pl.when(i == j)
        def _():
            step(True)

        @pl.when(i > j)
        def _():
            step(False)

        @pl.when(i == n - 1)
        def _():
            dk_ref[...] = (dk_s[...] * _A_SCALE).astype(dk_ref.dtype)
            dv_ref[...] = dv_s[...].astype(dv_ref.dtype)
            dc_ref[...] = dc_s[...]

    ic = lambda j, i: jnp.maximum(i, j)
    rowq = pl.BlockSpec((None, 1, t), lambda h, j, i: (h, 0, ic(j, i)))
    kv_out = pl.BlockSpec((t, DH_A), lambda h, j, i: (j, h))
    return _pcall(
        body, (qkv, qkv, qkv, do, c_row, c_col, lse_row, delta_row), jobs, name="fox_dkv", grid=(H_A, n, n),
        in_specs=[pl.BlockSpec((t, DH_A), lambda h, j, i: (ic(j, i), h)),
                  pl.BlockSpec((t, DH_A), lambda h, j, i: (j, kb + h)),
                  pl.BlockSpec((t, DH_A), lambda h, j, i: (j, vb + h)),
                  pl.BlockSpec((t, DH_A), lambda h, j, i: (ic(j, i), h)),
                  rowq,
                  pl.BlockSpec((None, t, LANE), lambda h, j, i: (h, j, 0)),
                  rowq, rowq],
        out_specs=[kv_out, kv_out, pl.BlockSpec((None, t, LANE), lambda h, j, i: (h, j, 0))],
        out_shape=[jax.ShapeDtypeStruct((s, W_A), BF16), jax.ShapeDtypeStruct((s, W_A), BF16),
                   jax.ShapeDtypeStruct((H_A, s, LANE), F32)],
        scratch_shapes=[pltpu.VMEM((t, DH_A), F32), pltpu.VMEM((t, DH_A), F32), pltpu.VMEM((t, LANE), F32)],
        sem=("parallel", "parallel", "arbitrary"))


_B_SCALE = DH_B ** -0.5
_HALF = LANE // 2


def _t5_bucket_table():
    ql = np.arange(BLOCK)[:, None]
    kl = np.arange(2 * BLOCK)[None, :]
    dist = np.clip(ql + BLOCK - kl, 0, None)
    max_exact = NUM_BUCKETS // 2
    large = max_exact + (np.log(np.maximum(dist, 1) / max_exact) / np.log(MAX_DISTANCE / max_exact)
                         * (NUM_BUCKETS - max_exact)).astype(np.int64)
    large = np.minimum(large, NUM_BUCKETS - 1)
    return np.where(dist < max_exact, dist, large).astype(np.int32).reshape(1, BLOCK * 2 * BLOCK)


def _one_hot_buckets(bucket_ref, n):
    return (lax.broadcasted_iota(jnp.int32, (NUM_BUCKETS, n), 0) == bucket_ref[...]).astype(F32)


def _bias_table(rel_bias_t, bucket):
    nqk = bucket.shape[1]
    tc = _tile(nqk, 4096)

    def body(rb_ref, bk_ref, o_ref):
        o_ref[...] = jnp.dot(rb_ref[...], _one_hot_buckets(bk_ref, tc), precision=lax.Precision.HIGHEST,
                             preferred_element_type=F32)

    return pl.pallas_call(
        body, name="bias_table", grid=(nqk // tc,),
        in_specs=[pl.BlockSpec((H_B, NUM_BUCKETS), lambda i: (0, 0)), pl.BlockSpec((1, tc), lambda i: (0, i))],
        out_specs=pl.BlockSpec((H_B, tc), lambda i: (0, i)),
        out_shape=jax.ShapeDtypeStruct((H_B, nqk), F32),
        compiler_params=_cparams(("parallel",)),
    )(rel_bias_t, bucket)


def _bias_table_bwd(dbias, bucket):
    nqk = bucket.shape[1]
    tc = _tile(nqk, 4096)
    n = nqk // tc

    def body(db_ref, bk_ref, o_ref):
        i = pl.program_id(0)

        @pl.when(i == 0)
        def _():
            o_ref[...] = jnp.zeros_like(o_ref)
        o_ref[...] += lax.dot_general(db_ref[...], _one_hot_buckets(bk_ref, tc), (((1,), (1,)), ((), ())),
                                      precision=lax.Precision.HIGHEST, preferred_element_type=F32)

    return pl.pallas_call(
        body, name="bias_table_bwd", grid=(n,),
        in_specs=[pl.BlockSpec((H_B, tc), lambda i: (0, i)), pl.BlockSpec((1, tc), lambda i: (0, i))],
        out_specs=pl.BlockSpec((H_B, NUM_BUCKETS), lambda i: (0, 0)),
        out_shape=jax.ShapeDtypeStruct((H_B, NUM_BUCKETS), F32),
        compiler_params=_cparams(("arbitrary",)),
    )(dbias, bucket)


def _lane_lo():
    return lax.broadcasted_iota(jnp.int32, (1, LANE), 1) < _HALF


def _dup_kv_head(cat, hk):
    xcol = cat[:, (hk // 2) * LANE:(hk // 2 + 1) * LANE].astype(F32)
    swapped = pltpu.roll(xcol, _HALF, 1)
    lo = _lane_lo()
    return (jnp.where(lo, xcol, swapped) if hk % 2 == 0 else jnp.where(lo, swapped, xcol)).astype(BF16)


def _band_mask(first_block):
    ql = lax.broadcasted_iota(jnp.int32, (BLOCK, 2 * BLOCK), 0)
    kl = lax.broadcasted_iota(jnp.int32, (BLOCK, 2 * BLOCK), 1)
    dist = ql + BLOCK - kl
    ok = jnp.logical_and(dist >= 0, dist < WINDOW)
    return jnp.logical_and(ok, jnp.logical_or(jnp.logical_not(first_block), kl >= BLOCK))


def _swa_probs(qh, kdup, bias_h, sink, mask):
    sc = _dot_nt(qh, kdup) * _B_SCALE + bias_h
    sc = jnp.where(mask, sc, -jnp.inf)
    m = jnp.maximum(jnp.max(sc, axis=1, keepdims=True), sink)
    p = jnp.exp(sc - m)
    e_sink = jnp.exp(sink - m)
    inv = 1.0 / (jnp.sum(p, axis=1, keepdims=True) + e_sink)
    return p * inv, e_sink * inv


def _split_pair(ref, col):
    x = ref[:, col * LANE:(col + 1) * LANE].astype(F32)
    lo = _lane_lo()
    return jnp.where(lo, x, 0.0).astype(BF16), jnp.where(lo, 0.0, x).astype(BF16)


def _swa_fwd(qkv, bias, sinks, jobs=()):
    s = qkv.shape[0]
    nb = s // BLOCK
    qb, kb, vb = OFF_QB // W_QB, OFF_KB // W_KB, OFF_VB // W_KB
    assert OFF_QB % W_QB == 0 and OFF_KB % W_KB == 0 and OFF_VB % W_KB == 0 and W_KB % LANE == 0 and G_B % 2 == 0

    def body(q_ref, kp_ref, kc_ref, vp_ref, vc_ref, bias_ref, sink_ref, o_ref):
        i = pl.program_id(0)
        mask = _band_mask(i == 0)
        kcat = jnp.concatenate([kp_ref[...], kc_ref[...]], axis=0)
        vcat = jnp.concatenate([vp_ref[...], vc_ref[...]], axis=0)
        lo = _lane_lo()
        for hk in range(HKV_B):
            kdup, vdup = _dup_kv_head(kcat, hk), _dup_kv_head(vcat, hk)
            for pr in range(G_B // 2):
                h0 = hk * G_B + 2 * pr
                q0, q1 = _split_pair(q_ref, h0 // 2)
                p0, _ = _swa_probs(q0, kdup, bias_ref[h0], sink_ref[0, h0], mask)
                p1, _ = _swa_probs(q1, kdup, bias_ref[h0 + 1], sink_ref[0, h0 + 1], mask)
                o0 = _dot(p0.astype(BF16), vdup)
                o1 = _dot(p1.astype(BF16), vdup)
                o_ref[:, (h0 // 2) * LANE:(h0 // 2 + 1) * LANE] = jnp.where(lo, o0, o1).astype(o_ref.dtype)

    prev = lambda i: jnp.maximum(i - 1, 0)
    return _pcall(
        body, (qkv, qkv, qkv, qkv, qkv, bias, sinks), jobs, name="swa_fwd", grid=(nb,),
        in_specs=[pl.BlockSpec((BLOCK, W_QB), lambda i: (i, qb)),
                  pl.BlockSpec((BLOCK, W_KB), lambda i: (prev(i), kb)),
                  pl.BlockSpec((BLOCK, W_KB), lambda i: (i, kb)),
                  pl.BlockSpec((BLOCK, W_KB), lambda i: (prev(i), vb)),
                  pl.BlockSpec((BLOCK, W_KB), lambda i: (i, vb)),
                  pl.BlockSpec((H_B, BLOCK, 2 * BLOCK), lambda i: (0, 0, 0)),
                  pl.BlockSpec(memory_space=pltpu.SMEM)],
        out_specs=pl.BlockSpec((BLOCK, W_QB), lambda i: (i, 0)),
        out_shape=jax.ShapeDtypeStruct((s, W_QB), BF16),
        sem=("parallel",))


def _swa_bwd(qkv, do, bias, sinks, jobs=()):
    s = qkv.shape[0]
    nb = s // BLOCK
    qb, kb, vb = OFF_QB // W_QB, OFF_KB // W_KB, OFF_VB // W_KB

    def body(q_ref, kp_ref, kc_ref, vp_ref, vc_ref, do_ref, bias_ref, sink_ref,
             dq_ref, dk_ref, dv_ref, dbias_ref, dsink_ref, carry_k, carry_v):
        i = pl.program_id(0)
        lo = _lane_lo()

        @pl.when(i == 0)
        def _():
            dbias_ref[...] = jnp.zeros_like(dbias_ref)
            dsink_ref[...] = jnp.zeros_like(dsink_ref)
            carry_k[...] = jnp.zeros_like(carry_k)
            carry_v[...] = jnp.zeros_like(carry_v)

        @pl.when(i < nb)
        def _():
            mask = _band_mask(i == 0)
            kcat = jnp.concatenate([kp_ref[...], kc_ref[...]], axis=0)
            vcat = jnp.concatenate([vp_ref[...], vc_ref[...]], axis=0)
            lane = lax.broadcasted_iota(jnp.int32, (1, LANE), 1)
            dsink = jnp.zeros((1, LANE), F32)
            dk_cols = [jnp.zeros((2 * BLOCK, LANE), F32) for _ in range(W_KB // LANE)]
            dv_cols = [jnp.zeros((2 * BLOCK, LANE), F32) for _ in range(W_KB // LANE)]
            for hk in range(HKV_B):
                kdup, vdup = _dup_kv_head(kcat, hk), _dup_kv_head(vcat, hk)
                dk_acc = jnp.zeros((2 * BLOCK, LANE), F32)
                dv_acc = jnp.zeros((2 * BLOCK, LANE), F32)
                for pr in range(G_B // 2):
                    h0 = hk * G_B + 2 * pr
                    col = h0 // 2
                    qs = _split_pair(q_ref, col)
                    dos = _split_pair(do_ref, col)
                    dqs = []
                    for e in range(2):
                        h = h0 + e
                        p, p_sink = _swa_probs(qs[e], kdup, bias_ref[h], sink_ref[0, h], mask)
                        dp = _dot_nt(dos[e], vdup)
                        delta = jnp.sum(p * dp, axis=1, keepdims=True)
                        ds = p * (dp - delta)
                        dbias_ref[h] += ds
                        dsink = dsink - jnp.where(lane == h, jnp.sum(p_sink * delta, axis=0, keepdims=True), 0.0)
                        dqs.append(_dot(ds.astype(BF16), kdup))
                        dk_acc = dk_acc + _dot(ds.T.astype(BF16), qs[e])
                        dv_acc = dv_acc + _dot(p.T.astype(BF16), dos[e])
                    dq_ref[:, col * LANE:(col + 1) * LANE] = (jnp.where(lo, dqs[0], dqs[1]) * _B_SCALE).astype(dq_ref.dtype)
                dk_tot = (dk_acc + pltpu.roll(dk_acc, _HALF, 1)) * _B_SCALE
                dv_tot = dv_acc + pltpu.roll(dv_acc, _HALF, 1)
                mine = lo if hk % 2 == 0 else jnp.logical_not(lo)
                dk_cols[hk // 2] = jnp.where(mine, dk_tot, dk_cols[hk // 2])
                dv_cols[hk // 2] = jnp.where(mine, dv_tot, dv_cols[hk // 2])
            dsink_ref[...] += dsink
            dk_cat = jnp.concatenate(dk_cols, axis=1)
            dv_cat = jnp.concatenate(dv_cols, axis=1)
            dk_ref[...] = (carry_k[...] + dk_cat[:BLOCK]).astype(dk_ref.dtype)
            dv_ref[...] = (carry_v[...] + dv_cat[:BLOCK]).astype(dv_ref.dtype)
            carry_k[...] = dk_cat[BLOCK:]
            carry_v[...] = dv_cat[BLOCK:]

        @pl.when(i == nb)
        def _():
            dk_ref[...] = carry_k[...].astype(dk_ref.dtype)
            dv_ref[...] = carry_v[...].astype(dv_ref.dtype)

    cur = lambda i: jnp.minimum(i, nb - 1)
    prev = lambda i: jnp.clip(i - 1, 0, nb - 1)
    kv_out = pl.BlockSpec((BLOCK, W_KB), lambda i: (prev(i), 0))
    return _pcall(
        body, (qkv, qkv, qkv, qkv, qkv, do, bias, sinks), jobs, name="swa_bwd", grid=(nb + 1,),
        in_specs=[pl.BlockSpec((BLOCK, W_QB), lambda i: (cur(i), qb)),
                  pl.BlockSpec((BLOCK, W_KB), lambda i: (prev(i), kb)),
                  pl.BlockSpec((BLOCK, W_KB), lambda i: (cur(i), kb)),
                  pl.BlockSpec((BLOCK, W_KB), lambda i: (prev(i), vb)),
                  pl.BlockSpec((BLOCK, W_KB), lambda i: (cur(i), vb)),
                  pl.BlockSpec((BLOCK, W_QB), lambda i: (cur(i), 0)),
                  pl.BlockSpec((H_B, BLOCK, 2 * BLOCK), lambda i: (0, 0, 0)),
                  pl.BlockSpec(memory_space=pltpu.SMEM)],
        out_specs=[pl.BlockSpec((BLOCK, W_QB), lambda i: (cur(i), 0)), kv_out, kv_out,
                   pl.BlockSpec((H_B, BLOCK, 2 * BLOCK), lambda i: (0, 0, 0)),
                   pl.BlockSpec((1, LANE), lambda i: (0, 0))],
        out_shape=[jax.ShapeDtypeStruct((s, W_QB), BF16), jax.ShapeDtypeStruct((s, W_KB), BF16),
                   jax.ShapeDtypeStruct((s, W_KB), BF16),
                   jax.ShapeDtypeStruct((H_B, BLOCK, 2 * BLOCK), F32), jax.ShapeDtypeStruct((1, LANE), F32)],
        scratch_shapes=[pltpu.VMEM((BLOCK, W_KB), F32), pltpu.VMEM((BLOCK, W_KB), F32)],
        sem=("arbitrary",))


_RELS = ((1, 0), (0, 1), (1, 1))


def _place():
    x, y, c = lax.axis_index("x"), lax.axis_index("y"), lax.axis_index("c")
    return x, y, c


def _rel_chip(x, y, rel):
    px = 1 - x if rel[0] else x
    py = 1 - y if rel[1] else y
    return px, py, 2 * px + py


def _half_rows(ref_shape, c):
    half = ref_shape[-2] // 2
    return pl.ds(pl.multiple_of(c * half, 16), half)


def _row_chunks(start, size):
    n = next(n for n in (8, 4, 2, 1) if size % (16 * n) == 0)
    step = size // n
    if isinstance(start, int):
        return [pl.ds(start + i * step, step) for i in range(n)]
    return [pl.ds(pl.multiple_of(start + i * step, 16), step) for i in range(n)]


def _dma_sems(*shape):
    return pltpu.SemaphoreType.DMA(shape)


def _gather_ici_job(srcs):
    nt = len(srcs)

    def copies(src, dst, sems):
        send, recv = sems
        x, y, c = _place()
        me_chip = 2 * x + y
        chunks, sends, recvs = [], [], []
        for t in range(nt):
            ns = src[t].shape[0]
            half = src[t].shape[-2] // 2
            for k, rel in enumerate(_RELS[:2]):
                px, py, chip = _rel_chip(x, y, rel)

                def mk(into, rows):
                    return pltpu.make_async_remote_copy(
                        src_ref=src[t].at[:, rows], dst_ref=dst[t].at[pl.ds(into * ns, ns), rows],
                        send_sem=send.at[t, k], recv_sem=recv.at[t, k], device_id=(px, py, c), device_id_type=MESH)
                chunks += [mk(me_chip, r) for r in _row_chunks(c * half, half)]
                sends.append(mk(me_chip, _half_rows(src[t].shape, c)))
                recvs.append(mk(chip, _half_rows(src[t].shape, c)))
        return chunks, sends, recvs

    def start(src, dst, sems):
        for cp in copies(src, dst, sems)[0]:
            cp.start()

    def wait(src, dst, sems):
        _, sends, recvs = copies(src, dst, sems)
        for cp in recvs:
            cp.wait_recv()
        for cp in sends:
            cp.wait_send()

    return _Job(srcs, [jax.ShapeDtypeStruct((N_CHIP * a.shape[0],) + a.shape[1:], a.dtype) for a in srcs],
                [_dma_sems(nt, 2), _dma_sems(nt, 2)], start, wait)


def _with_own(gathered, srcs):
    chip = 2 * lax.axis_index("x") + lax.axis_index("y")
    return [lax.dynamic_update_slice(g, s, (chip * s.shape[0], 0, 0)) for g, s in zip(gathered, srcs)]


def _gather_relay_job(gathered):
    nt = len(gathered)

    def copies(dst, sems):
        send, recv = sems
        x, y, c = _place()
        diag = 2 * (1 - x) + (1 - y)
        chunks, sends, recvs = [], [], []
        for t in range(nt):
            ns = dst[t].shape[0] // N_CHIP
            quarter = dst[t].shape[-2] // 4
            for k, rel in enumerate(_RELS[:2]):
                px, py, _ = _rel_chip(x, y, rel)
                _, _, origin = _rel_chip(x, y, _RELS[1 - k])

                def mk(chip, rows):
                    part = dst[t].at[pl.ds(chip * ns, ns), rows]
                    return pltpu.make_async_remote_copy(
                        src_ref=part, dst_ref=part, send_sem=send.at[t, k], recv_sem=recv.at[t, k],
                        device_id=(px, py, c), device_id_type=MESH)
                whole = pl.ds(pl.multiple_of((2 * c + k) * quarter, 16), quarter)
                chunks += [mk(origin, r) for r in _row_chunks((2 * c + k) * quarter, quarter)]
                sends.append(mk(origin, whole))
                recvs.append(mk(diag, whole))
        return chunks, sends, recvs

    def start(_, dst, sems):
        for cp in copies(dst, sems)[0]:
            cp.start()

    def wait(_, dst, sems):
        _, sends, recvs = copies(dst, sems)
        for cp in recvs:
            cp.wait_recv()
        for cp in sends:
            cp.wait_send()

    return _Job(gathered, [jax.ShapeDtypeStruct(a.shape, a.dtype) for a in gathered],
                [_dma_sems(nt, 2), _dma_sems(nt, 2)], start, wait, alias={t: t for t in range(nt)})


def _gather_d2d_job(gathered):
    nt = len(gathered)

    def copies(dst, sems):
        send, recv = sems
        x, y, c = _place()
        sends, recvs = [], []
        for t in range(nt):
            ns = dst[t].shape[0] // N_CHIP
            for k, rel in enumerate(_RELS):
                _, _, chip = _rel_chip(x, y, rel)
                for half, lst in ((c, sends), (1 - c, recvs)):
                    part = dst[t].at[pl.ds(chip * ns, ns), _half_rows(dst[t].shape, half)]
                    lst.append(pltpu.make_async_remote_copy(
                        src_ref=part, dst_ref=part, send_sem=send.at[t, k], recv_sem=recv.at[t, k],
                        device_id=(x, y, 1 - c), device_id_type=MESH))
        return sends, recvs

    def start(_, dst, sems):
        for cp in copies(dst, sems)[0]:
            cp.start()

    def wait(_, dst, sems):
        sends, recvs = copies(dst, sems)
        for cp in recvs:
            cp.wait_recv()
        for cp in sends:
            cp.wait_send()

    return _Job(gathered, [jax.ShapeDtypeStruct(a.shape, a.dtype) for a in gathered],
                [_dma_sems(nt, 3), _dma_sems(nt, 3)], start, wait, alias={t: t for t in range(nt)})


def _pair_job(grads):
    nt = len(grads)

    def copies(g, got, sems):
        send, recv = sems
        x, y, c = _place()
        return [pltpu.make_async_remote_copy(
            src_ref=g[t].at[:, _half_rows(g[t].shape, 1 - c)], dst_ref=got[t], send_sem=send.at[t],
            recv_sem=recv.at[t], device_id=(x, y, 1 - c), device_id_type=MESH) for t in range(nt)]

    def start(g, got, sems):
        for cp in copies(g, got, sems):
            cp.start()

    def wait(g, got, sems):
        for cp in copies(g, got, sems):
            cp.wait()

    return _Job(grads, [jax.ShapeDtypeStruct((a.shape[0], a.shape[1] // 2, a.shape[2]), a.dtype) for a in grads],
                [_dma_sems(nt), _dma_sems(nt)], start, wait)


def _ew_rows(rows, cols, itemsize=4):
    tr = 16
    while tr * 2 <= 1024 and rows % (tr * 2) == 0 and tr * 2 * cols * itemsize <= (1 << 22):
        tr *= 2
    assert rows % tr == 0, (rows, tr)
    return tr


def _pair_sum(name, grad, got, place):
    ns, rows, cols = grad.shape
    half = rows // 2
    tr = _ew_rows(half, cols)
    nh = half // tr

    def body(p_ref, g_ref, r_ref, o_ref):
        o_ref[...] = (g_ref[...].astype(F32) + r_ref[...].astype(F32)).astype(o_ref.dtype)

    return pl.pallas_call(
        body, name=name,
        grid_spec=pltpu.PrefetchScalarGridSpec(
            num_scalar_prefetch=1, grid=(ns, nh),
            in_specs=[pl.BlockSpec((None, tr, cols), lambda s, i, p: (s, p[1] * nh + i, 0)),
                      pl.BlockSpec((None, tr, cols), lambda s, i, p: (s, i, 0))],
            out_specs=pl.BlockSpec((None, tr, cols), lambda s, i, p: (s, i, 0))),
        out_shape=jax.ShapeDtypeStruct((ns, half, cols), BF16),
        compiler_params=_cparams(("parallel", "parallel")),
    )(place, grad, got)


def _chip_job(psums):
    nt = len(psums)

    def copies(p, got, sems):
        send, recv = sems
        x, y, c = _place()
        chunks, whole = [], []
        for t in range(nt):
            ns = p[t].shape[0] // N_CHIP
            for k, rel in enumerate(_RELS):
                px, py, chip = _rel_chip(x, y, rel)

                def mk(rows):
                    return pltpu.make_async_remote_copy(
                        src_ref=p[t].at[pl.ds(chip * ns, ns), rows], dst_ref=got[t].at[k, :, rows],
                        send_sem=send.at[t, k], recv_sem=recv.at[t, k], device_id=(px, py, c), device_id_type=MESH)
                chunks += [mk(r) for r in _row_chunks(0, p[t].shape[1])]
                whole.append(mk(pl.ds(0, p[t].shape[1])))
        return chunks, whole

    def start(p, got, sems):
        for cp in copies(p, got, sems)[0]:
            cp.start()

    def wait(p, got, sems):
        for cp in copies(p, got, sems)[1]:
            cp.wait()

    return _Job(psums, [jax.ShapeDtypeStruct((3, a.shape[0] // N_CHIP) + a.shape[1:], a.dtype) for a in psums],
                [_dma_sems(nt, 3), _dma_sems(nt, 3)], start, wait)


def _chip_sum(name, psum, got, place):
    ns4, half, cols = psum.shape
    ns = ns4 // N_CHIP
    tr = _ew_rows(half, cols)
    nh = half // tr

    def body(p_ref, mine_ref, got_ref, o_ref):
        acc = mine_ref[...].astype(F32)
        for k in range(3):
            acc = acc + got_ref[k].astype(F32)
        o_ref[...] = acc

    return pl.pallas_call(
        body, name=name,
        grid_spec=pltpu.PrefetchScalarGridSpec(
            num_scalar_prefetch=1, grid=(ns, nh),
            in_specs=[pl.BlockSpec((None, tr, cols), lambda s, i, p: (p[0] * ns + s, i, 0)),
                      pl.BlockSpec((3, None, tr, cols), lambda s, i, p: (0, s, i, 0))],
            out_specs=pl.BlockSpec((None, tr, cols), lambda s, i, p: (s, p[1] * nh + i, 0))),
        out_shape=jax.ShapeDtypeStruct((ns, 2 * half, cols), F32),
        compiler_params=_cparams(("parallel", "parallel")),
    )(place, psum, got)


def _share_job(halves):
    nt = len(halves)

    def copies(full, sems):
        send, recv = sems
        x, y, c = _place()
        sends, recvs = [], []
        for t in range(nt):
            for half, lst in ((c, sends), (1 - c, recvs)):
                part = full[t].at[:, _half_rows(full[t].shape, half)]
                lst.append(pltpu.make_async_remote_copy(
                    src_ref=part, dst_ref=part, send_sem=send.at[t], recv_sem=recv.at[t],
                    device_id=(x, y, 1 - c), device_id_type=MESH))
        return sends, recvs

    def start(_, full, sems):
        for cp in copies(full, sems)[0]:
            cp.start()

    def wait(_, full, sems):
        sends, recvs = copies(full, sems)
        for cp in sends:
            cp.wait_send()
        for cp in recvs:
            cp.wait_recv()

    return _Job(halves, [jax.ShapeDtypeStruct(a.shape, a.dtype) for a in halves],
                [_dma_sems(nt), _dma_sems(nt)], start, wait, alias={t: t for t in range(nt)})


def _small_all_reduce(pack):
    rows, d = pack.shape

    def body(x_ref, o_ref, land, send, recv):
        x, y, c = _place()
        me = 4 * x + 2 * y + c
        land[me] = x_ref[...]
        cps = []
        for k in range(1, 8):
            to = (1 - x if k & 4 else x, 1 - y if k & 2 else y, 1 - c if k & 1 else c)
            cps.append(pltpu.make_async_remote_copy(
                src_ref=x_ref, dst_ref=land.at[me], send_sem=send.at[k - 1], recv_sem=recv.at[k - 1],
                device_id=to, device_id_type=MESH))
        for cp in cps:
            cp.start()
        for cp in cps:
            cp.wait()
        acc = land[0]
        for dev in range(1, 8):
            acc = acc + land[dev]
        o_ref[...] = acc

    vm = pl.BlockSpec(memory_space=pltpu.VMEM)
    return pl.pallas_call(
        body, name="small_all_reduce", in_specs=[vm], out_specs=vm,
        out_shape=jax.ShapeDtypeStruct((rows, d), F32),
        scratch_shapes=[pltpu.VMEM((8, rows, d), F32), pltpu.SemaphoreType.DMA((7,)), pltpu.SemaphoreType.DMA((7,))],
    )(pack)


def _adamw(name, w, g, m, v, jobs=()):
    lead, rows, cols = w.shape
    fits = [t for t in range(8, rows + 1, 8) if rows % t == 0 and t * cols * 4 <= (1 << 20)]
    tr = fits[-1] if fits else rows
    tl = 1
    if tr == rows:
        tl = max(t for t in range(1, lead + 1) if lead % t == 0 and t * rows * cols * 4 <= (1 << 20))

    def body(w_ref, g_ref, m_ref, v_ref, d_ref, nm_ref, nv_ref):
        gv = g_ref[...]
        nm = ADAM_B1 * m_ref[...] + (1.0 - ADAM_B1) * gv
        nv = ADAM_B2 * v_ref[...] + (1.0 - ADAM_B2) * (gv * gv)
        m_hat = nm / (1.0 - ADAM_B1 ** ADAM_STEP)
        v_hat = nv / (1.0 - ADAM_B2 ** ADAM_STEP)
        d_ref[...] = -ADAM_LR * (m_hat / (jnp.sqrt(v_hat) + ADAM_EPS) + ADAM_WD * w_ref[...])
        nm_ref[...] = nm
        nv_ref[...] = nv

    blk = pl.BlockSpec((tl, tr, cols), lambda l, i: (l, i, 0))
    out = jax.ShapeDtypeStruct(w.shape, F32)
    return _pcall(
        body, (w, g, m, v), jobs, name=name, grid=(lead // tl, rows // tr), in_specs=[blk] * 4, out_specs=[blk] * 3,
        out_shape=[out, out, out], sem=("parallel", "parallel"))


def _pad_to(a, shape):
    return jnp.pad(a, [(0, t - s) for s, t in zip(a.shape, shape)])


def _pack_small(n1, n2, fg, bf, sk, rb, extra=None):
    rows = [n1.reshape(1, D_MODEL), n2.reshape(1, D_MODEL), fg.reshape(1, D_MODEL),
            _pad_to(bf.reshape(1, H_A), (1, D_MODEL)), _pad_to(sk.reshape(1, H_B), (1, D_MODEL)),
            jnp.zeros((1, D_MODEL), F32) if extra is None else _pad_to(extra.reshape(1, 1), (1, D_MODEL)),
            _pad_to(rb.reshape(1, NUM_BUCKETS * H_B), (1, RB_ROWS * D_MODEL)).reshape(RB_ROWS, D_MODEL)]
    return _pad_to(jnp.concatenate(rows, axis=0), (PACK_ROWS, D_MODEL))


def _unpack_small(p):
    return (p[0:1], p[1:2], p[2], p[3:4, :H_A], p[4:5, :H_B],
            p[6:6 + RB_ROWS].reshape(-1)[:NUM_BUCKETS * H_B].reshape(NUM_BUCKETS, H_B))


def kernel(x, norm1_g, w_in, b_forget, attn_sinks, rel_bias, w_branch_a, w_branch_b, w_out, norm2_g, w_ffn_gate, w_ffn_up, w_ffn_down, final_g, loss_target, m_norm1_g, m_w_in, m_b_forget, m_attn_sinks, m_rel_bias, m_w_branch_a, m_w_branch_b, m_w_out, m_norm2_g, m_w_ffn_gate, m_w_ffn_up, m_w_ffn_down, m_final_g, v_norm1_g, v_w_in, v_b_forget, v_attn_sinks, v_rel_bias, v_w_branch_a, v_w_branch_b, v_w_out, v_norm2_g, v_w_ffn_gate, v_w_ffn_up, v_w_ffn_down, v_final_g):
    s, d = SEQ, D_MODEL
    assert x.shape == (1, s, d) and w_in.shape == (1, d, W_IN_SH)
    xs = x[0]
    place = jnp.stack([2 * lax.axis_index("x") + lax.axis_index("y"), lax.axis_index("c")]).astype(jnp.int32)

    w_gu_l = jnp.stack([_pad_to(w_ffn_gate[0], (d, FF_P)), _pad_to(w_ffn_up[0], (d, FF_P))]).astype(BF16)
    w_dn_l = _pad_to(w_ffn_down, (1, FF_P, d)).astype(BF16)
    w_in_l, w_abo_l = w_in.astype(BF16), [w_branch_a.astype(BF16), w_branch_b.astype(BF16), w_out.astype(BF16)]
    j_in = _gather_ici_job([w_in_l])
    _comm_now("gather_w_in_ici", [j_in])
    j_in1 = _gather_relay_job(j_in.out)
    _comm_now("gather_w_in_relay", [j_in1])
    j_in2 = _gather_d2d_job(j_in1.out)
    _comm_now("gather_w_in_d2d", [j_in2])
    w_in4 = _with_own(j_in2.out, [w_in_l])[0]
    j_abo = _gather_ici_job(w_abo_l)
    j_dn = _gather_ici_job([w_dn_l])
    j_gu = _gather_ici_job([w_gu_l])
    w_full = jnp.concatenate([w_in4[j] for j in range(N_CHIP)], axis=1)
    n_qkv_a = 3 * W_A
    w_p = jnp.concatenate([w_full[:, :n_qkv_a], w_full[:, n_qkv_a + H_A:], w_full[:, n_qkv_a:n_qkv_a + H_A],
                           jnp.zeros((d, LANE - H_A), BF16)], axis=1)

    h1 = _rms_fwd("norm1_fwd", xs, norm1_g)
    qkv = _matmul("proj_qkv", h1, w_p, m=s, n=W_QKV, k=d, out_shape=(s, W_QKV), out_dtype=BF16, jobs=[j_abo])
    j_abo1 = _gather_relay_job(j_abo.out)
    proj_g = _matmul("proj_gates", h1, w_p, m=s, n=2 * d, k=d, b_noff=OFF_GA, out_shape=(s, 2 * d), out_dtype=F32,
                     jobs=[j_abo1])
    fa = _matmul("proj_forget", h1, w_p, m=s, n=LANE, k=d, b_noff=OFF_FA, tn=LANE, out_shape=(s, LANE), out_dtype=F32)
    b_f = _pad_to(b_forget, (1, LANE))
    c_cum = _forget_fwd(fa, b_f)
    c_t = c_cum[:, :H_A].T
    c_col = jnp.broadcast_to(c_t[:, :, None], (H_A, s, LANE))
    c_row = c_t[:, None, :]
    j_abo2 = _gather_d2d_job(j_abo1.out)
    attn_a, lse_col = _fox_fwd(qkv, c_col, c_row, jobs=[j_abo2, j_gu])
    w_a, w_b, w_o = _with_own(j_abo2.out, w_abo_l)
    w_o = w_o.reshape(d, d)

    bucket = jnp.asarray(_t5_bucket_table())
    bias = _bias_table(rel_bias.T, bucket).reshape(H_B, BLOCK, 2 * BLOCK)
    j_gu1 = _gather_relay_job(j_gu.out)
    attn_b = _swa_fwd(qkv, bias, attn_sinks, jobs=[j_gu1])

    j_gu2 = _gather_d2d_job(j_gu1.out)
    ya = _matmul("branch_a", attn_a, w_a, m=s, n=d, k=W_A, b_kind="col", tn=_tile(A_SH, TN),
                 out_shape=(s, d), out_dtype=F32, jobs=[j_gu2])
    w_gu = _with_own(j_gu2.out, [w_gu_l])[0]
    yb = _matmul("branch_b", attn_b, w_b, m=s, n=d, k=W_QB, b_kind="col", tn=_tile(A_SH, TN),
                 out_shape=(s, d), out_dtype=F32)
    mixed = _gate_fwd(ya, yb, proj_g)
    x1 = _matmul("out_proj", mixed, w_o, m=s, n=d, k=d, res=xs, out_shape=(s, d), out_dtype=F32)

    h2 = _rms_fwd("norm2_fwd", x1, norm2_g)
    gu = _matmul("ffn_gate_up", h2, w_gu, m=s, n=2 * FP, k=d, b_kind="col", tn=_tile(FF_P, 1408),
                 out_shape=(s, 2 * FP), out_dtype=BF16, jobs=[j_dn])
    j_dn1 = _gather_relay_job(j_dn.out)
    hidden = _swiglu_fwd(gu, jobs=[j_dn1])
    j_dn2 = _gather_d2d_job(j_dn1.out)
    _comm_now("gather_w_down_d2d", [j_dn2])
    w_dn = _with_own(j_dn2.out, [w_dn_l])[0].reshape(FP, d)
    x2 = _matmul("ffn_down", hidden, w_dn, m=s, n=d, k=FP, tk=_tile(FP, 5632), res=x1, out_shape=(s, d), out_dtype=F32)

    dx2, dx2_b, d_fg, loss_tile = _loss_head(x2, loss_target[0], final_g.reshape(1, d))
    dhidden = _matmul("ffn_down_dx", dx2_b, w_dn, m=s, n=FP, k=d, nt=True, tn=_tile(FF_P, 1408),
                      out_shape=(s, FP), out_dtype=BF16)
    dgu = _swiglu_bwd(dhidden, gu)
    g_dn = _matmul("ffn_down_dw", hidden.T, dx2_b, m=FP, n=d, k=s, tm=_tile(FF_P, 1408, 16),
                   out_shape=(FP, d), out_dtype=BF16).reshape(N_CHIP, FF_P, d)
    j_p_dn = _pair_job([g_dn])
    g_gu = _matmul("ffn_gate_up_dw", h2.T, dgu, m=d, n=2 * FP, k=s, o_kind="col", tm=_tile(d, 512, 16),
                   tn=_tile(FF_P, 1408), out_shape=(2 * N_CHIP, d, FF_P), out_dtype=BF16, jobs=[j_p_dn])
    ps_dn = _pair_sum("pair_sum_w_ffn_down", g_dn, j_p_dn.out[0], place)
    j_c_dn = _chip_job([ps_dn])
    j_p_gu = _pair_job([g_gu])
    dh2 = _matmul("ffn_gate_up_dx", dgu, w_gu, m=s, n=d, k=2 * FP, nt=True, b_kind="col", tn=_tile(d, 1024),
                  tk=_tile(FF_P, TK), out_shape=(s, d), out_dtype=F32, jobs=[j_c_dn, j_p_gu])
    h_dn = _chip_sum("chip_sum_w_ffn_down", ps_dn, j_c_dn.out[0], place)
    ps_gu = _pair_sum("pair_sum_w_ffn_gate_up", g_gu, j_p_gu.out[0], place)
    dx1, dx1_b, d_n2 = _rms_bwd("norm2_bwd", dh2, x1, norm2_g, dx2)

    dmixed = _matmul("out_proj_dx", dx1_b, w_o, m=s, n=d, k=d, nt=True, out_shape=(s, d), out_dtype=BF16)
    dya, dyb, dga, dgb = _gate_bwd(dmixed, ya, yb, proj_g)
    g_a = _matmul("branch_a_dw", attn_a.T, dya, m=W_A, n=d, k=s, o_kind="col", tn=_tile(A_SH, TN),
                  out_shape=(N_CHIP, W_A, A_SH), out_dtype=BF16)
    g_b = _matmul("branch_b_dw", attn_b.T, dyb, m=W_QB, n=d, k=s, o_kind="col", tn=_tile(A_SH, TN),
                  out_shape=(N_CHIP, W_QB, A_SH), out_dtype=BF16)
    j_p_ab = _pair_job([g_a, g_b])
    dattn_a = _matmul("branch_a_dx", dya, w_a, m=s, n=W_A, k=d, nt=True, b_kind="col", tn=_tile(W_A, 1024),
                      tk=_tile(A_SH, TK), out_shape=(s, W_A), out_dtype=BF16, jobs=[j_p_ab])
    dattn_b = _matmul("branch_b_dx", dyb, w_b, m=s, n=W_QB, k=d, nt=True, b_kind="col", tn=_tile(W_QB, 1024),
                      tk=_tile(A_SH, TK), out_shape=(s, W_QB), out_dtype=BF16)
    ps_a, ps_b = (_pair_sum("pair_sum_" + n, g, r, place) for n, g, r in
                  zip(("w_branch_a", "w_branch_b"), (g_a, g_b), j_p_ab.out))

    j_c_gu = _chip_job([ps_gu])
    dq_a, delta_col = _fox_dq(qkv, dattn_a, c_col, c_row, lse_col, jobs=[j_c_gu])
    lse_row = lse_col[:, :, 0][:, None, :]
    delta_row = delta_col[:, :, 0][:, None, :]
    j_c_ab = _chip_job([ps_a, ps_b])
    j_s_dn = _share_job([h_dn])
    dk_a, dv_a, dc_col = _fox_dkv(qkv, dattn_a, c_col, c_row, lse_row, delta_row, jobs=[j_c_ab, j_s_dn])
    dc = _pad_to(dc_col[:, :, 0].T, (s, LANE))
    df, d_bf = _forget_bwd(dc, fa, b_f)
    h_gu = _chip_sum("chip_sum_w_ffn_gate_up", ps_gu, j_c_gu.out[0], place)
    h_a, h_b = (_chip_sum("chip_sum_" + n, p, r, place) for n, p, r in
                zip(("w_branch_a", "w_branch_b"), (ps_a, ps_b), j_c_ab.out))

    j_s_rest = _share_job([h_gu, h_a, h_b])
    dq_b, dk_b, dv_b, dbias, d_sk = _swa_bwd(qkv, dattn_b, bias, attn_sinks, jobs=[j_s_rest])
    r_dn = j_s_dn.out[0]
    r_gu, r_a, r_b = j_s_rest.out
    d_rb = _bias_table_bwd(dbias.reshape(H_B, BLOCK * 2 * BLOCK), bucket).T

    dproj = jnp.concatenate([dq_a, dk_a, dv_a, dq_b, dk_b, dv_b, dga, dgb, df], axis=1)
    g_in_p = _matmul("in_proj_dw", h1.T, dproj, m=d, n=PW, k=s, tn=_tile(PW, 1024), out_shape=(d, PW), out_dtype=BF16)
    g_in_full = jnp.concatenate([g_in_p[:, :n_qkv_a], g_in_p[:, OFF_FA:OFF_FA + H_A], g_in_p[:, n_qkv_a:OFF_FA]], axis=1)
    g_in = jnp.stack([g_in_full[:, j * W_IN_SH:(j + 1) * W_IN_SH] for j in range(N_CHIP)])
    j_p_in = _pair_job([g_in])
    g_o = _matmul("out_proj_dw", mixed.T, dx1_b, m=d, n=d, k=s, out_shape=(d, d), out_dtype=BF16,
                  jobs=[j_p_in]).reshape(N_CHIP, A_SH, d)
    ps_in = _pair_sum("pair_sum_w_in", g_in, j_p_in.out[0], place)
    j_c_in = _chip_job([ps_in])
    j_p_o = _pair_job([g_o])
    dh1 = _matmul("in_proj_dx", dproj, w_p, m=s, n=d, k=PW, nt=True, tn=_tile(d, 1024), tk=_tile(PW, 2560),
                  out_shape=(s, d), out_dtype=F32, jobs=[j_c_in, j_p_o])
    h_in = _chip_sum("chip_sum_w_in", ps_in, j_c_in.out[0], place)
    ps_o = _pair_sum("pair_sum_w_out", g_o, j_p_o.out[0], place)
    j_c_o = _chip_job([ps_o])
    j_s_in = _share_job([h_in])
    grad_x, _, d_n1 = _rms_bwd("norm1_bwd", dh1, xs, norm1_g, dx1, jobs=[j_c_o, j_s_in])
    h_o = _chip_sum("chip_sum_w_out", ps_o, j_c_o.out[0], place)
    j_s_o = _share_job([h_o])

    small = _small_all_reduce(_pack_small(d_n1, d_n2, d_fg, d_bf[:, :H_A], d_sk[:, :H_B], d_rb, loss_tile[0:1, 0:1]))
    loss = small[5, 0]

    grads = {
        "w_branch_a": r_a, "w_branch_b": r_b,
        "w_ffn_gate": r_gu[0:1, :, :FF_SH], "w_ffn_up": r_gu[1:2, :, :FF_SH], "w_ffn_down": r_dn[:, :FF_SH, :],
    }
    given = dict(w_in=(w_in, m_w_in, v_w_in), w_branch_a=(w_branch_a, m_w_branch_a, v_w_branch_a),
                 w_branch_b=(w_branch_b, m_w_branch_b, v_w_branch_b), w_out=(w_out, m_w_out, v_w_out),
                 w_ffn_gate=(w_ffn_gate, m_w_ffn_gate, v_w_ffn_gate), w_ffn_up=(w_ffn_up, m_w_ffn_up, v_w_ffn_up),
                 w_ffn_down=(w_ffn_down, m_w_ffn_down, v_w_ffn_down))

    def col_major(a):
        return jnp.transpose(a[0])[None]

    views = {n: (lambda a: a, lambda a: a) for n in given}
    views["w_ffn_gate"] = views["w_ffn_up"] = (col_major, col_major)
    views["w_in"] = (lambda a: jnp.transpose(a, (2, 0, 1)).reshape(W_IN_SH, d // LANE, LANE),
                     lambda a: jnp.transpose(a.reshape(W_IN_SH, 1, d), (1, 2, 0)))

    def adamw(n, jobs=()):
        to_view, from_view = views[n]
        g_view = to_view(grads[n])
        outs = _adamw("adamw_" + n, to_view(given[n][0]), g_view, to_view(given[n][1]), to_view(given[n][2]), jobs=jobs)
        grads[n] = from_view(g_view)
        return [from_view(o) for o in outs]

    grads["w_in"] = j_s_in.out[0]
    upd = {n: adamw(n) for n in ("w_ffn_gate", "w_ffn_up", "w_ffn_down", "w_branch_a", "w_branch_b", "w_in")}
    sm = _adamw("adamw_small",
                _pack_small(norm1_g, norm2_g, final_g, b_forget, attn_sinks, rel_bias)[None],
                small.at[5].set(0.0)[None],
                _pack_small(m_norm1_g, m_norm2_g, m_final_g, m_b_forget, m_attn_sinks, m_rel_bias)[None],
                _pack_small(v_norm1_g, v_norm2_g, v_final_g, v_b_forget, v_attn_sinks, v_rel_bias)[None],
                jobs=[j_s_o])
    grads["w_out"] = j_s_o.out[0]
    upd["w_out"] = adamw("w_out")
    g_small = _unpack_small(small)
    d_small, m_small, v_small = (_unpack_small(t[0]) for t in sm)

    order = ["norm1_g", "w_in", "b_forget", "attn_sinks", "rel_bias", "w_branch_a", "w_branch_b", "w_out",
             "norm2_g", "w_ffn_gate", "w_ffn_up", "w_ffn_down", "final_g"]
    small_at = {"norm1_g": 0, "norm2_g": 1, "final_g": 2, "b_forget": 3, "attn_sinks": 4, "rel_bias": 5}

    def pick(big_idx, small_src, n):
        return small_src[small_at[n]] if n in small_at else (grads[n] if big_idx is None else upd[n][big_idx])

    return (loss, grad_x[None],
            *[pick(None, g_small, n) for n in order], *[pick(0, d_small, n) for n in order],
            *[pick(1, m_small, n) for n in order], *[pick(2, v_small, n) for n in order])
```

```python
import functools
import math

import numpy as np
import jax
import jax.numpy as jnp
from jax import lax
from jax.experimental import pallas as pl
from jax.experimental.pallas import tpu as pltpu

F32 = jnp.float32
BF16 = jnp.bfloat16
MESH = pl.DeviceIdType.MESH

D_MODEL = 4096
SEQ = 4096
H_A = 16
DH_A = 128
H_B = 32
HKV_B = 4
G_B = H_B // HKV_B
DH_B = 64
WINDOW = 128
NUM_BUCKETS = 32
MAX_DISTANCE = 128
BLOCK = 128
D_FF = ((8 * D_MODEL // 3 + 255) // 256) * 256
EPS = 1e-6
ADAM_LR = 0.001
ADAM_B1 = 0.9
ADAM_B2 = 0.999
ADAM_EPS = 1e-08
ADAM_WD = 0.01
ADAM_STEP = 10

N_CHIP = 4
LANE = 128
VMEM_LIMIT = 56 * 1024 * 1024

TM = 1024
TN = 512
TK = 4096
TQ_A = 1024
TR_EW = 256

W_A = H_A * DH_A
W_QB = H_B * DH_B
W_KB = HKV_B * DH_B
OFF_QA = 0
OFF_KA = W_A
OFF_VA = 2 * W_A
OFF_QB = 3 * W_A
OFF_KB = OFF_QB + W_QB
OFF_VB = OFF_KB + W_KB
OFF_GA = OFF_VB + W_KB
OFF_GB = OFF_GA + D_MODEL
OFF_FA = OFF_GB + D_MODEL
PW = OFF_FA + LANE
W_QKV = OFF_GA
W_IN = 3 * W_A + H_A + W_QB + 2 * W_KB + 2 * D_MODEL
W_IN_SH = W_IN // N_CHIP
A_SH = D_MODEL // N_CHIP
FF_SH = D_FF // N_CHIP
FF_P = -(-FF_SH // LANE) * LANE
FP = N_CHIP * FF_P
PACK_ROWS = 16
RB_ROWS = -(-(NUM_BUCKETS * H_B) // D_MODEL)


def _tile(n, target, mult=LANE):
    t = min(target, n) // mult * mult
    while t > mult and n % t:
        t -= mult
    assert t > 0 and n % t == 0, (n, target, mult)
    return t


def _cparams(sem):
    return pltpu.CompilerParams(dimension_semantics=sem, vmem_limit_bytes=VMEM_LIMIT)


def _sigmoid(x):
    return 1.0 / (1.0 + jnp.exp(-x))


def _dot(a, b):
    return jnp.dot(a, b, preferred_element_type=F32)


def _dot_nt(a, b):
    return lax.dot_general(a, b, (((1,), (1,)), ((), ())), preferred_element_type=F32)


ANY = pl.BlockSpec(memory_space=pl.ANY)


class _Job:
    def __init__(self, ins, outs, sems, start, wait, alias=None):
        self.ins, self.outs, self.sems = list(ins), list(outs), list(sems)
        self.start, self.wait, self.alias = start, wait, dict(alias or {})
        self.out = None


def _split(seq, sizes):
    parts, p = [], 0
    for n in sizes:
        parts.append(seq[p:p + n])
        p += n
    return parts


def _pcall(body, args, jobs, *, name, grid, in_specs, out_specs, out_shape, scratch_shapes=(), sem):
    if not jobs:
        return pl.pallas_call(body, name=name, grid=grid, in_specs=in_specs, out_specs=out_specs, out_shape=out_shape,
                              scratch_shapes=list(scratch_shapes), compiler_params=_cparams(sem))(*args)
    single = not isinstance(out_shape, (list, tuple))
    out_specs_l = [out_specs] if single else list(out_specs)
    out_shape_l = [out_shape] if single else list(out_shape)
    cin = [a for j in jobs for a in j.ins]
    cout = [o for j in jobs for o in j.outs]
    csem = [s for j in jobs for s in j.sems]
    sizes = [len(args), len(cin), len(out_shape_l), len(cout), len(scratch_shapes), len(csem)]
    aliases, io, oo = {}, len(args), len(out_shape_l)
    for j in jobs:
        for a, b in j.alias.items():
            aliases[io + a] = oo + b
        io, oo = io + len(j.ins), oo + len(j.outs)

    def wrapped(*refs):
        ins, cins, outs, couts, scr, sems = _split(refs, sizes)
        ids = [pl.program_id(a) for a in range(len(grid))]
        first = functools.reduce(jnp.logical_and, [i == 0 for i in ids])
        last = functools.reduce(jnp.logical_and, [i == g - 1 for i, g in zip(ids, grid)])
        per_job = list(zip(jobs, _split(cins, [len(j.ins) for j in jobs]), _split(couts, [len(j.outs) for j in jobs]),
                           _split(sems, [len(j.sems) for j in jobs])))

        @pl.when(first)
        def _():
            for j, ji, jo, js in per_job:
                j.start(ji, jo, js)

        body(*ins, *outs, *scr)

        @pl.when(last)
        def _():
            for j, ji, jo, js in per_job:
                j.wait(ji, jo, js)

    res = pl.pallas_call(
        wrapped, name=name, grid=grid, in_specs=list(in_specs) + [ANY] * len(cin),
        out_specs=out_specs_l + [ANY] * len(cout), out_shape=out_shape_l + cout,
        scratch_shapes=list(scratch_shapes) + csem, input_output_aliases=aliases,
        compiler_params=_cparams(("arbitrary",) * len(grid)))(*args, *cin)
    main, rest = res[:len(out_shape_l)], res[len(out_shape_l):]
    for j, o in zip(jobs, _split(rest, [len(j.outs) for j in jobs])):
        j.out = list(o)
    return main[0] if single else list(main)


def _comm_now(name, jobs):
    cin = [a for j in jobs for a in j.ins]
    cout = [o for j in jobs for o in j.outs]
    csem = [s for j in jobs for s in j.sems]
    sizes = [len(cin), len(cout), len(csem)]
    aliases, io, oo = {}, 0, 0
    for j in jobs:
        for a, b in j.alias.items():
            aliases[io + a] = oo + b
        io, oo = io + len(j.ins), oo + len(j.outs)

    def body(*refs):
        cins, couts, sems = _split(refs, sizes)
        per_job = list(zip(jobs, _split(cins, [len(j.ins) for j in jobs]), _split(couts, [len(j.outs) for j in jobs]),
                           _split(sems, [len(j.sems) for j in jobs])))
        for j, ji, jo, js in per_job:
            j.start(ji, jo, js)
        for j, ji, jo, js in per_job:
            j.wait(ji, jo, js)

    res = pl.pallas_call(body, name=name, in_specs=[ANY] * len(cin), out_specs=[ANY] * len(cout), out_shape=cout,
                         scratch_shapes=csem, input_output_aliases=aliases)(*cin)
    for j, o in zip(jobs, _split(res, [len(j.outs) for j in jobs])):
        j.out = list(o)


def _slot_spec(shape, kind, br, bc, rc):
    if kind == "2d":
        return pl.BlockSpec((br, bc), lambda *g: rc(*g))
    if kind == "col":
        assert shape[2] % bc == 0, (shape, bc)
        per = shape[2] // bc

        def im_col(*g):
            rb, cb = rc(*g)
            return (cb // per, rb, cb % per)
        return pl.BlockSpec((None, br, bc), im_col)
    assert kind == "row" and shape[1] % br == 0, (shape, kind, br)
    per = shape[1] // br

    def im_row(*g):
        rb, cb = rc(*g)
        return (rb // per, rb % per, cb)
    return pl.BlockSpec((None, br, bc), im_row)


def _matmul(name, a, b, *, m, n, k, nt=False, b_kind="2d", o_kind="2d", out_shape, out_dtype,
            tm=None, tn=None, tk=None, b_noff=0, res=None, b_outer=False, jobs=()):
    tm = tm or _tile(m, TM, 16)
    tn = tn or _tile(n, TN)
    tk = tk or _tile(k, TK)
    assert m % tm == 0 and n % tn == 0 and k % tk == 0 and b_noff % tn == 0
    nk = k // tk
    noff = b_noff // tn
    if b_outer:
        grid = (n // tn, m // tm, nk)
        ij = lambda g0, g1: (g1, g0)
    else:
        grid = (m // tm, n // tn, nk)
        ij = lambda g0, g1: (g0, g1)

    a_spec = pl.BlockSpec((tm, tk), lambda g0, g1, kk: (ij(g0, g1)[0], kk))
    if nt:
        b_spec = _slot_spec(b.shape, b_kind, tn, tk, lambda g0, g1, kk: (ij(g0, g1)[1] + noff, kk))
    else:
        b_spec = _slot_spec(b.shape, b_kind, tk, tn, lambda g0, g1, kk: (kk, ij(g0, g1)[1] + noff))
    o_spec = _slot_spec(out_shape, o_kind, tm, tn, lambda g0, g1, kk: ij(g0, g1))
    in_specs = [a_spec, b_spec]
    args = [a, b]
    if res is not None:
        in_specs.append(pl.BlockSpec((tm, tn), lambda g0, g1, kk: ij(g0, g1)))
        args.append(res)

    def body(*refs):
        a_ref, b_ref = refs[0], refs[1]
        r_ref = refs[2] if res is not None else None
        o_ref = refs[3] if res is not None else refs[2]

        def prod():
            return _dot_nt(a_ref[...], b_ref[...]) if nt else _dot(a_ref[...], b_ref[...])

        def finish(acc):
            if r_ref is not None:
                acc = acc + r_ref[...].astype(F32)
            o_ref[...] = acc.astype(o_ref.dtype)

        if nk == 1:
            finish(prod())
            return
        acc_ref = refs[-1]
        kk = pl.program_id(2)

        @pl.when(kk == 0)
        def _():
            acc_ref[...] = prod()

        @pl.when(jnp.logical_and(kk > 0, kk < nk - 1))
        def _():
            acc_ref[...] += prod()

        @pl.when(kk == nk - 1)
        def _():
            finish(acc_ref[...] + prod())

    return _pcall(
        body, args, jobs, name=name, grid=grid, in_specs=in_specs, out_specs=o_spec,
        out_shape=jax.ShapeDtypeStruct(out_shape, out_dtype),
        scratch_shapes=[pltpu.VMEM((tm, tn), F32)] if nk > 1 else [],
        sem=("parallel", "parallel", "arbitrary"))


def _rms_fwd(name, x, g):
    s, d = x.shape
    tr = _tile(s, TR_EW, 16)

    def body(x_ref, g_ref, h_ref):
        xf = x_ref[...]
        rstd = lax.rsqrt(jnp.mean(xf * xf, axis=-1, keepdims=True) + EPS)
        h_ref[...] = (xf * rstd * g_ref[...]).astype(h_ref.dtype)

    return pl.pallas_call(
        body, name=name, grid=(s // tr,),
        in_specs=[pl.BlockSpec((tr, d), lambda i: (i, 0)), pl.BlockSpec((1, d), lambda i: (0, 0))],
        out_specs=pl.BlockSpec((tr, d), lambda i: (i, 0)),
        out_shape=jax.ShapeDtypeStruct((s, d), BF16),
        compiler_params=_cparams(("parallel",)),
    )(x, g)


def _rms_bwd_rows(dh, xf, g):
    rstd = lax.rsqrt(jnp.mean(xf * xf, axis=-1, keepdims=True) + EPS)
    xhat = xf * rstd
    dxhat = dh * g
    dx = rstd * (dxhat - xhat * jnp.mean(dxhat * xhat, axis=-1, keepdims=True))
    return dx, dh * xhat


def _fold8(v):
    return jnp.sum(v.reshape(v.shape[0] // 8, 8, v.shape[1]), axis=0)


def _rms_bwd(name, dh, x, g, dres, jobs=()):
    s, d = x.shape
    tr = _tile(s, TR_EW, 16)
    n = s // tr

    def body(dh_ref, x_ref, g_ref, r_ref, dx_ref, dxb_ref, dg_ref, acc_ref):
        i = pl.program_id(0)
        dx, dgrows = _rms_bwd_rows(dh_ref[...].astype(F32), x_ref[...], g_ref[...])
        dx = dx + r_ref[...]
        dx_ref[...] = dx
        dxb_ref[...] = dx.astype(BF16)

        @pl.when(i == 0)
        def _():
            acc_ref[...] = jnp.zeros_like(acc_ref)
        acc_ref[...] += _fold8(dgrows)

        @pl.when(i == n - 1)
        def _():
            dg_ref[...] = jnp.sum(acc_ref[...], axis=0, keepdims=True)

    row = pl.BlockSpec((tr, d), lambda i: (i, 0))
    vec = pl.BlockSpec((1, d), lambda i: (0, 0))
    return _pcall(
        body, (dh, x, g, dres), jobs, name=name, grid=(n,), in_specs=[row, row, vec, row], out_specs=[row, row, vec],
        out_shape=[jax.ShapeDtypeStruct((s, d), F32), jax.ShapeDtypeStruct((s, d), BF16),
                   jax.ShapeDtypeStruct((1, d), F32)],
        scratch_shapes=[pltpu.VMEM((8, d), F32)],
        sem=("arbitrary",))


def _loss_head(x2, target, g):
    s, d = x2.shape
    tr = _tile(s, TR_EW, 16)
    n = s // tr

    def body(x_ref, t_ref, g_ref, dx_ref, dxb_ref, dg_ref, loss_ref, acc_ref):
        i = pl.program_id(0)
        xf = x_ref[...]
        gv = g_ref[...]
        rstd = lax.rsqrt(jnp.mean(xf * xf, axis=-1, keepdims=True) + EPS)
        err = xf * rstd * gv - t_ref[...]
        row_loss = jnp.mean(err * err, axis=-1, keepdims=True)
        dx, dgrows = _rms_bwd_rows(err / d, xf, gv)
        dx_ref[...] = dx
        dxb_ref[...] = dx.astype(BF16)

        @pl.when(i == 0)
        def _():
            acc_ref[...] = jnp.zeros_like(acc_ref)
            loss_ref[...] = jnp.zeros_like(loss_ref)
        acc_ref[...] += _fold8(dgrows)
        loss_ref[...] += jnp.broadcast_to(0.5 * jnp.sum(row_loss, axis=0, keepdims=True), (8, LANE))

        @pl.when(i == n - 1)
        def _():
            dg_ref[...] = jnp.sum(acc_ref[...], axis=0, keepdims=True)

    row = pl.BlockSpec((tr, d), lambda i: (i, 0))
    vec = pl.BlockSpec((1, d), lambda i: (0, 0))
    return pl.pallas_call(
        body, name="loss_head", grid=(n,), in_specs=[row, row, vec],
        out_specs=[row, row, vec, pl.BlockSpec((8, LANE), lambda i: (0, 0))],
        out_shape=[jax.ShapeDtypeStruct((s, d), F32), jax.ShapeDtypeStruct((s, d), BF16),
                   jax.ShapeDtypeStruct((1, d), F32), jax.ShapeDtypeStruct((8, LANE), F32)],
        scratch_shapes=[pltpu.VMEM((8, d), F32)],
        compiler_params=_cparams(("arbitrary",)),
    )(x2, target, g)


def _gate_fwd(ya, yb, proj_g):
    s, d = ya.shape
    tr = _tile(s, TR_EW, 16)
    tc = _tile(d, 1024)
    nc = d // tc

    def body(ya_ref, yb_ref, ga_ref, gb_ref, o_ref):
        o_ref[...] = (_sigmoid(ga_ref[...]) * ya_ref[...] + _sigmoid(gb_ref[...]) * yb_ref[...]).astype(o_ref.dtype)

    blk = pl.BlockSpec((tr, tc), lambda i, j: (i, j))
    return pl.pallas_call(
        body, name="gate_fwd", grid=(s // tr, nc),
        in_specs=[blk, blk, blk, pl.BlockSpec((tr, tc), lambda i, j: (i, j + nc))],
        out_specs=blk, out_shape=jax.ShapeDtypeStruct((s, d), BF16),
        compiler_params=_cparams(("parallel", "parallel")),
    )(ya, yb, proj_g, proj_g)


def _gate_bwd(dmixed, ya, yb, proj_g):
    s, d = ya.shape
    tr = _tile(s, TR_EW, 16)
    tc = _tile(d, 1024)
    nc = d // tc

    def body(dm_ref, ya_ref, yb_ref, ga_ref, gb_ref, dya_ref, dyb_ref, dga_ref, dgb_ref):
        dm = dm_ref[...].astype(F32)
        sa = _sigmoid(ga_ref[...])
        sb = _sigmoid(gb_ref[...])
        dya_ref[...] = (dm * sa).astype(BF16)
        dyb_ref[...] = (dm * sb).astype(BF16)
        dga_ref[...] = (dm * ya_ref[...] * sa * (1.0 - sa)).astype(BF16)
        dgb_ref[...] = (dm * yb_ref[...] * sb * (1.0 - sb)).astype(BF16)

    blk = pl.BlockSpec((tr, tc), lambda i, j: (i, j))
    out = jax.ShapeDtypeStruct((s, d), BF16)
    return pl.pallas_call(
        body, name="gate_bwd", grid=(s // tr, nc),
        in_specs=[blk, blk, blk, blk, pl.BlockSpec((tr, tc), lambda i, j: (i, j + nc))],
        out_specs=[blk, blk, blk, blk], out_shape=[out, out, out, out],
        compiler_params=_cparams(("parallel", "parallel")),
    )(dmixed, ya, yb, proj_g, proj_g)


def _swiglu_fwd(gu, jobs=()):
    s = gu.shape[0]
    tr = _tile(s, 128, 16)

    def body(gu_ref, h_ref):
        gate = gu_ref[:, :FF_P].astype(F32)
        up = gu_ref[:, FF_P:].astype(F32)
        h_ref[...] = (gate * _sigmoid(gate) * up).astype(h_ref.dtype)

    return _pcall(
        body, (gu,), jobs, name="swiglu_fwd", grid=(s // tr, N_CHIP),
        in_specs=[pl.BlockSpec((tr, 2 * FF_P), lambda i, j: (i, j))],
        out_specs=pl.BlockSpec((tr, FF_P), lambda i, j: (i, j)),
        out_shape=jax.ShapeDtypeStruct((s, FP), BF16),
        sem=("parallel", "parallel"))


def _swiglu_bwd(dhidden, gu):
    s = gu.shape[0]
    tr = _tile(s, 128, 16)

    def body(dh_ref, gu_ref, dgu_ref):
        gate = gu_ref[:, :FF_P].astype(F32)
        up = gu_ref[:, FF_P:].astype(F32)
        dh = dh_ref[...].astype(F32)
        sg = _sigmoid(gate)
        dgu_ref[:, :FF_P] = (dh * up * sg * (1.0 + gate * (1.0 - sg))).astype(BF16)
        dgu_ref[:, FF_P:] = (dh * gate * sg).astype(BF16)

    return pl.pallas_call(
        body, name="swiglu_bwd", grid=(s // tr, N_CHIP),
        in_specs=[pl.BlockSpec((tr, FF_P), lambda i, j: (i, j)), pl.BlockSpec((tr, 2 * FF_P), lambda i, j: (i, j))],
        out_specs=pl.BlockSpec((tr, 2 * FF_P), lambda i, j: (i, j)),
        out_shape=jax.ShapeDtypeStruct((s, 2 * FP), BF16),
        compiler_params=_cparams(("parallel", "parallel")),
    )(dhidden, gu)


def _tri(n, upper):
    r = lax.broadcasted_iota(jnp.int32, (n, n), 0)
    c = lax.broadcasted_iota(jnp.int32, (n, n), 1)
    return (c >= r if upper else c <= r).astype(F32)


def _forget_fwd(fa, bias):
    s = fa.shape[0]
    tb = _tile(s, 512, 8)

    def body(f_ref, b_ref, c_ref, carry_ref):
        i = pl.program_id(0)

        @pl.when(i == 0)
        def _():
            carry_ref[...] = jnp.zeros_like(carry_ref)
        z = f_ref[...] + b_ref[...]
        logf = jnp.minimum(z, 0.0) - jnp.log(1.0 + jnp.exp(-jnp.abs(z)))
        c = jnp.dot(_tri(tb, False), logf, precision=lax.Precision.HIGHEST, preferred_element_type=F32)
        c_ref[...] = c + carry_ref[0:1, :]
        carry_ref[...] = jnp.broadcast_to(c_ref[tb - 1:tb, :], carry_ref.shape)

    return pl.pallas_call(
        body, name="forget_fwd", grid=(s // tb,),
        in_specs=[pl.BlockSpec((tb, LANE), lambda i: (i, 0)), pl.BlockSpec((1, LANE), lambda i: (0, 0))],
        out_specs=pl.BlockSpec((tb, LANE), lambda i: (i, 0)),
        out_shape=jax.ShapeDtypeStruct((s, LANE), F32),
        scratch_shapes=[pltpu.VMEM((8, LANE), F32)],
        compiler_params=_cparams(("arbitrary",)),
    )(fa, bias)


def _forget_bwd(dc, fa, bias):
    s = fa.shape[0]
    tb = _tile(s, 512, 16)
    n = s // tb

    def body(dc_ref, f_ref, b_ref, df_ref, db_ref, carry_ref, acc_ref, tmp_ref):
        i = pl.program_id(0)

        @pl.when(i == 0)
        def _():
            carry_ref[...] = jnp.zeros_like(carry_ref)
            acc_ref[...] = jnp.zeros_like(acc_ref)
        dlogf = jnp.dot(_tri(tb, True), dc_ref[...], precision=lax.Precision.HIGHEST, preferred_element_type=F32)
        tmp_ref[...] = dlogf + carry_ref[0:1, :]
        carry_ref[...] = jnp.broadcast_to(tmp_ref[0:1, :], carry_ref.shape)
        df = tmp_ref[...] * _sigmoid(-(f_ref[...] + b_ref[...]))
        df_ref[...] = df.astype(BF16)
        acc_ref[...] += _fold8(df)

        @pl.when(i == n - 1)
        def _():
            db_ref[...] = jnp.sum(acc_ref[...], axis=0, keepdims=True)

    rev = pl.BlockSpec((tb, LANE), lambda i: (n - 1 - i, 0))
    vec = pl.BlockSpec((1, LANE), lambda i: (0, 0))
    return pl.pallas_call(
        body, name="forget_bwd", grid=(n,), in_specs=[rev, rev, vec], out_specs=[rev, vec],
        out_shape=[jax.ShapeDtypeStruct((s, LANE), BF16), jax.ShapeDtypeStruct((1, LANE), F32)],
        scratch_shapes=[pltpu.VMEM((8, LANE), F32), pltpu.VMEM((8, LANE), F32), pltpu.VMEM((tb, LANE), F32)],
        compiler_params=_cparams(("arbitrary",)),
    )(dc, fa, bias)


_A_SCALE = DH_A ** -0.5


def _causal(tq, tk, transposed):
    r = lax.broadcasted_iota(jnp.int32, (tq, tk), 0)
    c = lax.broadcasted_iota(jnp.int32, (tq, tk), 1)
    return c >= r if transposed else r >= c


def _fox_fwd(qkv, c_col, c_row, jobs=()):
    s = qkv.shape[0]
    t = _tile(s, TQ_A)
    n = s // t
    kb, vb = OFF_KA // DH_A, OFF_VA // DH_A

    def body(q_ref, k_ref, v_ref, cq_ref, ck_ref, o_ref, lse_ref, m_s, l_s, acc_s):
        i, j = pl.program_id(1), pl.program_id(2)

        @pl.when(j == 0)
        def _():
            m_s[...] = jnp.full_like(m_s, -jnp.inf)
            l_s[...] = jnp.zeros_like(l_s)
            acc_s[...] = jnp.zeros_like(acc_s)

        def step(diag):
            sc = _dot_nt(q_ref[...], k_ref[...]) * _A_SCALE + (cq_ref[:, 0:1] - ck_ref[...])
            if diag:
                sc = jnp.where(_causal(t, t, False), sc, -jnp.inf)
            m_prev = m_s[...]
            m_new = jnp.maximum(m_prev, jnp.max(sc, axis=1, keepdims=True))
            alpha = jnp.exp(m_prev - m_new)
            p = jnp.exp(sc - m_new[:, 0:1])
            l_s[...] = alpha * l_s[...] + jnp.sum(p, axis=1, keepdims=True)
            acc_s[...] = alpha * acc_s[...] + _dot(p.astype(BF16), v_ref[...])
            m_s[...] = m_new

        @pl.when(j < i)
        def _():
            step(False)

        @pl.when(j == i)
        def _():
            step(True)
            o_ref[...] = (acc_s[...] / l_s[...]).astype(o_ref.dtype)
            lse_ref[...] = m_s[...] + jnp.log(l_s[...])

    jc = lambda i, j: jnp.minimum(j, i)
    return _pcall(
        body, (qkv, qkv, qkv, c_col, c_row), jobs, name="fox_fwd", grid=(H_A, n, n),
        in_specs=[pl.BlockSpec((t, DH_A), lambda h, i, j: (i, h)),
                  pl.BlockSpec((t, DH_A), lambda h, i, j: (jc(i, j), kb + h)),
                  pl.BlockSpec((t, DH_A), lambda h, i, j: (jc(i, j), vb + h)),
                  pl.BlockSpec((None, t, LANE), lambda h, i, j: (h, i, 0)),
                  pl.BlockSpec((None, 1, t), lambda h, i, j: (h, 0, jc(i, j)))],
        out_specs=[pl.BlockSpec((t, DH_A), lambda h, i, j: (i, h)),
                   pl.BlockSpec((None, t, LANE), lambda h, i, j: (h, i, 0))],
        out_shape=[jax.ShapeDtypeStruct((s, W_A), BF16), jax.ShapeDtypeStruct((H_A, s, LANE), F32)],
        scratch_shapes=[pltpu.VMEM((t, LANE), F32)] * 3,
        sem=("parallel", "parallel", "arbitrary"))


def _fox_dq(qkv, do, c_col, c_row, lse_col, jobs=()):
    s = qkv.shape[0]
    t = _tile(s, TQ_A)
    n = s // t
    kb, vb = OFF_KA // DH_A, OFF_VA // DH_A

    def body(q_ref, k_ref, v_ref, do_ref, cq_ref, ck_ref, lse_ref, dq_ref, dl_ref, pdk_s, pk_s, dl_s):
        i, j = pl.program_id(1), pl.program_id(2)

        @pl.when(j == 0)
        def _():
            pdk_s[...] = jnp.zeros_like(pdk_s)
            pk_s[...] = jnp.zeros_like(pk_s)
            dl_s[...] = jnp.zeros_like(dl_s)

        def step(diag):
            sc = _dot_nt(q_ref[...], k_ref[...]) * _A_SCALE + (cq_ref[:, 0:1] - ck_ref[...])
            if diag:
                sc = jnp.where(_causal(t, t, False), sc, -jnp.inf)
            p = jnp.exp(sc - lse_ref[:, 0:1])
            pdp = p * _dot_nt(do_ref[...], v_ref[...])
            dl_s[...] += jnp.sum(pdp, axis=1, keepdims=True)
            pdk_s[...] += _dot(pdp.astype(BF16), k_ref[...])
            pk_s[...] += _dot(p.astype(BF16), k_ref[...])

        @pl.when(j < i)
        def _():
            step(False)

        @pl.when(j == i)
        def _():
            step(True)
            dq_ref[...] = ((pdk_s[...] - dl_s[...] * pk_s[...]) * _A_SCALE).astype(dq_ref.dtype)
            dl_ref[...] = dl_s[...]

    jc = lambda i, j: jnp.minimum(j, i)
    col = pl.BlockSpec((None, t, LANE), lambda h, i, j: (h, i, 0))
    return _pcall(
        body, (qkv, qkv, qkv, do, c_col, c_row, lse_col), jobs, name="fox_dq", grid=(H_A, n, n),
        in_specs=[pl.BlockSpec((t, DH_A), lambda h, i, j: (i, h)),
                  pl.BlockSpec((t, DH_A), lambda h, i, j: (jc(i, j), kb + h)),
                  pl.BlockSpec((t, DH_A), lambda h, i, j: (jc(i, j), vb + h)),
                  pl.BlockSpec((t, DH_A), lambda h, i, j: (i, h)),
                  col,
                  pl.BlockSpec((None, 1, t), lambda h, i, j: (h, 0, jc(i, j))),
                  col],
        out_specs=[pl.BlockSpec((t, DH_A), lambda h, i, j: (i, h)), col],
        out_shape=[jax.ShapeDtypeStruct((s, W_A), BF16), jax.ShapeDtypeStruct((H_A, s, LANE), F32)],
        scratch_shapes=[pltpu.VMEM((t, DH_A), F32), pltpu.VMEM((t, DH_A), F32), pltpu.VMEM((t, LANE), F32)],
        sem=("parallel", "parallel", "arbitrary"))


def _fox_dkv(qkv, do, c_col, c_row, lse_row, delta_row, jobs=()):
    s = qkv.shape[0]
    t = _tile(s, TQ_A)
    n = s // t
    kb, vb = OFF_KA // DH_A, OFF_VA // DH_A

    def body(q_ref, k_ref, v_ref, do_ref, cq_ref, ck_ref, lse_ref, dl_ref, dk_ref, dv_ref, dc_ref,
             dk_s, dv_s, dc_s):
        j, i = pl.program_id(1), pl.program_id(2)

        @pl.when(i == 0)
        def _():
            dk_s[...] = jnp.zeros_like(dk_s)
            dv_s[...] = jnp.zeros_like(dv_s)
            dc_s[...] = jnp.zeros_like(dc_s)

        def step(diag):
            st = _dot_nt(k_ref[...], q_ref[...]) * _A_SCALE + (cq_ref[...] - ck_ref[:, 0:1])
            if diag:
                st = jnp.where(_causal(t, t, True), st, -jnp.inf)
            pt = jnp.exp(st - lse_ref[...])
            dv_s[...] += _dot(pt.astype(BF16), do_ref[...])
            dpt = _dot_nt(v_ref[...], do_ref[...])
            dst = pt * (dpt - dl_ref[...])
            dk_s[...] += _dot(dst.astype(BF16), q_ref[...])
            dc_s[...] -= jnp.sum(dst, axis=1, keepdims=True)

        @pl.when(i == j)
        def _():
            step(True)

        @pl.when(i > j)
        def _():
            step(False)

        @pl.when(i == n - 1)
        def _():
            dk_ref[...] = (dk_s[...] * _A_SCALE).astype(dk_ref.dtype)
            dv_ref[...] = dv_s[...].astype(dv_ref.dtype)
            dc_ref[...] = dc_s[...]

    ic = lambda j, i: jnp.maximum(i, j)
    rowq = pl.BlockSpec((None, 1, t), lambda h, j, i: (h, 0, ic(j, i)))
    kv_out = pl.BlockSpec((t, DH_A), lambda h, j, i: (j, h))
    return _pcall(
        body, (qkv, qkv, qkv, do, c_row, c_col, lse_row, delta_row), jobs, name="fox_dkv", grid=(H_A, n, n),
        in_specs=[pl.BlockSpec((t, DH_A), lambda h, j, i: (ic(j, i), h)),
                  pl.BlockSpec((t, DH_A), lambda h, j, i: (j, kb + h)),
                  pl.BlockSpec((t, DH_A), lambda h, j, i: (j, vb + h)),
                  pl.BlockSpec((t, DH_A), lambda h, j, i: (ic(j, i), h)),
                  rowq,
                  pl.BlockSpec((None, t, LANE), lambda h, j, i: (h, j, 0)),
                  rowq, rowq],
        out_specs=[kv_out, kv_out, pl.BlockSpec((None, t, LANE), lambda h, j, i: (h, j, 0))],
        out_shape=[jax.ShapeDtypeStruct((s, W_A), BF16), jax.ShapeDtypeStruct((s, W_A), BF16),
                   jax.ShapeDtypeStruct((H_A, s, LANE), F32)],
        scratch_shapes=[pltpu.VMEM((t, DH_A), F32), pltpu.VMEM((t, DH_A), F32), pltpu.VMEM((t, LANE), F32)],
        sem=("parallel", "parallel", "arbitrary"))


_B_SCALE = DH_B ** -0.5
_HALF = LANE // 2


def _t5_bucket_table():
    ql = np.arange(BLOCK)[:, None]
    kl = np.arange(2 * BLOCK)[None, :]
    dist = np.clip(ql + BLOCK - kl, 0, None)
    max_exact = NUM_BUCKETS // 2
    large = max_exact + (np.log(np.maximum(dist, 1) / max_exact) / np.log(MAX_DISTANCE / max_exact)
                         * (NUM_BUCKETS - max_exact)).astype(np.int64)
    large = np.minimum(large, NUM_BUCKETS - 1)
    return np.where(dist < max_exact, dist, large).astype(np.int32).reshape(1, BLOCK * 2 * BLOCK)


def _one_hot_buckets(bucket_ref, n):
    return (lax.broadcasted_iota(jnp.int32, (NUM_BUCKETS, n), 0) == bucket_ref[...]).astype(F32)


def _bias_table(rel_bias_t, bucket):
    nqk = bucket.shape[1]
    tc = _tile(nqk, 4096)

    def body(rb_ref, bk_ref, o_ref):
        o_ref[...] = jnp.dot(rb_ref[...], _one_hot_buckets(bk_ref, tc), precision=lax.Precision.HIGHEST,
                             preferred_element_type=F32)

    return pl.pallas_call(
        body, name="bias_table", grid=(nqk // tc,),
        in_specs=[pl.BlockSpec((H_B, NUM_BUCKETS), lambda i: (0, 0)), pl.BlockSpec((1, tc), lambda i: (0, i))],
        out_specs=pl.BlockSpec((H_B, tc), lambda i: (0, i)),
        out_shape=jax.ShapeDtypeStruct((H_B, nqk), F32),
        compiler_params=_cparams(("parallel",)),
    )(rel_bias_t, bucket)


def _bias_table_bwd(dbias, bucket):
    nqk = bucket.shape[1]
    tc = _tile(nqk, 4096)
    n = nqk // tc

    def body(db_ref, bk_ref, o_ref):
        i = pl.program_id(0)

        @pl.when(i == 0)
        def _():
            o_ref[...] = jnp.zeros_like(o_ref)
        o_ref[...] += lax.dot_general(db_ref[...], _one_hot_buckets(bk_ref, tc), (((1,), (1,)), ((), ())),
                                      precision=lax.Precision.HIGHEST, preferred_element_type=F32)

    return pl.pallas_call(
        body, name="bias_table_bwd", grid=(n,),
        in_specs=[pl.BlockSpec((H_B, tc), lambda i: (0, i)), pl.BlockSpec((1, tc), lambda i: (0, i))],
        out_specs=pl.BlockSpec((H_B, NUM_BUCKETS), lambda i: (0, 0)),
        out_shape=jax.ShapeDtypeStruct((H_B, NUM_BUCKETS), F32),
        compiler_params=_cparams(("arbitrary",)),
    )(dbias, bucket)


def _lane_lo():
    return lax.broadcasted_iota(jnp.int32, (1, LANE), 1) < _HALF


def _dup_kv_head(cat, hk):
    xcol = cat[:, (hk // 2) * LANE:(hk // 2 + 1) * LANE].astype(F32)
    swapped = pltpu.roll(xcol, _HALF, 1)
    lo = _lane_lo()
    return (jnp.where(lo, xcol, swapped) if hk % 2 == 0 else jnp.where(lo, swapped, xcol)).astype(BF16)


def _band_mask(first_block):
    ql = lax.broadcasted_iota(jnp.int32, (BLOCK, 2 * BLOCK), 0)
    kl = lax.broadcasted_iota(jnp.int32, (BLOCK, 2 * BLOCK), 1)
    dist = ql + BLOCK - kl
    ok = jnp.logical_and(dist >= 0, dist < WINDOW)
    return jnp.logical_and(ok, jnp.logical_or(jnp.logical_not(first_block), kl >= BLOCK))


def _swa_probs(qh, kdup, bias_h, sink, mask):
    sc = _dot_nt(qh, kdup) * _B_SCALE + bias_h
    sc = jnp.where(mask, sc, -jnp.inf)
    m = jnp.maximum(jnp.max(sc, axis=1, keepdims=True), sink)
    p = jnp.exp(sc - m)
    e_sink = jnp.exp(sink - m)
    inv = 1.0 / (jnp.sum(p, axis=1, keepdims=True) + e_sink)
    return p * inv, e_sink * inv


def _split_pair(ref, col):
    x = ref[:, col * LANE:(col + 1) * LANE].astype(F32)
    lo = _lane_lo()
    return jnp.where(lo, x, 0.0).astype(BF16), jnp.where(lo, 0.0, x).astype(BF16)


def _swa_fwd(qkv, bias, sinks, jobs=()):
    s = qkv.shape[0]
    nb = s // BLOCK
    qb, kb, vb = OFF_QB // W_QB, OFF_KB // W_KB, OFF_VB // W_KB
    assert OFF_QB % W_QB == 0 and OFF_KB % W_KB == 0 and OFF_VB % W_KB == 0 and W_KB % LANE == 0 and G_B % 2 == 0

    def body(q_ref, kp_ref, kc_ref, vp_ref, vc_ref, bias_ref, sink_ref, o_ref):
        i = pl.program_id(0)
        mask = _band_mask(i == 0)
        kcat = jnp.concatenate([kp_ref[...], kc_ref[...]], axis=0)
        vcat = jnp.concatenate([vp_ref[...], vc_ref[...]], axis=0)
        lo = _lane_lo()
        for hk in range(HKV_B):
            kdup, vdup = _dup_kv_head(kcat, hk), _dup_kv_head(vcat, hk)
            for pr in range(G_B // 2):
                h0 = hk * G_B + 2 * pr
                q0, q1 = _split_pair(q_ref, h0 // 2)
                p0, _ = _swa_probs(q0, kdup, bias_ref[h0], sink_ref[0, h0], mask)
                p1, _ = _swa_probs(q1, kdup, bias_ref[h0 + 1], sink_ref[0, h0 + 1], mask)
                o0 = _dot(p0.astype(BF16), vdup)
                o1 = _dot(p1.astype(BF16), vdup)
                o_ref[:, (h0 // 2) * LANE:(h0 // 2 + 1) * LANE] = jnp.where(lo, o0, o1).astype(o_ref.dtype)

    prev = lambda i: jnp.maximum(i - 1, 0)
    return _pcall(
        body, (qkv, qkv, qkv, qkv, qkv, bias, sinks), jobs, name="swa_fwd", grid=(nb,),
        in_specs=[pl.BlockSpec((BLOCK, W_QB), lambda i: (i, qb)),
                  pl.BlockSpec((BLOCK, W_KB), lambda i: (prev(i), kb)),
                  pl.BlockSpec((BLOCK, W_KB), lambda i: (i, kb)),
                  pl.BlockSpec((BLOCK, W_KB), lambda i: (prev(i), vb)),
                  pl.BlockSpec((BLOCK, W_KB), lambda i: (i, vb)),
                  pl.BlockSpec((H_B, BLOCK, 2 * BLOCK), lambda i: (0, 0, 0)),
                  pl.BlockSpec(memory_space=pltpu.SMEM)],
        out_specs=pl.BlockSpec((BLOCK, W_QB), lambda i: (i, 0)),
        out_shape=jax.ShapeDtypeStruct((s, W_QB), BF16),
        sem=("parallel",))


def _swa_bwd(qkv, do, bias, sinks, jobs=()):
    s = qkv.shape[0]
    nb = s // BLOCK
    qb, kb, vb = OFF_QB // W_QB, OFF_KB // W_KB, OFF_VB // W_KB

    def body(q_ref, kp_ref, kc_ref, vp_ref, vc_ref, do_ref, bias_ref, sink_ref,
             dq_ref, dk_ref, dv_ref, dbias_ref, dsink_ref, carry_k, carry_v):
        i = pl.program_id(0)
        lo = _lane_lo()

        @pl.when(i == 0)
        def _():
            dbias_ref[...] = jnp.zeros_like(dbias_ref)
            dsink_ref[...] = jnp.zeros_like(dsink_ref)
            carry_k[...] = jnp.zeros_like(carry_k)
            carry_v[...] = jnp.zeros_like(carry_v)

        @pl.when(i < nb)
        def _():
            mask = _band_mask(i == 0)
            kcat = jnp.concatenate([kp_ref[...], kc_ref[...]], axis=0)
            vcat = jnp.concatenate([vp_ref[...], vc_ref[...]], axis=0)
            lane = lax.broadcasted_iota(jnp.int32, (1, LANE), 1)
            dsink = jnp.zeros((1, LANE), F32)
            dk_cols = [jnp.zeros((2 * BLOCK, LANE), F32) for _ in range(W_KB // LANE)]
            dv_cols = [jnp.zeros((2 * BLOCK, LANE), F32) for _ in range(W_KB // LANE)]
            for hk in range(HKV_B):
                kdup, vdup = _dup_kv_head(kcat, hk), _dup_kv_head(vcat, hk)
                dk_acc = jnp.zeros((2 * BLOCK, LANE), F32)
                dv_acc = jnp.zeros((2 * BLOCK, LANE), F32)
                for pr in range(G_B // 2):
                    h0 = hk * G_B + 2 * pr
                    col = h0 // 2
                    qs = _split_pair(q_ref, col)
                    dos = _split_pair(do_ref, col)
                    dqs = []
                    for e in range(2):
                        h = h0 + e
                        p, p_sink = _swa_probs(qs[e], kdup, bias_ref[h], sink_ref[0, h], mask)
                        dp = _dot_nt(dos[e], vdup)
                        delta = jnp.sum(p * dp, axis=1, keepdims=True)
                        ds = p * (dp - delta)
                        dbias_ref[h] += ds
                        dsink = dsink - jnp.where(lane == h, jnp.sum(p_sink * delta, axis=0, keepdims=True), 0.0)
                        dqs.append(_dot(ds.astype(BF16), kdup))
                        dk_acc = dk_acc + _dot(ds.T.astype(BF16), qs[e])
                        dv_acc = dv_acc + _dot(p.T.astype(BF16), dos[e])
                    dq_ref[:, col * LANE:(col + 1) * LANE] = (jnp.where(lo, dqs[0], dqs[1]) * _B_SCALE).astype(dq_ref.dtype)
                dk_tot = (dk_acc + pltpu.roll(dk_acc, _HALF, 1)) * _B_SCALE
                dv_tot = dv_acc + pltpu.roll(dv_acc, _HALF, 1)
                mine = lo if hk % 2 == 0 else jnp.logical_not(lo)
                dk_cols[hk // 2] = jnp.where(mine, dk_tot, dk_cols[hk // 2])
                dv_cols[hk // 2] = jnp.where(mine, dv_tot, dv_cols[hk // 2])
            dsink_ref[...] += dsink
            dk_cat = jnp.concatenate(dk_cols, axis=1)
            dv_cat = jnp.concatenate(dv_cols, axis=1)
            dk_ref[...] = (carry_k[...] + dk_cat[:BLOCK]).astype(dk_ref.dtype)
            dv_ref[...] = (carry_v[...] + dv_cat[:BLOCK]).astype(dv_ref.dtype)
            carry_k[...] = dk_cat[BLOCK:]
            carry_v[...] = dv_cat[BLOCK:]

        @pl.when(i == nb)
        def _():
            dk_ref[...] = carry_k[...].astype(dk_ref.dtype)
            dv_ref[...] = carry_v[...].astype(dv_ref.dtype)

    cur = lambda i: jnp.minimum(i, nb - 1)
    prev = lambda i: jnp.clip(i - 1, 0, nb - 1)
    kv_out = pl.BlockSpec((BLOCK, W_KB), lambda i: (prev(i), 0))
    return _pcall(
        body, (qkv, qkv, qkv, qkv, qkv, do, bias, sinks), jobs, name="swa_bwd", grid=(nb + 1,),
        in_specs=[pl.BlockSpec((BLOCK, W_QB), lambda i: (cur(i), qb)),
                  pl.BlockSpec((BLOCK, W_KB), lambda i: (prev(i), kb)),
                  pl.BlockSpec((BLOCK, W_KB), lambda i: (cur(i), kb)),
                  pl.BlockSpec((BLOCK, W_KB), lambda i: (prev(i), vb)),
                  pl.BlockSpec((BLOCK, W_KB), lambda i: (cur(i), vb)),
                  pl.BlockSpec((BLOCK, W_QB), lambda i: (cur(i), 0)),
                  pl.BlockSpec((H_B, BLOCK, 2 * BLOCK), lambda i: (0, 0, 0)),
                  pl.BlockSpec(memory_space=pltpu.SMEM)],
        out_specs=[pl.BlockSpec((BLOCK, W_QB), lambda i: (cur(i), 0)), kv_out, kv_out,
                   pl.BlockSpec((H_B, BLOCK, 2 * BLOCK), lambda i: (0, 0, 0)),
                   pl.BlockSpec((1, LANE), lambda i: (0, 0))],
        out_shape=[jax.ShapeDtypeStruct((s, W_QB), BF16), jax.ShapeDtypeStruct((s, W_KB), BF16),
                   jax.ShapeDtypeStruct((s, W_KB), BF16),
                   jax.ShapeDtypeStruct((H_B, BLOCK, 2 * BLOCK), F32), jax.ShapeDtypeStruct((1, LANE), F32)],
        scratch_shapes=[pltpu.VMEM((BLOCK, W_KB), F32), pltpu.VMEM((BLOCK, W_KB), F32)],
        sem=("arbitrary",))


_RELS = ((1, 0), (0, 1), (1, 1))


def _place():
    x, y, c = lax.axis_index("x"), lax.axis_index("y"), lax.axis_index("c")
    return x, y, c


def _rel_chip(x, y, rel):
    px = 1 - x if rel[0] else x
    py = 1 - y if rel[1] else y
    return px, py, 2 * px + py


def _half_rows(ref_shape, c):
    half = ref_shape[-2] // 2
    return pl.ds(pl.multiple_of(c * half, 16), half)


def _row_chunks(start, size):
    n = next(n for n in (8, 4, 2, 1) if size % (16 * n) == 0)
    step = size // n
    if isinstance(start, int):
        return [pl.ds(start + i * step, step) for i in range(n)]
    return [pl.ds(pl.multiple_of(start + i * step, 16), step) for i in range(n)]


def _dma_sems(*shape):
    return pltpu.SemaphoreType.DMA(shape)


def _gather_ici_job(srcs):
    nt = len(srcs)

    def copies(src, dst, sems):
        send, recv = sems
        x, y, c = _place()
        me_chip = 2 * x + y
        chunks, sends, recvs = [], [], []
        for t in range(nt):
            ns = src[t].shape[0]
            half = src[t].shape[-2] // 2
            for k, rel in enumerate(_RELS[:2]):
                px, py, chip = _rel_chip(x, y, rel)

                def mk(into, rows):
                    return pltpu.make_async_remote_copy(
                        src_ref=src[t].at[:, rows], dst_ref=dst[t].at[pl.ds(into * ns, ns), rows],
                        send_sem=send.at[t, k], recv_sem=recv.at[t, k], device_id=(px, py, c), device_id_type=MESH)
                chunks += [mk(me_chip, r) for r in _row_chunks(c * half, half)]
                sends.append(mk(me_chip, _half_rows(src[t].shape, c)))
                recvs.append(mk(chip, _half_rows(src[t].shape, c)))
        return chunks, sends, recvs

    def start(src, dst, sems):
        for cp in copies(src, dst, sems)[0]:
            cp.start()

    def wait(src, dst, sems):
        _, sends, recvs = copies(src, dst, sems)
        for cp in recvs:
            cp.wait_recv()
        for cp in sends:
            cp.wait_send()

    return _Job(srcs, [jax.ShapeDtypeStruct((N_CHIP * a.shape[0],) + a.shape[1:], a.dtype) for a in srcs],
                [_dma_sems(nt, 2), _dma_sems(nt, 2)], start, wait)


def _with_own(gathered, srcs):
    chip = 2 * lax.axis_index("x") + lax.axis_index("y")
    return [lax.dynamic_update_slice(g, s, (chip * s.shape[0], 0, 0)) for g, s in zip(gathered, srcs)]


def _gather_relay_job(gathered):
    nt = len(gathered)

    def copies(dst, sems):
        send, recv = sems
        x, y, c = _place()
        diag = 2 * (1 - x) + (1 - y)
        chunks, sends, recvs = [], [], []
        for t in range(nt):
            ns = dst[t].shape[0] // N_CHIP
            quarter = dst[t].shape[-2] // 4
            for k, rel in enumerate(_RELS[:2]):
                px, py, _ = _rel_chip(x, y, rel)
                _, _, origin = _rel_chip(x, y, _RELS[1 - k])

                def mk(chip, rows):
                    part = dst[t].at[pl.ds(chip * ns, ns), rows]
                    return pltpu.make_async_remote_copy(
                        src_ref=part, dst_ref=part, send_sem=send.at[t, k], recv_sem=recv.at[t, k],
                        device_id=(px, py, c), device_id_type=MESH)
                whole = pl.ds(pl.multiple_of((2 * c + k) * quarter, 16), quarter)
                chunks += [mk(origin, r) for r in _row_chunks((2 * c + k) * quarter, quarter)]
                sends.append(mk(origin, whole))
                recvs.append(mk(diag, whole))
        return chunks, sends, recvs

    def start(_, dst, sems):
        for cp in copies(dst, sems)[0]:
            cp.start()

    def wait(_, dst, sems):
        _, sends, recvs = copies(dst, sems)
        for cp in recvs:
            cp.wait_recv()
        for cp in sends:
            cp.wait_send()

    return _Job(gathered, [jax.ShapeDtypeStruct(a.shape, a.dtype) for a in gathered],
                [_dma_sems(nt, 2), _dma_sems(nt, 2)], start, wait, alias={t: t for t in range(nt)})


def _gather_d2d_job(gathered):
    nt = len(gathered)

    def copies(dst, sems):
        send, recv = sems
        x, y, c = _place()
        sends, recvs = [], []
        for t in range(nt):
            ns = dst[t].shape[0] // N_CHIP
            for k, rel in enumerate(_RELS):
                _, _, chip = _rel_chip(x, y, rel)
                for half, lst in ((c, sends), (1 - c, recvs)):
                    part = dst[t].at[pl.ds(chip * ns, ns), _half_rows(dst[t].shape, half)]
                    lst.append(pltpu.make_async_remote_copy(
                        src_ref=part, dst_ref=part, send_sem=send.at[t, k], recv_sem=recv.at[t, k],
                        device_id=(x, y, 1 - c), device_id_type=MESH))
        return sends, recvs

    def start(_, dst, sems):
        for cp in copies(dst, sems)[0]:
            cp.start()

    def wait(_, dst, sems):
        sends, recvs = copies(dst, sems)
        for cp in recvs:
            cp.wait_recv()
        for cp in sends:
            cp.wait_send()

    return _Job(gathered, [jax.ShapeDtypeStruct(a.shape, a.dtype) for a in gathered],
                [_dma_sems(nt, 3), _dma_sems(nt, 3)], start, wait, alias={t: t for t in range(nt)})


def _pair_job(grads):
    nt = len(grads)

    def copies(g, got, sems):
        send, recv = sems
        x, y, c = _place()
        return [pltpu.make_async_remote_copy(
            src_ref=g[t].at[:, _half_rows(g[t].shape, 1 - c)], dst_ref=got[t], send_sem=send.at[t],
            recv_sem=recv.at[t], device_id=(x, y, 1 - c), device_id_type=MESH) for t in range(nt)]

    def start(g, got, sems):
        for cp in copies(g, got, sems):
            cp.start()

    def wait(g, got, sems):
        for cp in copies(g, got, sems):
            cp.wait()

    return _Job(grads, [jax.ShapeDtypeStruct((a.shape[0], a.shape[1] // 2, a.shape[2]), a.dtype) for a in grads],
                [_dma_sems(nt), _dma_sems(nt)], start, wait)


def _ew_rows(rows, cols, itemsize=4):
    tr = 16
    while tr * 2 <= 1024 and rows % (tr * 2) == 0 and tr * 2 * cols * itemsize <= (1 << 22):
        tr *= 2
    assert rows % tr == 0, (rows, tr)
    return tr


def _pair_sum(name, grad, got, place):
    ns, rows, cols = grad.shape
    half = rows // 2
    tr = _ew_rows(half, cols)
    nh = half // tr

    def body(p_ref, g_ref, r_ref, o_ref):
        o_ref[...] = (g_ref[...].astype(F32) + r_ref[...].astype(F32)).astype(o_ref.dtype)

    return pl.pallas_call(
        body, name=name,
        grid_spec=pltpu.PrefetchScalarGridSpec(
            num_scalar_prefetch=1, grid=(ns, nh),
            in_specs=[pl.BlockSpec((None, tr, cols), lambda s, i, p: (s, p[1] * nh + i, 0)),
                      pl.BlockSpec((None, tr, cols), lambda s, i, p: (s, i, 0))],
            out_specs=pl.BlockSpec((None, tr, cols), lambda s, i, p: (s, i, 0))),
        out_shape=jax.ShapeDtypeStruct((ns, half, cols), BF16),
        compiler_params=_cparams(("parallel", "parallel")),
    )(place, grad, got)


def _chip_job(psums):
    nt = len(psums)

    def copies(p, got, sems):
        send, recv = sems
        x, y, c = _place()
        chunks, whole = [], []
        for t in range(nt):
            ns = p[t].shape[0] // N_CHIP
            for k, rel in enumerate(_RELS):
                px, py, chip = _rel_chip(x, y, rel)

                def mk(rows):
                    return pltpu.make_async_remote_copy(
                        src_ref=p[t].at[pl.ds(chip * ns, ns), rows], dst_ref=got[t].at[k, :, rows],
                        send_sem=send.at[t, k], recv_sem=recv.at[t, k], device_id=(px, py, c), device_id_type=MESH)
                chunks += [mk(r) for r in _row_chunks(0, p[t].shape[1])]
                whole.append(mk(pl.ds(0, p[t].shape[1])))
        return chunks, whole

    def start(p, got, sems):
        for cp in copies(p, got, sems)[0]:
            cp.start()

    def wait(p, got, sems):
        for cp in copies(p, got, sems)[1]:
            cp.wait()

    return _Job(psums, [jax.ShapeDtypeStruct((3, a.shape[0] // N_CHIP) + a.shape[1:], a.dtype) for a in psums],
                [_dma_sems(nt, 3), _dma_sems(nt, 3)], start, wait)


def _chip_sum(name, psum, got, place):
    ns4, half, cols = psum.shape
    ns = ns4 // N_CHIP
    tr = _ew_rows(half, cols)
    nh = half // tr

    def body(p_ref, mine_ref, got_ref, o_ref):
        acc = mine_ref[...].astype(F32)
        for k in range(3):
            acc = acc + got_ref[k].astype(F32)
        o_ref[...] = acc

    return pl.pallas_call(
        body, name=name,
        grid_spec=pltpu.PrefetchScalarGridSpec(
            num_scalar_prefetch=1, grid=(ns, nh),
            in_specs=[pl.BlockSpec((None, tr, cols), lambda s, i, p: (p[0] * ns + s, i, 0)),
                      pl.BlockSpec((3, None, tr, cols), lambda s, i, p: (0, s, i, 0))],
            out_specs=pl.BlockSpec((None, tr, cols), lambda s, i, p: (s, p[1] * nh + i, 0))),
        out_shape=jax.ShapeDtypeStruct((ns, 2 * half, cols), F32),
        compiler_params=_cparams(("parallel", "parallel")),
    )(place, psum, got)


def _share_job(halves):
    nt = len(halves)

    def copies(full, sems):
        send, recv = sems
        x, y, c = _place()
        sends, recvs = [], []
        for t in range(nt):
            for half, lst in ((c, sends), (1 - c, recvs)):
                part = full[t].at[:, _half_rows(full[t].shape, half)]
                lst.append(pltpu.make_async_remote_copy(
                    src_ref=part, dst_ref=part, send_sem=send.at[t], recv_sem=recv.at[t],
                    device_id=(x, y, 1 - c), device_id_type=MESH))
        return sends, recvs

    def start(_, full, sems):
        for cp in copies(full, sems)[0]:
            cp.start()

    def wait(_, full, sems):
        sends, recvs = copies(full, sems)
        for cp in sends:
            cp.wait_send()
        for cp in recvs:
            cp.wait_recv()

    return _Job(halves, [jax.ShapeDtypeStruct(a.shape, a.dtype) for a in halves],
                [_dma_sems(nt), _dma_sems(nt)], start, wait, alias={t: t for t in range(nt)})


def _small_all_reduce(pack):
    rows, d = pack.shape

    def body(x_ref, o_ref, land, send, recv):
        x, y, c = _place()
        me = 4 * x + 2 * y + c
        land[me] = x_ref[...]
        cps = []
        for k in range(1, 8):
            to = (1 - x if k & 4 else x, 1 - y if k & 2 else y, 1 - c if k & 1 else c)
            cps.append(pltpu.make_async_remote_copy(
                src_ref=x_ref, dst_ref=land.at[me], send_sem=send.at[k - 1], recv_sem=recv.at[k - 1],
                device_id=to, device_id_type=MESH))
        for cp in cps:
            cp.start()
        for cp in cps:
            cp.wait()
        acc = land[0]
        for dev in range(1, 8):
            acc = acc + land[dev]
        o_ref[...] = acc

    vm = pl.BlockSpec(memory_space=pltpu.VMEM)
    return pl.pallas_call(
        body, name="small_all_reduce", in_specs=[vm], out_specs=vm,
        out_shape=jax.ShapeDtypeStruct((rows, d), F32),
        scratch_shapes=[pltpu.VMEM((8, rows, d), F32), pltpu.SemaphoreType.DMA((7,)), pltpu.SemaphoreType.DMA((7,))],
    )(pack)


def _adamw(name, w, g, m, v, jobs=()):
    lead, rows, cols = w.shape
    fits = [t for t in range(8, rows + 1, 8) if rows % t == 0 and t * cols * 4 <= (1 << 20)]
    tr = fits[-1] if fits else rows
    tl = 1
    if tr == rows:
        tl = max(t for t in range(1, lead + 1) if lead % t == 0 and t * rows * cols * 4 <= (1 << 20))

    def body(w_ref, g_ref, m_ref, v_ref, d_ref, nm_ref, nv_ref):
        gv = g_ref[...]
        nm = ADAM_B1 * m_ref[...] + (1.0 - ADAM_B1) * gv
        nv = ADAM_B2 * v_ref[...] + (1.0 - ADAM_B2) * (gv * gv)
        m_hat = nm / (1.0 - ADAM_B1 ** ADAM_STEP)
        v_hat = nv / (1.0 - ADAM_B2 ** ADAM_STEP)
        d_ref[...] = -ADAM_LR * (m_hat / (jnp.sqrt(v_hat) + ADAM_EPS) + ADAM_WD * w_ref[...])
        nm_ref[...] = nm
        nv_ref[...] = nv

    blk = pl.BlockSpec((tl, tr, cols), lambda l, i: (l, i, 0))
    out = jax.ShapeDtypeStruct(w.shape, F32)
    return _pcall(
        body, (w, g, m, v), jobs, name=name, grid=(lead // tl, rows // tr), in_specs=[blk] * 4, out_specs=[blk] * 3,
        out_shape=[out, out, out], sem=("parallel", "parallel"))


def _pad_to(a, shape):
    return jnp.pad(a, [(0, t - s) for s, t in zip(a.shape, shape)])


def _pack_small(n1, n2, fg, bf, sk, rb, extra=None):
    rows = [n1.reshape(1, D_MODEL), n2.reshape(1, D_MODEL), fg.reshape(1, D_MODEL),
            _pad_to(bf.reshape(1, H_A), (1, D_MODEL)), _pad_to(sk.reshape(1, H_B), (1, D_MODEL)),
            jnp.zeros((1, D_MODEL), F32) if extra is None else _pad_to(extra.reshape(1, 1), (1, D_MODEL)),
            _pad_to(rb.reshape(1, NUM_BUCKETS * H_B), (1, RB_ROWS * D_MODEL)).reshape(RB_ROWS, D_MODEL)]
    return _pad_to(jnp.concatenate(rows, axis=0), (PACK_ROWS, D_MODEL))


def _unpack_small(p):
    return (p[0:1], p[1:2], p[2], p[3:4, :H_A], p[4:5, :H_B],
            p[6:6 + RB_ROWS].reshape(-1)[:NUM_BUCKETS * H_B].reshape(NUM_BUCKETS, H_B))


def kernel(x, norm1_g, w_in, b_forget, attn_sinks, rel_bias, w_branch_a, w_branch_b, w_out, norm2_g, w_ffn_gate, w_ffn_up, w_ffn_down, final_g, loss_target, m_norm1_g, m_w_in, m_b_forget, m_attn_sinks, m_rel_bias, m_w_branch_a, m_w_branch_b, m_w_out, m_norm2_g, m_w_ffn_gate, m_w_ffn_up, m_w_ffn_down, m_final_g, v_norm1_g, v_w_in, v_b_forget, v_attn_sinks, v_rel_bias, v_w_branch_a, v_w_branch_b, v_w_out, v_norm2_g, v_w_ffn_gate, v_w_ffn_up, v_w_ffn_down, v_final_g):
    s, d = SEQ, D_MODEL
    assert x.shape == (1, s, d) and w_in.shape == (1, d, W_IN_SH)
    xs = x[0]
    place = jnp.stack([2 * lax.axis_index("x") + lax.axis_index("y"), lax.axis_index("c")]).astype(jnp.int32)

    w_gu_l = jnp.stack([_pad_to(w_ffn_gate[0], (d, FF_P)), _pad_to(w_ffn_up[0], (d, FF_P))]).astype(BF16)
    w_dn_l = _pad_to(w_ffn_down, (1, FF_P, d)).astype(BF16)
    w_in_l, w_abo_l = w_in.astype(BF16), [w_branch_a.astype(BF16), w_branch_b.astype(BF16), w_out.astype(BF16)]
    j_in = _gather_ici_job([w_in_l])
    _comm_now("gather_w_in_ici", [j_in])
    j_in1 = _gather_relay_job(j_in.out)
    _comm_now("gather_w_in_relay", [j_in1])
    j_in2 = _gather_d2d_job(j_in1.out)
    _comm_now("gather_w_in_d2d", [j_in2])
    w_in4 = _with_own(j_in2.out, [w_in_l])[0]
    j_abo = _gather_ici_job(w_abo_l)
    j_dn = _gather_ici_job([w_dn_l])
    j_gu = _gather_ici_job([w_gu_l])
    w_full = jnp.concatenate([w_in4[j] for j in range(N_CHIP)], axis=1)
    n_qkv_a = 3 * W_A
    w_p = jnp.concatenate([w_full[:, :n_qkv_a], w_full[:, n_qkv_a + H_A:], w_full[:, n_qkv_a:n_qkv_a + H_A],
                           jnp.zeros((d, LANE - H_A), BF16)], axis=1)

    h1 = _rms_fwd("norm1_fwd", xs, norm1_g)
    qkv = _matmul("proj_qkv", h1, w_p, m=s, n=W_QKV, k=d, out_shape=(s, W_QKV), out_dtype=BF16, jobs=[j_abo])
    j_abo1 = _gather_relay_job(j_abo.out)
    proj_g = _matmul("proj_gates", h1, w_p, m=s, n=2 * d, k=d, b_noff=OFF_GA, out_shape=(s, 2 * d), out_dtype=F32,
                     jobs=[j_abo1])
    fa = _matmul("proj_forget", h1, w_p, m=s, n=LANE, k=d, b_noff=OFF_FA, tn=LANE, out_shape=(s, LANE), out_dtype=F32)
    b_f = _pad_to(b_forget, (1, LANE))
    c_cum = _forget_fwd(fa, b_f)
    c_t = c_cum[:, :H_A].T
    c_col = jnp.broadcast_to(c_t[:, :, None], (H_A, s, LANE))
    c_row = c_t[:, None, :]
    j_abo2 = _gather_d2d_job(j_abo1.out)
    attn_a, lse_col = _fox_fwd(qkv, c_col, c_row, jobs=[j_abo2, j_gu])
    w_a, w_b, w_o = _with_own(j_abo2.out, w_abo_l)
    w_o = w_o.reshape(d, d)

    bucket = jnp.asarray(_t5_bucket_table())
    bias = _bias_table(rel_bias.T, bucket).reshape(H_B, BLOCK, 2 * BLOCK)
    j_gu1 = _gather_relay_job(j_gu.out)
    attn_b = _swa_fwd(qkv, bias, attn_sinks, jobs=[j_gu1])

    j_gu2 = _gather_d2d_job(j_gu1.out)
    ya = _matmul("branch_a", attn_a, w_a, m=s, n=d, k=W_A, b_kind="col", tn=_tile(A_SH, TN),
                 out_shape=(s, d), out_dtype=F32, jobs=[j_gu2])
    w_gu = _with_own(j_gu2.out, [w_gu_l])[0]
    yb = _matmul("branch_b", attn_b, w_b, m=s, n=d, k=W_QB, b_kind="col", tn=_tile(A_SH, TN),
                 out_shape=(s, d), out_dtype=F32)
    mixed = _gate_fwd(ya, yb, proj_g)
    x1 = _matmul("out_proj", mixed, w_o, m=s, n=d, k=d, res=xs, out_shape=(s, d), out_dtype=F32)

    h2 = _rms_fwd("norm2_fwd", x1, norm2_g)
    gu = _matmul("ffn_gate_up", h2, w_gu, m=s, n=2 * FP, k=d, b_kind="col", tn=_tile(FF_P, 1408),
                 out_shape=(s, 2 * FP), out_dtype=BF16, jobs=[j_dn])
    j_dn1 = _gather_relay_job(j_dn.out)
    hidden = _swiglu_fwd(gu, jobs=[j_dn1])
    j_dn2 = _gather_d2d_job(j_dn1.out)
    _comm_now("gather_w_down_d2d", [j_dn2])
    w_dn = _with_own(j_dn2.out, [w_dn_l])[0].reshape(FP, d)
    x2 = _matmul("ffn_down", hidden, w_dn, m=s, n=d, k=FP, tk=_tile(FP, 5632), res=x1, out_shape=(s, d), out_dtype=F32)

    dx2, dx2_b, d_fg, loss_tile = _loss_head(x2, loss_target[0], final_g.reshape(1, d))
    dhidden = _matmul("ffn_down_dx", dx2_b, w_dn, m=s, n=FP, k=d, nt=True, tn=_tile(FF_P, 1408),
                      out_shape=(s, FP), out_dtype=BF16)
    dgu = _swiglu_bwd(dhidden, gu)
    g_dn = _matmul("ffn_down_dw", hidden.T, dx2_b, m=FP, n=d, k=s, tm=_tile(FF_P, 1408, 16),
                   out_shape=(FP, d), out_dtype=BF16).reshape(N_CHIP, FF_P, d)
    j_p_dn = _pair_job([g_dn])
    g_gu = _matmul("ffn_gate_up_dw", h2.T, dgu, m=d, n=2 * FP, k=s, o_kind="col", tm=_tile(d, 512, 16),
                   tn=_tile(FF_P, 1408), out_shape=(2 * N_CHIP, d, FF_P), out_dtype=BF16, jobs=[j_p_dn])
    ps_dn = _pair_sum("pair_sum_w_ffn_down", g_dn, j_p_dn.out[0], place)
    j_c_dn = _chip_job([ps_dn])
    j_p_gu = _pair_job([g_gu])
    dh2 = _matmul("ffn_gate_up_dx", dgu, w_gu, m=s, n=d, k=2 * FP, nt=True, b_kind="col", tn=_tile(d, 1024),
                  tk=_tile(FF_P, TK), out_shape=(s, d), out_dtype=F32, jobs=[j_c_dn, j_p_gu])
    h_dn = _chip_sum("chip_sum_w_ffn_down", ps_dn, j_c_dn.out[0], place)
    ps_gu = _pair_sum("pair_sum_w_ffn_gate_up", g_gu, j_p_gu.out[0], place)
    dx1, dx1_b, d_n2 = _rms_bwd("norm2_bwd", dh2, x1, norm2_g, dx2)

    dmixed = _matmul("out_proj_dx", dx1_b, w_o, m=s, n=d, k=d, nt=True, out_shape=(s, d), out_dtype=BF16)
    dya, dyb, dga, dgb = _gate_bwd(dmixed, ya, yb, proj_g)
    g_a = _matmul("branch_a_dw", attn_a.T, dya, m=W_A, n=d, k=s, o_kind="col", tn=_tile(A_SH, TN),
                  out_shape=(N_CHIP, W_A, A_SH), out_dtype=BF16)
    g_b = _matmul("branch_b_dw", attn_b.T, dyb, m=W_QB, n=d, k=s, o_kind="col", tn=_tile(A_SH, TN),
                  out_shape=(N_CHIP, W_QB, A_SH), out_dtype=BF16)
    j_p_ab = _pair_job([g_a, g_b])
    dattn_a = _matmul("branch_a_dx", dya, w_a, m=s, n=W_A, k=d, nt=True, b_kind="col", tn=_tile(W_A, 1024),
                      tk=_tile(A_SH, TK), out_shape=(s, W_A), out_dtype=BF16, jobs=[j_p_ab])
    dattn_b = _matmul("branch_b_dx", dyb, w_b, m=s, n=W_QB, k=d, nt=True, b_kind="col", tn=_tile(W_QB, 1024),
                      tk=_tile(A_SH, TK), out_shape=(s, W_QB), out_dtype=BF16)
    ps_a, ps_b = (_pair_sum("pair_sum_" + n, g, r, place) for n, g, r in
                  zip(("w_branch_a", "w_branch_b"), (g_a, g_b), j_p_ab.out))

    ps_g, ps_u = ps_gu[0::2], ps_gu[1::2]
    j_c_g = _chip_job([ps_g])
    dq_a, delta_col = _fox_dq(qkv, dattn_a, c_col, c_row, lse_col, jobs=[j_c_g])
    lse_row = lse_col[:, :, 0][:, None, :]
    delta_row = delta_col[:, :, 0][:, None, :]
    j_c_ab = _chip_job([ps_a, ps_b])
    j_s_dn = _share_job([h_dn])
    dk_a, dv_a, dc_col = _fox_dkv(qkv, dattn_a, c_col, c_row, lse_row, delta_row, jobs=[j_c_ab, j_s_dn])
    dc = _pad_to(dc_col[:, :, 0].T, (s, LANE))
    df, d_bf = _forget_bwd(dc, fa, b_f)
    h_g = _chip_sum("chip_sum_w_ffn_gate", ps_g, j_c_g.out[0], place)
    h_a, h_b = (_chip_sum("chip_sum_" + n, p, r, place) for n, p, r in
                zip(("w_branch_a", "w_branch_b"), (ps_a, ps_b), j_c_ab.out))

    j_c_u = _chip_job([ps_u])
    dq_b, dk_b, dv_b, dbias, d_sk = _swa_bwd(qkv, dattn_b, bias, attn_sinks, jobs=[j_c_u])
    h_u = _chip_sum("chip_sum_w_ffn_up", ps_u, j_c_u.out[0], place)
    j_s_rest = _share_job([h_g, h_u, h_a, h_b])
    r_dn = j_s_dn.out[0]
    d_rb = _bias_table_bwd(dbias.reshape(H_B, BLOCK * 2 * BLOCK), bucket).T

    dproj = jnp.concatenate([dq_a, dk_a, dv_a, dq_b, dk_b, dv_b, dga, dgb, df], axis=1)
    g_in_p = _matmul("in_proj_dw", h1.T, dproj, m=d, n=PW, k=s, tn=_tile(PW, 1024), out_shape=(d, PW), out_dtype=BF16,
                     jobs=[j_s_rest])
    r_g, r_u, r_a, r_b = j_s_rest.out
    g_in_full = jnp.concatenate([g_in_p[:, :n_qkv_a], g_in_p[:, OFF_FA:OFF_FA + H_A], g_in_p[:, n_qkv_a:OFF_FA]], axis=1)
    g_in = jnp.stack([g_in_full[:, j * W_IN_SH:(j + 1) * W_IN_SH] for j in range(N_CHIP)])
    j_p_in = _pair_job([g_in])
    g_o = _matmul("out_proj_dw", mixed.T, dx1_b, m=d, n=d, k=s, out_shape=(d, d), out_dtype=BF16,
                  jobs=[j_p_in]).reshape(N_CHIP, A_SH, d)
    ps_in = _pair_sum("pair_sum_w_in", g_in, j_p_in.out[0], place)
    j_c_in = _chip_job([ps_in])
    j_p_o = _pair_job([g_o])
    dh1 = _matmul("in_proj_dx", dproj, w_p, m=s, n=d, k=PW, nt=True, tn=_tile(d, 1024), tk=_tile(PW, 2560),
                  out_shape=(s, d), out_dtype=F32, jobs=[j_c_in, j_p_o])
    h_in = _chip_sum("chip_sum_w_in", ps_in, j_c_in.out[0], place)
    ps_o = _pair_sum("pair_sum_w_out", g_o, j_p_o.out[0], place)
    j_c_o = _chip_job([ps_o])
    j_s_in = _share_job([h_in])
    grad_x, _, d_n1 = _rms_bwd("norm1_bwd", dh1, xs, norm1_g, dx1, jobs=[j_c_o, j_s_in])
    h_o = _chip_sum("chip_sum_w_out", ps_o, j_c_o.out[0], place)
    j_s_o = _share_job([h_o])

    small = _small_all_reduce(_pack_small(d_n1, d_n2, d_fg, d_bf[:, :H_A], d_sk[:, :H_B], d_rb, loss_tile[0:1, 0:1]))
    loss = small[5, 0]

    grads = {
        "w_branch_a": r_a, "w_branch_b": r_b,
        "w_ffn_gate": r_g[:, :, :FF_SH], "w_ffn_up": r_u[:, :, :FF_SH], "w_ffn_down": r_dn[:, :FF_SH, :],
    }
    given = dict(w_in=(w_in, m_w_in, v_w_in), w_branch_a=(w_branch_a, m_w_branch_a, v_w_branch_a),
                 w_branch_b=(w_branch_b, m_w_branch_b, v_w_branch_b), w_out=(w_out, m_w_out, v_w_out),
                 w_ffn_gate=(w_ffn_gate, m_w_ffn_gate, v_w_ffn_gate), w_ffn_up=(w_ffn_up, m_w_ffn_up, v_w_ffn_up),
                 w_ffn_down=(w_ffn_down, m_w_ffn_down, v_w_ffn_down))

    def col_major(a):
        return jnp.transpose(a[0])[None]

    views = {n: (lambda a: a, lambda a: a) for n in given}
    views["w_ffn_gate"] = views["w_ffn_up"] = (col_major, col_major)
    views["w_in"] = (lambda a: jnp.transpose(a, (2, 0, 1)).reshape(W_IN_SH, d // LANE, LANE),
                     lambda a: jnp.transpose(a.reshape(W_IN_SH, 1, d), (1, 2, 0)))

    def adamw(n, jobs=()):
        to_view, from_view = views[n]
        g_view = to_view(grads[n])
        outs = _adamw("adamw_" + n, to_view(given[n][0]), g_view, to_view(given[n][1]), to_view(given[n][2]), jobs=jobs)
        grads[n] = from_view(g_view)
        return [from_view(o) for o in outs]

    grads["w_in"] = j_s_in.out[0]
    upd = {n: adamw(n) for n in ("w_ffn_gate", "w_ffn_up", "w_ffn_down", "w_branch_a", "w_branch_b", "w_in")}
    sm = _adamw("adamw_small",
                _pack_small(norm1_g, norm2_g, final_g, b_forget, attn_sinks, rel_bias)[None],
                small.at[5].set(0.0)[None],
                _pack_small(m_norm1_g, m_norm2_g, m_final_g, m_b_forget, m_attn_sinks, m_rel_bias)[None],
                _pack_small(v_norm1_g, v_norm2_g, v_final_g, v_b_forget, v_attn_sinks, v_rel_bias)[None],
                jobs=[j_s_o])
    grads["w_out"] = j_s_o.out[0]
    upd["w_out"] = adamw("w_out")
    g_small = _unpack_small(small)
    d_small, m_small, v_small = (_unpack_small(t[0]) for t in sm)

    order = ["norm1_g", "w_in", "b_forget", "attn_sinks", "rel_bias", "w_branch_a", "w_branch_b", "w_out",
             "norm2_g", "w_ffn_gate", "w_ffn_up", "w_ffn_down", "final_g"]
    small_at = {"norm1_g": 0, "norm2_g": 1, "final_g": 2, "b_forget": 3, "attn_sinks": 4, "rel_bias": 5}

    def pick(big_idx, small_src, n):
        return small_src[small_at[n]] if n in small_at else (grads[n] if big_idx is None else upd[n][big_idx])

    return (loss, grad_x[None],
            *[pick(None, g_small, n) for n in order], *[pick(0, d_small, n) for n in order],
            *[pick(1, m_small, n) for n in order], *[pick(2, v_small, n) for n in order])
```
